```python
import math
import jax, jax.numpy as jnp
from jax import lax
import numpy as np


D_MODEL = 1024
BATCH = 8
SEQ = 4096
DEPTH = 2

N_META = 16
N_EVEN = (DEPTH + 1) // 2
N_ODD = DEPTH // 2
EPS = 1e-6

CONV_CH = D_MODEL // 2
CONV_WIDTH = 31
SB_HEADS = 8
SB_HEAD_DIM = 64
SB_WIDTH = SB_HEADS * SB_HEAD_DIM
Q_BLOCK = 128
EVEN_IN = 2 * CONV_CH + 3 * SB_WIDTH
EVEN_MIX = CONV_CH + SB_WIDTH

LRU_WIDTH = 1408
LRU_BLOCKS = 16
LRU_BLOCK_DIM = LRU_WIDTH // LRU_BLOCKS
LRU_CONV_WIDTH = 4
LRU_C = 8.0

D_FF = 2688
N_EXPERTS = 8
TOP_K = 2
D_FF_EXPERT = 3584

kernel_name = 'hybrid_conv_stickbreak_rglru_moe'


def rms_norm(x, g):
    xf = x.astype(jnp.float32)
    y = xf * lax.rsqrt(jnp.mean(xf * xf, axis=-1, keepdims=True) + EPS)
    return (y * g.astype(jnp.float32)).astype(x.dtype)


def layer_norm(x, g, b):
    xf = x.astype(jnp.float32)
    mu = jnp.mean(xf, axis=-1, keepdims=True)
    xc = xf - mu
    var = jnp.mean(xc * xc, axis=-1, keepdims=True)
    y = xc * lax.rsqrt(var + EPS) * g.astype(jnp.float32) + b.astype(jnp.float32)
    return y.astype(x.dtype)


def causal_depthwise_conv(x, w, b):
    k_w = w.shape[0]
    c = x.shape[-1]
    y = lax.conv_general_dilated(
        x, w[:, None, :].astype(x.dtype), window_strides=(1,), padding=[(k_w - 1, 0)],
        dimension_numbers=('NWC', 'WIO', 'NWC'), feature_group_count=c)
    return y + b.astype(x.dtype)


def stick_breaking_block(q_blk, q_pos, k, v):
    z = jnp.einsum('bhqd,bhkd->bhqk', q_blk, k,
                   preferred_element_type=jnp.float32) / math.sqrt(SB_HEAD_DIM)
    k_pos = jnp.arange(k.shape[2])
    mask = k_pos[None, :] < q_pos[:, None]
    log_not = jnp.where(mask, jax.nn.log_sigmoid(-z), 0.0)
    rev = lax.cumsum(log_not, axis=3, reverse=True)
    later = jnp.concatenate([rev[..., 1:], jnp.zeros_like(rev[..., :1])], axis=-1)
    w = jnp.where(mask, jnp.exp(jax.nn.log_sigmoid(z) + later), 0.0)
    return jnp.einsum('bhqk,bhkd->bhqd', w.astype(v.dtype), v)


def stick_breaking(q, k, v):
    b, h, t, dh = q.shape
    s = t - N_META
    nb = s // Q_BLOCK
    meta_out = stick_breaking_block(q[:, :, :N_META], jnp.arange(N_META),
                                    k[:, :, :N_META], v[:, :, :N_META])
    q_real = q[:, :, N_META:].reshape(b, h, nb, Q_BLOCK, dh).transpose(2, 0, 1, 3, 4)
    pos = (N_META + jnp.arange(s)).reshape(nb, Q_BLOCK)
    out = lax.map(lambda a: stick_breaking_block(a[0], a[1], k, v), (q_real, pos))
    out = out.transpose(1, 2, 0, 3, 4).reshape(b, h, s, dh)
    return jnp.concatenate([meta_out, out], axis=2)


def conv_sb_mixer(x, norm_g, w_in, conv_w, conv_b, conv_ln_g, conv_ln_b, q_norm_g, k_norm_g, w_out):
    b, t, _ = x.shape
    hcat = rms_norm(x, norm_g) @ w_in
    conv_a, conv_gate, q, k, v = jnp.split(
        hcat, [CONV_CH, 2 * CONV_CH, 2 * CONV_CH + SB_WIDTH, 2 * CONV_CH + 2 * SB_WIDTH], axis=-1)
    u = conv_a * jax.nn.sigmoid(conv_gate)
    u = causal_depthwise_conv(u, conv_w, conv_b)
    u = jax.nn.silu(layer_norm(u, conv_ln_g, conv_ln_b))
    def heads(a):
        return a.reshape(b, t, SB_HEADS, SB_HEAD_DIM).transpose(0, 2, 1, 3)
    qh = rms_norm(heads(q), q_norm_g)
    kh = rms_norm(heads(k), k_norm_g)
    o = stick_breaking(qh, kh, heads(v)).transpose(0, 2, 1, 3).reshape(b, t, SB_WIDTH)
    return jnp.concatenate([u, o.astype(u.dtype)], axis=-1) @ w_out


def rglru_mixer(x, norm_g, w_in, conv_w, conv_b, gate_r_w, gate_r_b, gate_i_w, gate_i_b, lru_lambda, w_out):
    b, t, _ = x.shape
    hcat = rms_norm(x, norm_g) @ w_in
    gate, xb = jnp.split(hcat, 2, axis=-1)
    gate = jax.nn.gelu(gate, approximate=True)
    xb = causal_depthwise_conv(xb, conv_w, conv_b)
    xblk = xb.reshape(b, t, LRU_BLOCKS, LRU_BLOCK_DIM)
    r = jax.nn.sigmoid(jnp.einsum('btni,nij->btnj', xblk, gate_r_w).reshape(b, t, LRU_WIDTH) + gate_r_b)
    i = jax.nn.sigmoid(jnp.einsum('btni,nij->btnj', xblk, gate_i_w).reshape(b, t, LRU_WIDTH) + gate_i_b)
    log_a = -LRU_C * r.astype(jnp.float32) * jax.nn.softplus(-lru_lambda.astype(jnp.float32))
    a = jnp.exp(log_a)
    bx = jnp.sqrt(-jnp.expm1(2.0 * log_a)) * (i * xb).astype(jnp.float32)

    def step(hstate, ab):
        a_t, b_t = ab
        hstate = a_t * hstate + b_t
        return hstate, hstate

    _, hs = lax.scan(step, jnp.zeros((b, LRU_WIDTH), jnp.float32),
                     (a.transpose(1, 0, 2), bx.transpose(1, 0, 2)))
    y = hs.transpose(1, 0, 2).astype(x.dtype)
    return (gate * y) @ w_out


def swiglu(x, w_gate, w_up, w_down):
    return (jax.nn.silu(x @ w_gate) * (x @ w_up)) @ w_down


def moe_swiglu(x, router_w, w_gate, w_up, w_down):
    b, t, d = x.shape
    xf = x.reshape(b * t, d)
    logits = (xf @ router_w).astype(jnp.float32)
    top_vals, top_idx = lax.top_k(logits, TOP_K)
    top_w = jax.nn.softmax(top_vals, axis=-1)
    gates = jnp.sum(jax.nn.one_hot(top_idx, N_EXPERTS, dtype=jnp.float32) * top_w[..., None], axis=1)
    y = jnp.zeros_like(xf)
    for e in range(N_EXPERTS):
        y = y + gates[:, e:e + 1].astype(x.dtype) * swiglu(xf, w_gate[e], w_up[e], w_down[e])
    return y.reshape(b, t, d)


def setup_inputs(seed: int = 0) -> dict:
    key = jax.random.key(seed)
    ks = list(jax.random.split(key, 40))
    D = D_MODEL
    PE, PO = N_EVEN, N_ODD

    def nrm(k, shape, scale):
        return scale * jax.random.normal(k, shape, jnp.float32)

    def gain(k, shape):
        return 1.0 + 0.05 * jax.random.normal(k, shape, jnp.float32)

    u = jax.random.uniform(ks[26], (PO, LRU_WIDTH), jnp.float32, 0.9, 0.999)
    p = u ** (1.0 / LRU_C)
    lru_lambda = jnp.log(p) - jnp.log1p(-p)

    return {
        'x': nrm(ks[0], (BATCH, SEQ, D), 1.0),
        'meta_tokens': nrm(ks[1], (N_META, D), 1.0),
        'mix_norm_even': gain(ks[2], (PE, D)),
        'w_in_even': nrm(ks[3], (PE, D, EVEN_IN), D ** -0.5),
        'conv_w': nrm(ks[4], (PE, CONV_WIDTH, CONV_CH), CONV_WIDTH ** -0.5),
        'conv_b': nrm(ks[5], (PE, CONV_CH), 0.01),
        'conv_ln_g': gain(ks[6], (PE, CONV_CH)),
        'conv_ln_b': nrm(ks[7], (PE, CONV_CH), 0.01),
        'q_norm_g': gain(ks[8], (PE, SB_HEAD_DIM)),
        'k_norm_g': gain(ks[9], (PE, SB_HEAD_DIM)),
        'w_out_even': nrm(ks[10], (PE, EVEN_MIX, D), EVEN_MIX ** -0.5),
        'ffn_norm_even': gain(ks[11], (PE, D)),
        'ffn_w_gate': nrm(ks[12], (PE, D, D_FF), D ** -0.5),
        'ffn_w_up': nrm(ks[13], (PE, D, D_FF), D ** -0.5),
        'ffn_w_down': nrm(ks[14], (PE, D_FF, D), D_FF ** -0.5),
        'mix_norm_odd': gain(ks[15], (PO, D)),
        'w_in_odd': nrm(ks[16], (PO, D, 2 * LRU_WIDTH), D ** -0.5),
        'lru_conv_w': nrm(ks[17], (PO, LRU_CONV_WIDTH, LRU_WIDTH), LRU_CONV_WIDTH ** -0.5),
        'lru_conv_b': nrm(ks[18], (PO, LRU_WIDTH), 0.01),
        'gate_r_w': nrm(ks[19], (PO, LRU_BLOCKS, LRU_BLOCK_DIM, LRU_BLOCK_DIM), LRU_BLOCK_DIM ** -0.5),
        'gate_r_b': nrm(ks[20], (PO, LRU_WIDTH), 0.01),
        'gate_i_w': nrm(ks[21], (PO, LRU_BLOCKS, LRU_BLOCK_DIM, LRU_BLOCK_DIM), LRU_BLOCK_DIM ** -0.5),
        'gate_i_b': nrm(ks[22], (PO, LRU_WIDTH), 0.01),
        'lru_lambda': lru_lambda,
        'w_out_odd': nrm(ks[23], (PO, LRU_WIDTH, D), LRU_WIDTH ** -0.5),
        'ffn_norm_odd': gain(ks[24], (PO, D)),
        'router_w': nrm(ks[25], (PO, D, N_EXPERTS), D ** -0.5),
        'moe_w_gate': nrm(ks[27], (PO, N_EXPERTS, D, D_FF_EXPERT), D ** -0.5),
        'moe_w_up': nrm(ks[28], (PO, N_EXPERTS, D, D_FF_EXPERT), D ** -0.5),
        'moe_w_down': nrm(ks[29], (PO, N_EXPERTS, D_FF_EXPERT, D), D_FF_EXPERT ** -0.5),
    }


def reference(x, meta_tokens, mix_norm_even, w_in_even, conv_w, conv_b, conv_ln_g, conv_ln_b,
              q_norm_g, k_norm_g, w_out_even, ffn_norm_even, ffn_w_gate, ffn_w_up, ffn_w_down,
              mix_norm_odd, w_in_odd, lru_conv_w, lru_conv_b, gate_r_w, gate_r_b, gate_i_w, gate_i_b,
              lru_lambda, w_out_odd, ffn_norm_odd, router_w, moe_w_gate, moe_w_up, moe_w_down):
    b = x.shape[0]
    meta = jnp.broadcast_to(meta_tokens[None].astype(x.dtype), (b, N_META, D_MODEL))
    h = jnp.concatenate([meta, x], axis=1)
    for layer in range(DEPTH):
        p = layer // 2
        if layer % 2 == 0:
            h = h + conv_sb_mixer(h, mix_norm_even[p], w_in_even[p], conv_w[p], conv_b[p],
                                  conv_ln_g[p], conv_ln_b[p], q_norm_g[p], k_norm_g[p], w_out_even[p])
            h = h + swiglu(rms_norm(h, ffn_norm_even[p]), ffn_w_gate[p], ffn_w_up[p], ffn_w_down[p])
        else:
            h = h + rglru_mixer(h, mix_norm_odd[p], w_in_odd[p], lru_conv_w[p], lru_conv_b[p],
                                gate_r_w[p], gate_r_b[p], gate_i_w[p], gate_i_b[p], lru_lambda[p],
                                w_out_odd[p])
            h = h + moe_swiglu(rms_norm(h, ffn_norm_odd[p]), router_w[p], moe_w_gate[p],
                               moe_w_up[p], moe_w_down[p])
    return h[:, N_META:]
```

```python
import functools
import math

import jax
import jax.numpy as jnp
from jax import lax
from jax.experimental import pallas as pl
from jax.experimental.pallas import tpu as pltpu

F32 = jnp.float32
BF16 = jnp.bfloat16

EPS = 1e-6
N_META = 16
CONV_CH = 512
CONV_WIDTH = 31
SB_HEADS = 8
SB_HEAD_DIM = 64
SB_WIDTH = SB_HEADS * SB_HEAD_DIM
LRU_BLOCKS = 16
LRU_CONV_WIDTH = 4
LRU_C = 8.0
N_EXPERTS = 8

LANES = 128
MXU_DIM = 256
TIME_TILE = 384
VMEM_LIMIT = 50 * 1024 * 1024

ROW_TILE = 512
CONV_TT = 128
CONV_HALO = 32
CONV_CHUNK = 32
ATT_T = 128
LRU_TT = 384
LRU_GATE_TILE = 256


def _dot(a, b):
    return jnp.dot(a, b, preferred_element_type=F32)


def _params(*sem):
    return pltpu.CompilerParams(dimension_semantics=sem, vmem_limit_bytes=VMEM_LIMIT)


def _rms(x, g):
    ms = jnp.mean(x * x, axis=-1, keepdims=True)
    return x * lax.rsqrt(ms + EPS) * g


def _even_in_kernel(h_ref, g_ref, w_ref, qg_ref, kg_ref, hm_ref, u_ref, q_ref, k_ref, v_ref):
    xn = _rms(h_ref[...], g_ref[...]).astype(BF16)
    a = _dot(xn, w_ref[:, 0:CONV_CH])
    gate = _dot(xn, w_ref[:, CONV_CH:2 * CONV_CH])
    u_ref[...] = (a * jax.nn.sigmoid(gate)).astype(BF16)

    def head_norm(y, gain):
        yy = (y * y).astype(BF16)
        parts = [_dot(yy[:, c:c + MXU_DIM], hm_ref[...]) for c in range(0, SB_WIDTH, MXU_DIM)]
        ms = jnp.concatenate(parts, axis=1)
        return y * lax.rsqrt(ms + EPS) * gain

    c0 = 2 * CONV_CH
    q = head_norm(_dot(xn, w_ref[:, c0:c0 + SB_WIDTH]), qg_ref[...])
    q_ref[...] = (q * (1.0 / math.sqrt(SB_HEAD_DIM))).astype(BF16)
    k = head_norm(_dot(xn, w_ref[:, c0 + SB_WIDTH:c0 + 2 * SB_WIDTH]), kg_ref[...])
    k_ref[...] = k.astype(BF16)
    v_ref[...] = _dot(xn, w_ref[:, c0 + 2 * SB_WIDTH:c0 + 3 * SB_WIDTH]).astype(BF16)


def _even_in_proj(h, g, w, qg, kg, hm):
    n, d = h.shape
    tm = ROW_TILE
    row = lambda i: (i, 0)
    const = lambda i: (0, 0)
    out = jax.ShapeDtypeStruct((n, SB_WIDTH), BF16)
    return pl.pallas_call(
        _even_in_kernel,
        grid=(n // tm,),
        in_specs=[pl.BlockSpec((tm, d), row), pl.BlockSpec((1, d), const),
                  pl.BlockSpec(w.shape, const), pl.BlockSpec((1, SB_WIDTH), const),
                  pl.BlockSpec((1, SB_WIDTH), const), pl.BlockSpec(hm.shape, const)],
        out_specs=[pl.BlockSpec((tm, SB_WIDTH), row)] * 4,
        out_shape=[out] * 4,
        compiler_params=_params("parallel"),
        name="even_in_proj",
    )(h, g, w, qg, kg, hm)


def _conv_kernel(cur_ref, halo_ref, w_ref, b_ref, lg_ref, lb_ref, o_ref, buf_ref):
    i = pl.program_id(1)
    tt = cur_ref.shape[1]
    halo = halo_ref[0].astype(F32)
    buf_ref[0:CONV_HALO, :] = jnp.where(i > 0, halo, 0.0)
    buf_ref[CONV_HALO:CONV_HALO + tt, :] = cur_ref[0].astype(F32)
    off = CONV_HALO - (CONV_WIDTH - 1)
    for c in range(tt // CONV_CHUNK):
        r0 = c * CONV_CHUNK
        acc = jnp.broadcast_to(b_ref[...], (CONV_CHUNK, CONV_CH))
        for k in range(CONV_WIDTH):
            acc = acc + w_ref[k:k + 1, :] * buf_ref[r0 + off + k:r0 + off + k + CONV_CHUNK, :]
        mu = jnp.mean(acc, axis=-1, keepdims=True)
        xc = acc - mu
        var = jnp.mean(xc * xc, axis=-1, keepdims=True)
        y = xc * lax.rsqrt(var + EPS) * lg_ref[...] + lb_ref[...]
        o_ref[0, r0:r0 + CONV_CHUNK, :] = (y * jax.nn.sigmoid(y)).astype(BF16)


def _conv_module(u, w, b, lg, lb):
    bsz, tp, c = u.shape
    tt = CONV_TT
    per = tt // CONV_HALO
    const = lambda bi, i: (0, 0)
    return pl.pallas_call(
        _conv_kernel,
        grid=(bsz, tp // tt),
        in_specs=[pl.BlockSpec((1, tt, c), lambda bi, i: (bi, i, 0)),
                  pl.BlockSpec((1, CONV_HALO, c), lambda bi, i: (bi, jnp.maximum(i * per - 1, 0), 0)),
                  pl.BlockSpec(w.shape, const), pl.BlockSpec((1, c), const),
                  pl.BlockSpec((1, c), const), pl.BlockSpec((1, c), const)],
        out_specs=pl.BlockSpec((1, tt, c), lambda bi, i: (bi, i, 0)),
        out_shape=jax.ShapeDtypeStruct(u.shape, BF16),
        scratch_shapes=[pltpu.VMEM((CONV_HALO + tt, c), F32)],
        compiler_params=_params("parallel", "parallel"),
        name="conv_module",
    )(u, u, w, b, lg, lb)


def _attn_kernel(q_ref, k_ref, v_ref, uo_ref, o_ref, acc_ref, ca_ref, cb_ref):
    i = pl.program_id(2)
    t = ATT_T
    q = q_ref[0]
    lane = lax.broadcasted_iota(jnp.int32, (t, LANES), 1)
    zero = jnp.zeros_like(q)
    q2 = jnp.concatenate([jnp.where(lane < SB_HEAD_DIM, q, zero),
                          jnp.where(lane >= SB_HEAD_DIM, q, zero)], axis=0)

    def tile(j, mask2, ca, cb):
        kt = k_ref[0, pl.ds(pl.multiple_of(j * t, t), t), :]
        vt = v_ref[0, pl.ds(pl.multiple_of(j * t, t), t), :]
        z = lax.dot_general(q2, kt, (((1,), (1,)), ((), ())), preferred_element_type=F32)
        log_not = -(jnp.maximum(z, 0.0) + jnp.log(1.0 + jnp.exp(-jnp.abs(z))))
        if mask2 is not None:
            log_not = jnp.where(mask2, log_not, 0.0)
        r = _dot(log_not.astype(BF16), uo_ref[...])
        carry = jnp.concatenate([ca, cb], axis=0)
        w = jnp.exp(z + r[:, 0:t] + carry)
        if mask2 is not None:
            w = jnp.where(mask2, w, 0.0)
        w = w.astype(BF16)
        vzero = jnp.zeros_like(vt)
        v2 = jnp.concatenate([jnp.where(lane < SB_HEAD_DIM, vt, vzero),
                              jnp.where(lane >= SB_HEAD_DIM, vt, vzero)], axis=0)
        pv = _dot(jnp.concatenate([w[0:t], w[t:2 * t]], axis=1), v2)
        return pv, ca + r[0:t, t:2 * t], cb + r[t:2 * t, t:2 * t]

    row = lax.broadcasted_iota(jnp.int32, (2 * t, t), 0)
    col = lax.broadcasted_iota(jnp.int32, (2 * t, t), 1)
    diag = col < jnp.where(row >= t, row - t, row)
    zc = jnp.zeros((t, t), F32)
    pv, ca, cb = tile(i, diag, zc, zc)
    acc_ref[...] = pv
    ca_ref[...] = ca
    cb_ref[...] = cb

    def body(jj, carry):
        pv, ca, cb = tile(i - 1 - jj, None, ca_ref[...], cb_ref[...])
        acc_ref[...] += pv
        ca_ref[...] = ca
        cb_ref[...] = cb
        return carry

    lax.fori_loop(0, i, body, 0)
    o_ref[0] = acc_ref[...].astype(BF16)


def _attention(q, k, v, uo):
    bsz, tp, width = q.shape
    t = ATT_T
    return pl.pallas_call(
        _attn_kernel,
        grid=(bsz, width // LANES, tp // t),
        in_specs=[pl.BlockSpec((1, t, LANES), lambda b, h, i: (b, i, h)),
                  pl.BlockSpec((1, tp, LANES), lambda b, h, i: (b, 0, h)),
                  pl.BlockSpec((1, tp, LANES), lambda b, h, i: (b, 0, h)),
                  pl.BlockSpec(uo.shape, lambda b, h, i: (0, 0))],
        out_specs=pl.BlockSpec((1, t, LANES), lambda b, h, i: (b, i, h)),
        out_shape=jax.ShapeDtypeStruct(q.shape, BF16),
        scratch_shapes=[pltpu.VMEM((t, LANES), F32), pltpu.VMEM((t, t), F32), pltpu.VMEM((t, t), F32)],
        compiler_params=_params("parallel", "parallel", "arbitrary"),
        name="stick_breaking",
    )(q, k, v, uo)


def _even_out_kernel(h_ref, u_ref, o_ref, w_ref, g_ref, h1_ref, xn_ref):
    h1 = h_ref[...] + _dot(u_ref[...], w_ref[0:CONV_CH, :]) + _dot(o_ref[...], w_ref[CONV_CH:, :])
    h1_ref[...] = h1
    xn_ref[...] = _rms(h1, g_ref[...]).astype(BF16)


def _even_out_proj(h, u, o, w, g):
    n, d = h.shape
    tm = ROW_TILE
    row = lambda i: (i, 0)
    const = lambda i: (0, 0)
    return pl.pallas_call(
        _even_out_kernel,
        grid=(n // tm,),
        in_specs=[pl.BlockSpec((tm, d), row), pl.BlockSpec((tm, CONV_CH), row),
                  pl.BlockSpec((tm, SB_WIDTH), row), pl.BlockSpec(w.shape, const),
                  pl.BlockSpec((1, d), const)],
        out_specs=[pl.BlockSpec((tm, d), row), pl.BlockSpec((tm, d), row)],
        out_shape=[jax.ShapeDtypeStruct((n, d), F32), jax.ShapeDtypeStruct((n, d), BF16)],
        compiler_params=_params("parallel"),
        name="even_out_proj",
    )(h, u, o, w, g)


def _ffn_kernel(*refs, tf, gated):
    if gated:
        xn_ref, res_ref, gates_ref, wgu_ref, wd_ref, o_ref = refs
    else:
        xn_ref, res_ref, wgu_ref, wd_ref, o_ref = refs
    e = pl.program_id(1)
    f = pl.program_id(2)

    @pl.when((e == 0) & (f == 0))
    def _():
        o_ref[...] = res_ref[...]

    gu = _dot(xn_ref[...], wgu_ref[0])
    g = gu[:, 0:tf]
    mid = g * jax.nn.sigmoid(g) * gu[:, tf:2 * tf]
    if gated:
        gates = gates_ref[...]
        lane = lax.broadcasted_iota(jnp.int32, gates.shape, 1)
        mid = mid * jnp.sum(jnp.where(lane == e, gates, 0.0), axis=1, keepdims=True)
    o_ref[...] += _dot(mid.astype(BF16), wd_ref[0])


def _ffn(xn, res, gates, wgu, wd, tf):
    n, d = xn.shape
    n_e, ff = wd.shape[0], wd.shape[1]
    tm = ROW_TILE
    row = lambda i, e, f: (i, 0)
    gated = gates is not None
    in_specs = [pl.BlockSpec((tm, d), row), pl.BlockSpec((tm, d), row)]
    args = [xn, res]
    if gated:
        in_specs.append(pl.BlockSpec((tm, LANES), row))
        args.append(gates)
    in_specs += [pl.BlockSpec((1, d, 2 * tf), lambda i, e, f: (e, 0, f)),
                 pl.BlockSpec((1, tf, d), lambda i, e, f: (e, f, 0))]
    return pl.pallas_call(
        functools.partial(_ffn_kernel, tf=tf, gated=gated),
        grid=(n // tm, n_e, ff // tf),
        in_specs=in_specs,
        out_specs=pl.BlockSpec((tm, d), row),
        out_shape=jax.ShapeDtypeStruct((n, d), F32),
        compiler_params=_params("parallel", "arbitrary", "arbitrary"),
        name="swiglu_gated" if gated else "swiglu",
    )(*args, wgu, wd)


def _odd_in_kernel(h_ref, g_ref, w_ref, gate_ref, xb_ref):
    xn = _rms(h_ref[...], g_ref[...]).astype(BF16)
    y = _dot(xn, w_ref[...])
    lw = gate_ref.shape[1]
    gate_ref[...] = jax.nn.gelu(y[:, 0:lw], approximate=True).astype(BF16)
    xb_ref[...] = y[:, lw:2 * lw].astype(BF16)


def _odd_in_proj(h, g, w):
    n, d = h.shape
    lw = w.shape[1] // 2
    tm = ROW_TILE
    row = lambda i: (i, 0)
    const = lambda i: (0, 0)
    out = jax.ShapeDtypeStruct((n, lw), BF16)
    return pl.pallas_call(
        _odd_in_kernel,
        grid=(n // tm,),
        in_specs=[pl.BlockSpec((tm, d), row), pl.BlockSpec((1, d), const), pl.BlockSpec(w.shape, const)],
        out_specs=[pl.BlockSpec((tm, lw), row)] * 2,
        out_shape=[out] * 2,
        compiler_params=_params("parallel"),
        name="odd_in_proj",
    )(h, g, w)


def _gate_windows(lw, bd):
    out = []
    for c0 in range(0, lw, LRU_GATE_TILE):
        nc = min(LRU_GATE_TILE, lw - c0)
        r0 = (c0 // bd) * bd
        r1 = ((c0 + nc - 1) // bd + 1) * bd
        r0 = (r0 // LANES) * LANES
        r1 = min(-(-r1 // LANES) * LANES, lw)
        out.append((c0, nc, r0, r1 - r0))
    return out


def _lru_kernel(xb_ref, gate_ref, h_ref, cw_ref, cb_ref, wg_ref, br_ref, bi_ref, lam_ref, wo_ref,
                o_ref, xbuf_ref, hs_ref, a_ref, b_ref, y_ref, *, windows):
    ti = pl.program_id(1)
    tt, lw = a_ref.shape

    @pl.when(ti == 0)
    def _():
        xbuf_ref[0:8, :] = jnp.zeros((8, lw), F32)
        hs_ref[...] = jnp.zeros((8, lw), F32)

    @pl.when(ti > 0)
    def _():
        xbuf_ref[0:8, :] = xbuf_ref[tt:tt + 8, :]

    xbuf_ref[8:8 + tt, :] = xb_ref[0].astype(F32)
    off = 8 - (LRU_CONV_WIDTH - 1)
    xc = jnp.broadcast_to(cb_ref[...], (tt, lw))
    for k in range(LRU_CONV_WIDTH):
        xc = xc + cw_ref[k:k + 1, :] * xbuf_ref[off + k:off + k + tt, :]
    xcb = xc.astype(BF16)
    sp_lam = jnp.log(1.0 + jnp.exp(-lam_ref[...]))
    wcol = 0
    for (c0, nc, r0, nr) in windows:
        rg = _dot(xcb[:, r0:r0 + nr], wg_ref[r0:r0 + nr, wcol:wcol + 2 * nc])
        wcol += 2 * nc
        r = jax.nn.sigmoid(rg[:, 0:nc] + br_ref[:, c0:c0 + nc])
        ig = jax.nn.sigmoid(rg[:, nc:2 * nc] + bi_ref[:, c0:c0 + nc])
        a = jnp.exp(-LRU_C * r * sp_lam[:, c0:c0 + nc])
        a_ref[:, c0:c0 + nc] = a
        b_ref[:, c0:c0 + nc] = jnp.sqrt(1.0 - a * a) * (ig * xc[:, c0:c0 + nc])

    rowi = lax.broadcasted_iota(jnp.int32, (8, lw), 0)

    def group(gi, hprev):
        r0 = pl.multiple_of(gi * 8, 8)
        a = a_ref[pl.ds(r0, 8), :]
        b = b_ref[pl.ds(r0, 8), :]
        for s in (1, 2, 4):
            m = rowi >= s
            a_sh = jnp.where(m, pltpu.roll(a, s, axis=0), 1.0)
            b_sh = jnp.where(m, pltpu.roll(b, s, axis=0), 0.0)
            b = a * b_sh + b
            a = a * a_sh
        hrows = a * hprev + b
        y_ref[pl.ds(r0, 8), :] = hrows
        return jnp.broadcast_to(hrows[7:8, :], (8, lw))

    hs_ref[...] = lax.fori_loop(0, tt // 8, group, hs_ref[...])
    gy = (gate_ref[0].astype(F32) * y_ref[...]).astype(BF16)
    o_ref[0] = h_ref[0] + _dot(gy, wo_ref[...])


def _lru_mixer(xb, gate, h, cw, cb, wg, br, bi, lam, wo, windows):
    bsz, tp, lw = xb.shape
    d = h.shape[-1]
    tt = LRU_TT
    const = lambda b, t: (0, 0)
    blk = lambda b, t: (b, t, 0)
    return pl.pallas_call(
        functools.partial(_lru_kernel, windows=windows),
        grid=(bsz, tp // tt),
        in_specs=[pl.BlockSpec((1, tt, lw), blk), pl.BlockSpec((1, tt, lw), blk),
                  pl.BlockSpec((1, tt, d), blk), pl.BlockSpec(cw.shape, const),
                  pl.BlockSpec((1, lw), const), pl.BlockSpec(wg.shape, const),
                  pl.BlockSpec((1, lw), const), pl.BlockSpec((1, lw), const),
                  pl.BlockSpec((1, lw), const), pl.BlockSpec(wo.shape, const)],
        out_specs=pl.BlockSpec((1, tt, d), blk),
        out_shape=jax.ShapeDtypeStruct(h.shape, F32),
        scratch_shapes=[pltpu.VMEM((8 + tt, lw), F32), pltpu.VMEM((8, lw), F32),
                        pltpu.VMEM((tt, lw), F32), pltpu.VMEM((tt, lw), F32), pltpu.VMEM((tt, lw), F32)],
        compiler_params=_params("arbitrary", "arbitrary"),
        name="rglru_mixer",
    )(xb, gate, h, cw, cb, wg, br, bi, lam, wo)


def _router_kernel(h_ref, g_ref, rw_ref, xn_ref, gates_ref):
    xn = _rms(h_ref[...], g_ref[...])
    xn_ref[...] = xn.astype(BF16)
    logits = jnp.dot(xn, rw_ref[...], preferred_element_type=F32, precision=lax.Precision.HIGHEST)
    lane = lax.broadcasted_iota(jnp.int32, logits.shape, 1)
    neg = jnp.float32(-jnp.inf)
    logits = jnp.where(lane < N_EXPERTS, logits, neg)
    top1 = jnp.max(logits, axis=1, keepdims=True)
    idx1 = jnp.min(jnp.where(logits == top1, lane, LANES), axis=1, keepdims=True)
    rest = jnp.where(lane == idx1, neg, logits)
    top2 = jnp.max(rest, axis=1, keepdims=True)
    idx2 = jnp.min(jnp.where(rest == top2, lane, LANES), axis=1, keepdims=True)
    w1 = 1.0 / (1.0 + jnp.exp(top2 - top1))
    gates_ref[...] = jnp.where(lane == idx1, w1, 0.0) + jnp.where(lane == idx2, 1.0 - w1, 0.0)


def _router(h, g, rw):
    n, d = h.shape
    tm = ROW_TILE
    row = lambda i: (i, 0)
    const = lambda i: (0, 0)
    return pl.pallas_call(
        _router_kernel,
        grid=(n // tm,),
        in_specs=[pl.BlockSpec((tm, d), row), pl.BlockSpec((1, d), const), pl.BlockSpec(rw.shape, const)],
        out_specs=[pl.BlockSpec((tm, d), row), pl.BlockSpec((tm, LANES), row)],
        out_shape=[jax.ShapeDtypeStruct((n, d), BF16), jax.ShapeDtypeStruct((n, LANES), F32)],
        compiler_params=_params("parallel"),
        name="router",
    )(h, g, rw)


def _interleave_gate_up(wg, wu, tf):
    e, d, ff = wg.shape
    g = wg.astype(BF16).reshape(e, d, ff // tf, tf)
    u = wu.astype(BF16).reshape(e, d, ff // tf, tf)
    return jnp.concatenate([g, u], axis=3).reshape(e, d, 2 * ff)


def _dense_block_diag(w):
    nb, bd, _ = w.shape
    eye = jnp.eye(nb, dtype=w.dtype)
    return (w[:, :, None, :] * eye[:, None, :, None]).reshape(nb * bd, nb * bd)


def _pack_lru_gates(wr, wi, windows):
    dr = _dense_block_diag(wr).astype(BF16)
    di = _dense_block_diag(wi).astype(BF16)
    cols = []
    for (c0, nc, _, _) in windows:
        cols += [dr[:, c0:c0 + nc], di[:, c0:c0 + nc]]
    return jnp.concatenate(cols, axis=1)


def _ffn_chunk(ff):
    for tf in (896, 512, 384, 256, 128):
        if ff % tf == 0:
            return tf
    raise ValueError(f"unsupported d_ff {ff}")


def kernel(x, meta_tokens, mix_norm_even, w_in_even, conv_w, conv_b, conv_ln_g, conv_ln_b, q_norm_g, k_norm_g, w_out_even, ffn_norm_even, ffn_w_gate, ffn_w_up, ffn_w_down, mix_norm_odd, w_in_odd, lru_conv_w, lru_conv_b, gate_r_w, gate_r_b, gate_i_w, gate_i_b, lru_lambda, w_out_odd, ffn_norm_odd, router_w, moe_w_gate, moe_w_up, moe_w_down):
    bsz, seq, d = x.shape
    t_real = N_META + seq
    tp = -(-t_real // TIME_TILE) * TIME_TILE
    n = bsz * tp
    assert n % ROW_TILE == 0
    depth = mix_norm_even.shape[0] + mix_norm_odd.shape[0]

    meta = jnp.broadcast_to(meta_tokens[None].astype(x.dtype), (bsz, N_META, d))
    h = jnp.concatenate([meta, x, jnp.zeros((bsz, tp - t_real, d), x.dtype)], axis=1).reshape(n, d)

    head_mean = jnp.kron(jnp.eye(MXU_DIM // SB_HEAD_DIM, dtype=F32),
                         jnp.full((SB_HEAD_DIM, SB_HEAD_DIM), 1.0 / SB_HEAD_DIM, F32)).astype(BF16)
    kk = jnp.arange(ATT_T)
    tri_ones = jnp.concatenate([(kk[:, None] >= kk[None, :]).astype(BF16),
                                jnp.ones((ATT_T, ATT_T), BF16)], axis=1)
    row2 = lambda a: a.reshape(1, -1)

    for layer in range(depth):
        p = layer // 2
        if layer % 2 == 0:
            u, q, k, v = _even_in_proj(h, row2(mix_norm_even[p]), w_in_even[p].astype(BF16),
                                       row2(jnp.tile(q_norm_g[p], SB_HEADS)),
                                       row2(jnp.tile(k_norm_g[p], SB_HEADS)), head_mean)
            u = _conv_module(u.reshape(bsz, tp, CONV_CH), conv_w[p], row2(conv_b[p]),
                             row2(conv_ln_g[p]), row2(conv_ln_b[p]))
            o = _attention(q.reshape(bsz, tp, SB_WIDTH), k.reshape(bsz, tp, SB_WIDTH),
                           v.reshape(bsz, tp, SB_WIDTH), tri_ones)
            h, xn = _even_out_proj(h, u.reshape(n, CONV_CH), o.reshape(n, SB_WIDTH),
                                   w_out_even[p].astype(BF16), row2(ffn_norm_even[p]))
            tf = _ffn_chunk(ffn_w_gate.shape[-1])
            h = _ffn(xn, h, None, _interleave_gate_up(ffn_w_gate[p][None], ffn_w_up[p][None], tf),
                     ffn_w_down[p][None].astype(BF16), tf)
        else:
            lw = lru_lambda.shape[-1]
            windows = _gate_windows(lw, lw // LRU_BLOCKS)
            gate, xb = _odd_in_proj(h, row2(mix_norm_odd[p]), w_in_odd[p].astype(BF16))
            h = _lru_mixer(xb.reshape(bsz, tp, lw), gate.reshape(bsz, tp, lw), h.reshape(bsz, tp, d),
                           lru_conv_w[p], row2(lru_conv_b[p]),
                           _pack_lru_gates(gate_r_w[p], gate_i_w[p], windows),
                           row2(gate_r_b[p]), row2(gate_i_b[p]), row2(lru_lambda[p]),
                           w_out_odd[p].astype(BF16), windows).reshape(n, d)
            rw = jnp.pad(router_w[p], ((0, 0), (0, LANES - N_EXPERTS)))
            xn, gates = _router(h, row2(ffn_norm_odd[p]), rw)
            tf = _ffn_chunk(moe_w_gate.shape[-1])
            h = _ffn(xn, h, gates, _interleave_gate_up(moe_w_gate[p], moe_w_up[p], tf),
                     moe_w_down[p].astype(BF16), tf)
    return h.reshape(bsz, tp, d)[:, N_META:t_real]
```

```python
import functools
import math

import jax
import jax.numpy as jnp
from jax import lax
from jax.experimental import pallas as pl
from jax.experimental.pallas import tpu as pltpu

F32 = jnp.float32
BF16 = jnp.bfloat16

EPS = 1e-6
N_META = 16
CONV_CH = 512
CONV_WIDTH = 31
SB_HEADS = 8
SB_HEAD_DIM = 64
SB_WIDTH = SB_HEADS * SB_HEAD_DIM
LRU_BLOCKS = 16
LRU_CONV_WIDTH = 4
LRU_C = 8.0
N_EXPERTS = 8

LANES = 128
MXU_DIM = 256
TIME_TILE = 256
VMEM_LIMIT = 50 * 1024 * 1024

ROW_TILE = 512
CONV_TT = 128
CONV_HALO = 32
CONV_CHUNK = 32
ATT_TQ = 128
ATT_TK = 256
LOG2E = 1.4426950408889634
MASKED_EXPONENT = -1e30
LRU_TT = 256
LRU_GATE_TILE = 256


def _dot(a, b):
    return jnp.dot(a, b, preferred_element_type=F32)


def _params(*sem):
    return pltpu.CompilerParams(dimension_semantics=sem, vmem_limit_bytes=VMEM_LIMIT)


def _rms(x, g):
    ms = jnp.mean(x * x, axis=-1, keepdims=True)
    return x * lax.rsqrt(ms + EPS) * g


def _even_in_kernel(h_ref, g_ref, w_ref, qg_ref, kg_ref, hm_ref, u_ref, q_ref, k_ref, v_ref):
    xn = _rms(h_ref[...], g_ref[...]).astype(BF16)
    a = _dot(xn, w_ref[:, 0:CONV_CH])
    gate = _dot(xn, w_ref[:, CONV_CH:2 * CONV_CH])
    u_ref[...] = (a * jax.nn.sigmoid(gate)).astype(BF16)

    def head_norm(y, gain):
        yy = (y * y).astype(BF16)
        parts = [_dot(yy[:, c:c + MXU_DIM], hm_ref[...]) for c in range(0, SB_WIDTH, MXU_DIM)]
        ms = jnp.concatenate(parts, axis=1)
        return y * lax.rsqrt(ms + EPS) * gain

    c0 = 2 * CONV_CH
    q = head_norm(_dot(xn, w_ref[:, c0:c0 + SB_WIDTH]), qg_ref[...])
    q_ref[...] = (q * (LOG2E / math.sqrt(SB_HEAD_DIM))).astype(BF16)
    k = head_norm(_dot(xn, w_ref[:, c0 + SB_WIDTH:c0 + 2 * SB_WIDTH]), kg_ref[...])
    k_ref[...] = k.astype(BF16)
    v_ref[...] = _dot(xn, w_ref[:, c0 + 2 * SB_WIDTH:c0 + 3 * SB_WIDTH]).astype(BF16)


def _even_in_proj(h, g, w, qg, kg, hm):
    n, d = h.shape
    tm = ROW_TILE
    row = lambda i: (i, 0)
    const = lambda i: (0, 0)
    out = jax.ShapeDtypeStruct((n, SB_WIDTH), BF16)
    return pl.pallas_call(
        _even_in_kernel,
        grid=(n // tm,),
        in_specs=[pl.BlockSpec((tm, d), row), pl.BlockSpec((1, d), const),
                  pl.BlockSpec(w.shape, const), pl.BlockSpec((1, SB_WIDTH), const),
                  pl.BlockSpec((1, SB_WIDTH), const), pl.BlockSpec(hm.shape, const)],
        out_specs=[pl.BlockSpec((tm, SB_WIDTH), row)] * 4,
        out_shape=[out] * 4,
        compiler_params=_params("parallel"),
        name="even_in_proj",
    )(h, g, w, qg, kg, hm)


def _conv_kernel(cur_ref, halo_ref, w_ref, b_ref, lg_ref, lb_ref, o_ref, buf_ref):
    i = pl.program_id(1)
    tt = cur_ref.shape[1]
    halo = halo_ref[0].astype(F32)
    buf_ref[0:CONV_HALO, :] = jnp.where(i > 0, halo, 0.0)
    buf_ref[CONV_HALO:CONV_HALO + tt, :] = cur_ref[0].astype(F32)
    off = CONV_HALO - (CONV_WIDTH - 1)
    for c in range(tt // CONV_CHUNK):
        r0 = c * CONV_CHUNK
        acc = jnp.broadcast_to(b_ref[...], (CONV_CHUNK, CONV_CH))
        for k in range(CONV_WIDTH):
            acc = acc + w_ref[k:k + 1, :] * buf_ref[r0 + off + k:r0 + off + k + CONV_CHUNK, :]
        mu = jnp.mean(acc, axis=-1, keepdims=True)
        xc = acc - mu
        var = jnp.mean(xc * xc, axis=-1, keepdims=True)
        y = xc * lax.rsqrt(var + EPS) * lg_ref[...] + lb_ref[...]
        o_ref[0, r0:r0 + CONV_CHUNK, :] = (y * jax.nn.sigmoid(y)).astype(BF16)


def _conv_module(u, w, b, lg, lb):
    bsz, tp, c = u.shape
    tt = CONV_TT
    per = tt // CONV_HALO
    const = lambda bi, i: (0, 0)
    return pl.pallas_call(
        _conv_kernel,
        grid=(bsz, tp // tt),
        in_specs=[pl.BlockSpec((1, tt, c), lambda bi, i: (bi, i, 0)),
                  pl.BlockSpec((1, CONV_HALO, c), lambda bi, i: (bi, jnp.maximum(i * per - 1, 0), 0)),
                  pl.BlockSpec(w.shape, const), pl.BlockSpec((1, c), const),
                  pl.BlockSpec((1, c), const), pl.BlockSpec((1, c), const)],
        out_specs=pl.BlockSpec((1, tt, c), lambda bi, i: (bi, i, 0)),
        out_shape=jax.ShapeDtypeStruct(u.shape, BF16),
        scratch_shapes=[pltpu.VMEM((CONV_HALO + tt, c), F32)],
        compiler_params=_params("parallel", "parallel"),
        name="conv_module",
    )(u, u, w, b, lg, lb)


def _attn_kernel(q_ref, k_ref, v_ref, tri_ref, o_ref, q2_ref, e_ref, tot_ref, acc_ref, car_ref):
    i = pl.program_id(1)
    tq, tk = ATT_TQ, ATT_TK
    npairs = q_ref.shape[2] // LANES
    lane_q = lax.broadcasted_iota(jnp.int32, (tq, LANES), 1)

    for hp in range(npairs):
        q = q_ref[0, :, hp * LANES:(hp + 1) * LANES]
        zero = jnp.zeros_like(q)
        q2_ref[hp, 0:tq, :] = jnp.where(lane_q < SB_HEAD_DIM, q, zero)
        q2_ref[hp, tq:2 * tq, :] = jnp.where(lane_q >= SB_HEAD_DIM, q, zero)
        car_ref[hp] = jnp.zeros((2 * tq, LANES), F32)
        acc_ref[hp] = jnp.zeros((2 * tq, LANES), F32)

    def stage_a(j, mask):
        k0 = pl.multiple_of(j * tk, tk)
        for hp in range(npairs):
            kt = k_ref[0, pl.ds(k0, tk), hp * LANES:(hp + 1) * LANES]
            z = lax.dot_general(q2_ref[hp], kt, (((1,), (1,)), ((), ())), preferred_element_type=F32)
            sp = jnp.maximum(z, 0.0) + jnp.log(1.0 + jnp.exp2(-jnp.abs(z))) * LOG2E
            if mask is not None:
                sp = jnp.where(mask, sp, 0.0)
            e = z + _dot(sp.astype(BF16), tri_ref[...])
            if mask is not None:
                e = jnp.where(mask, e, MASKED_EXPONENT)
            e_ref[hp] = e
            tot_ref[hp] = jnp.broadcast_to(jnp.sum(sp, axis=1, keepdims=True), (2 * tq, LANES))

    def stage_b(j):
        k0 = pl.multiple_of(j * tk, tk)
        for hp in range(npairs):
            car = car_ref[hp]
            w = jnp.exp2(e_ref[hp] + jnp.concatenate([car] * (tk // LANES), axis=1)).astype(BF16)
            vt = v_ref[0, pl.ds(k0, tk), hp * LANES:(hp + 1) * LANES]
            acc_ref[hp] += _dot(w, vt)
            car_ref[hp] = car - tot_ref[hp]

    j_last = (i * tq) // tk
    row = lax.broadcasted_iota(jnp.int32, (2 * tq, tk), 0)
    col = lax.broadcasted_iota(jnp.int32, (2 * tq, tk), 1)
    row = jnp.where(row >= tq, row - tq, row) + i * tq
    stage_a(j_last, (col + j_last * tk) < row)

    def body(jj, carry):
        j = j_last - 1 - jj
        stage_b(j + 1)
        stage_a(j, None)
        return carry

    lax.fori_loop(0, j_last, body, 0)
    stage_b(0)
    for hp in range(npairs):
        o_ref[0, :, hp * LANES:(hp + 1) * LANES] = jnp.where(
            lane_q < SB_HEAD_DIM, acc_ref[hp, 0:tq, :], acc_ref[hp, tq:2 * tq, :]).astype(BF16)


def _attention(q, k, v, tri):
    bsz, tp, width = q.shape
    tq, tk = ATT_TQ, ATT_TK
    npairs = width // LANES
    return pl.pallas_call(
        _attn_kernel,
        grid=(bsz, tp // tq),
        in_specs=[pl.BlockSpec((1, tq, width), lambda b, i: (b, i, 0)),
                  pl.BlockSpec((1, tp, width), lambda b, i: (b, 0, 0)),
                  pl.BlockSpec((1, tp, width), lambda b, i: (b, 0, 0)),
                  pl.BlockSpec(tri.shape, lambda b, i: (0, 0))],
        out_specs=pl.BlockSpec((1, tq, width), lambda b, i: (b, i, 0)),
        out_shape=jax.ShapeDtypeStruct(q.shape, BF16),
        scratch_shapes=[pltpu.VMEM((npairs, 2 * tq, LANES), BF16),
                        pltpu.VMEM((npairs, 2 * tq, tk), F32),
                        pltpu.VMEM((npairs, 2 * tq, LANES), F32),
                        pltpu.VMEM((npairs, 2 * tq, LANES), F32),
                        pltpu.VMEM((npairs, 2 * tq, LANES), F32)],
        compiler_params=_params("parallel", "arbitrary"),
        name="stick_breaking",
    )(q, k, v, tri)


def _even_out_kernel(h_ref, u_ref, o_ref, w_ref, g_ref, h1_ref, xn_ref):
    h1 = h_ref[...] + _dot(u_ref[...], w_ref[0:CONV_CH, :]) + _dot(o_ref[...], w_ref[CONV_CH:, :])
    h1_ref[...] = h1
    xn_ref[...] = _rms(h1, g_ref[...]).astype(BF16)


def _even_out_proj(h, u, o, w, g):
    n, d = h.shape
    tm = ROW_TILE
    row = lambda i: (i, 0)
    const = lambda i: (0, 0)
    return pl.pallas_call(
        _even_out_kernel,
        grid=(n // tm,),
        in_specs=[pl.BlockSpec((tm, d), row), pl.BlockSpec((tm, CONV_CH), row),
                  pl.BlockSpec((tm, SB_WIDTH), row), pl.BlockSpec(w.shape, const),
                  pl.BlockSpec((1, d), const)],
        out_specs=[pl.BlockSpec((tm, d), row), pl.BlockSpec((tm, d), row)],
        out_shape=[jax.ShapeDtypeStruct((n, d), F32), jax.ShapeDtypeStruct((n, d), BF16)],
        compiler_params=_params("parallel"),
        name="even_out_proj",
    )(h, u, o, w, g)


def _ffn_kernel(*refs, tf, gated):
    if gated:
        xn_ref, res_ref, gates_ref, wgu_ref, wd_ref, o_ref = refs
    else:
        xn_ref, res_ref, wgu_ref, wd_ref, o_ref = refs
    e = pl.program_id(1)
    f = pl.program_id(2)

    @pl.when((e == 0) & (f == 0))
    def _():
        o_ref[...] = res_ref[...]

    gu = _dot(xn_ref[...], wgu_ref[0])
    g = gu[:, 0:tf]
    mid = g * jax.nn.sigmoid(g) * gu[:, tf:2 * tf]
    if gated:
        gates = gates_ref[...]
        lane = lax.broadcasted_iota(jnp.int32, gates.shape, 1)
        mid = mid * jnp.sum(jnp.where(lane == e, gates, 0.0), axis=1, keepdims=True)
    o_ref[...] += _dot(mid.astype(BF16), wd_ref[0])


def _ffn(xn, res, gates, wgu, wd, tf):
    n, d = xn.shape
    n_e, ff = wd.shape[0], wd.shape[1]
    tm = ROW_TILE
    row = lambda i, e, f: (i, 0)
    gated = gates is not None
    in_specs = [pl.BlockSpec((tm, d), row), pl.BlockSpec((tm, d), row)]
    args = [xn, res]
    if gated:
        in_specs.append(pl.BlockSpec((tm, LANES), row))
        args.append(gates)
    in_specs += [pl.BlockSpec((1, d, 2 * tf), lambda i, e, f: (e, 0, f)),
                 pl.BlockSpec((1, tf, d), lambda i, e, f: (e, f, 0))]
    return pl.pallas_call(
        functools.partial(_ffn_kernel, tf=tf, gated=gated),
        grid=(n // tm, n_e, ff // tf),
        in_specs=in_specs,
        out_specs=pl.BlockSpec((tm, d), row),
        out_shape=jax.ShapeDtypeStruct((n, d), F32),
        compiler_params=_params("parallel", "arbitrary", "arbitrary"),
        name="swiglu_gated" if gated else "swiglu",
    )(*args, wgu, wd)


def _odd_in_kernel(h_ref, g_ref, w_ref, gate_ref, xb_ref):
    xn = _rms(h_ref[...], g_ref[...]).astype(BF16)
    y = _dot(xn, w_ref[...])
    lw = gate_ref.shape[1]
    gate_ref[...] = jax.nn.gelu(y[:, 0:lw], approximate=True).astype(BF16)
    xb_ref[...] = y[:, lw:2 * lw].astype(BF16)


def _odd_in_proj(h, g, w):
    n, d = h.shape
    lw = w.shape[1] // 2
    tm = ROW_TILE
    row = lambda i: (i, 0)
    const = lambda i: (0, 0)
    out = jax.ShapeDtypeStruct((n, lw), BF16)
    return pl.pallas_call(
        _odd_in_kernel,
        grid=(n // tm,),
        in_specs=[pl.BlockSpec((tm, d), row), pl.BlockSpec((1, d), const), pl.BlockSpec(w.shape, const)],
        out_specs=[pl.BlockSpec((tm, lw), row)] * 2,
        out_shape=[out] * 2,
        compiler_params=_params("parallel"),
        name="odd_in_proj",
    )(h, g, w)


def _gate_windows(lw, bd):
    out = []
    for c0 in range(0, lw, LRU_GATE_TILE):
        nc = min(LRU_GATE_TILE, lw - c0)
        r0 = (c0 // bd) * bd
        r1 = ((c0 + nc - 1) // bd + 1) * bd
        r0 = (r0 // LANES) * LANES
        r1 = min(-(-r1 // LANES) * LANES, lw)
        out.append((c0, nc, r0, r1 - r0))
    return out


def _lru_kernel(xb_ref, gate_ref, h_ref, cw_ref, cb_ref, wg_ref, br_ref, bi_ref, lam_ref, wo_ref,
                o_ref, xbuf_ref, hs_ref, a_ref, b_ref, y_ref, *, windows):
    ti = pl.program_id(1)
    tt, lw = a_ref.shape

    @pl.when(ti == 0)
    def _():
        xbuf_ref[0:8, :] = jnp.zeros((8, lw), F32)
        hs_ref[...] = jnp.zeros((8, lw), F32)

    @pl.when(ti > 0)
    def _():
        xbuf_ref[0:8, :] = xbuf_ref[tt:tt + 8, :]

    xbuf_ref[8:8 + tt, :] = xb_ref[0].astype(F32)
    off = 8 - (LRU_CONV_WIDTH - 1)
    xc = jnp.broadcast_to(cb_ref[...], (tt, lw))
    for k in range(LRU_CONV_WIDTH):
        xc = xc + cw_ref[k:k + 1, :] * xbuf_ref[off + k:off + k + tt, :]
    xcb = xc.astype(BF16)
    sp_lam = jnp.log(1.0 + jnp.exp(-lam_ref[...]))
    wcol = 0
    for (c0, nc, r0, nr) in windows:
        rg = _dot(xcb[:, r0:r0 + nr], wg_ref[r0:r0 + nr, wcol:wcol + 2 * nc])
        wcol += 2 * nc
        r = jax.nn.sigmoid(rg[:, 0:nc] + br_ref[:, c0:c0 + nc])
        ig = jax.nn.sigmoid(rg[:, nc:2 * nc] + bi_ref[:, c0:c0 + nc])
        a = jnp.exp(-LRU_C * r * sp_lam[:, c0:c0 + nc])
        a_ref[:, c0:c0 + nc] = a
        b_ref[:, c0:c0 + nc] = jnp.sqrt(1.0 - a * a) * (ig * xc[:, c0:c0 + nc])

    rowi = lax.broadcasted_iota(jnp.int32, (8, lw), 0)

    def group(gi, hprev):
        r0 = pl.multiple_of(gi * 8, 8)
        a = a_ref[pl.ds(r0, 8), :]
        b = b_ref[pl.ds(r0, 8), :]
        for s in (1, 2, 4):
            m = rowi >= s
            a_sh = jnp.where(m, pltpu.roll(a, s, axis=0), 1.0)
            b_sh = jnp.where(m, pltpu.roll(b, s, axis=0), 0.0)
            b = a * b_sh + b
            a = a * a_sh
        hrows = a * hprev + b
        y_ref[pl.ds(r0, 8), :] = hrows
        return jnp.broadcast_to(hrows[7:8, :], (8, lw))

    hs_ref[...] = lax.fori_loop(0, tt // 8, group, hs_ref[...])
    gy = (gate_ref[0].astype(F32) * y_ref[...]).astype(BF16)
    o_ref[0] = h_ref[0] + _dot(gy, wo_ref[...])


def _lru_mixer(xb, gate, h, cw, cb, wg, br, bi, lam, wo, windows):
    bsz, tp, lw = xb.shape
    d = h.shape[-1]
    tt = LRU_TT
    const = lambda b, t: (0, 0)
    blk = lambda b, t: (b, t, 0)
    return pl.pallas_call(
        functools.partial(_lru_kernel, windows=windows),
        grid=(bsz, tp // tt),
        in_specs=[pl.BlockSpec((1, tt, lw), blk), pl.BlockSpec((1, tt, lw), blk),
                  pl.BlockSpec((1, tt, d), blk), pl.BlockSpec(cw.shape, const),
                  pl.BlockSpec((1, lw), const), pl.BlockSpec(wg.shape, const),
                  pl.BlockSpec((1, lw), const), pl.BlockSpec((1, lw), const),
                  pl.BlockSpec((1, lw), const), pl.BlockSpec(wo.shape, const)],
        out_specs=pl.BlockSpec((1, tt, d), blk),
        out_shape=jax.ShapeDtypeStruct(h.shape, F32),
        scratch_shapes=[pltpu.VMEM((8 + tt, lw), F32), pltpu.VMEM((8, lw), F32),
                        pltpu.VMEM((tt, lw), F32), pltpu.VMEM((tt, lw), F32), pltpu.VMEM((tt, lw), F32)],
        compiler_params=_params("arbitrary", "arbitrary"),
        name="rglru_mixer",
    )(xb, gate, h, cw, cb, wg, br, bi, lam, wo)


def _router_kernel(h_ref, g_ref, rw_ref, xn_ref, gates_ref):
    xn = _rms(h_ref[...], g_ref[...])
    xn_ref[...] = xn.astype(BF16)
    logits = jnp.dot(xn, rw_ref[...], preferred_element_type=F32, precision=lax.Precision.HIGHEST)
    lane = lax.broadcasted_iota(jnp.int32, logits.shape, 1)
    neg = jnp.float32(-jnp.inf)
    logits = jnp.where(lane < N_EXPERTS, logits, neg)
    top1 = jnp.max(logits, axis=1, keepdims=True)
    idx1 = jnp.min(jnp.where(logits == top1, lane, LANES), axis=1, keepdims=True)
    rest = jnp.where(lane == idx1, neg, logits)
    top2 = jnp.max(rest, axis=1, keepdims=True)
    idx2 = jnp.min(jnp.where(rest == top2, lane, LANES), axis=1, keepdims=True)
    w1 = 1.0 / (1.0 + jnp.exp(top2 - top1))
    gates_ref[...] = jnp.where(lane == idx1, w1, 0.0) + jnp.where(lane == idx2, 1.0 - w1, 0.0)


def _router(h, g, rw):
    n, d = h.shape
    tm = ROW_TILE
    row = lambda i: (i, 0)
    const = lambda i: (0, 0)
    return pl.pallas_call(
        _router_kernel,
        grid=(n // tm,),
        in_specs=[pl.BlockSpec((tm, d), row), pl.BlockSpec((1, d), const), pl.BlockSpec(rw.shape, const)],
        out_specs=[pl.BlockSpec((tm, d), row), pl.BlockSpec((tm, LANES), row)],
        out_shape=[jax.ShapeDtypeStruct((n, d), BF16), jax.ShapeDtypeStruct((n, LANES), F32)],
        compiler_params=_params("parallel"),
        name="router",
    )(h, g, rw)


def _interleave_gate_up(wg, wu, tf):
    e, d, ff = wg.shape
    g = wg.astype(BF16).reshape(e, d, ff // tf, tf)
    u = wu.astype(BF16).reshape(e, d, ff // tf, tf)
    return jnp.concatenate([g, u], axis=3).reshape(e, d, 2 * ff)


def _dense_block_diag(w):
    nb, bd, _ = w.shape
    eye = jnp.eye(nb, dtype=w.dtype)
    return (w[:, :, None, :] * eye[:, None, :, None]).reshape(nb * bd, nb * bd)


def _pack_lru_gates(wr, wi, windows):
    dr = _dense_block_diag(wr).astype(BF16)
    di = _dense_block_diag(wi).astype(BF16)
    cols = []
    for (c0, nc, _, _) in windows:
        cols += [dr[:, c0:c0 + nc], di[:, c0:c0 + nc]]
    return jnp.concatenate(cols, axis=1)


def _ffn_chunk(ff):
    for tf in (896, 512, 384, 256, 128):
        if ff % tf == 0:
            return tf
    raise ValueError(f"unsupported d_ff {ff}")


def kernel(x, meta_tokens, mix_norm_even, w_in_even, conv_w, conv_b, conv_ln_g, conv_ln_b, q_norm_g, k_norm_g, w_out_even, ffn_norm_even, ffn_w_gate, ffn_w_up, ffn_w_down, mix_norm_odd, w_in_odd, lru_conv_w, lru_conv_b, gate_r_w, gate_r_b, gate_i_w, gate_i_b, lru_lambda, w_out_odd, ffn_norm_odd, router_w, moe_w_gate, moe_w_up, moe_w_down):
    bsz, seq, d = x.shape
    t_real = N_META + seq
    tp = -(-t_real // TIME_TILE) * TIME_TILE
    n = bsz * tp
    assert n % ROW_TILE == 0
    depth = mix_norm_even.shape[0] + mix_norm_odd.shape[0]

    meta = jnp.broadcast_to(meta_tokens[None].astype(x.dtype), (bsz, N_META, d))
    h = jnp.concatenate([meta, x, jnp.zeros((bsz, tp - t_real, d), x.dtype)], axis=1).reshape(n, d)

    head_mean = jnp.kron(jnp.eye(MXU_DIM // SB_HEAD_DIM, dtype=F32),
                         jnp.full((SB_HEAD_DIM, SB_HEAD_DIM), 1.0 / SB_HEAD_DIM, F32)).astype(BF16)
    kk = jnp.arange(ATT_TK)
    neg_tri = -(kk[:, None] >= kk[None, :]).astype(BF16)
    row2 = lambda a: a.reshape(1, -1)

    for layer in range(depth):
        p = layer // 2
        if layer % 2 == 0:
            u, q, k, v = _even_in_proj(h, row2(mix_norm_even[p]), w_in_even[p].astype(BF16),
                                       row2(jnp.tile(q_norm_g[p], SB_HEADS)),
                                       row2(jnp.tile(k_norm_g[p], SB_HEADS)), head_mean)
            u = _conv_module(u.reshape(bsz, tp, CONV_CH), conv_w[p], row2(conv_b[p]),
                             row2(conv_ln_g[p]), row2(conv_ln_b[p]))
            o = _attention(q.reshape(bsz, tp, SB_WIDTH), k.reshape(bsz, tp, SB_WIDTH),
                           v.reshape(bsz, tp, SB_WIDTH), neg_tri)
            h, xn = _even_out_proj(h, u.reshape(n, CONV_CH), o.reshape(n, SB_WIDTH),
                                   w_out_even[p].astype(BF16), row2(ffn_norm_even[p]))
            tf = _ffn_chunk(ffn_w_gate.shape[-1])
            h = _ffn(xn, h, None, _interleave_gate_up(ffn_w_gate[p][None], ffn_w_up[p][None], tf),
                     ffn_w_down[p][None].astype(BF16), tf)
        else:
            lw = lru_lambda.shape[-1]
            windows = _gate_windows(lw, lw // LRU_BLOCKS)
            gate, xb = _odd_in_proj(h, row2(mix_norm_odd[p]), w_in_odd[p].astype(BF16))
            h = _lru_mixer(xb.reshape(bsz, tp, lw), gate.reshape(bsz, tp, lw), h.reshape(bsz, tp, d),
                           lru_conv_w[p], row2(lru_conv_b[p]),
                           _pack_lru_gates(gate_r_w[p], gate_i_w[p], windows),
                           row2(gate_r_b[p]), row2(gate_i_b[p]), row2(lru_lambda[p]),
                           w_out_odd[p].astype(BF16), windows).reshape(n, d)
            rw = jnp.pad(router_w[p], ((0, 0), (0, LANES - N_EXPERTS)))
            xn, gates = _router(h, row2(ffn_norm_odd[p]), rw)
            tf = _ffn_chunk(moe_w_gate.shape[-1])
            h = _ffn(xn, h, gates, _interleave_gate_up(moe_w_gate[p], moe_w_up[p], tf),
                     moe_w_down[p].astype(BF16), tf)
    return h.reshape(bsz, tp, d)[:, N_META:t_real]
```

```python
import functools
import math

import jax
import jax.numpy as jnp
from jax import lax
from jax.experimental import pallas as pl
from jax.experimental.pallas import tpu as pltpu

F32 = jnp.float32
BF16 = jnp.bfloat16

EPS = 1e-6
N_META = 16
CONV_CH = 512
CONV_WIDTH = 31
SB_HEADS = 8
SB_HEAD_DIM = 64
SB_WIDTH = SB_HEADS * SB_HEAD_DIM
LRU_BLOCKS = 16
LRU_CONV_WIDTH = 4
LRU_C = 8.0
N_EXPERTS = 8

LANES = 128
MXU_DIM = 256
TIME_TILE = 256
VMEM_LIMIT = 50 * 1024 * 1024

ROW_TILE = 512
CONV_TT = 128
CONV_HALO = 32
CONV_CHUNK = 32
ATT_TQ = 128
ATT_TK = 256
LOG2E = 1.4426950408889634
MASKED_EXPONENT = -1e30
LRU_TT = 256
LRU_GATE_TILE = 256
MOE_TILE = 512
COMBINE_TILE = 512
SUBLANES = 8


def _dot(a, b):
    return jnp.dot(a, b, preferred_element_type=F32)


def _params(*sem):
    return pltpu.CompilerParams(dimension_semantics=sem, vmem_limit_bytes=VMEM_LIMIT)


def _rms(x, g):
    ms = jnp.mean(x * x, axis=-1, keepdims=True)
    return x * lax.rsqrt(ms + EPS) * g


def _even_in_kernel(h_ref, g_ref, w_ref, qg_ref, kg_ref, hm_ref, u_ref, q_ref, k_ref, v_ref):
    xn = _rms(h_ref[...], g_ref[...]).astype(BF16)
    a = _dot(xn, w_ref[:, 0:CONV_CH])
    gate = _dot(xn, w_ref[:, CONV_CH:2 * CONV_CH])
    u_ref[...] = (a * jax.nn.sigmoid(gate)).astype(BF16)

    def head_norm(y, gain):
        yy = (y * y).astype(BF16)
        parts = [_dot(yy[:, c:c + MXU_DIM], hm_ref[...]) for c in range(0, SB_WIDTH, MXU_DIM)]
        ms = jnp.concatenate(parts, axis=1)
        return y * lax.rsqrt(ms + EPS) * gain

    c0 = 2 * CONV_CH
    q = head_norm(_dot(xn, w_ref[:, c0:c0 + SB_WIDTH]), qg_ref[...])
    q_ref[...] = (q * (LOG2E / math.sqrt(SB_HEAD_DIM))).astype(BF16)
    k = head_norm(_dot(xn, w_ref[:, c0 + SB_WIDTH:c0 + 2 * SB_WIDTH]), kg_ref[...])
    k_ref[...] = k.astype(BF16)
    v_ref[...] = _dot(xn, w_ref[:, c0 + 2 * SB_WIDTH:c0 + 3 * SB_WIDTH]).astype(BF16)


def _even_in_proj(h, g, w, qg, kg, hm):
    n, d = h.shape
    tm = ROW_TILE
    row = lambda i: (i, 0)
    const = lambda i: (0, 0)
    out = jax.ShapeDtypeStruct((n, SB_WIDTH), BF16)
    return pl.pallas_call(
        _even_in_kernel,
        grid=(n // tm,),
        in_specs=[pl.BlockSpec((tm, d), row), pl.BlockSpec((1, d), const),
                  pl.BlockSpec(w.shape, const), pl.BlockSpec((1, SB_WIDTH), const),
                  pl.BlockSpec((1, SB_WIDTH), const), pl.BlockSpec(hm.shape, const)],
        out_specs=[pl.BlockSpec((tm, SB_WIDTH), row)] * 4,
        out_shape=[out] * 4,
        compiler_params=_params("parallel"),
        name="even_in_proj",
    )(h, g, w, qg, kg, hm)


def _conv_kernel(cur_ref, halo_ref, w_ref, b_ref, lg_ref, lb_ref, o_ref, buf_ref):
    i = pl.program_id(1)
    tt = cur_ref.shape[1]
    halo = halo_ref[0].astype(F32)
    buf_ref[0:CONV_HALO, :] = jnp.where(i > 0, halo, 0.0)
    buf_ref[CONV_HALO:CONV_HALO + tt, :] = cur_ref[0].astype(F32)
    off = CONV_HALO - (CONV_WIDTH - 1)
    for c in range(tt // CONV_CHUNK):
        r0 = c * CONV_CHUNK
        acc = jnp.broadcast_to(b_ref[...], (CONV_CHUNK, CONV_CH))
        for k in range(CONV_WIDTH):
            acc = acc + w_ref[k:k + 1, :] * buf_ref[r0 + off + k:r0 + off + k + CONV_CHUNK, :]
        mu = jnp.mean(acc, axis=-1, keepdims=True)
        xc = acc - mu
        var = jnp.mean(xc * xc, axis=-1, keepdims=True)
        y = xc * lax.rsqrt(var + EPS) * lg_ref[...] + lb_ref[...]
        o_ref[0, r0:r0 + CONV_CHUNK, :] = (y * jax.nn.sigmoid(y)).astype(BF16)


def _conv_module(u, w, b, lg, lb):
    bsz, tp, c = u.shape
    tt = CONV_TT
    per = tt // CONV_HALO
    const = lambda bi, i: (0, 0)
    return pl.pallas_call(
        _conv_kernel,
        grid=(bsz, tp // tt),
        in_specs=[pl.BlockSpec((1, tt, c), lambda bi, i: (bi, i, 0)),
                  pl.BlockSpec((1, CONV_HALO, c), lambda bi, i: (bi, jnp.maximum(i * per - 1, 0), 0)),
                  pl.BlockSpec(w.shape, const), pl.BlockSpec((1, c), const),
                  pl.BlockSpec((1, c), const), pl.BlockSpec((1, c), const)],
        out_specs=pl.BlockSpec((1, tt, c), lambda bi, i: (bi, i, 0)),
        out_shape=jax.ShapeDtypeStruct(u.shape, BF16),
        scratch_shapes=[pltpu.VMEM((CONV_HALO + tt, c), F32)],
        compiler_params=_params("parallel", "parallel"),
        name="conv_module",
    )(u, u, w, b, lg, lb)


def _attn_kernel(q_ref, k_ref, v_ref, tri_ref, o_ref, q2_ref, e_ref, tot_ref, acc_ref, car_ref):
    i = pl.program_id(1)
    tq, tk = ATT_TQ, ATT_TK
    npairs = q_ref.shape[2] // LANES
    lane_q = lax.broadcasted_iota(jnp.int32, (tq, LANES), 1)

    for hp in range(npairs):
        q = q_ref[0, :, hp * LANES:(hp + 1) * LANES]
        zero = jnp.zeros_like(q)
        q2_ref[hp, 0:tq, :] = jnp.where(lane_q < SB_HEAD_DIM, q, zero)
        q2_ref[hp, tq:2 * tq, :] = jnp.where(lane_q >= SB_HEAD_DIM, q, zero)
        car_ref[hp] = jnp.zeros((2 * tq, LANES), F32)
        acc_ref[hp] = jnp.zeros((2 * tq, LANES), F32)

    def stage_a(j, mask):
        k0 = pl.multiple_of(j * tk, tk)
        for hp in range(npairs):
            kt = k_ref[0, pl.ds(k0, tk), hp * LANES:(hp + 1) * LANES]
            z = lax.dot_general(q2_ref[hp], kt, (((1,), (1,)), ((), ())), preferred_element_type=F32)
            sp = jnp.maximum(z, 0.0) + jnp.log(1.0 + jnp.exp2(-jnp.abs(z))) * LOG2E
            if mask is not None:
                sp = jnp.where(mask, sp, 0.0)
            e = z + _dot(sp.astype(BF16), tri_ref[...])
            if mask is not None:
                e = jnp.where(mask, e, MASKED_EXPONENT)
            e_ref[hp] = e
            tot_ref[hp] = jnp.broadcast_to(jnp.sum(sp, axis=1, keepdims=True), (2 * tq, LANES))

    def stage_b(j):
        k0 = pl.multiple_of(j * tk, tk)
        for hp in range(npairs):
            car = car_ref[hp]
            w = jnp.exp2(e_ref[hp] + jnp.concatenate([car] * (tk // LANES), axis=1)).astype(BF16)
            vt = v_ref[0, pl.ds(k0, tk), hp * LANES:(hp + 1) * LANES]
            acc_ref[hp] += _dot(w, vt)
            car_ref[hp] = car - tot_ref[hp]

    j_last = (i * tq) // tk
    row = lax.broadcasted_iota(jnp.int32, (2 * tq, tk), 0)
    col = lax.broadcasted_iota(jnp.int32, (2 * tq, tk), 1)
    row = jnp.where(row >= tq, row - tq, row) + i * tq
    stage_a(j_last, (col + j_last * tk) < row)

    def body(jj, carry):
        j = j_last - 1 - jj
        stage_b(j + 1)
        stage_a(j, None)
        return carry

    lax.fori_loop(0, j_last, body, 0)
    stage_b(0)
    for hp in range(npairs):
        o_ref[0, :, hp * LANES:(hp + 1) * LANES] = jnp.where(
            lane_q < SB_HEAD_DIM, acc_ref[hp, 0:tq, :], acc_ref[hp, tq:2 * tq, :]).astype(BF16)


def _attention(q, k, v, tri):
    bsz, tp, width = q.shape
    tq, tk = ATT_TQ, ATT_TK
    npairs = width // LANES
    return pl.pallas_call(
        _attn_kernel,
        grid=(bsz, tp // tq),
        in_specs=[pl.BlockSpec((1, tq, width), lambda b, i: (b, i, 0)),
                  pl.BlockSpec((1, tp, width), lambda b, i: (b, 0, 0)),
                  pl.BlockSpec((1, tp, width), lambda b, i: (b, 0, 0)),
                  pl.BlockSpec(tri.shape, lambda b, i: (0, 0))],
        out_specs=pl.BlockSpec((1, tq, width), lambda b, i: (b, i, 0)),
        out_shape=jax.ShapeDtypeStruct(q.shape, BF16),
        scratch_shapes=[pltpu.VMEM((npairs, 2 * tq, LANES), BF16),
                        pltpu.VMEM((npairs, 2 * tq, tk), F32),
                        pltpu.VMEM((npairs, 2 * tq, LANES), F32),
                        pltpu.VMEM((npairs, 2 * tq, LANES), F32),
                        pltpu.VMEM((npairs, 2 * tq, LANES), F32)],
        compiler_params=_params("parallel", "arbitrary"),
        name="stick_breaking",
    )(q, k, v, tri)


def _even_out_kernel(h_ref, u_ref, o_ref, w_ref, g_ref, h1_ref, xn_ref):
    h1 = h_ref[...] + _dot(u_ref[...], w_ref[0:CONV_CH, :]) + _dot(o_ref[...], w_ref[CONV_CH:, :])
    h1_ref[...] = h1
    xn_ref[...] = _rms(h1, g_ref[...]).astype(BF16)


def _even_out_proj(h, u, o, w, g):
    n, d = h.shape
    tm = ROW_TILE
    row = lambda i: (i, 0)
    const = lambda i: (0, 0)
    return pl.pallas_call(
        _even_out_kernel,
        grid=(n // tm,),
        in_specs=[pl.BlockSpec((tm, d), row), pl.BlockSpec((tm, CONV_CH), row),
                  pl.BlockSpec((tm, SB_WIDTH), row), pl.BlockSpec(w.shape, const),
                  pl.BlockSpec((1, d), const)],
        out_specs=[pl.BlockSpec((tm, d), row), pl.BlockSpec((tm, d), row)],
        out_shape=[jax.ShapeDtypeStruct((n, d), F32), jax.ShapeDtypeStruct((n, d), BF16)],
        compiler_params=_params("parallel"),
        name="even_out_proj",
    )(h, u, o, w, g)


def _ffn_kernel(xn_ref, res_ref, wgu_ref, wd_ref, o_ref, *, tf):
    @pl.when(pl.program_id(1) == 0)
    def _():
        o_ref[...] = res_ref[...]

    gu = _dot(xn_ref[...], wgu_ref[...])
    g = gu[:, 0:tf]
    mid = g * jax.nn.sigmoid(g) * gu[:, tf:2 * tf]
    o_ref[...] += _dot(mid.astype(BF16), wd_ref[...])


def _ffn(xn, res, wgu, wd, tf):
    n, d = xn.shape
    ff = wd.shape[0]
    tm = ROW_TILE
    row = lambda i, f: (i, 0)
    return pl.pallas_call(
        functools.partial(_ffn_kernel, tf=tf),
        grid=(n // tm, ff // tf),
        in_specs=[pl.BlockSpec((tm, d), row), pl.BlockSpec((tm, d), row),
                  pl.BlockSpec((d, 2 * tf), lambda i, f: (0, f)),
                  pl.BlockSpec((tf, d), lambda i, f: (f, 0))],
        out_specs=pl.BlockSpec((tm, d), row),
        out_shape=jax.ShapeDtypeStruct((n, d), F32),
        compiler_params=_params("parallel", "arbitrary"),
        name="swiglu",
    )(xn, res, wgu, wd)


def _row_gather_kernel(src_ref, x_hbm, o_hbm, sems):
    p = pl.program_id(0)
    tg = MOE_TILE
    slot = p % 2

    def tile_copy(tile, sem):
        return pltpu.make_async_copy(x_hbm.at[pl.ds(0, tg)], o_hbm.at[pl.ds(tile * tg, tg)], sem)

    def issue(r, c):
        row = p * tg + r
        pltpu.make_async_copy(x_hbm.at[src_ref[row]], o_hbm.at[row], sems.at[slot]).start()
        return c

    lax.fori_loop(0, tg, issue, 0, unroll=8)

    @pl.when(p > 0)
    def _():
        tile_copy(p - 1, sems.at[1 - slot]).wait()

    @pl.when(p == pl.num_programs(0) - 1)
    def _():
        tile_copy(p, sems.at[slot]).wait()


def _row_gather(src, x3, n_rows):
    return pl.pallas_call(
        _row_gather_kernel,
        grid_spec=pltpu.PrefetchScalarGridSpec(
            num_scalar_prefetch=1, grid=(n_rows // MOE_TILE,),
            in_specs=[pl.BlockSpec(memory_space=pl.ANY)],
            out_specs=pl.BlockSpec(memory_space=pl.ANY),
            scratch_shapes=[pltpu.SemaphoreType.DMA((2,))]),
        out_shape=jax.ShapeDtypeStruct((n_rows,) + x3.shape[1:], x3.dtype),
        compiler_params=_params("arbitrary"),
        name="moe_row_gather",
    )(src, x3)


def _gmm_kernel(te_ref, nu_ref, xs_ref, gs_ref, wgu_ref, wd_ref, o_ref, acc_ref, *, tf):
    p = pl.program_id(0)
    f = pl.program_id(1)
    last = pl.num_programs(1) - 1
    used = p < nu_ref[0]

    @pl.when(used)
    def _():
        gu = _dot(xs_ref[...], wgu_ref[0])
        g = gu[:, 0:tf]
        mid = g * jax.nn.sigmoid(g) * gu[:, tf:2 * tf] * gs_ref[...]
        part = _dot(mid.astype(BF16), wd_ref[0])

        @pl.when(f == 0)
        def _():
            acc_ref[...] = part

        @pl.when(f > 0)
        def _():
            acc_ref[...] += part

        @pl.when(f == last)
        def _():
            o_ref[...] = acc_ref[...].astype(o_ref.dtype)

    @pl.when(jnp.logical_not(used) & (f == last))
    def _():
        o_ref[...] = jnp.zeros(o_ref.shape, o_ref.dtype)


def _grouped_ffn(tile_expert, n_used, xs, gs, wgu, wd, tf):
    n_rows, d = xs.shape
    ff = wd.shape[1]
    nf = ff // tf
    tg = MOE_TILE
    row = lambda p, f, te, nu: (p, 0)
    fidx = lambda p, f, nu: jnp.where(p < nu[0], f, nf - 1)
    return pl.pallas_call(
        functools.partial(_gmm_kernel, tf=tf),
        grid_spec=pltpu.PrefetchScalarGridSpec(
            num_scalar_prefetch=2, grid=(n_rows // tg, nf),
            in_specs=[pl.BlockSpec((tg, d), row), pl.BlockSpec((tg, 1), row),
                      pl.BlockSpec((1, d, 2 * tf), lambda p, f, te, nu: (te[p], 0, fidx(p, f, nu))),
                      pl.BlockSpec((1, tf, d), lambda p, f, te, nu: (te[p], fidx(p, f, nu), 0))],
            out_specs=pl.BlockSpec((tg, d), row),
            scratch_shapes=[pltpu.VMEM((tg, d), F32)]),
        out_shape=jax.ShapeDtypeStruct((n_rows, d), BF16),
        compiler_params=_params("arbitrary", "arbitrary"),
        name="moe_grouped_ffn",
    )(tile_expert, n_used, xs, gs, wgu, wd)


def _combine_kernel(dest_ref, h_ref, ys_hbm, o_ref, ybuf, sem):
    i = pl.program_id(0)
    tm = h_ref.shape[0]
    for k in range(2):
        def issue(r, c, k=k):
            pltpu.make_async_copy(ys_hbm.at[dest_ref[(i * 2 + k) * tm + r]], ybuf.at[k, r], sem).start()
            return c
        lax.fori_loop(0, tm, issue, 0, unroll=8)
    for k in range(2):
        pltpu.make_async_copy(ys_hbm.at[pl.ds(0, tm)], ybuf.at[k], sem).wait()
    o_ref[...] = h_ref[...] + ybuf[0].astype(F32) + ybuf[1].astype(F32)


def _combine(dest, h3, ys3):
    n = h3.shape[0]
    tm = COMBINE_TILE
    blk = lambda i, dest: (i, 0, 0)
    return pl.pallas_call(
        _combine_kernel,
        grid_spec=pltpu.PrefetchScalarGridSpec(
            num_scalar_prefetch=1, grid=(n // tm,),
            in_specs=[pl.BlockSpec((tm,) + h3.shape[1:], blk), pl.BlockSpec(memory_space=pl.ANY)],
            out_specs=pl.BlockSpec((tm,) + h3.shape[1:], blk),
            scratch_shapes=[pltpu.VMEM((2, tm) + ys3.shape[1:], ys3.dtype), pltpu.SemaphoreType.DMA]),
        out_shape=jax.ShapeDtypeStruct(h3.shape, F32),
        compiler_params=_params("arbitrary"),
        name="moe_combine",
    )(dest, h3, ys3)


def _odd_in_kernel(h_ref, g_ref, w_ref, gate_ref, xb_ref):
    xn = _rms(h_ref[...], g_ref[...]).astype(BF16)
    y = _dot(xn, w_ref[...])
    lw = gate_ref.shape[1]
    gate_ref[...] = jax.nn.gelu(y[:, 0:lw], approximate=True).astype(BF16)
    xb_ref[...] = y[:, lw:2 * lw].astype(BF16)


def _odd_in_proj(h, g, w):
    n, d = h.shape
    lw = w.shape[1] // 2
    tm = ROW_TILE
    row = lambda i: (i, 0)
    const = lambda i: (0, 0)
    out = jax.ShapeDtypeStruct((n, lw), BF16)
    return pl.pallas_call(
        _odd_in_kernel,
        grid=(n // tm,),
        in_specs=[pl.BlockSpec((tm, d), row), pl.BlockSpec((1, d), const), pl.BlockSpec(w.shape, const)],
        out_specs=[pl.BlockSpec((tm, lw), row)] * 2,
        out_shape=[out] * 2,
        compiler_params=_params("parallel"),
        name="odd_in_proj",
    )(h, g, w)


def _gate_windows(lw, bd):
    out = []
    for c0 in range(0, lw, LRU_GATE_TILE):
        nc = min(LRU_GATE_TILE, lw - c0)
        r0 = (c0 // bd) * bd
        r1 = ((c0 + nc - 1) // bd + 1) * bd
        r0 = (r0 // LANES) * LANES
        r1 = min(-(-r1 // LANES) * LANES, lw)
        out.append((c0, nc, r0, r1 - r0))
    return out


def _lru_kernel(xb_ref, gate_ref, h_ref, cw_ref, cb_ref, wg_ref, br_ref, bi_ref, lam_ref, wo_ref,
                o_ref, xbuf_ref, hs_ref, a_ref, b_ref, y_ref, *, windows):
    ti = pl.program_id(1)
    tt, lw = a_ref.shape

    @pl.when(ti == 0)
    def _():
        xbuf_ref[0:8, :] = jnp.zeros((8, lw), F32)
        hs_ref[...] = jnp.zeros((8, lw), F32)

    @pl.when(ti > 0)
    def _():
        xbuf_ref[0:8, :] = xbuf_ref[tt:tt + 8, :]

    xbuf_ref[8:8 + tt, :] = xb_ref[0].astype(F32)
    off = 8 - (LRU_CONV_WIDTH - 1)
    xc = jnp.broadcast_to(cb_ref[...], (tt, lw))
    for k in range(LRU_CONV_WIDTH):
        xc = xc + cw_ref[k:k + 1, :] * xbuf_ref[off + k:off + k + tt, :]
    xcb = xc.astype(BF16)
    sp_lam = jnp.log(1.0 + jnp.exp(-lam_ref[...]))
    wcol = 0
    for (c0, nc, r0, nr) in windows:
        rg = _dot(xcb[:, r0:r0 + nr], wg_ref[r0:r0 + nr, wcol:wcol + 2 * nc])
        wcol += 2 * nc
        r = jax.nn.sigmoid(rg[:, 0:nc] + br_ref[:, c0:c0 + nc])
        ig = jax.nn.sigmoid(rg[:, nc:2 * nc] + bi_ref[:, c0:c0 + nc])
        a = jnp.exp(-LRU_C * r * sp_lam[:, c0:c0 + nc])
        a_ref[:, c0:c0 + nc] = a
        b_ref[:, c0:c0 + nc] = jnp.sqrt(1.0 - a * a) * (ig * xc[:, c0:c0 + nc])

    rowi = lax.broadcasted_iota(jnp.int32, (8, lw), 0)

    def group(gi, hprev):
        r0 = pl.multiple_of(gi * 8, 8)
        a = a_ref[pl.ds(r0, 8), :]
        b = b_ref[pl.ds(r0, 8), :]
        for s in (1, 2, 4):
            m = rowi >= s
            a_sh = jnp.where(m, pltpu.roll(a, s, axis=0), 1.0)
            b_sh = jnp.where(m, pltpu.roll(b, s, axis=0), 0.0)
            b = a * b_sh + b
            a = a * a_sh
        hrows = a * hprev + b
        y_ref[pl.ds(r0, 8), :] = hrows
        return jnp.broadcast_to(hrows[7:8, :], (8, lw))

    hs_ref[...] = lax.fori_loop(0, tt // 8, group, hs_ref[...])
    gy = (gate_ref[0].astype(F32) * y_ref[...]).astype(BF16)
    o_ref[0] = h_ref[0] + _dot(gy, wo_ref[...])


def _lru_mixer(xb, gate, h, cw, cb, wg, br, bi, lam, wo, windows):
    bsz, tp, lw = xb.shape
    d = h.shape[-1]
    tt = LRU_TT
    const = lambda b, t: (0, 0)
    blk = lambda b, t: (b, t, 0)
    return pl.pallas_call(
        functools.partial(_lru_kernel, windows=windows),
        grid=(bsz, tp // tt),
        in_specs=[pl.BlockSpec((1, tt, lw), blk), pl.BlockSpec((1, tt, lw), blk),
                  pl.BlockSpec((1, tt, d), blk), pl.BlockSpec(cw.shape, const),
                  pl.BlockSpec((1, lw), const), pl.BlockSpec(wg.shape, const),
                  pl.BlockSpec((1, lw), const), pl.BlockSpec((1, lw), const),
                  pl.BlockSpec((1, lw), const), pl.BlockSpec(wo.shape, const)],
        out_specs=pl.BlockSpec((1, tt, d), blk),
        out_shape=jax.ShapeDtypeStruct(h.shape, F32),
        scratch_shapes=[pltpu.VMEM((8 + tt, lw), F32), pltpu.VMEM((8, lw), F32),
                        pltpu.VMEM((tt, lw), F32), pltpu.VMEM((tt, lw), F32), pltpu.VMEM((tt, lw), F32)],
        compiler_params=_params("arbitrary", "arbitrary"),
        name="rglru_mixer",
    )(xb, gate, h, cw, cb, wg, br, bi, lam, wo)


ROUTE_E1, ROUTE_E2, ROUTE_W1, ROUTE_W2, ROUTE_R1, ROUTE_R2, ROUTE_VALID = range(7)


def _router_kernel(h_ref, g_ref, rw_ref, ltri_ref, xn_ref, info_ref, cnt_ref, run_ref, *, tp, t_real):
    i = pl.program_id(0)
    tm = h_ref.shape[0]

    @pl.when(i == 0)
    def _():
        run_ref[...] = jnp.zeros(run_ref.shape, F32)

    xn = _rms(h_ref[...], g_ref[...])
    xn_ref[...] = xn.astype(BF16)
    logits = jnp.dot(xn, rw_ref[...], preferred_element_type=F32, precision=lax.Precision.HIGHEST)
    lane = lax.broadcasted_iota(jnp.int32, logits.shape, 1)
    neg = jnp.float32(-jnp.inf)
    logits = jnp.where(lane < N_EXPERTS, logits, neg)
    top1 = jnp.max(logits, axis=1, keepdims=True)
    idx1 = jnp.min(jnp.where(logits == top1, lane, LANES), axis=1, keepdims=True)
    rest = jnp.where(lane == idx1, neg, logits)
    top2 = jnp.max(rest, axis=1, keepdims=True)
    idx2 = jnp.min(jnp.where(rest == top2, lane, LANES), axis=1, keepdims=True)
    w1 = 1.0 / (1.0 + jnp.exp(top2 - top1))

    rowg = (lax.broadcasted_iota(jnp.int32, (tm, 1), 0) + i * tm).astype(F32)
    pos = rowg - jnp.floor((rowg + 0.5) / tp) * tp
    valid = jnp.where(pos < t_real, 1.0, 0.0)
    onehot = (jnp.where(lane == idx1, 1.0, 0.0) + jnp.where(lane == idx2, 1.0, 0.0)) * valid
    before = _dot(ltri_ref[...], onehot.astype(BF16)) + run_ref[...]
    rank1 = jnp.sum(jnp.where(lane == idx1, before, 0.0), axis=1, keepdims=True)
    rank2 = jnp.sum(jnp.where(lane == idx2, before, 0.0), axis=1, keepdims=True)
    run_ref[...] += jnp.sum(onehot, axis=0, keepdims=True)
    cnt_ref[...] = run_ref[...]
    fields = {ROUTE_E1: idx1.astype(F32), ROUTE_E2: idx2.astype(F32), ROUTE_W1: w1, ROUTE_W2: 1.0 - w1,
              ROUTE_R1: rank1, ROUTE_R2: rank2, ROUTE_VALID: valid}
    info = jnp.zeros(logits.shape, F32)
    for col, val in fields.items():
        info = jnp.where(lane == col, val, info)
    info_ref[...] = info


def _router(h, g, rw, ltri, tp, t_real):
    n, d = h.shape
    tm = ROW_TILE
    row = lambda i: (i, 0)
    const = lambda i: (0, 0)
    return pl.pallas_call(
        functools.partial(_router_kernel, tp=tp, t_real=t_real),
        grid=(n // tm,),
        in_specs=[pl.BlockSpec((tm, d), row), pl.BlockSpec((1, d), const), pl.BlockSpec(rw.shape, const),
                  pl.BlockSpec(ltri.shape, const)],
        out_specs=[pl.BlockSpec((tm, d), row), pl.BlockSpec((tm, LANES), row), pl.BlockSpec((1, LANES), const)],
        out_shape=[jax.ShapeDtypeStruct((n, d), BF16), jax.ShapeDtypeStruct((n, LANES), F32),
                   jax.ShapeDtypeStruct((1, LANES), F32)],
        scratch_shapes=[pltpu.VMEM((1, LANES), F32)],
        compiler_params=_params("arbitrary"),
        name="router",
    )(h, g, rw, ltri)


def _routing_tables(info, counts, n_rows):
    n = info.shape[0]
    tg = MOE_TILE
    e1 = info[:, ROUTE_E1].astype(jnp.int32)
    e2 = info[:, ROUTE_E2].astype(jnp.int32)
    valid = info[:, ROUTE_VALID] > 0.5
    cnt = counts[0, :N_EXPERTS].astype(jnp.int32)
    padded = (cnt + tg - 1) // tg * tg
    ends = jnp.cumsum(padded)
    starts = ends - padded
    d1 = jnp.where(valid, starts[e1] + info[:, ROUTE_R1].astype(jnp.int32), n_rows)
    d2 = jnp.where(valid, starts[e2] + info[:, ROUTE_R2].astype(jnp.int32), n_rows)
    dst = jnp.concatenate([d1, d2])
    tok = jnp.arange(n, dtype=jnp.int32)
    src = jnp.zeros((n_rows,), jnp.int32).at[dst].set(jnp.concatenate([tok, tok]), mode="drop")
    gs = jnp.zeros((n_rows,), F32).at[dst].set(
        jnp.concatenate([info[:, ROUTE_W1], info[:, ROUTE_W2]]), mode="drop")
    tile_start = jnp.arange(n_rows // tg, dtype=jnp.int32) * tg
    tile_expert = jnp.minimum(jnp.sum(tile_start[:, None] >= ends[None, :], axis=1), N_EXPERTS - 1)
    n_used = (ends[-1] // tg).reshape(1)
    tm = COMBINE_TILE
    dest = jnp.stack([jnp.where(valid, d1, 0), jnp.where(valid, d2, 0)], axis=0)
    dest = dest.reshape(2, n // tm, tm).transpose(1, 0, 2).reshape(-1)
    return src, gs.reshape(n_rows, 1), tile_expert.astype(jnp.int32), n_used.astype(jnp.int32), dest


def _interleave_gate_up(wg, wu, tf):
    e, d, ff = wg.shape
    g = wg.astype(BF16).reshape(e, d, ff // tf, tf)
    u = wu.astype(BF16).reshape(e, d, ff // tf, tf)
    return jnp.concatenate([g, u], axis=3).reshape(e, d, 2 * ff)


def _dense_block_diag(w):
    nb, bd, _ = w.shape
    eye = jnp.eye(nb, dtype=w.dtype)
    return (w[:, :, None, :] * eye[:, None, :, None]).reshape(nb * bd, nb * bd)


def _pack_lru_gates(wr, wi, windows):
    dr = _dense_block_diag(wr).astype(BF16)
    di = _dense_block_diag(wi).astype(BF16)
    cols = []
    for (c0, nc, _, _) in windows:
        cols += [dr[:, c0:c0 + nc], di[:, c0:c0 + nc]]
    return jnp.concatenate(cols, axis=1)


def _ffn_chunk(ff):
    for tf in (896, 512, 384, 256, 128):
        if ff % tf == 0:
            return tf
    raise ValueError(f"unsupported d_ff {ff}")


def kernel(x, meta_tokens, mix_norm_even, w_in_even, conv_w, conv_b, conv_ln_g, conv_ln_b, q_norm_g, k_norm_g, w_out_even, ffn_norm_even, ffn_w_gate, ffn_w_up, ffn_w_down, mix_norm_odd, w_in_odd, lru_conv_w, lru_conv_b, gate_r_w, gate_r_b, gate_i_w, gate_i_b, lru_lambda, w_out_odd, ffn_norm_odd, router_w, moe_w_gate, moe_w_up, moe_w_down):
    bsz, seq, d = x.shape
    t_real = N_META + seq
    tp = -(-t_real // TIME_TILE) * TIME_TILE
    n = bsz * tp
    assert n % ROW_TILE == 0
    depth = mix_norm_even.shape[0] + mix_norm_odd.shape[0]

    meta = jnp.broadcast_to(meta_tokens[None].astype(x.dtype), (bsz, N_META, d))
    h = jnp.concatenate([meta, x, jnp.zeros((bsz, tp - t_real, d), x.dtype)], axis=1).reshape(n, d)

    head_mean = jnp.kron(jnp.eye(MXU_DIM // SB_HEAD_DIM, dtype=F32),
                         jnp.full((SB_HEAD_DIM, SB_HEAD_DIM), 1.0 / SB_HEAD_DIM, F32)).astype(BF16)
    kk = jnp.arange(ATT_TK)
    neg_tri = -(kk[:, None] >= kk[None, :]).astype(BF16)
    row2 = lambda a: a.reshape(1, -1)

    for layer in range(depth):
        p = layer // 2
        if layer % 2 == 0:
            u, q, k, v = _even_in_proj(h, row2(mix_norm_even[p]), w_in_even[p].astype(BF16),
                                       row2(jnp.tile(q_norm_g[p], SB_HEADS)),
                                       row2(jnp.tile(k_norm_g[p], SB_HEADS)), head_mean)
            u = _conv_module(u.reshape(bsz, tp, CONV_CH), conv_w[p], row2(conv_b[p]),
                             row2(conv_ln_g[p]), row2(conv_ln_b[p]))
            o = _attention(q.reshape(bsz, tp, SB_WIDTH), k.reshape(bsz, tp, SB_WIDTH),
                           v.reshape(bsz, tp, SB_WIDTH), neg_tri)
            h, xn = _even_out_proj(h, u.reshape(n, CONV_CH), o.reshape(n, SB_WIDTH),
                                   w_out_even[p].astype(BF16), row2(ffn_norm_even[p]))
            tf = _ffn_chunk(ffn_w_gate.shape[-1])
            h = _ffn(xn, h, _interleave_gate_up(ffn_w_gate[p][None], ffn_w_up[p][None], tf)[0],
                     ffn_w_down[p].astype(BF16), tf)
        else:
            lw = lru_lambda.shape[-1]
            windows = _gate_windows(lw, lw // LRU_BLOCKS)
            gate, xb = _odd_in_proj(h, row2(mix_norm_odd[p]), w_in_odd[p].astype(BF16))
            h = _lru_mixer(xb.reshape(bsz, tp, lw), gate.reshape(bsz, tp, lw), h.reshape(bsz, tp, d),
                           lru_conv_w[p], row2(lru_conv_b[p]),
                           _pack_lru_gates(gate_r_w[p], gate_i_w[p], windows),
                           row2(gate_r_b[p]), row2(gate_i_b[p]), row2(lru_lambda[p]),
                           w_out_odd[p].astype(BF16), windows).reshape(n, d)
            rw = jnp.pad(router_w[p], ((0, 0), (0, LANES - N_EXPERTS)))
            kk = jnp.arange(ROW_TILE)
            ltri = (kk[:, None] > kk[None, :]).astype(BF16)
            xn, info, counts = _router(h, row2(ffn_norm_odd[p]), rw, ltri, tp, t_real)
            tg = MOE_TILE
            n_rows = -(-(2 * bsz * t_real + N_EXPERTS * (tg - 1)) // tg) * tg
            src, gs, tile_expert, n_used, dest = _routing_tables(info, counts, n_rows)
            sub = d // LANES
            assert sub == SUBLANES
            xs = _row_gather(src, xn.reshape(n, sub, LANES), n_rows).reshape(n_rows, d)
            tf = _ffn_chunk(moe_w_gate.shape[-1])
            ys = _grouped_ffn(tile_expert, n_used, xs, gs, _interleave_gate_up(moe_w_gate[p], moe_w_up[p], tf),
                              moe_w_down[p].astype(BF16), tf)
            h = _combine(dest, h.reshape(n, sub, LANES), ys.reshape(n_rows, sub, LANES)).reshape(n, d)
    return h.reshape(bsz, tp, d)[:, N_META:t_real]
```

```python
import functools
import math

import jax
import jax.numpy as jnp
from jax import lax
from jax.experimental import pallas as pl
from jax.experimental.pallas import tpu as pltpu

F32 = jnp.float32
BF16 = jnp.bfloat16

EPS = 1e-6
N_META = 16
CONV_CH = 512
CONV_WIDTH = 31
SB_HEADS = 8
SB_HEAD_DIM = 64
SB_WIDTH = SB_HEADS * SB_HEAD_DIM
LRU_BLOCKS = 16
LRU_CONV_WIDTH = 4
LRU_C = 8.0
N_EXPERTS = 8

LANES = 128
MXU_DIM = 256
TIME_TILE = 256
VMEM_LIMIT = 50 * 1024 * 1024

ROW_TILE = 512
CONV_TT = 128
CONV_HALO = 32
CONV_CHUNK = 32
ATT_TQ = 128
ATT_TK = 256
LOG2E = 1.4426950408889634
MASKED_EXPONENT = -1e30
LRU_TT = 256
LRU_GATE_TILE = 256
MOE_TILE = 512
COMBINE_TILE = 512
SUBLANES = 8


def _dot(a, b):
    return jnp.dot(a, b, preferred_element_type=F32)


def _params(*sem):
    return pltpu.CompilerParams(dimension_semantics=sem, vmem_limit_bytes=VMEM_LIMIT)


def _rms(x, g):
    ms = jnp.mean(x * x, axis=-1, keepdims=True)
    return x * lax.rsqrt(ms + EPS) * g


def _even_in_kernel(h_ref, g_ref, w_ref, qg_ref, kg_ref, hm_ref, u_ref, q_ref, k_ref, v_ref):
    xn = _rms(h_ref[...], g_ref[...]).astype(BF16)
    a = _dot(xn, w_ref[:, 0:CONV_CH])
    gate = _dot(xn, w_ref[:, CONV_CH:2 * CONV_CH])
    u_ref[...] = (a * jax.nn.sigmoid(gate)).astype(BF16)

    def head_norm(y, gain):
        yy = (y * y).astype(BF16)
        parts = [_dot(yy[:, c:c + MXU_DIM], hm_ref[...]) for c in range(0, SB_WIDTH, MXU_DIM)]
        ms = jnp.concatenate(parts, axis=1)
        return y * lax.rsqrt(ms + EPS) * gain

    c0 = 2 * CONV_CH
    q = head_norm(_dot(xn, w_ref[:, c0:c0 + SB_WIDTH]), qg_ref[...])
    q_ref[...] = (q * (LOG2E / math.sqrt(SB_HEAD_DIM))).astype(BF16)
    k = head_norm(_dot(xn, w_ref[:, c0 + SB_WIDTH:c0 + 2 * SB_WIDTH]), kg_ref[...])
    k_ref[...] = k.astype(BF16)
    v_ref[...] = _dot(xn, w_ref[:, c0 + 2 * SB_WIDTH:c0 + 3 * SB_WIDTH]).astype(BF16)


def _even_in_proj(h, g, w, qg, kg, hm):
    n, d = h.shape
    tm = ROW_TILE
    row = lambda i: (i, 0)
    const = lambda i: (0, 0)
    out = jax.ShapeDtypeStruct((n, SB_WIDTH), BF16)
    return pl.pallas_call(
        _even_in_kernel,
        grid=(n // tm,),
        in_specs=[pl.BlockSpec((tm, d), row), pl.BlockSpec((1, d), const),
                  pl.BlockSpec(w.shape, const), pl.BlockSpec((1, SB_WIDTH), const),
                  pl.BlockSpec((1, SB_WIDTH), const), pl.BlockSpec(hm.shape, const)],
        out_specs=[pl.BlockSpec((tm, SB_WIDTH), row)] * 4,
        out_shape=[out] * 4,
        compiler_params=_params("parallel"),
        name="even_in_proj",
    )(h, g, w, qg, kg, hm)


def _conv_kernel(cur_ref, halo_ref, w_ref, b_ref, lg_ref, lb_ref, o_ref, buf_ref):
    i = pl.program_id(1)
    tt = cur_ref.shape[1]
    halo = halo_ref[0].astype(F32)
    buf_ref[0:CONV_HALO, :] = jnp.where(i > 0, halo, 0.0)
    buf_ref[CONV_HALO:CONV_HALO + tt, :] = cur_ref[0].astype(F32)
    off = CONV_HALO - (CONV_WIDTH - 1)
    for c in range(tt // CONV_CHUNK):
        r0 = c * CONV_CHUNK
        acc = jnp.broadcast_to(b_ref[...], (CONV_CHUNK, CONV_CH))
        for k in range(CONV_WIDTH):
            acc = acc + w_ref[k:k + 1, :] * buf_ref[r0 + off + k:r0 + off + k + CONV_CHUNK, :]
        mu = jnp.mean(acc, axis=-1, keepdims=True)
        xc = acc - mu
        var = jnp.mean(xc * xc, axis=-1, keepdims=True)
        y = xc * lax.rsqrt(var + EPS) * lg_ref[...] + lb_ref[...]
        o_ref[0, r0:r0 + CONV_CHUNK, :] = (y * jax.nn.sigmoid(y)).astype(BF16)


def _conv_module(u, w, b, lg, lb):
    bsz, tp, c = u.shape
    tt = CONV_TT
    per = tt // CONV_HALO
    const = lambda bi, i: (0, 0)
    return pl.pallas_call(
        _conv_kernel,
        grid=(bsz, tp // tt),
        in_specs=[pl.BlockSpec((1, tt, c), lambda bi, i: (bi, i, 0)),
                  pl.BlockSpec((1, CONV_HALO, c), lambda bi, i: (bi, jnp.maximum(i * per - 1, 0), 0)),
                  pl.BlockSpec(w.shape, const), pl.BlockSpec((1, c), const),
                  pl.BlockSpec((1, c), const), pl.BlockSpec((1, c), const)],
        out_specs=pl.BlockSpec((1, tt, c), lambda bi, i: (bi, i, 0)),
        out_shape=jax.ShapeDtypeStruct(u.shape, BF16),
        scratch_shapes=[pltpu.VMEM((CONV_HALO + tt, c), F32)],
        compiler_params=_params("parallel", "parallel"),
        name="conv_module",
    )(u, u, w, b, lg, lb)


def _attn_kernel(q_ref, k_ref, v_ref, tri_ref, o_ref, q2_ref, e_ref, tot_ref, acc_ref, car_ref):
    i = pl.program_id(1)
    tq, tk = ATT_TQ, ATT_TK
    npairs = q_ref.shape[2] // LANES
    lane_q = lax.broadcasted_iota(jnp.int32, (tq, LANES), 1)

    for hp in range(npairs):
        q = q_ref[0, :, hp * LANES:(hp + 1) * LANES]
        zero = jnp.zeros_like(q)
        q2_ref[hp, 0:tq, :] = jnp.where(lane_q < SB_HEAD_DIM, q, zero)
        q2_ref[hp, tq:2 * tq, :] = jnp.where(lane_q >= SB_HEAD_DIM, q, zero)
        car_ref[hp] = jnp.zeros((2 * tq, LANES), F32)
        acc_ref[hp] = jnp.zeros((2 * tq, LANES), F32)

    def stage_a(j, mask):
        k0 = pl.multiple_of(j * tk, tk)
        for hp in range(npairs):
            kt = k_ref[0, pl.ds(k0, tk), hp * LANES:(hp + 1) * LANES]
            z = lax.dot_general(q2_ref[hp], kt, (((1,), (1,)), ((), ())), preferred_element_type=F32)
            sp = jnp.maximum(z, 0.0) + jnp.log(1.0 + jnp.exp2(-jnp.abs(z))) * LOG2E
            if mask is not None:
                sp = jnp.where(mask, sp, 0.0)
            e = z + _dot(sp.astype(BF16), tri_ref[...])
            if mask is not None:
                e = jnp.where(mask, e, MASKED_EXPONENT)
            e_ref[hp] = e
            tot_ref[hp] = jnp.broadcast_to(jnp.sum(sp, axis=1, keepdims=True), (2 * tq, LANES))

    def stage_b(j):
        k0 = pl.multiple_of(j * tk, tk)
        for hp in range(npairs):
            car = car_ref[hp]
            w = jnp.exp2(e_ref[hp] + jnp.concatenate([car] * (tk // LANES), axis=1)).astype(BF16)
            vt = v_ref[0, pl.ds(k0, tk), hp * LANES:(hp + 1) * LANES]
            acc_ref[hp] += _dot(w, vt)
            car_ref[hp] = car - tot_ref[hp]

    j_last = (i * tq) // tk
    row = lax.broadcasted_iota(jnp.int32, (2 * tq, tk), 0)
    col = lax.broadcasted_iota(jnp.int32, (2 * tq, tk), 1)
    row = jnp.where(row >= tq, row - tq, row) + i * tq
    stage_a(j_last, (col + j_last * tk) < row)

    def body(jj, carry):
        j = j_last - 1 - jj
        stage_b(j + 1)
        stage_a(j, None)
        return carry

    lax.fori_loop(0, j_last, body, 0)
    stage_b(0)
    for hp in range(npairs):
        o_ref[0, :, hp * LANES:(hp + 1) * LANES] = jnp.where(
            lane_q < SB_HEAD_DIM, acc_ref[hp, 0:tq, :], acc_ref[hp, tq:2 * tq, :]).astype(BF16)


def _attention(q, k, v, tri):
    bsz, tp, width = q.shape
    tq, tk = ATT_TQ, ATT_TK
    npairs = width // LANES
    return pl.pallas_call(
        _attn_kernel,
        grid=(bsz, tp // tq),
        in_specs=[pl.BlockSpec((1, tq, width), lambda b, i: (b, i, 0)),
                  pl.BlockSpec((1, tp, width), lambda b, i: (b, 0, 0)),
                  pl.BlockSpec((1, tp, width), lambda b, i: (b, 0, 0)),
                  pl.BlockSpec(tri.shape, lambda b, i: (0, 0))],
        out_specs=pl.BlockSpec((1, tq, width), lambda b, i: (b, i, 0)),
        out_shape=jax.ShapeDtypeStruct(q.shape, BF16),
        scratch_shapes=[pltpu.VMEM((npairs, 2 * tq, LANES), BF16),
                        pltpu.VMEM((npairs, 2 * tq, tk), F32),
                        pltpu.VMEM((npairs, 2 * tq, LANES), F32),
                        pltpu.VMEM((npairs, 2 * tq, LANES), F32),
                        pltpu.VMEM((npairs, 2 * tq, LANES), F32)],
        compiler_params=_params("parallel", "arbitrary"),
        name="stick_breaking",
    )(q, k, v, tri)


def _even_out_kernel(h_ref, u_ref, o_ref, w_ref, g_ref, h1_ref, xn_ref):
    h1 = h_ref[...] + _dot(u_ref[...], w_ref[0:CONV_CH, :]) + _dot(o_ref[...], w_ref[CONV_CH:, :])
    h1_ref[...] = h1
    xn_ref[...] = _rms(h1, g_ref[...]).astype(BF16)


def _even_out_proj(h, u, o, w, g):
    n, d = h.shape
    tm = ROW_TILE
    row = lambda i: (i, 0)
    const = lambda i: (0, 0)
    return pl.pallas_call(
        _even_out_kernel,
        grid=(n // tm,),
        in_specs=[pl.BlockSpec((tm, d), row), pl.BlockSpec((tm, CONV_CH), row),
                  pl.BlockSpec((tm, SB_WIDTH), row), pl.BlockSpec(w.shape, const),
                  pl.BlockSpec((1, d), const)],
        out_specs=[pl.BlockSpec((tm, d), row), pl.BlockSpec((tm, d), row)],
        out_shape=[jax.ShapeDtypeStruct((n, d), F32), jax.ShapeDtypeStruct((n, d), BF16)],
        compiler_params=_params("parallel"),
        name="even_out_proj",
    )(h, u, o, w, g)


def _ffn_kernel(xn_ref, res_ref, wgu_ref, wd_ref, o_ref, *, tf):
    @pl.when(pl.program_id(1) == 0)
    def _():
        o_ref[...] = res_ref[...]

    gu = _dot(xn_ref[...], wgu_ref[...])
    g = gu[:, 0:tf]
    mid = g * jax.nn.sigmoid(g) * gu[:, tf:2 * tf]
    o_ref[...] += _dot(mid.astype(BF16), wd_ref[...])


def _ffn(xn, res, wgu, wd, tf):
    n, d = xn.shape
    ff = wd.shape[0]
    tm = ROW_TILE
    row = lambda i, f: (i, 0)
    return pl.pallas_call(
        functools.partial(_ffn_kernel, tf=tf),
        grid=(n // tm, ff // tf),
        in_specs=[pl.BlockSpec((tm, d), row), pl.BlockSpec((tm, d), row),
                  pl.BlockSpec((d, 2 * tf), lambda i, f: (0, f)),
                  pl.BlockSpec((tf, d), lambda i, f: (f, 0))],
        out_specs=pl.BlockSpec((tm, d), row),
        out_shape=jax.ShapeDtypeStruct((n, d), F32),
        compiler_params=_params("parallel", "arbitrary"),
        name="swiglu",
    )(xn, res, wgu, wd)


def _row_scatter_kernel(dest_ref, x_ref, init_hbm, o_hbm, sem):
    del init_hbm
    i = pl.program_id(0)
    tm = x_ref.shape[0]
    for k in range(2):
        def issue(r, c, k=k):
            pltpu.make_async_copy(x_ref.at[r], o_hbm.at[dest_ref[(i * 2 + k) * tm + r]], sem).start()
            return c
        lax.fori_loop(0, tm, issue, 0, unroll=8)
    for k in range(2):
        pltpu.make_async_copy(x_ref, o_hbm.at[pl.ds(0, tm)], sem).wait()


def _row_scatter(dest, x3, n_rows_alloc):
    n = x3.shape[0]
    tm = COMBINE_TILE
    init = jnp.zeros((n_rows_alloc,) + x3.shape[1:], x3.dtype)
    return pl.pallas_call(
        _row_scatter_kernel,
        grid_spec=pltpu.PrefetchScalarGridSpec(
            num_scalar_prefetch=1, grid=(n // tm,),
            in_specs=[pl.BlockSpec((tm,) + x3.shape[1:], lambda i, dest: (i, 0, 0)),
                      pl.BlockSpec(memory_space=pl.ANY)],
            out_specs=pl.BlockSpec(memory_space=pl.ANY),
            scratch_shapes=[pltpu.SemaphoreType.DMA]),
        out_shape=jax.ShapeDtypeStruct(init.shape, init.dtype),
        input_output_aliases={2: 0},
        compiler_params=_params("arbitrary"),
        name="moe_row_scatter",
    )(dest, x3, init)


def _gmm_kernel(te_ref, nu_ref, xs_ref, wgu_ref, wd_ref, o_ref, acc_ref, *, tf):
    p = pl.program_id(0)
    f = pl.program_id(1)
    last = pl.num_programs(1) - 1
    used = p < nu_ref[0]

    @pl.when(used)
    def _():
        gu = _dot(xs_ref[...], wgu_ref[0])
        g = gu[:, 0:tf]
        mid = g * jax.nn.sigmoid(g) * gu[:, tf:2 * tf]
        part = _dot(mid.astype(BF16), wd_ref[0])

        @pl.when(f == 0)
        def _():
            acc_ref[...] = part

        @pl.when(f > 0)
        def _():
            acc_ref[...] += part

        @pl.when(f == last)
        def _():
            o_ref[...] = acc_ref[...].astype(o_ref.dtype)

    @pl.when(jnp.logical_not(used) & (f == last))
    def _():
        o_ref[...] = jnp.zeros(o_ref.shape, o_ref.dtype)


def _grouped_ffn(tile_expert, n_used, xs, n_rows, wgu, wd, tf):
    d = xs.shape[1]
    ff = wd.shape[1]
    nf = ff // tf
    tg = MOE_TILE
    row = lambda p, f, te, nu: (p, 0)
    fidx = lambda p, f, nu: jnp.where(p < nu[0], f, nf - 1)
    return pl.pallas_call(
        functools.partial(_gmm_kernel, tf=tf),
        grid_spec=pltpu.PrefetchScalarGridSpec(
            num_scalar_prefetch=2, grid=(n_rows // tg, nf),
            in_specs=[pl.BlockSpec((tg, d), row),
                      pl.BlockSpec((1, d, 2 * tf), lambda p, f, te, nu: (te[p], 0, fidx(p, f, nu))),
                      pl.BlockSpec((1, tf, d), lambda p, f, te, nu: (te[p], fidx(p, f, nu), 0))],
            out_specs=pl.BlockSpec((tg, d), row),
            scratch_shapes=[pltpu.VMEM((tg, d), F32)]),
        out_shape=jax.ShapeDtypeStruct((n_rows, d), BF16),
        compiler_params=_params("arbitrary", "arbitrary"),
        name="moe_grouped_ffn",
    )(tile_expert, n_used, xs, wgu, wd)


def _combine_kernel(dest_ref, h_ref, w1_ref, w2_ref, ys_hbm, o_ref, ybuf, sem):
    i = pl.program_id(0)
    tm = h_ref.shape[0]
    for k in range(2):
        def issue(r, c, k=k):
            pltpu.make_async_copy(ys_hbm.at[dest_ref[(i * 2 + k) * tm + r]], ybuf.at[k, r], sem).start()
            return c
        lax.fori_loop(0, tm, issue, 0, unroll=8)
    for k in range(2):
        pltpu.make_async_copy(ys_hbm.at[pl.ds(0, tm)], ybuf.at[k], sem).wait()
    o_ref[...] = h_ref[...] + w1_ref[...] * ybuf[0].astype(F32) + w2_ref[...] * ybuf[1].astype(F32)


def _combine(dest, h3, w1b, w2b, ys3):
    n = h3.shape[0]
    tm = COMBINE_TILE
    blk = lambda i, dest: (i, 0, 0)
    return pl.pallas_call(
        _combine_kernel,
        grid_spec=pltpu.PrefetchScalarGridSpec(
            num_scalar_prefetch=1, grid=(n // tm,),
            in_specs=[pl.BlockSpec((tm,) + h3.shape[1:], blk), pl.BlockSpec((tm, 1, LANES), blk),
                      pl.BlockSpec((tm, 1, LANES), blk), pl.BlockSpec(memory_space=pl.ANY)],
            out_specs=pl.BlockSpec((tm,) + h3.shape[1:], blk),
            scratch_shapes=[pltpu.VMEM((2, tm) + ys3.shape[1:], ys3.dtype), pltpu.SemaphoreType.DMA]),
        out_shape=jax.ShapeDtypeStruct(h3.shape, F32),
        compiler_params=_params("arbitrary"),
        name="moe_combine",
    )(dest, h3, w1b, w2b, ys3)


def _odd_in_kernel(h_ref, g_ref, w_ref, gate_ref, xb_ref):
    xn = _rms(h_ref[...], g_ref[...]).astype(BF16)
    y = _dot(xn, w_ref[...])
    lw = gate_ref.shape[1]
    gate_ref[...] = jax.nn.gelu(y[:, 0:lw], approximate=True).astype(BF16)
    xb_ref[...] = y[:, lw:2 * lw].astype(BF16)


def _odd_in_proj(h, g, w):
    n, d = h.shape
    lw = w.shape[1] // 2
    tm = ROW_TILE
    row = lambda i: (i, 0)
    const = lambda i: (0, 0)
    out = jax.ShapeDtypeStruct((n, lw), BF16)
    return pl.pallas_call(
        _odd_in_kernel,
        grid=(n // tm,),
        in_specs=[pl.BlockSpec((tm, d), row), pl.BlockSpec((1, d), const), pl.BlockSpec(w.shape, const)],
        out_specs=[pl.BlockSpec((tm, lw), row)] * 2,
        out_shape=[out] * 2,
        compiler_params=_params("parallel"),
        name="odd_in_proj",
    )(h, g, w)


def _gate_windows(lw, bd):
    out = []
    for c0 in range(0, lw, LRU_GATE_TILE):
        nc = min(LRU_GATE_TILE, lw - c0)
        r0 = (c0 // bd) * bd
        r1 = ((c0 + nc - 1) // bd + 1) * bd
        r0 = (r0 // LANES) * LANES
        r1 = min(-(-r1 // LANES) * LANES, lw)
        out.append((c0, nc, r0, r1 - r0))
    return out


def _lru_kernel(xb_ref, gate_ref, h_ref, cw_ref, cb_ref, wg_ref, br_ref, bi_ref, lam_ref, wo_ref,
                o_ref, xbuf_ref, hs_ref, a_ref, b_ref, y_ref, *, windows):
    ti = pl.program_id(1)
    tt, lw = a_ref.shape

    @pl.when(ti == 0)
    def _():
        xbuf_ref[0:8, :] = jnp.zeros((8, lw), F32)
        hs_ref[...] = jnp.zeros((8, lw), F32)

    @pl.when(ti > 0)
    def _():
        xbuf_ref[0:8, :] = xbuf_ref[tt:tt + 8, :]

    xbuf_ref[8:8 + tt, :] = xb_ref[0].astype(F32)
    off = 8 - (LRU_CONV_WIDTH - 1)
    xc = jnp.broadcast_to(cb_ref[...], (tt, lw))
    for k in range(LRU_CONV_WIDTH):
        xc = xc + cw_ref[k:k + 1, :] * xbuf_ref[off + k:off + k + tt, :]
    xcb = xc.astype(BF16)
    sp_lam = jnp.log(1.0 + jnp.exp(-lam_ref[...]))
    wcol = 0
    for (c0, nc, r0, nr) in windows:
        rg = _dot(xcb[:, r0:r0 + nr], wg_ref[r0:r0 + nr, wcol:wcol + 2 * nc])
        wcol += 2 * nc
        r = jax.nn.sigmoid(rg[:, 0:nc] + br_ref[:, c0:c0 + nc])
        ig = jax.nn.sigmoid(rg[:, nc:2 * nc] + bi_ref[:, c0:c0 + nc])
        a = jnp.exp(-LRU_C * r * sp_lam[:, c0:c0 + nc])
        a_ref[:, c0:c0 + nc] = a
        b_ref[:, c0:c0 + nc] = jnp.sqrt(1.0 - a * a) * (ig * xc[:, c0:c0 + nc])

    rowi = lax.broadcasted_iota(jnp.int32, (8, lw), 0)

    def group(gi, hprev):
        r0 = pl.multiple_of(gi * 8, 8)
        a = a_ref[pl.ds(r0, 8), :]
        b = b_ref[pl.ds(r0, 8), :]
        for s in (1, 2, 4):
            m = rowi >= s
            a_sh = jnp.where(m, pltpu.roll(a, s, axis=0), 1.0)
            b_sh = jnp.where(m, pltpu.roll(b, s, axis=0), 0.0)
            b = a * b_sh + b
            a = a * a_sh
        hrows = a * hprev + b
        y_ref[pl.ds(r0, 8), :] = hrows
        return jnp.broadcast_to(hrows[7:8, :], (8, lw))

    hs_ref[...] = lax.fori_loop(0, tt // 8, group, hs_ref[...])
    gy = (gate_ref[0].astype(F32) * y_ref[...]).astype(BF16)
    o_ref[0] = h_ref[0] + _dot(gy, wo_ref[...])


def _lru_mixer(xb, gate, h, cw, cb, wg, br, bi, lam, wo, windows):
    bsz, tp, lw = xb.shape
    d = h.shape[-1]
    tt = LRU_TT
    const = lambda b, t: (0, 0)
    blk = lambda b, t: (b, t, 0)
    return pl.pallas_call(
        functools.partial(_lru_kernel, windows=windows),
        grid=(bsz, tp // tt),
        in_specs=[pl.BlockSpec((1, tt, lw), blk), pl.BlockSpec((1, tt, lw), blk),
                  pl.BlockSpec((1, tt, d), blk), pl.BlockSpec(cw.shape, const),
                  pl.BlockSpec((1, lw), const), pl.BlockSpec(wg.shape, const),
                  pl.BlockSpec((1, lw), const), pl.BlockSpec((1, lw), const),
                  pl.BlockSpec((1, lw), const), pl.BlockSpec(wo.shape, const)],
        out_specs=pl.BlockSpec((1, tt, d), blk),
        out_shape=jax.ShapeDtypeStruct(h.shape, F32),
        scratch_shapes=[pltpu.VMEM((8 + tt, lw), F32), pltpu.VMEM((8, lw), F32),
                        pltpu.VMEM((tt, lw), F32), pltpu.VMEM((tt, lw), F32), pltpu.VMEM((tt, lw), F32)],
        compiler_params=_params("arbitrary", "arbitrary"),
        name="rglru_mixer",
    )(xb, gate, h, cw, cb, wg, br, bi, lam, wo)


ROUTE_E1, ROUTE_E2, ROUTE_W1, ROUTE_W2, ROUTE_R1, ROUTE_R2, ROUTE_VALID = range(7)


def _router_kernel(h_ref, g_ref, rw_ref, ltri_ref, xn_ref, info_ref, cnt_ref, run_ref, *, tp, t_real):
    i = pl.program_id(0)
    tm = h_ref.shape[0]

    @pl.when(i == 0)
    def _():
        run_ref[...] = jnp.zeros(run_ref.shape, F32)

    xn = _rms(h_ref[...], g_ref[...])
    xn_ref[...] = xn.astype(BF16)
    logits = jnp.dot(xn, rw_ref[...], preferred_element_type=F32, precision=lax.Precision.HIGHEST)
    lane = lax.broadcasted_iota(jnp.int32, logits.shape, 1)
    neg = jnp.float32(-jnp.inf)
    logits = jnp.where(lane < N_EXPERTS, logits, neg)
    top1 = jnp.max(logits, axis=1, keepdims=True)
    idx1 = jnp.min(jnp.where(logits == top1, lane, LANES), axis=1, keepdims=True)
    rest = jnp.where(lane == idx1, neg, logits)
    top2 = jnp.max(rest, axis=1, keepdims=True)
    idx2 = jnp.min(jnp.where(rest == top2, lane, LANES), axis=1, keepdims=True)
    w1 = 1.0 / (1.0 + jnp.exp(top2 - top1))

    rowg = (lax.broadcasted_iota(jnp.int32, (tm, 1), 0) + i * tm).astype(F32)
    pos = rowg - jnp.floor((rowg + 0.5) / tp) * tp
    valid = jnp.where(pos < t_real, 1.0, 0.0)
    onehot = (jnp.where(lane == idx1, 1.0, 0.0) + jnp.where(lane == idx2, 1.0, 0.0)) * valid
    before = _dot(ltri_ref[...], onehot.astype(BF16)) + run_ref[...]
    rank1 = jnp.sum(jnp.where(lane == idx1, before, 0.0), axis=1, keepdims=True)
    rank2 = jnp.sum(jnp.where(lane == idx2, before, 0.0), axis=1, keepdims=True)
    run_ref[...] += jnp.sum(onehot, axis=0, keepdims=True)
    cnt_ref[...] = run_ref[...]
    fields = {ROUTE_E1: idx1.astype(F32), ROUTE_E2: idx2.astype(F32), ROUTE_W1: w1, ROUTE_W2: 1.0 - w1,
              ROUTE_R1: rank1, ROUTE_R2: rank2, ROUTE_VALID: valid}
    info = jnp.zeros(logits.shape, F32)
    for col, val in fields.items():
        info = jnp.where(lane == col, val, info)
    info_ref[...] = info


def _router(h, g, rw, ltri, tp, t_real):
    n, d = h.shape
    tm = ROW_TILE
    row = lambda i: (i, 0)
    const = lambda i: (0, 0)
    return pl.pallas_call(
        functools.partial(_router_kernel, tp=tp, t_real=t_real),
        grid=(n // tm,),
        in_specs=[pl.BlockSpec((tm, d), row), pl.BlockSpec((1, d), const), pl.BlockSpec(rw.shape, const),
                  pl.BlockSpec(ltri.shape, const)],
        out_specs=[pl.BlockSpec((tm, d), row), pl.BlockSpec((tm, LANES), row), pl.BlockSpec((1, LANES), const)],
        out_shape=[jax.ShapeDtypeStruct((n, d), BF16), jax.ShapeDtypeStruct((n, LANES), F32),
                   jax.ShapeDtypeStruct((1, LANES), F32)],
        scratch_shapes=[pltpu.VMEM((1, LANES), F32)],
        compiler_params=_params("arbitrary"),
        name="router",
    )(h, g, rw, ltri)


def _routing_tables(info, counts, n_rows):
    n = info.shape[0]
    tg = MOE_TILE
    e1 = info[:, ROUTE_E1].astype(jnp.int32)
    e2 = info[:, ROUTE_E2].astype(jnp.int32)
    valid = info[:, ROUTE_VALID] > 0.5
    cnt = counts[0, :N_EXPERTS].astype(jnp.int32)
    padded = (cnt + tg - 1) // tg * tg
    ends = jnp.cumsum(padded)
    starts = ends - padded
    d1 = starts[e1] + info[:, ROUTE_R1].astype(jnp.int32)
    d2 = starts[e2] + info[:, ROUTE_R2].astype(jnp.int32)
    tile_start = jnp.arange(n_rows // tg, dtype=jnp.int32) * tg
    tile_expert = jnp.minimum(jnp.sum(tile_start[:, None] >= ends[None, :], axis=1), N_EXPERTS - 1)
    n_used = (ends[-1] // tg).reshape(1)
    tm = COMBINE_TILE
    per_tile = lambda a, b: jnp.stack([a, b]).reshape(2, n // tm, tm).transpose(1, 0, 2).reshape(-1)
    dump = n_rows + jnp.arange(n, dtype=jnp.int32) % tm
    scatter_dest = per_tile(jnp.where(valid, d1, dump), jnp.where(valid, d2, dump + tm))
    gather_dest = per_tile(jnp.where(valid, d1, 0), jnp.where(valid, d2, 0))
    bcast = lambda col: jnp.broadcast_to(info[:, col][:, None, None], (n, 1, LANES))
    return (scatter_dest, gather_dest, tile_expert.astype(jnp.int32), n_used.astype(jnp.int32),
            bcast(ROUTE_W1), bcast(ROUTE_W2))


def _interleave_gate_up(wg, wu, tf):
    e, d, ff = wg.shape
    g = wg.astype(BF16).reshape(e, d, ff // tf, tf)
    u = wu.astype(BF16).reshape(e, d, ff // tf, tf)
    return jnp.concatenate([g, u], axis=3).reshape(e, d, 2 * ff)


def _dense_block_diag(w):
    nb, bd, _ = w.shape
    eye = jnp.eye(nb, dtype=w.dtype)
    return (w[:, :, None, :] * eye[:, None, :, None]).reshape(nb * bd, nb * bd)


def _pack_lru_gates(wr, wi, windows):
    dr = _dense_block_diag(wr).astype(BF16)
    di = _dense_block_diag(wi).astype(BF16)
    cols = []
    for (c0, nc, _, _) in windows:
        cols += [dr[:, c0:c0 + nc], di[:, c0:c0 + nc]]
    return jnp.concatenate(cols, axis=1)


def _ffn_chunk(ff):
    for tf in (896, 512, 384, 256, 128):
        if ff % tf == 0:
            return tf
    raise ValueError(f"unsupported d_ff {ff}")


def kernel(x, meta_tokens, mix_norm_even, w_in_even, conv_w, conv_b, conv_ln_g, conv_ln_b, q_norm_g, k_norm_g, w_out_even, ffn_norm_even, ffn_w_gate, ffn_w_up, ffn_w_down, mix_norm_odd, w_in_odd, lru_conv_w, lru_conv_b, gate_r_w, gate_r_b, gate_i_w, gate_i_b, lru_lambda, w_out_odd, ffn_norm_odd, router_w, moe_w_gate, moe_w_up, moe_w_down):
    bsz, seq, d = x.shape
    t_real = N_META + seq
    tp = -(-t_real // TIME_TILE) * TIME_TILE
    n = bsz * tp
    assert n % ROW_TILE == 0
    depth = mix_norm_even.shape[0] + mix_norm_odd.shape[0]

    meta = jnp.broadcast_to(meta_tokens[None].astype(x.dtype), (bsz, N_META, d))
    h = jnp.concatenate([meta, x, jnp.zeros((bsz, tp - t_real, d), x.dtype)], axis=1).reshape(n, d)

    head_mean = jnp.kron(jnp.eye(MXU_DIM // SB_HEAD_DIM, dtype=F32),
                         jnp.full((SB_HEAD_DIM, SB_HEAD_DIM), 1.0 / SB_HEAD_DIM, F32)).astype(BF16)
    kk = jnp.arange(ATT_TK)
    neg_tri = -(kk[:, None] >= kk[None, :]).astype(BF16)
    row2 = lambda a: a.reshape(1, -1)

    for layer in range(depth):
        p = layer // 2
        if layer % 2 == 0:
            u, q, k, v = _even_in_proj(h, row2(mix_norm_even[p]), w_in_even[p].astype(BF16),
                                       row2(jnp.tile(q_norm_g[p], SB_HEADS)),
                                       row2(jnp.tile(k_norm_g[p], SB_HEADS)), head_mean)
            u = _conv_module(u.reshape(bsz, tp, CONV_CH), conv_w[p], row2(conv_b[p]),
                             row2(conv_ln_g[p]), row2(conv_ln_b[p]))
            o = _attention(q.reshape(bsz, tp, SB_WIDTH), k.reshape(bsz, tp, SB_WIDTH),
                           v.reshape(bsz, tp, SB_WIDTH), neg_tri)
            h, xn = _even_out_proj(h, u.reshape(n, CONV_CH), o.reshape(n, SB_WIDTH),
                                   w_out_even[p].astype(BF16), row2(ffn_norm_even[p]))
            tf = _ffn_chunk(ffn_w_gate.shape[-1])
            h = _ffn(xn, h, _interleave_gate_up(ffn_w_gate[p][None], ffn_w_up[p][None], tf)[0],
                     ffn_w_down[p].astype(BF16), tf)
        else:
            lw = lru_lambda.shape[-1]
            windows = _gate_windows(lw, lw // LRU_BLOCKS)
            gate, xb = _odd_in_proj(h, row2(mix_norm_odd[p]), w_in_odd[p].astype(BF16))
            h = _lru_mixer(xb.reshape(bsz, tp, lw), gate.reshape(bsz, tp, lw), h.reshape(bsz, tp, d),
                           lru_conv_w[p], row2(lru_conv_b[p]),
                           _pack_lru_gates(gate_r_w[p], gate_i_w[p], windows),
                           row2(gate_r_b[p]), row2(gate_i_b[p]), row2(lru_lambda[p]),
                           w_out_odd[p].astype(BF16), windows).reshape(n, d)
            rw = jnp.pad(router_w[p], ((0, 0), (0, LANES - N_EXPERTS)))
            kk = jnp.arange(ROW_TILE)
            ltri = (kk[:, None] > kk[None, :]).astype(BF16)
            xn, info, counts = _router(h, row2(ffn_norm_odd[p]), rw, ltri, tp, t_real)
            tg = MOE_TILE
            n_rows = -(-(2 * bsz * t_real + N_EXPERTS * (tg - 1)) // tg) * tg
            sdest, gdest, tile_expert, n_used, w1b, w2b = _routing_tables(info, counts, n_rows)
            sub = d // LANES
            assert sub == SUBLANES
            n_alloc = n_rows + 2 * COMBINE_TILE
            xs = _row_scatter(sdest, xn.reshape(n, sub, LANES), n_alloc).reshape(n_alloc, d)
            tf = _ffn_chunk(moe_w_gate.shape[-1])
            ys = _grouped_ffn(tile_expert, n_used, xs, n_rows,
                              _interleave_gate_up(moe_w_gate[p], moe_w_up[p], tf),
                              moe_w_down[p].astype(BF16), tf)
            h = _combine(gdest, h.reshape(n, sub, LANES), w1b, w2b,
                         ys.reshape(n_rows, sub, LANES)).reshape(n, d)
    return h.reshape(bsz, tp, d)[:, N_META:t_real]
```

```python
import functools
import math

import jax
import jax.numpy as jnp
from jax import lax
from jax.experimental import pallas as pl
from jax.experimental.pallas import tpu as pltpu

F32 = jnp.float32
BF16 = jnp.bfloat16

EPS = 1e-6
N_META = 16
CONV_CH = 512
CONV_WIDTH = 31
SB_HEADS = 8
SB_HEAD_DIM = 64
SB_WIDTH = SB_HEADS * SB_HEAD_DIM
LRU_BLOCKS = 16
LRU_CONV_WIDTH = 4
LRU_C = 8.0
N_EXPERTS = 8

LANES = 128
MXU_DIM = 256
TIME_TILE = 256
VMEM_LIMIT = 50 * 1024 * 1024

ROW_TILE = 512
CONV_TT = 128
CONV_HALO = 32
CONV_CHUNK = 32
ATT_TQ = 256
ATT_TK = 256
LOG2E = 1.4426950408889634
MASKED_EXPONENT = -1e30
SOFTPLUS_CUTOFF = 126.0
LRU_TT = 256
LRU_GATE_TILE = 256
MOE_TILE = 512
MOE_FF_TILE = 1792
COMBINE_TILE = 512
SUBLANES = 8


def _dot(a, b):
    return jnp.dot(a, b, preferred_element_type=F32)


def _params(*sem):
    return pltpu.CompilerParams(dimension_semantics=sem, vmem_limit_bytes=VMEM_LIMIT)


def _rms(x, g):
    ms = jnp.mean(x * x, axis=-1, keepdims=True)
    return x * lax.rsqrt(ms + EPS) * g


def _even_in_kernel(h_ref, g_ref, w_ref, qg_ref, kg_ref, hm_ref, u_ref, q_ref, k_ref, v_ref):
    xn = _rms(h_ref[...], g_ref[...]).astype(BF16)
    a = _dot(xn, w_ref[:, 0:CONV_CH])
    gate = _dot(xn, w_ref[:, CONV_CH:2 * CONV_CH])
    u_ref[...] = (a * jax.nn.sigmoid(gate)).astype(BF16)

    def head_norm(y, gain):
        yy = (y * y).astype(BF16)
        parts = [_dot(yy[:, c:c + MXU_DIM], hm_ref[...]) for c in range(0, SB_WIDTH, MXU_DIM)]
        ms = jnp.concatenate(parts, axis=1)
        return y * lax.rsqrt(ms + EPS) * gain

    c0 = 2 * CONV_CH
    q = head_norm(_dot(xn, w_ref[:, c0:c0 + SB_WIDTH]), qg_ref[...])
    q_ref[...] = (q * (LOG2E / math.sqrt(SB_HEAD_DIM))).astype(BF16)
    k = head_norm(_dot(xn, w_ref[:, c0 + SB_WIDTH:c0 + 2 * SB_WIDTH]), kg_ref[...])
    k_ref[...] = k.astype(BF16)
    v_ref[...] = _dot(xn, w_ref[:, c0 + 2 * SB_WIDTH:c0 + 3 * SB_WIDTH]).astype(BF16)


def _even_in_proj(h, g, w, qg, kg, hm):
    n, d = h.shape
    tm = ROW_TILE
    row = lambda i: (i, 0)
    const = lambda i: (0, 0)
    out = jax.ShapeDtypeStruct((n, SB_WIDTH), BF16)
    return pl.pallas_call(
        _even_in_kernel,
        grid=(n // tm,),
        in_specs=[pl.BlockSpec((tm, d), row), pl.BlockSpec((1, d), const),
                  pl.BlockSpec(w.shape, const), pl.BlockSpec((1, SB_WIDTH), const),
                  pl.BlockSpec((1, SB_WIDTH), const), pl.BlockSpec(hm.shape, const)],
        out_specs=[pl.BlockSpec((tm, SB_WIDTH), row)] * 4,
        out_shape=[out] * 4,
        compiler_params=_params("parallel"),
        name="even_in_proj",
    )(h, g, w, qg, kg, hm)


def _conv_kernel(cur_ref, halo_ref, w_ref, b_ref, lg_ref, lb_ref, o_ref, buf_ref, sh_ref):
    i = pl.program_id(1)
    tt = cur_ref.shape[1]
    rows = CONV_HALO + tt
    halo = halo_ref[0].astype(F32)
    buf_ref[0:CONV_HALO, :] = jnp.where(i > 0, halo, 0.0)
    buf_ref[CONV_HALO:rows, :] = cur_ref[0].astype(F32)
    for s in range(1, SUBLANES):
        sh_ref[s - 1, 0:rows - SUBLANES, :] = buf_ref[s:s + rows - SUBLANES, :]
    off = CONV_HALO - (CONV_WIDTH - 1)
    for c in range(tt // CONV_CHUNK):
        r0 = c * CONV_CHUNK
        acc = jnp.broadcast_to(b_ref[...], (CONV_CHUNK, CONV_CH))
        for k in range(CONV_WIDTH):
            shift = (off + k) % SUBLANES
            base = r0 + off + k - shift
            if shift == 0:
                tap = buf_ref[base:base + CONV_CHUNK, :]
            else:
                tap = sh_ref[shift - 1, base:base + CONV_CHUNK, :]
            acc = acc + w_ref[k:k + 1, :] * tap
        mu = jnp.mean(acc, axis=-1, keepdims=True)
        xc = acc - mu
        var = jnp.mean(xc * xc, axis=-1, keepdims=True)
        y = xc * lax.rsqrt(var + EPS) * lg_ref[...] + lb_ref[...]
        o_ref[0, r0:r0 + CONV_CHUNK, :] = (y * jax.nn.sigmoid(y)).astype(BF16)


def _conv_module(u, w, b, lg, lb):
    bsz, tp, c = u.shape
    tt = CONV_TT
    per = tt // CONV_HALO
    const = lambda bi, i: (0, 0)
    return pl.pallas_call(
        _conv_kernel,
        grid=(bsz, tp // tt),
        in_specs=[pl.BlockSpec((1, tt, c), lambda bi, i: (bi, i, 0)),
                  pl.BlockSpec((1, CONV_HALO, c), lambda bi, i: (bi, jnp.maximum(i * per - 1, 0), 0)),
                  pl.BlockSpec(w.shape, const), pl.BlockSpec((1, c), const),
                  pl.BlockSpec((1, c), const), pl.BlockSpec((1, c), const)],
        out_specs=pl.BlockSpec((1, tt, c), lambda bi, i: (bi, i, 0)),
        out_shape=jax.ShapeDtypeStruct(u.shape, BF16),
        scratch_shapes=[pltpu.VMEM((CONV_HALO + tt, c), F32),
                        pltpu.VMEM((SUBLANES - 1, CONV_HALO + tt - SUBLANES, c), F32)],
        compiler_params=_params("parallel", "parallel"),
        name="conv_module",
    )(u, u, w, b, lg, lb)


def _attn_kernel(q_ref, k_ref, v_ref, tri_ref, o_ref, q2_ref, e_ref, tot_ref, acc_ref, car_ref):
    i = pl.program_id(1)
    tq, tk = ATT_TQ, ATT_TK
    npairs = q_ref.shape[2] // LANES
    lane_q = lax.broadcasted_iota(jnp.int32, (tq, LANES), 1)

    for hp in range(npairs):
        q = q_ref[0, :, hp * LANES:(hp + 1) * LANES]
        zero = jnp.zeros_like(q)
        q2_ref[hp, 0:tq, :] = jnp.where(lane_q < SB_HEAD_DIM, q, zero)
        q2_ref[hp, tq:2 * tq, :] = jnp.where(lane_q >= SB_HEAD_DIM, q, zero)
        car_ref[hp] = jnp.zeros((2 * tq, LANES), F32)
        acc_ref[hp] = jnp.zeros((2 * tq, LANES), F32)

    def stage_a(j, mask):
        k0 = pl.multiple_of(j * tk, tk)
        for hp in range(npairs):
            kt = k_ref[0, pl.ds(k0, tk), hp * LANES:(hp + 1) * LANES]
            z = lax.dot_general(q2_ref[hp], kt, (((1,), (1,)), ((), ())), preferred_element_type=F32)
            sp = jnp.where(z > SOFTPLUS_CUTOFF, z, jnp.log(1.0 + jnp.exp2(z)) * LOG2E)
            if mask is not None:
                sp = jnp.where(mask, sp, 0.0)
            e = z + _dot(sp.astype(BF16), tri_ref[...])
            if mask is not None:
                e = jnp.where(mask, e, MASKED_EXPONENT)
            e_ref[hp] = e
            tot_ref[hp] = jnp.broadcast_to(jnp.sum(sp, axis=1, keepdims=True), (2 * tq, LANES))

    def stage_b(j):
        k0 = pl.multiple_of(j * tk, tk)
        for hp in range(npairs):
            car = car_ref[hp]
            w = jnp.exp2(e_ref[hp] + jnp.concatenate([car] * (tk // LANES), axis=1)).astype(BF16)
            vt = v_ref[0, pl.ds(k0, tk), hp * LANES:(hp + 1) * LANES]
            acc_ref[hp] += _dot(w, vt)
            car_ref[hp] = car - tot_ref[hp]

    j_last = (i * tq) // tk
    row = lax.broadcasted_iota(jnp.int32, (2 * tq, tk), 0)
    col = lax.broadcasted_iota(jnp.int32, (2 * tq, tk), 1)
    row = jnp.where(row >= tq, row - tq, row) + i * tq
    stage_a(j_last, (col + j_last * tk) < row)

    def body(jj, carry):
        j = j_last - 1 - jj
        stage_b(j + 1)
        stage_a(j, None)
        return carry

    lax.fori_loop(0, j_last, body, 0)
    stage_b(0)
    for hp in range(npairs):
        o_ref[0, :, hp * LANES:(hp + 1) * LANES] = jnp.where(
            lane_q < SB_HEAD_DIM, acc_ref[hp, 0:tq, :], acc_ref[hp, tq:2 * tq, :]).astype(BF16)


def _attention(q, k, v, tri):
    bsz, tp, width = q.shape
    tq, tk = ATT_TQ, ATT_TK
    npairs = width // LANES
    return pl.pallas_call(
        _attn_kernel,
        grid=(bsz, tp // tq),
        in_specs=[pl.BlockSpec((1, tq, width), lambda b, i: (b, i, 0)),
                  pl.BlockSpec((1, tp, width), lambda b, i: (b, 0, 0)),
                  pl.BlockSpec((1, tp, width), lambda b, i: (b, 0, 0)),
                  pl.BlockSpec(tri.shape, lambda b, i: (0, 0))],
        out_specs=pl.BlockSpec((1, tq, width), lambda b, i: (b, i, 0)),
        out_shape=jax.ShapeDtypeStruct(q.shape, BF16),
        scratch_shapes=[pltpu.VMEM((npairs, 2 * tq, LANES), BF16),
                        pltpu.VMEM((npairs, 2 * tq, tk), F32),
                        pltpu.VMEM((npairs, 2 * tq, LANES), F32),
                        pltpu.VMEM((npairs, 2 * tq, LANES), F32),
                        pltpu.VMEM((npairs, 2 * tq, LANES), F32)],
        compiler_params=_params("parallel", "arbitrary"),
        name="stick_breaking",
    )(q, k, v, tri)


def _even_out_kernel(h_ref, u_ref, o_ref, w_ref, g_ref, h1_ref, xn_ref):
    h1 = h_ref[...] + _dot(u_ref[...], w_ref[0:CONV_CH, :]) + _dot(o_ref[...], w_ref[CONV_CH:, :])
    h1_ref[...] = h1
    xn_ref[...] = _rms(h1, g_ref[...]).astype(BF16)


def _even_out_proj(h, u, o, w, g):
    n, d = h.shape
    tm = ROW_TILE
    row = lambda i: (i, 0)
    const = lambda i: (0, 0)
    return pl.pallas_call(
        _even_out_kernel,
        grid=(n // tm,),
        in_specs=[pl.BlockSpec((tm, d), row), pl.BlockSpec((tm, CONV_CH), row),
                  pl.BlockSpec((tm, SB_WIDTH), row), pl.BlockSpec(w.shape, const),
                  pl.BlockSpec((1, d), const)],
        out_specs=[pl.BlockSpec((tm, d), row), pl.BlockSpec((tm, d), row)],
        out_shape=[jax.ShapeDtypeStruct((n, d), F32), jax.ShapeDtypeStruct((n, d), BF16)],
        compiler_params=_params("parallel"),
        name="even_out_proj",
    )(h, u, o, w, g)


def _ffn_kernel(xn_ref, res_ref, wgu_ref, wd_ref, o_ref, *, tf):
    @pl.when(pl.program_id(1) == 0)
    def _():
        o_ref[...] = res_ref[...]

    gu = _dot(xn_ref[...], wgu_ref[...])
    g = gu[:, 0:tf]
    mid = g * jax.nn.sigmoid(g) * gu[:, tf:2 * tf]
    o_ref[...] += _dot(mid.astype(BF16), wd_ref[...])


def _ffn(xn, res, wgu, wd, tf):
    n, d = xn.shape
    ff = wd.shape[0]
    tm = ROW_TILE
    row = lambda i, f: (i, 0)
    return pl.pallas_call(
        functools.partial(_ffn_kernel, tf=tf),
        grid=(n // tm, ff // tf),
        in_specs=[pl.BlockSpec((tm, d), row), pl.BlockSpec((tm, d), row),
                  pl.BlockSpec((d, 2 * tf), lambda i, f: (0, f)),
                  pl.BlockSpec((tf, d), lambda i, f: (f, 0))],
        out_specs=pl.BlockSpec((tm, d), row),
        out_shape=jax.ShapeDtypeStruct((n, d), F32),
        compiler_params=_params("parallel", "arbitrary"),
        name="swiglu",
    )(xn, res, wgu, wd)


def _row_scatter_kernel(dest_ref, x_hbm, init_hbm, o_hbm, xbuf, in_sems, row_sems):
    del init_hbm
    i = pl.program_id(0)
    last = pl.num_programs(0) - 1
    tm = xbuf.shape[1]

    def tile_in(tile, slab):
        return pltpu.make_async_copy(x_hbm.at[pl.ds(tile * tm, tm)], xbuf.at[slab], in_sems.at[slab])

    def rows_out(sem):
        return pltpu.make_async_copy(xbuf.at[0], o_hbm.at[pl.ds(0, tm)], row_sems.at[sem])

    @pl.when(i == 0)
    def _():
        tile_in(0, 0).start()

    @pl.when(i < last)
    def _():
        tile_in(i + 1, (i + 1) % 3).start()

    slab = i % 3
    tile_in(i, slab).wait()
    for k in range(2):
        def issue(r, c, k=k):
            row = dest_ref[(i * 2 + k) * tm + r]
            pltpu.make_async_copy(xbuf.at[slab, r], o_hbm.at[row], row_sems.at[i % 2]).start()
            return c
        lax.fori_loop(0, tm, issue, 0, unroll=8)

    @pl.when(i > 0)
    def _():
        for k in range(2):
            rows_out((i - 1) % 2).wait()

    @pl.when(i == last)
    def _():
        for k in range(2):
            rows_out(i % 2).wait()


def _row_scatter(dest, x3, n_rows_alloc):
    n = x3.shape[0]
    tm = COMBINE_TILE
    init = jnp.zeros((n_rows_alloc,) + x3.shape[1:], x3.dtype)
    return pl.pallas_call(
        _row_scatter_kernel,
        grid_spec=pltpu.PrefetchScalarGridSpec(
            num_scalar_prefetch=1, grid=(n // tm,),
            in_specs=[pl.BlockSpec(memory_space=pl.ANY), pl.BlockSpec(memory_space=pl.ANY)],
            out_specs=pl.BlockSpec(memory_space=pl.ANY),
            scratch_shapes=[pltpu.VMEM((3, tm) + x3.shape[1:], x3.dtype),
                            pltpu.SemaphoreType.DMA((3,)), pltpu.SemaphoreType.DMA((2,))]),
        out_shape=jax.ShapeDtypeStruct(init.shape, init.dtype),
        input_output_aliases={2: 0},
        compiler_params=_params("arbitrary"),
        name="moe_row_scatter",
    )(dest, x3, init)


def _gmm_kernel(te_ref, nu_ref, xs_ref, wg_ref, wu_ref, wd_ref, o_ref, acc_ref):
    p = pl.program_id(0)
    f = pl.program_id(1)
    last = pl.num_programs(1) - 1
    used = p < nu_ref[0]

    @pl.when(used)
    def _():
        x = xs_ref[...]
        g = _dot(x, wg_ref[0])
        mid = g * jax.nn.sigmoid(g) * _dot(x, wu_ref[0])
        part = _dot(mid.astype(BF16), wd_ref[0])

        @pl.when(f == 0)
        def _():
            acc_ref[...] = part

        @pl.when(f > 0)
        def _():
            acc_ref[...] += part

        @pl.when(f == last)
        def _():
            o_ref[...] = acc_ref[...].astype(o_ref.dtype)

    @pl.when(jnp.logical_not(used) & (f == last))
    def _():
        o_ref[...] = jnp.zeros(o_ref.shape, o_ref.dtype)


def _grouped_ffn(tile_expert, n_used, xs, n_rows, wg, wu, wd):
    d = xs.shape[1]
    ff = wd.shape[1]
    tf = MOE_FF_TILE
    nf = ff // tf
    tg = MOE_TILE
    row = lambda p, f, te, nu: (p, 0)
    fidx = lambda p, f, nu: jnp.where(p < nu[0], f, nf - 1)
    up = pl.BlockSpec((1, d, tf), lambda p, f, te, nu: (te[p], 0, fidx(p, f, nu)))
    return pl.pallas_call(
        _gmm_kernel,
        grid_spec=pltpu.PrefetchScalarGridSpec(
            num_scalar_prefetch=2, grid=(n_rows // tg, nf),
            in_specs=[pl.BlockSpec((tg, d), row), up, up,
                      pl.BlockSpec((1, tf, d), lambda p, f, te, nu: (te[p], fidx(p, f, nu), 0))],
            out_specs=pl.BlockSpec((tg, d), row),
            scratch_shapes=[pltpu.VMEM((tg, d), F32)]),
        out_shape=jax.ShapeDtypeStruct((n_rows, d), BF16),
        compiler_params=_params("arbitrary", "arbitrary"),
        name="moe_grouped_ffn",
    )(tile_expert, n_used, xs, wg, wu, wd)


def _combine_kernel(dest_ref, h_ref, w1_ref, w2_ref, ys_hbm, o_ref, ybuf, sems):
    i = pl.program_id(0)
    tm = h_ref.shape[0]

    def fetch(tile, slot):
        for k in range(2):
            def issue(r, c, k=k):
                row = dest_ref[(tile * 2 + k) * tm + r]
                pltpu.make_async_copy(ys_hbm.at[row], ybuf.at[slot, k, r], sems.at[slot]).start()
                return c
            lax.fori_loop(0, tm, issue, 0, unroll=8)

    @pl.when(i == 0)
    def _():
        fetch(0, 0)

    @pl.when(i + 1 < pl.num_programs(0))
    def _():
        fetch(i + 1, (i + 1) % 2)

    slot = i % 2
    for k in range(2):
        pltpu.make_async_copy(ys_hbm.at[pl.ds(0, tm)], ybuf.at[slot, k], sems.at[slot]).wait()
    w1 = w1_ref[...][:, None, :]
    w2 = w2_ref[...][:, None, :]
    o_ref[...] = h_ref[...] + w1 * ybuf[slot, 0].astype(F32) + w2 * ybuf[slot, 1].astype(F32)


def _combine(dest, h3, w1b, w2b, ys3):
    n = h3.shape[0]
    tm = COMBINE_TILE
    blk = lambda i, dest: (i, 0, 0)
    return pl.pallas_call(
        _combine_kernel,
        grid_spec=pltpu.PrefetchScalarGridSpec(
            num_scalar_prefetch=1, grid=(n // tm,),
            in_specs=[pl.BlockSpec((tm,) + h3.shape[1:], blk), pl.BlockSpec((tm, LANES), lambda i, dest: (i, 0)),
                      pl.BlockSpec((tm, LANES), lambda i, dest: (i, 0)), pl.BlockSpec(memory_space=pl.ANY)],
            out_specs=pl.BlockSpec((tm,) + h3.shape[1:], blk),
            scratch_shapes=[pltpu.VMEM((2, 2, tm) + ys3.shape[1:], ys3.dtype), pltpu.SemaphoreType.DMA((2,))]),
        out_shape=jax.ShapeDtypeStruct(h3.shape, F32),
        compiler_params=_params("arbitrary"),
        name="moe_combine",
    )(dest, h3, w1b, w2b, ys3)


def _odd_in_kernel(h_ref, g_ref, w_ref, gate_ref, xb_ref):
    xn = _rms(h_ref[...], g_ref[...]).astype(BF16)
    y = _dot(xn, w_ref[...])
    lw = gate_ref.shape[1]
    gate_ref[...] = jax.nn.gelu(y[:, 0:lw], approximate=True).astype(BF16)
    xb_ref[...] = y[:, lw:2 * lw].astype(BF16)


def _odd_in_proj(h, g, w):
    n, d = h.shape
    lw = w.shape[1] // 2
    tm = ROW_TILE
    row = lambda i: (i, 0)
    const = lambda i: (0, 0)
    out = jax.ShapeDtypeStruct((n, lw), BF16)
    return pl.pallas_call(
        _odd_in_kernel,
        grid=(n // tm,),
        in_specs=[pl.BlockSpec((tm, d), row), pl.BlockSpec((1, d), const), pl.BlockSpec(w.shape, const)],
        out_specs=[pl.BlockSpec((tm, lw), row)] * 2,
        out_shape=[out] * 2,
        compiler_params=_params("parallel"),
        name="odd_in_proj",
    )(h, g, w)


def _gate_windows(lw, bd):
    out = []
    for c0 in range(0, lw, LRU_GATE_TILE):
        nc = min(LRU_GATE_TILE, lw - c0)
        r0 = (c0 // bd) * bd
        r1 = ((c0 + nc - 1) // bd + 1) * bd
        r0 = (r0 // LANES) * LANES
        r1 = min(-(-r1 // LANES) * LANES, lw)
        out.append((c0, nc, r0, r1 - r0))
    return out


def _lru_kernel(xb_ref, gate_ref, h_ref, cw_ref, cb_ref, wg_ref, br_ref, bi_ref, lam_ref, wo_ref,
                o_ref, xbuf_ref, hs_ref, a_ref, b_ref, y_ref, *, windows):
    ti = pl.program_id(1)
    tt, lw = a_ref.shape

    @pl.when(ti == 0)
    def _():
        xbuf_ref[0:8, :] = jnp.zeros((8, lw), F32)
        hs_ref[...] = jnp.zeros((8, lw), F32)

    @pl.when(ti > 0)
    def _():
        xbuf_ref[0:8, :] = xbuf_ref[tt:tt + 8, :]

    xbuf_ref[8:8 + tt, :] = xb_ref[0].astype(F32)
    off = 8 - (LRU_CONV_WIDTH - 1)
    xc = jnp.broadcast_to(cb_ref[...], (tt, lw))
    for k in range(LRU_CONV_WIDTH):
        xc = xc + cw_ref[k:k + 1, :] * xbuf_ref[off + k:off + k + tt, :]
    xcb = xc.astype(BF16)
    sp_lam = jnp.log(1.0 + jnp.exp(-lam_ref[...]))
    wcol = 0
    for (c0, nc, r0, nr) in windows:
        rg = _dot(xcb[:, r0:r0 + nr], wg_ref[r0:r0 + nr, wcol:wcol + 2 * nc])
        wcol += 2 * nc
        r = jax.nn.sigmoid(rg[:, 0:nc] + br_ref[:, c0:c0 + nc])
        ig = jax.nn.sigmoid(rg[:, nc:2 * nc] + bi_ref[:, c0:c0 + nc])
        a = jnp.exp(-LRU_C * r * sp_lam[:, c0:c0 + nc])
        a_ref[:, c0:c0 + nc] = a
        b_ref[:, c0:c0 + nc] = jnp.sqrt(1.0 - a * a) * (ig * xc[:, c0:c0 + nc])

    rowi = lax.broadcasted_iota(jnp.int32, (8, lw), 0)

    def group(gi, hprev):
        r0 = pl.multiple_of(gi * 8, 8)
        a = a_ref[pl.ds(r0, 8), :]
        b = b_ref[pl.ds(r0, 8), :]
        for s in (1, 2, 4):
            m = rowi >= s
            a_sh = jnp.where(m, pltpu.roll(a, s, axis=0), 1.0)
            b_sh = jnp.where(m, pltpu.roll(b, s, axis=0), 0.0)
            b = a * b_sh + b
            a = a * a_sh
        hrows = a * hprev + b
        y_ref[pl.ds(r0, 8), :] = hrows
        return jnp.broadcast_to(hrows[7:8, :], (8, lw))

    hs_ref[...] = lax.fori_loop(0, tt // 8, group, hs_ref[...])
    gy = (gate_ref[0].astype(F32) * y_ref[...]).astype(BF16)
    o_ref[0] = h_ref[0] + _dot(gy, wo_ref[...])


def _lru_mixer(xb, gate, h, cw, cb, wg, br, bi, lam, wo, windows):
    bsz, tp, lw = xb.shape
    d = h.shape[-1]
    tt = LRU_TT
    const = lambda b, t: (0, 0)
    blk = lambda b, t: (b, t, 0)
    return pl.pallas_call(
        functools.partial(_lru_kernel, windows=windows),
        grid=(bsz, tp // tt),
        in_specs=[pl.BlockSpec((1, tt, lw), blk), pl.BlockSpec((1, tt, lw), blk),
                  pl.BlockSpec((1, tt, d), blk), pl.BlockSpec(cw.shape, const),
                  pl.BlockSpec((1, lw), const), pl.BlockSpec(wg.shape, const),
                  pl.BlockSpec((1, lw), const), pl.BlockSpec((1, lw), const),
                  pl.BlockSpec((1, lw), const), pl.BlockSpec(wo.shape, const)],
        out_specs=pl.BlockSpec((1, tt, d), blk),
        out_shape=jax.ShapeDtypeStruct(h.shape, F32),
        scratch_shapes=[pltpu.VMEM((8 + tt, lw), F32), pltpu.VMEM((8, lw), F32),
                        pltpu.VMEM((tt, lw), F32), pltpu.VMEM((tt, lw), F32), pltpu.VMEM((tt, lw), F32)],
        compiler_params=_params("arbitrary", "arbitrary"),
        name="rglru_mixer",
    )(xb, gate, h, cw, cb, wg, br, bi, lam, wo)


ROUTE_E1, ROUTE_E2, ROUTE_R1, ROUTE_R2, ROUTE_VALID = range(5)


def _router_kernel(h_ref, g_ref, rw_ref, ltri_ref, xn_ref, info_ref, w1_ref, w2_ref, cnt_ref, run_ref,
                   *, tp, t_real):
    i = pl.program_id(0)
    tm = h_ref.shape[0]

    @pl.when(i == 0)
    def _():
        run_ref[...] = jnp.zeros(run_ref.shape, F32)

    xn = _rms(h_ref[...], g_ref[...])
    xn_ref[...] = xn.astype(BF16)
    logits = jnp.dot(xn, rw_ref[...], preferred_element_type=F32, precision=lax.Precision.HIGHEST)
    lane = lax.broadcasted_iota(jnp.int32, logits.shape, 1)
    neg = jnp.float32(-jnp.inf)
    logits = jnp.where(lane < N_EXPERTS, logits, neg)
    top1 = jnp.max(logits, axis=1, keepdims=True)
    idx1 = jnp.min(jnp.where(logits == top1, lane, LANES), axis=1, keepdims=True)
    rest = jnp.where(lane == idx1, neg, logits)
    top2 = jnp.max(rest, axis=1, keepdims=True)
    idx2 = jnp.min(jnp.where(rest == top2, lane, LANES), axis=1, keepdims=True)
    w1 = 1.0 / (1.0 + jnp.exp(top2 - top1))

    rowg = (lax.broadcasted_iota(jnp.int32, (tm, 1), 0) + i * tm).astype(F32)
    pos = rowg - jnp.floor((rowg + 0.5) / tp) * tp
    valid = jnp.where(pos < t_real, 1.0, 0.0)
    onehot = (jnp.where(lane == idx1, 1.0, 0.0) + jnp.where(lane == idx2, 1.0, 0.0)) * valid
    before = _dot(ltri_ref[...], onehot.astype(BF16)) + run_ref[...]
    rank1 = jnp.sum(jnp.where(lane == idx1, before, 0.0), axis=1, keepdims=True)
    rank2 = jnp.sum(jnp.where(lane == idx2, before, 0.0), axis=1, keepdims=True)
    run_ref[...] += jnp.sum(onehot, axis=0, keepdims=True)
    cnt_ref[...] = run_ref[...]
    w1_ref[...] = jnp.broadcast_to(w1, (tm, LANES))
    w2_ref[...] = jnp.broadcast_to(1.0 - w1, (tm, LANES))
    fields = {ROUTE_E1: idx1.astype(F32), ROUTE_E2: idx2.astype(F32),
              ROUTE_R1: rank1, ROUTE_R2: rank2, ROUTE_VALID: valid}
    info = jnp.zeros(logits.shape, F32)
    for col, val in fields.items():
        info = jnp.where(lane == col, val, info)
    info_ref[...] = info.T[0:SUBLANES, :]


def _router(h, g, rw, ltri, tp, t_real):
    n, d = h.shape
    tm = ROW_TILE
    row = lambda i: (i, 0)
    const = lambda i: (0, 0)
    return pl.pallas_call(
        functools.partial(_router_kernel, tp=tp, t_real=t_real),
        grid=(n // tm,),
        in_specs=[pl.BlockSpec((tm, d), row), pl.BlockSpec((1, d), const), pl.BlockSpec(rw.shape, const),
                  pl.BlockSpec(ltri.shape, const)],
        out_specs=[pl.BlockSpec((tm, d), row), pl.BlockSpec((SUBLANES, tm), row),
                   pl.BlockSpec((tm, LANES), row), pl.BlockSpec((tm, LANES), row),
                   pl.BlockSpec((1, LANES), const)],
        out_shape=[jax.ShapeDtypeStruct((n, d), BF16), jax.ShapeDtypeStruct((n // tm * SUBLANES, tm), F32),
                   jax.ShapeDtypeStruct((n, LANES), F32), jax.ShapeDtypeStruct((n, LANES), F32),
                   jax.ShapeDtypeStruct((1, LANES), F32)],
        scratch_shapes=[pltpu.VMEM((1, LANES), F32)],
        compiler_params=_params("arbitrary"),
        name="router",
    )(h, g, rw, ltri)


def _routing_tables(info, counts, n_rows):
    tm = ROW_TILE
    assert tm == COMBINE_TILE
    tg = MOE_TILE
    info = info.reshape(-1, SUBLANES, tm)
    field = lambda r: info[:, r, :].astype(jnp.int32)
    valid = info[:, ROUTE_VALID, :] > 0.5
    cnt = counts[0, :N_EXPERTS].astype(jnp.int32)
    padded = (cnt + tg - 1) // tg * tg
    ends = jnp.cumsum(padded)
    starts = ends - padded
    experts = jnp.arange(N_EXPERTS, dtype=jnp.int32)
    start_of = lambda e: jnp.sum(jnp.where(e[..., None] == experts, starts, 0), axis=-1)
    d1 = start_of(field(ROUTE_E1)) + field(ROUTE_R1)
    d2 = start_of(field(ROUTE_E2)) + field(ROUTE_R2)
    tile_start = jnp.arange(n_rows // tg, dtype=jnp.int32) * tg
    tile_expert = jnp.minimum(jnp.sum(tile_start[:, None] >= ends[None, :], axis=1), N_EXPERTS - 1)
    n_used = (ends[-1] // tg).reshape(1)
    per_tile = lambda a, b: jnp.stack([a, b], axis=1).reshape(-1)
    dump = n_rows + jnp.arange(tm, dtype=jnp.int32)[None, :]
    scatter_dest = per_tile(jnp.where(valid, d1, dump), jnp.where(valid, d2, dump + tm))
    gather_dest = per_tile(jnp.where(valid, d1, 0), jnp.where(valid, d2, 0))
    return scatter_dest, gather_dest, tile_expert.astype(jnp.int32), n_used.astype(jnp.int32)


def _interleave_gate_up(wg, wu, tf):
    e, d, ff = wg.shape
    g = wg.astype(BF16).reshape(e, d, ff // tf, tf)
    u = wu.astype(BF16).reshape(e, d, ff // tf, tf)
    return jnp.concatenate([g, u], axis=3).reshape(e, d, 2 * ff)


def _dense_block_diag(w):
    nb, bd, _ = w.shape
    eye = jnp.eye(nb, dtype=w.dtype)
    return (w[:, :, None, :] * eye[:, None, :, None]).reshape(nb * bd, nb * bd)


def _pack_lru_gates(wr, wi, windows):
    dr = _dense_block_diag(wr).astype(BF16)
    di = _dense_block_diag(wi).astype(BF16)
    cols = []
    for (c0, nc, _, _) in windows:
        cols += [dr[:, c0:c0 + nc], di[:, c0:c0 + nc]]
    return jnp.concatenate(cols, axis=1)


def _ffn_chunk(ff):
    for tf in (896, 512, 384, 256, 128):
        if ff % tf == 0:
            return tf
    raise ValueError(f"unsupported d_ff {ff}")


def kernel(x, meta_tokens, mix_norm_even, w_in_even, conv_w, conv_b, conv_ln_g, conv_ln_b, q_norm_g, k_norm_g, w_out_even, ffn_norm_even, ffn_w_gate, ffn_w_up, ffn_w_down, mix_norm_odd, w_in_odd, lru_conv_w, lru_conv_b, gate_r_w, gate_r_b, gate_i_w, gate_i_b, lru_lambda, w_out_odd, ffn_norm_odd, router_w, moe_w_gate, moe_w_up, moe_w_down):
    bsz, seq, d = x.shape
    t_real = N_META + seq
    tp = -(-t_real // TIME_TILE) * TIME_TILE
    n = bsz * tp
    assert n % ROW_TILE == 0
    depth = mix_norm_even.shape[0] + mix_norm_odd.shape[0]

    meta = jnp.broadcast_to(meta_tokens[None].astype(x.dtype), (bsz, N_META, d))
    h = jnp.concatenate([meta, x, jnp.zeros((bsz, tp - t_real, d), x.dtype)], axis=1).reshape(n, d)

    head_mean = jnp.kron(jnp.eye(MXU_DIM // SB_HEAD_DIM, dtype=F32),
                         jnp.full((SB_HEAD_DIM, SB_HEAD_DIM), 1.0 / SB_HEAD_DIM, F32)).astype(BF16)
    kk = jnp.arange(ATT_TK)
    neg_tri = -(kk[:, None] >= kk[None, :]).astype(BF16)
    row2 = lambda a: a.reshape(1, -1)

    for layer in range(depth):
        p = layer // 2
        if layer % 2 == 0:
            u, q, k, v = _even_in_proj(h, row2(mix_norm_even[p]), w_in_even[p].astype(BF16),
                                       row2(jnp.tile(q_norm_g[p], SB_HEADS)),
                                       row2(jnp.tile(k_norm_g[p], SB_HEADS)), head_mean)
            u = _conv_module(u.reshape(bsz, tp, CONV_CH), conv_w[p], row2(conv_b[p]),
                             row2(conv_ln_g[p]), row2(conv_ln_b[p]))
            o = _attention(q.reshape(bsz, tp, SB_WIDTH), k.reshape(bsz, tp, SB_WIDTH),
                           v.reshape(bsz, tp, SB_WIDTH), neg_tri)
            h, xn = _even_out_proj(h, u.reshape(n, CONV_CH), o.reshape(n, SB_WIDTH),
                                   w_out_even[p].astype(BF16), row2(ffn_norm_even[p]))
            tf = _ffn_chunk(ffn_w_gate.shape[-1])
            h = _ffn(xn, h, _interleave_gate_up(ffn_w_gate[p][None], ffn_w_up[p][None], tf)[0],
                     ffn_w_down[p].astype(BF16), tf)
        else:
            lw = lru_lambda.shape[-1]
            windows = _gate_windows(lw, lw // LRU_BLOCKS)
            gate, xb = _odd_in_proj(h, row2(mix_norm_odd[p]), w_in_odd[p].astype(BF16))
            h = _lru_mixer(xb.reshape(bsz, tp, lw), gate.reshape(bsz, tp, lw), h.reshape(bsz, tp, d),
                           lru_conv_w[p], row2(lru_conv_b[p]),
                           _pack_lru_gates(gate_r_w[p], gate_i_w[p], windows),
                           row2(gate_r_b[p]), row2(gate_i_b[p]), row2(lru_lambda[p]),
                           w_out_odd[p].astype(BF16), windows).reshape(n, d)
            rw = jnp.pad(router_w[p], ((0, 0), (0, LANES - N_EXPERTS)))
            kk = jnp.arange(ROW_TILE)
            ltri = (kk[:, None] > kk[None, :]).astype(BF16)
            xn, info, w1b, w2b, counts = _router(h, row2(ffn_norm_odd[p]), rw, ltri, tp, t_real)
            tg = MOE_TILE
            n_rows = -(-(2 * bsz * t_real + N_EXPERTS * (tg - 1)) // tg) * tg
            sdest, gdest, tile_expert, n_used = _routing_tables(info, counts, n_rows)
            sub = d // LANES
            assert sub == SUBLANES
            n_alloc = n_rows + 2 * COMBINE_TILE
            xs = _row_scatter(sdest, xn.reshape(n, sub, LANES), n_alloc).reshape(n_alloc, d)
            ys = _grouped_ffn(tile_expert, n_used, xs, n_rows, moe_w_gate[p].astype(BF16),
                              moe_w_up[p].astype(BF16), moe_w_down[p].astype(BF16))
            h = _combine(gdest, h.reshape(n, sub, LANES), w1b, w2b,
                         ys.reshape(n_rows, sub, LANES)).reshape(n, d)
    return h.reshape(bsz, tp, d)[:, N_META:t_real]
```

```python
import functools
import math

import jax
import jax.numpy as jnp
from jax import lax
from jax.experimental import pallas as pl
from jax.experimental.pallas import tpu as pltpu

F32 = jnp.float32
BF16 = jnp.bfloat16

EPS = 1e-6
N_META = 16
CONV_CH = 512
CONV_WIDTH = 31
SB_HEADS = 8
SB_HEAD_DIM = 64
SB_WIDTH = SB_HEADS * SB_HEAD_DIM
LRU_BLOCKS = 16
LRU_CONV_WIDTH = 4
LRU_C = 8.0
N_EXPERTS = 8

LANES = 128
MXU_DIM = 256
TIME_TILE = 256
VMEM_LIMIT = 50 * 1024 * 1024

ROW_TILE = 512
CONV_TT = 128
CONV_HALO = 32
CONV_CHUNK = 32
ATT_TQ = 256
ATT_TK = 256
LOG2E = 1.4426950408889634
MASKED_EXPONENT = -1e30
SOFTPLUS_CUTOFF = 126.0
LRU_TT = 256
LRU_GATE_TILE = 256
MOE_TILE = 512
MOE_FF_TILE = 1792
COMBINE_TILE = 512
SUBLANES = 8
DMA_GROUP = 8


def _dot(a, b):
    return jnp.dot(a, b, preferred_element_type=F32)


def _params(*sem):
    return pltpu.CompilerParams(dimension_semantics=sem, vmem_limit_bytes=VMEM_LIMIT)


def _sigmoid(x):
    return 0.5 * jnp.tanh(0.5 * x) + 0.5


def _rms(x, g):
    ms = jnp.mean(x * x, axis=-1, keepdims=True)
    return x * lax.rsqrt(ms + EPS) * g


def _even_in_kernel(h_ref, g_ref, w_ref, qg_ref, kg_ref, hm_ref, u_ref, q_ref, k_ref, v_ref):
    xn = _rms(h_ref[...], g_ref[...]).astype(BF16)
    a = _dot(xn, w_ref[:, 0:CONV_CH])
    gate = _dot(xn, w_ref[:, CONV_CH:2 * CONV_CH])
    u_ref[...] = (a * _sigmoid(gate)).astype(BF16)

    def head_norm(y, gain):
        yy = (y * y).astype(BF16)
        parts = [_dot(yy[:, c:c + MXU_DIM], hm_ref[...]) for c in range(0, SB_WIDTH, MXU_DIM)]
        ms = jnp.concatenate(parts, axis=1)
        return y * lax.rsqrt(ms + EPS) * gain

    c0 = 2 * CONV_CH
    q = head_norm(_dot(xn, w_ref[:, c0:c0 + SB_WIDTH]), qg_ref[...])
    q_ref[...] = (q * (LOG2E / math.sqrt(SB_HEAD_DIM))).astype(BF16)
    k = head_norm(_dot(xn, w_ref[:, c0 + SB_WIDTH:c0 + 2 * SB_WIDTH]), kg_ref[...])
    k_ref[...] = k.astype(BF16)
    v_ref[...] = _dot(xn, w_ref[:, c0 + 2 * SB_WIDTH:c0 + 3 * SB_WIDTH]).astype(BF16)


def _even_in_proj(h, g, w, qg, kg, hm):
    n, d = h.shape
    tm = ROW_TILE
    row = lambda i: (i, 0)
    const = lambda i: (0, 0)
    out = jax.ShapeDtypeStruct((n, SB_WIDTH), BF16)
    return pl.pallas_call(
        _even_in_kernel,
        grid=(n // tm,),
        in_specs=[pl.BlockSpec((tm, d), row), pl.BlockSpec((1, d), const),
                  pl.BlockSpec(w.shape, const), pl.BlockSpec((1, SB_WIDTH), const),
                  pl.BlockSpec((1, SB_WIDTH), const), pl.BlockSpec(hm.shape, const)],
        out_specs=[pl.BlockSpec((tm, SB_WIDTH), row)] * 4,
        out_shape=[out] * 4,
        compiler_params=_params("parallel"),
        name="even_in_proj",
    )(h, g, w, qg, kg, hm)


def _conv_kernel(cur_ref, halo_ref, w_ref, b_ref, lg_ref, lb_ref, o_ref, buf_ref, sh_ref):
    i = pl.program_id(1)
    tt = cur_ref.shape[1]
    rows = CONV_HALO + tt
    halo = halo_ref[0].astype(F32)
    buf_ref[0:CONV_HALO, :] = jnp.where(i > 0, halo, 0.0)
    buf_ref[CONV_HALO:rows, :] = cur_ref[0].astype(F32)
    for s in range(1, SUBLANES):
        sh_ref[s - 1, 0:rows - SUBLANES, :] = buf_ref[s:s + rows - SUBLANES, :]
    off = CONV_HALO - (CONV_WIDTH - 1)
    for c in range(tt // CONV_CHUNK):
        r0 = c * CONV_CHUNK
        acc = jnp.broadcast_to(b_ref[...], (CONV_CHUNK, CONV_CH))
        for k in range(CONV_WIDTH):
            shift = (off + k) % SUBLANES
            base = r0 + off + k - shift
            if shift == 0:
                tap = buf_ref[base:base + CONV_CHUNK, :]
            else:
                tap = sh_ref[shift - 1, base:base + CONV_CHUNK, :]
            acc = acc + w_ref[k:k + 1, :] * tap
        mu = jnp.mean(acc, axis=-1, keepdims=True)
        xc = acc - mu
        var = jnp.mean(xc * xc, axis=-1, keepdims=True)
        y = xc * lax.rsqrt(var + EPS) * lg_ref[...] + lb_ref[...]
        o_ref[0, r0:r0 + CONV_CHUNK, :] = (y * _sigmoid(y)).astype(BF16)


def _conv_module(u, w, b, lg, lb):
    bsz, tp, c = u.shape
    tt = CONV_TT
    per = tt // CONV_HALO
    const = lambda bi, i: (0, 0)
    return pl.pallas_call(
        _conv_kernel,
        grid=(bsz, tp // tt),
        in_specs=[pl.BlockSpec((1, tt, c), lambda bi, i: (bi, i, 0)),
                  pl.BlockSpec((1, CONV_HALO, c), lambda bi, i: (bi, jnp.maximum(i * per - 1, 0), 0)),
                  pl.BlockSpec(w.shape, const), pl.BlockSpec((1, c), const),
                  pl.BlockSpec((1, c), const), pl.BlockSpec((1, c), const)],
        out_specs=pl.BlockSpec((1, tt, c), lambda bi, i: (bi, i, 0)),
        out_shape=jax.ShapeDtypeStruct(u.shape, BF16),
        scratch_shapes=[pltpu.VMEM((CONV_HALO + tt, c), F32),
                        pltpu.VMEM((SUBLANES - 1, CONV_HALO + tt - SUBLANES, c), F32)],
        compiler_params=_params("parallel", "parallel"),
        name="conv_module",
    )(u, u, w, b, lg, lb)


def _attn_kernel(q_ref, k_ref, v_ref, tri_ref, o_ref, q2_ref, e_ref, tot_ref, acc_ref, car_ref):
    i = pl.program_id(1)
    tq, tk = ATT_TQ, ATT_TK
    npairs = q_ref.shape[2] // LANES
    lane_q = lax.broadcasted_iota(jnp.int32, (tq, LANES), 1)

    for hp in range(npairs):
        q = q_ref[0, :, hp * LANES:(hp + 1) * LANES]
        zero = jnp.zeros_like(q)
        q2_ref[hp, 0:tq, :] = jnp.where(lane_q < SB_HEAD_DIM, q, zero)
        q2_ref[hp, tq:2 * tq, :] = jnp.where(lane_q >= SB_HEAD_DIM, q, zero)
        car_ref[hp] = jnp.zeros((2 * tq, LANES), F32)
        acc_ref[hp] = jnp.zeros((2 * tq, LANES), F32)

    def stage_a(j, mask):
        k0 = pl.multiple_of(j * tk, tk)
        for hp in range(npairs):
            kt = k_ref[0, pl.ds(k0, tk), hp * LANES:(hp + 1) * LANES]
            z = lax.dot_general(q2_ref[hp], kt, (((1,), (1,)), ((), ())), preferred_element_type=F32)
            sp = jnp.where(z > SOFTPLUS_CUTOFF, z, jnp.log(1.0 + jnp.exp2(z)) * LOG2E)
            if mask is not None:
                sp = jnp.where(mask, sp, 0.0)
            e = z + _dot(sp.astype(BF16), tri_ref[...])
            if mask is not None:
                e = jnp.where(mask, e, MASKED_EXPONENT)
            e_ref[hp] = e
            tot_ref[hp] = jnp.broadcast_to(jnp.sum(sp, axis=1, keepdims=True), (2 * tq, LANES))

    def stage_b(j):
        k0 = pl.multiple_of(j * tk, tk)
        for hp in range(npairs):
            car = car_ref[hp]
            w = jnp.exp2(e_ref[hp] + jnp.concatenate([car] * (tk // LANES), axis=1)).astype(BF16)
            vt = v_ref[0, pl.ds(k0, tk), hp * LANES:(hp + 1) * LANES]
            acc_ref[hp] += _dot(w, vt)
            car_ref[hp] = car - tot_ref[hp]

    j_last = (i * tq) // tk
    row = lax.broadcasted_iota(jnp.int32, (2 * tq, tk), 0)
    col = lax.broadcasted_iota(jnp.int32, (2 * tq, tk), 1)
    row = jnp.where(row >= tq, row - tq, row) + i * tq
    stage_a(j_last, (col + j_last * tk) < row)

    def body(jj, carry):
        j = j_last - 1 - jj
        stage_b(j + 1)
        stage_a(j, None)
        return carry

    lax.fori_loop(0, j_last, body, 0)
    stage_b(0)
    for hp in range(npairs):
        o_ref[0, :, hp * LANES:(hp + 1) * LANES] = jnp.where(
            lane_q < SB_HEAD_DIM, acc_ref[hp, 0:tq, :], acc_ref[hp, tq:2 * tq, :]).astype(BF16)


def _attention(q, k, v, tri):
    bsz, tp, width = q.shape
    tq, tk = ATT_TQ, ATT_TK
    npairs = width // LANES
    return pl.pallas_call(
        _attn_kernel,
        grid=(bsz, tp // tq),
        in_specs=[pl.BlockSpec((1, tq, width), lambda b, i: (b, i, 0)),
                  pl.BlockSpec((1, tp, width), lambda b, i: (b, 0, 0)),
                  pl.BlockSpec((1, tp, width), lambda b, i: (b, 0, 0)),
                  pl.BlockSpec(tri.shape, lambda b, i: (0, 0))],
        out_specs=pl.BlockSpec((1, tq, width), lambda b, i: (b, i, 0)),
        out_shape=jax.ShapeDtypeStruct(q.shape, BF16),
        scratch_shapes=[pltpu.VMEM((npairs, 2 * tq, LANES), BF16),
                        pltpu.VMEM((npairs, 2 * tq, tk), F32),
                        pltpu.VMEM((npairs, 2 * tq, LANES), F32),
                        pltpu.VMEM((npairs, 2 * tq, LANES), F32),
                        pltpu.VMEM((npairs, 2 * tq, LANES), F32)],
        compiler_params=_params("parallel", "arbitrary"),
        name="stick_breaking",
    )(q, k, v, tri)


def _even_ffn_kernel(h_ref, u_ref, o_ref, wo_ref, g_ref, wgu_ref, wd_ref, out_ref, xn_ref, *, tf):
    @pl.when(pl.program_id(1) == 0)
    def _():
        h1 = h_ref[...] + _dot(u_ref[...], wo_ref[0:CONV_CH, :]) + _dot(o_ref[...], wo_ref[CONV_CH:, :])
        out_ref[...] = h1
        xn_ref[...] = _rms(h1, g_ref[...]).astype(BF16)

    gu = _dot(xn_ref[...], wgu_ref[...])
    g = gu[:, 0:tf]
    mid = g * _sigmoid(g) * gu[:, tf:2 * tf]
    out_ref[...] += _dot(mid.astype(BF16), wd_ref[...])


def _even_ffn(h, u, o, wo, g, wgu, wd, tf):
    n, d = h.shape
    ff = wd.shape[0]
    tm = ROW_TILE
    row = lambda i, f: (i, 0)
    const = lambda i, f: (0, 0)
    return pl.pallas_call(
        functools.partial(_even_ffn_kernel, tf=tf),
        grid=(n // tm, ff // tf),
        in_specs=[pl.BlockSpec((tm, d), row), pl.BlockSpec((tm, CONV_CH), row),
                  pl.BlockSpec((tm, SB_WIDTH), row), pl.BlockSpec(wo.shape, const),
                  pl.BlockSpec((1, d), const),
                  pl.BlockSpec((d, 2 * tf), lambda i, f: (0, f)),
                  pl.BlockSpec((tf, d), lambda i, f: (f, 0))],
        out_specs=pl.BlockSpec((tm, d), row),
        out_shape=jax.ShapeDtypeStruct((n, d), F32),
        scratch_shapes=[pltpu.VMEM((tm, d), BF16)],
        compiler_params=_params("parallel", "arbitrary"),
        name="even_out_swiglu",
    )(h, u, o, wo, g, wgu, wd)


def _row_scatter_kernel(dest_ref, x_hbm, init_hbm, o_hbm, xbuf, in_sems, row_sems):
    del init_hbm
    i = pl.program_id(0)
    last = pl.num_programs(0) - 1
    tm = xbuf.shape[1]

    def tile_in(tile, slab):
        return pltpu.make_async_copy(x_hbm.at[pl.ds(tile * tm, tm)], xbuf.at[slab], in_sems.at[slab])

    def rows_out(sem):
        return pltpu.make_async_copy(xbuf.at[0], o_hbm.at[pl.ds(0, tm)], row_sems.at[sem])

    @pl.when(i == 0)
    def _():
        tile_in(0, 0).start()

    @pl.when(i < last)
    def _():
        tile_in(i + 1, (i + 1) % 3).start()

    slab = i % 3
    tile_in(i, slab).wait()
    for k in range(2):
        def issue(grp, c, k=k):
            for j in range(DMA_GROUP):
                r = grp * DMA_GROUP + j
                row = dest_ref[(i * 2 + k) * tm + r]
                pltpu.make_async_copy(xbuf.at[slab, r], o_hbm.at[row], row_sems.at[i % 2]).start(priority=j % 2)
            return c
        lax.fori_loop(0, tm // DMA_GROUP, issue, 0)

    @pl.when(i > 0)
    def _():
        for k in range(2):
            rows_out((i - 1) % 2).wait()

    @pl.when(i == last)
    def _():
        for k in range(2):
            rows_out(i % 2).wait()


def _row_scatter(dest, x3, n_rows_alloc):
    n = x3.shape[0]
    tm = COMBINE_TILE
    init = jnp.zeros((n_rows_alloc,) + x3.shape[1:], x3.dtype)
    return pl.pallas_call(
        _row_scatter_kernel,
        grid_spec=pltpu.PrefetchScalarGridSpec(
            num_scalar_prefetch=1, grid=(n // tm,),
            in_specs=[pl.BlockSpec(memory_space=pl.ANY), pl.BlockSpec(memory_space=pl.ANY)],
            out_specs=pl.BlockSpec(memory_space=pl.ANY),
            scratch_shapes=[pltpu.VMEM((3, tm) + x3.shape[1:], x3.dtype),
                            pltpu.SemaphoreType.DMA((3,)), pltpu.SemaphoreType.DMA((2,))]),
        out_shape=jax.ShapeDtypeStruct(init.shape, init.dtype),
        input_output_aliases={2: 0},
        compiler_params=_params("arbitrary"),
        name="moe_row_scatter",
    )(dest, x3, init)


def _gmm_kernel(te_ref, nu_ref, xs_ref, wg_ref, wu_ref, wd_ref, o_ref, acc_ref):
    p = pl.program_id(0)
    f = pl.program_id(1)
    last = pl.num_programs(1) - 1
    used = p < nu_ref[0]

    @pl.when(used)
    def _():
        x = xs_ref[...]
        g = _dot(x, wg_ref[0])
        mid = g * _sigmoid(g) * _dot(x, wu_ref[0])
        part = _dot(mid.astype(BF16), wd_ref[0])

        @pl.when(f == 0)
        def _():
            acc_ref[...] = part

        @pl.when(f > 0)
        def _():
            acc_ref[...] += part

        @pl.when(f == last)
        def _():
            o_ref[...] = acc_ref[...].astype(o_ref.dtype)

    @pl.when(jnp.logical_not(used) & (f == last))
    def _():
        o_ref[...] = jnp.zeros(o_ref.shape, o_ref.dtype)


def _grouped_ffn(tile_expert, n_used, xs, n_rows, wg, wu, wd):
    d = xs.shape[1]
    ff = wd.shape[1]
    tf = MOE_FF_TILE
    nf = ff // tf
    tg = MOE_TILE
    row = lambda p, f, te, nu: (p, 0)
    fidx = lambda p, f, nu: jnp.where(p < nu[0], f, nf - 1)
    up = pl.BlockSpec((1, d, tf), lambda p, f, te, nu: (te[p], 0, fidx(p, f, nu)))
    return pl.pallas_call(
        _gmm_kernel,
        grid_spec=pltpu.PrefetchScalarGridSpec(
            num_scalar_prefetch=2, grid=(n_rows // tg, nf),
            in_specs=[pl.BlockSpec((tg, d), row), up, up,
                      pl.BlockSpec((1, tf, d), lambda p, f, te, nu: (te[p], fidx(p, f, nu), 0))],
            out_specs=pl.BlockSpec((tg, d), row),
            scratch_shapes=[pltpu.VMEM((tg, d), F32)]),
        out_shape=jax.ShapeDtypeStruct((n_rows, d), BF16),
        compiler_params=_params("arbitrary", "arbitrary"),
        name="moe_grouped_ffn",
    )(tile_expert, n_used, xs, wg, wu, wd)


def _combine_kernel(dest_ref, h_ref, w1_ref, w2_ref, ys_hbm, o_ref, ybuf, sems):
    i = pl.program_id(0)
    tm = h_ref.shape[0]

    def fetch(tile, slot):
        for k in range(2):
            def issue(grp, c, k=k):
                for j in range(DMA_GROUP):
                    r = grp * DMA_GROUP + j
                    row = dest_ref[(tile * 2 + k) * tm + r]
                    pltpu.make_async_copy(ys_hbm.at[row], ybuf.at[slot, k, r], sems.at[slot]).start(priority=j % 2)
                return c
            lax.fori_loop(0, tm // DMA_GROUP, issue, 0)

    @pl.when(i == 0)
    def _():
        fetch(0, 0)

    @pl.when(i + 1 < pl.num_programs(0))
    def _():
        fetch(i + 1, (i + 1) % 2)

    slot = i % 2
    for k in range(2):
        pltpu.make_async_copy(ys_hbm.at[pl.ds(0, tm)], ybuf.at[slot, k], sems.at[slot]).wait()
    w1 = w1_ref[...][:, None, :]
    w2 = w2_ref[...][:, None, :]
    o_ref[...] = h_ref[...] + w1 * ybuf[slot, 0].astype(F32) + w2 * ybuf[slot, 1].astype(F32)


def _combine(dest, h3, w1b, w2b, ys3):
    n = h3.shape[0]
    tm = COMBINE_TILE
    blk = lambda i, dest: (i, 0, 0)
    return pl.pallas_call(
        _combine_kernel,
        grid_spec=pltpu.PrefetchScalarGridSpec(
            num_scalar_prefetch=1, grid=(n // tm,),
            in_specs=[pl.BlockSpec((tm,) + h3.shape[1:], blk), pl.BlockSpec((tm, LANES), lambda i, dest: (i, 0)),
                      pl.BlockSpec((tm, LANES), lambda i, dest: (i, 0)), pl.BlockSpec(memory_space=pl.ANY)],
            out_specs=pl.BlockSpec((tm,) + h3.shape[1:], blk),
            scratch_shapes=[pltpu.VMEM((2, 2, tm) + ys3.shape[1:], ys3.dtype), pltpu.SemaphoreType.DMA((2,))]),
        out_shape=jax.ShapeDtypeStruct(h3.shape, F32),
        compiler_params=_params("arbitrary"),
        name="moe_combine",
    )(dest, h3, w1b, w2b, ys3)


def _odd_in_kernel(h_ref, g_ref, w_ref, gate_ref, xb_ref):
    xn = _rms(h_ref[...], g_ref[...]).astype(BF16)
    y = _dot(xn, w_ref[...])
    lw = gate_ref.shape[1]
    gate_ref[...] = jax.nn.gelu(y[:, 0:lw], approximate=True).astype(BF16)
    xb_ref[...] = y[:, lw:2 * lw].astype(BF16)


def _odd_in_proj(h, g, w):
    n, d = h.shape
    lw = w.shape[1] // 2
    tm = ROW_TILE
    row = lambda i: (i, 0)
    const = lambda i: (0, 0)
    out = jax.ShapeDtypeStruct((n, lw), BF16)
    return pl.pallas_call(
        _odd_in_kernel,
        grid=(n // tm,),
        in_specs=[pl.BlockSpec((tm, d), row), pl.BlockSpec((1, d), const), pl.BlockSpec(w.shape, const)],
        out_specs=[pl.BlockSpec((tm, lw), row)] * 2,
        out_shape=[out] * 2,
        compiler_params=_params("parallel"),
        name="odd_in_proj",
    )(h, g, w)


def _gate_windows(lw, bd):
    out = []
    for c0 in range(0, lw, LRU_GATE_TILE):
        nc = min(LRU_GATE_TILE, lw - c0)
        r0 = (c0 // bd) * bd
        r1 = ((c0 + nc - 1) // bd + 1) * bd
        r0 = (r0 // LANES) * LANES
        r1 = min(-(-r1 // LANES) * LANES, lw)
        out.append((c0, nc, r0, r1 - r0))
    return out


def _lru_kernel(xb_ref, gate_ref, h_ref, cw_ref, cb_ref, wg_ref, br_ref, bi_ref, lam_ref, wo_ref,
                o_ref, xbuf_ref, hs_ref, a_ref, b_ref, y_ref, *, windows):
    ti = pl.program_id(1)
    tt, lw = a_ref.shape

    @pl.when(ti == 0)
    def _():
        xbuf_ref[0:8, :] = jnp.zeros((8, lw), F32)
        hs_ref[...] = jnp.zeros((8, lw), F32)

    @pl.when(ti > 0)
    def _():
        xbuf_ref[0:8, :] = xbuf_ref[tt:tt + 8, :]

    xbuf_ref[8:8 + tt, :] = xb_ref[0].astype(F32)
    off = 8 - (LRU_CONV_WIDTH - 1)
    xc = jnp.broadcast_to(cb_ref[...], (tt, lw))
    for k in range(LRU_CONV_WIDTH):
        xc = xc + cw_ref[k:k + 1, :] * xbuf_ref[off + k:off + k + tt, :]
    xcb = xc.astype(BF16)
    sp_lam = jnp.log(1.0 + jnp.exp(-lam_ref[...]))
    log2_a_per_r = (-LRU_C * LOG2E) * sp_lam
    wcol = 0
    for (c0, nc, r0, nr) in windows:
        rg = _dot(xcb[:, r0:r0 + nr], wg_ref[r0:r0 + nr, wcol:wcol + 2 * nc])
        wcol += 2 * nc
        r = _sigmoid(rg[:, 0:nc] + br_ref[:, c0:c0 + nc])
        ig = _sigmoid(rg[:, nc:2 * nc] + bi_ref[:, c0:c0 + nc])
        a = jnp.exp2(r * log2_a_per_r[:, c0:c0 + nc])
        a_ref[:, c0:c0 + nc] = a
        one_m_a2 = 1.0 - a * a
        root = jnp.where(one_m_a2 > 0.0, one_m_a2 * lax.rsqrt(one_m_a2), 0.0)
        b_ref[:, c0:c0 + nc] = root * (ig * xc[:, c0:c0 + nc])

    rowi = lax.broadcasted_iota(jnp.int32, (8, lw), 0)

    def group(gi, hprev):
        r0 = pl.multiple_of(gi * 8, 8)
        a = a_ref[pl.ds(r0, 8), :]
        b = b_ref[pl.ds(r0, 8), :]
        for s in (1, 2, 4):
            m = rowi >= s
            a_sh = jnp.where(m, pltpu.roll(a, s, axis=0), 1.0)
            b_sh = jnp.where(m, pltpu.roll(b, s, axis=0), 0.0)
            b = a * b_sh + b
            a = a * a_sh
        hrows = a * hprev + b
        y_ref[pl.ds(r0, 8), :] = hrows
        return jnp.broadcast_to(hrows[7:8, :], (8, lw))

    hs_ref[...] = lax.fori_loop(0, tt // 8, group, hs_ref[...])
    gy = (gate_ref[0].astype(F32) * y_ref[...]).astype(BF16)
    o_ref[0] = h_ref[0] + _dot(gy, wo_ref[...])


def _lru_mixer(xb, gate, h, cw, cb, wg, br, bi, lam, wo, windows):
    bsz, tp, lw = xb.shape
    d = h.shape[-1]
    tt = LRU_TT
    const = lambda b, t: (0, 0)
    blk = lambda b, t: (b, t, 0)
    return pl.pallas_call(
        functools.partial(_lru_kernel, windows=windows),
        grid=(bsz, tp // tt),
        in_specs=[pl.BlockSpec((1, tt, lw), blk), pl.BlockSpec((1, tt, lw), blk),
                  pl.BlockSpec((1, tt, d), blk), pl.BlockSpec(cw.shape, const),
                  pl.BlockSpec((1, lw), const), pl.BlockSpec(wg.shape, const),
                  pl.BlockSpec((1, lw), const), pl.BlockSpec((1, lw), const),
                  pl.BlockSpec((1, lw), const), pl.BlockSpec(wo.shape, const)],
        out_specs=pl.BlockSpec((1, tt, d), blk),
        out_shape=jax.ShapeDtypeStruct(h.shape, F32),
        scratch_shapes=[pltpu.VMEM((8 + tt, lw), F32), pltpu.VMEM((8, lw), F32),
                        pltpu.VMEM((tt, lw), F32), pltpu.VMEM((tt, lw), F32), pltpu.VMEM((tt, lw), F32)],
        compiler_params=_params("arbitrary", "arbitrary"),
        name="rglru_mixer",
    )(xb, gate, h, cw, cb, wg, br, bi, lam, wo)


ROUTE_E1, ROUTE_E2, ROUTE_R1, ROUTE_R2, ROUTE_VALID = range(5)


def _router_kernel(h_ref, g_ref, rw_ref, ltri_ref, xn_ref, info_ref, w1_ref, w2_ref, cnt_ref, run_ref,
                   *, tp, t_real):
    i = pl.program_id(0)
    tm = h_ref.shape[0]

    @pl.when(i == 0)
    def _():
        run_ref[...] = jnp.zeros(run_ref.shape, F32)

    xn = _rms(h_ref[...], g_ref[...])
    xn_ref[...] = xn.astype(BF16)
    logits = jnp.dot(xn, rw_ref[...], preferred_element_type=F32, precision=lax.Precision.HIGHEST)
    lane = lax.broadcasted_iota(jnp.int32, logits.shape, 1)
    neg = jnp.float32(-jnp.inf)
    logits = jnp.where(lane < N_EXPERTS, logits, neg)
    top1 = jnp.max(logits, axis=1, keepdims=True)
    idx1 = jnp.min(jnp.where(logits == top1, lane, LANES), axis=1, keepdims=True)
    rest = jnp.where(lane == idx1, neg, logits)
    top2 = jnp.max(rest, axis=1, keepdims=True)
    idx2 = jnp.min(jnp.where(rest == top2, lane, LANES), axis=1, keepdims=True)
    w1 = 1.0 / (1.0 + jnp.exp(top2 - top1))

    rowg = (lax.broadcasted_iota(jnp.int32, (tm, 1), 0) + i * tm).astype(F32)
    pos = rowg - jnp.floor((rowg + 0.5) / tp) * tp
    valid = jnp.where(pos < t_real, 1.0, 0.0)
    onehot = (jnp.where(lane == idx1, 1.0, 0.0) + jnp.where(lane == idx2, 1.0, 0.0)) * valid
    before = _dot(ltri_ref[...], onehot.astype(BF16)) + run_ref[...]
    rank1 = jnp.sum(jnp.where(lane == idx1, before, 0.0), axis=1, keepdims=True)
    rank2 = jnp.sum(jnp.where(lane == idx2, before, 0.0), axis=1, keepdims=True)
    run_ref[...] += jnp.sum(onehot, axis=0, keepdims=True)
    cnt_ref[...] = run_ref[...]
    w1_ref[...] = jnp.broadcast_to(w1, (tm, LANES))
    w2_ref[...] = jnp.broadcast_to(1.0 - w1, (tm, LANES))
    fields = {ROUTE_E1: idx1.astype(F32), ROUTE_E2: idx2.astype(F32),
              ROUTE_R1: rank1, ROUTE_R2: rank2, ROUTE_VALID: valid}
    info = jnp.zeros(logits.shape, F32)
    for col, val in fields.items():
        info = jnp.where(lane == col, val, info)
    info_ref[...] = info.T[0:SUBLANES, :]


def _router(h, g, rw, ltri, tp, t_real):
    n, d = h.shape
    tm = ROW_TILE
    row = lambda i: (i, 0)
    const = lambda i: (0, 0)
    return pl.pallas_call(
        functools.partial(_router_kernel, tp=tp, t_real=t_real),
        grid=(n // tm,),
        in_specs=[pl.BlockSpec((tm, d), row), pl.BlockSpec((1, d), const), pl.BlockSpec(rw.shape, const),
                  pl.BlockSpec(ltri.shape, const)],
        out_specs=[pl.BlockSpec((tm, d), row), pl.BlockSpec((SUBLANES, tm), row),
                   pl.BlockSpec((tm, LANES), row), pl.BlockSpec((tm, LANES), row),
                   pl.BlockSpec((1, LANES), const)],
        out_shape=[jax.ShapeDtypeStruct((n, d), BF16), jax.ShapeDtypeStruct((n // tm * SUBLANES, tm), F32),
                   jax.ShapeDtypeStruct((n, LANES), F32), jax.ShapeDtypeStruct((n, LANES), F32),
                   jax.ShapeDtypeStruct((1, LANES), F32)],
        scratch_shapes=[pltpu.VMEM((1, LANES), F32)],
        compiler_params=_params("arbitrary"),
        name="router",
    )(h, g, rw, ltri)


def _routing_tables(info, counts, n_rows):
    tm = ROW_TILE
    assert tm == COMBINE_TILE
    tg = MOE_TILE
    info = info.reshape(-1, SUBLANES, tm)
    field = lambda r: info[:, r, :].astype(jnp.int32)
    valid = info[:, ROUTE_VALID, :] > 0.5
    cnt = counts[0, :N_EXPERTS].astype(jnp.int32)
    padded = (cnt + tg - 1) // tg * tg
    ends = jnp.cumsum(padded)
    starts = ends - padded
    experts = jnp.arange(N_EXPERTS, dtype=jnp.int32)
    start_of = lambda e: jnp.sum(jnp.where(e[..., None] == experts, starts, 0), axis=-1)
    d1 = start_of(field(ROUTE_E1)) + field(ROUTE_R1)
    d2 = start_of(field(ROUTE_E2)) + field(ROUTE_R2)
    tile_start = jnp.arange(n_rows // tg, dtype=jnp.int32) * tg
    tile_expert = jnp.minimum(jnp.sum(tile_start[:, None] >= ends[None, :], axis=1), N_EXPERTS - 1)
    n_used = (ends[-1] // tg).reshape(1)
    per_tile = lambda a, b: jnp.stack([a, b], axis=1).reshape(-1)
    dump = n_rows + jnp.arange(tm, dtype=jnp.int32)[None, :]
    scatter_dest = per_tile(jnp.where(valid, d1, dump), jnp.where(valid, d2, dump + tm))
    gather_dest = per_tile(jnp.where(valid, d1, 0), jnp.where(valid, d2, 0))
    return scatter_dest, gather_dest, tile_expert.astype(jnp.int32), n_used.astype(jnp.int32)


def _interleave_gate_up(wg, wu, tf):
    e, d, ff = wg.shape
    g = wg.astype(BF16).reshape(e, d, ff // tf, tf)
    u = wu.astype(BF16).reshape(e, d, ff // tf, tf)
    return jnp.concatenate([g, u], axis=3).reshape(e, d, 2 * ff)


def _dense_block_diag(w):
    nb, bd, _ = w.shape
    eye = jnp.eye(nb, dtype=w.dtype)
    return (w[:, :, None, :] * eye[:, None, :, None]).reshape(nb * bd, nb * bd)


def _pack_lru_gates(wr, wi, windows):
    dr = _dense_block_diag(wr).astype(BF16)
    di = _dense_block_diag(wi).astype(BF16)
    cols = []
    for (c0, nc, _, _) in windows:
        cols += [dr[:, c0:c0 + nc], di[:, c0:c0 + nc]]
    return jnp.concatenate(cols, axis=1)


def _ffn_chunk(ff):
    for tf in (896, 512, 384, 256, 128):
        if ff % tf == 0:
            return tf
    raise ValueError(f"unsupported d_ff {ff}")


def kernel(x, meta_tokens, mix_norm_even, w_in_even, conv_w, conv_b, conv_ln_g, conv_ln_b, q_norm_g, k_norm_g, w_out_even, ffn_norm_even, ffn_w_gate, ffn_w_up, ffn_w_down, mix_norm_odd, w_in_odd, lru_conv_w, lru_conv_b, gate_r_w, gate_r_b, gate_i_w, gate_i_b, lru_lambda, w_out_odd, ffn_norm_odd, router_w, moe_w_gate, moe_w_up, moe_w_down):
    bsz, seq, d = x.shape
    t_real = N_META + seq
    tp = -(-t_real // TIME_TILE) * TIME_TILE
    n = bsz * tp
    assert n % ROW_TILE == 0
    depth = mix_norm_even.shape[0] + mix_norm_odd.shape[0]

    meta = jnp.broadcast_to(meta_tokens[None].astype(x.dtype), (bsz, N_META, d))
    h = jnp.concatenate([meta, x, jnp.zeros((bsz, tp - t_real, d), x.dtype)], axis=1).reshape(n, d)

    head_mean = jnp.kron(jnp.eye(MXU_DIM // SB_HEAD_DIM, dtype=F32),
                         jnp.full((SB_HEAD_DIM, SB_HEAD_DIM), 1.0 / SB_HEAD_DIM, F32)).astype(BF16)
    kk = jnp.arange(ATT_TK)
    neg_tri = -(kk[:, None] >= kk[None, :]).astype(BF16)
    row2 = lambda a: a.reshape(1, -1)

    for layer in range(depth):
        p = layer // 2
        if layer % 2 == 0:
            u, q, k, v = _even_in_proj(h, row2(mix_norm_even[p]), w_in_even[p].astype(BF16),
                                       row2(jnp.tile(q_norm_g[p], SB_HEADS)),
                                       row2(jnp.tile(k_norm_g[p], SB_HEADS)), head_mean)
            u = _conv_module(u.reshape(bsz, tp, CONV_CH), conv_w[p], row2(conv_b[p]),
                             row2(conv_ln_g[p]), row2(conv_ln_b[p]))
            o = _attention(q.reshape(bsz, tp, SB_WIDTH), k.reshape(bsz, tp, SB_WIDTH),
                           v.reshape(bsz, tp, SB_WIDTH), neg_tri)
            tf = _ffn_chunk(ffn_w_gate.shape[-1])
            h = _even_ffn(h, u.reshape(n, CONV_CH), o.reshape(n, SB_WIDTH), w_out_even[p].astype(BF16),
                          row2(ffn_norm_even[p]),
                          _interleave_gate_up(ffn_w_gate[p][None], ffn_w_up[p][None], tf)[0],
                          ffn_w_down[p].astype(BF16), tf)
        else:
            lw = lru_lambda.shape[-1]
            windows = _gate_windows(lw, lw // LRU_BLOCKS)
            gate, xb = _odd_in_proj(h, row2(mix_norm_odd[p]), w_in_odd[p].astype(BF16))
            h = _lru_mixer(xb.reshape(bsz, tp, lw), gate.reshape(bsz, tp, lw), h.reshape(bsz, tp, d),
                           lru_conv_w[p], row2(lru_conv_b[p]),
                           _pack_lru_gates(gate_r_w[p], gate_i_w[p], windows),
                           row2(gate_r_b[p]), row2(gate_i_b[p]), row2(lru_lambda[p]),
                           w_out_odd[p].astype(BF16), windows).reshape(n, d)
            rw = jnp.pad(router_w[p], ((0, 0), (0, LANES - N_EXPERTS)))
            kk = jnp.arange(ROW_TILE)
            ltri = (kk[:, None] > kk[None, :]).astype(BF16)
            xn, info, w1b, w2b, counts = _router(h, row2(ffn_norm_odd[p]), rw, ltri, tp, t_real)
            tg = MOE_TILE
            n_rows = -(-(2 * bsz * t_real + N_EXPERTS * (tg - 1)) // tg) * tg
            sdest, gdest, tile_expert, n_used = _routing_tables(info, counts, n_rows)
            sub = d // LANES
            assert sub == SUBLANES
            n_alloc = n_rows + 2 * COMBINE_TILE
            xs = _row_scatter(sdest, xn.reshape(n, sub, LANES), n_alloc).reshape(n_alloc, d)
            ys = _grouped_ffn(tile_expert, n_used, xs, n_rows, moe_w_gate[p].astype(BF16),
                              moe_w_up[p].astype(BF16), moe_w_down[p].astype(BF16))
            h = _combine(gdest, h.reshape(n, sub, LANES), w1b, w2b,
                         ys.reshape(n_rows, sub, LANES)).reshape(n, d)
    return h.reshape(bsz, tp, d)[:, N_META:t_real]
```

```python
import functools
import math

import jax
import jax.numpy as jnp
from jax import lax
from jax.experimental import pallas as pl
from jax.experimental.pallas import tpu as pltpu

F32 = jnp.float32
BF16 = jnp.bfloat16

EPS = 1e-6
N_META = 16
CONV_CH = 512
CONV_WIDTH = 31
SB_HEADS = 8
SB_HEAD_DIM = 64
SB_WIDTH = SB_HEADS * SB_HEAD_DIM
LRU_BLOCKS = 16
LRU_CONV_WIDTH = 4
LRU_C = 8.0
N_EXPERTS = 8

LANES = 128
MXU_DIM = 256
TIME_TILE = 256
VMEM_LIMIT = 50 * 1024 * 1024

ROW_TILE = 512
CONV_TT = 128
CONV_HALO = 32
CONV_CHUNK = 32
ATT_TQ = 256
ATT_TK = 256
LOG2E = 1.4426950408889634
MASKED_EXPONENT = -1e30
SOFTPLUS_CUTOFF = 126.0
LRU_TT = 256
LRU_GATE_TILE = 256
MOE_TILE = 512
MOE_FF_TILE = 1792
COMBINE_TILE = 512
SUBLANES = 8
BF16_ROWS = 16
DMA_GROUP = 8


def _dot(a, b):
    return jnp.dot(a, b, preferred_element_type=F32)


def _params(*sem):
    return pltpu.CompilerParams(dimension_semantics=sem, vmem_limit_bytes=VMEM_LIMIT)


def _sigmoid(x):
    return 0.5 * jnp.tanh(0.5 * x) + 0.5


def _rms(x, g):
    ms = jnp.mean(x * x, axis=-1, keepdims=True)
    return x * lax.rsqrt(ms + EPS) * g


def _even_in_kernel(h_ref, g_ref, w_ref, qg_ref, kg_ref, hm_ref, u_ref, q_ref, k_ref, v_ref):
    xn = _rms(h_ref[...], g_ref[...]).astype(BF16)
    a = _dot(xn, w_ref[:, 0:CONV_CH])
    gate = _dot(xn, w_ref[:, CONV_CH:2 * CONV_CH])
    u_ref[...] = (a * _sigmoid(gate)).astype(BF16)

    def head_norm(y, gain):
        yy = (y * y).astype(BF16)
        parts = [_dot(yy[:, c:c + MXU_DIM], hm_ref[...]) for c in range(0, SB_WIDTH, MXU_DIM)]
        ms = jnp.concatenate(parts, axis=1)
        return y * lax.rsqrt(ms + EPS) * gain

    c0 = 2 * CONV_CH
    q = head_norm(_dot(xn, w_ref[:, c0:c0 + SB_WIDTH]), qg_ref[...])
    q_ref[...] = (q * (LOG2E / math.sqrt(SB_HEAD_DIM))).astype(BF16)
    k = head_norm(_dot(xn, w_ref[:, c0 + SB_WIDTH:c0 + 2 * SB_WIDTH]), kg_ref[...])
    k_ref[...] = k.astype(BF16)
    v_ref[...] = _dot(xn, w_ref[:, c0 + 2 * SB_WIDTH:c0 + 3 * SB_WIDTH]).astype(BF16)


def _even_in_proj(h, g, w, qg, kg, hm):
    n, d = h.shape
    tm = ROW_TILE
    row = lambda i: (i, 0)
    const = lambda i: (0, 0)
    out = jax.ShapeDtypeStruct((n, SB_WIDTH), BF16)
    return pl.pallas_call(
        _even_in_kernel,
        grid=(n // tm,),
        in_specs=[pl.BlockSpec((tm, d), row), pl.BlockSpec((1, d), const),
                  pl.BlockSpec(w.shape, const), pl.BlockSpec((1, SB_WIDTH), const),
                  pl.BlockSpec((1, SB_WIDTH), const), pl.BlockSpec(hm.shape, const)],
        out_specs=[pl.BlockSpec((tm, SB_WIDTH), row)] * 4,
        out_shape=[out] * 4,
        compiler_params=_params("parallel"),
        name="even_in_proj",
    )(h, g, w, qg, kg, hm)


def _conv_kernel(cur_ref, halo_ref, w_ref, b_ref, lg_ref, lb_ref, o_ref, buf_ref, sh_ref):
    i = pl.program_id(1)
    tt = cur_ref.shape[1]
    rows = CONV_HALO + tt
    halo = halo_ref[0].astype(F32)
    buf_ref[0:CONV_HALO, :] = jnp.where(i > 0, halo, 0.0)
    buf_ref[CONV_HALO:rows, :] = cur_ref[0].astype(F32)
    for s in range(1, SUBLANES):
        sh_ref[s - 1, 0:rows - SUBLANES, :] = buf_ref[s:s + rows - SUBLANES, :]
    off = CONV_HALO - (CONV_WIDTH - 1)
    for c in range(tt // CONV_CHUNK):
        r0 = c * CONV_CHUNK
        acc = jnp.broadcast_to(b_ref[...], (CONV_CHUNK, CONV_CH))
        for k in range(CONV_WIDTH):
            shift = (off + k) % SUBLANES
            base = r0 + off + k - shift
            if shift == 0:
                tap = buf_ref[base:base + CONV_CHUNK, :]
            else:
                tap = sh_ref[shift - 1, base:base + CONV_CHUNK, :]
            acc = acc + w_ref[k:k + 1, :] * tap
        mu = jnp.mean(acc, axis=-1, keepdims=True)
        xc = acc - mu
        var = jnp.mean(xc * xc, axis=-1, keepdims=True)
        y = xc * lax.rsqrt(var + EPS) * lg_ref[...] + lb_ref[...]
        o_ref[0, r0:r0 + CONV_CHUNK, :] = (y * _sigmoid(y)).astype(BF16)


def _conv_module(u, w, b, lg, lb):
    bsz, tp, c = u.shape
    tt = CONV_TT
    per = tt // CONV_HALO
    const = lambda bi, i: (0, 0)
    return pl.pallas_call(
        _conv_kernel,
        grid=(bsz, tp // tt),
        in_specs=[pl.BlockSpec((1, tt, c), lambda bi, i: (bi, i, 0)),
                  pl.BlockSpec((1, CONV_HALO, c), lambda bi, i: (bi, jnp.maximum(i * per - 1, 0), 0)),
                  pl.BlockSpec(w.shape, const), pl.BlockSpec((1, c), const),
                  pl.BlockSpec((1, c), const), pl.BlockSpec((1, c), const)],
        out_specs=pl.BlockSpec((1, tt, c), lambda bi, i: (bi, i, 0)),
        out_shape=jax.ShapeDtypeStruct(u.shape, BF16),
        scratch_shapes=[pltpu.VMEM((CONV_HALO + tt, c), F32),
                        pltpu.VMEM((SUBLANES - 1, CONV_HALO + tt - SUBLANES, c), F32)],
        compiler_params=_params("parallel", "parallel"),
        name="conv_module",
    )(u, u, w, b, lg, lb)


def _attn_kernel(*refs, tq, tile0, aliased, n_compute):
    if aliased:
        refs = refs[1:]
    o_ref = refs[4]
    step = pl.program_id(1)

    @pl.when(step < n_compute)
    def _():
        _attn_tile(*refs, tq=tq, row0=(step + tile0) * ATT_TQ)

    @pl.when(step >= n_compute)
    def _():
        o_ref[...] = jnp.zeros(o_ref.shape, BF16)


def _attn_tile(q_ref, k_ref, v_ref, tri_ref, o_ref, q2_ref, e_ref, tot_ref, acc_ref, car_ref, *, tq, row0):
    tk = ATT_TK
    npairs = q_ref.shape[2] // LANES
    lane_q = lax.broadcasted_iota(jnp.int32, (tq, LANES), 1)

    for hp in range(npairs):
        q = q_ref[0, 0:tq, hp * LANES:(hp + 1) * LANES]
        zero = jnp.zeros_like(q)
        q2_ref[hp, 0:tq, :] = jnp.where(lane_q < SB_HEAD_DIM, q, zero)
        q2_ref[hp, tq:2 * tq, :] = jnp.where(lane_q >= SB_HEAD_DIM, q, zero)
        car_ref[hp] = jnp.zeros((2 * tq, LANES), F32)
        acc_ref[hp] = jnp.zeros((2 * tq, LANES), F32)

    def stage_a(j, mask):
        k0 = pl.multiple_of(j * tk, tk)
        for hp in range(npairs):
            kt = k_ref[0, pl.ds(k0, tk), hp * LANES:(hp + 1) * LANES]
            z = lax.dot_general(q2_ref[hp], kt, (((1,), (1,)), ((), ())), preferred_element_type=F32)
            sp = jnp.where(z > SOFTPLUS_CUTOFF, z, jnp.log(1.0 + jnp.exp2(z)) * LOG2E)
            if mask is not None:
                sp = jnp.where(mask, sp, 0.0)
            e = z + _dot(sp.astype(BF16), tri_ref[...])
            if mask is not None:
                e = jnp.where(mask, e, MASKED_EXPONENT)
            e_ref[hp] = e
            tot_ref[hp] = jnp.broadcast_to(jnp.sum(sp, axis=1, keepdims=True), (2 * tq, LANES))

    def stage_b(j):
        k0 = pl.multiple_of(j * tk, tk)
        for hp in range(npairs):
            car = car_ref[hp]
            w = jnp.exp2(e_ref[hp] + jnp.concatenate([car] * (tk // LANES), axis=1)).astype(BF16)
            vt = v_ref[0, pl.ds(k0, tk), hp * LANES:(hp + 1) * LANES]
            acc_ref[hp] += _dot(w, vt)
            car_ref[hp] = car - tot_ref[hp]

    j_last = row0 // tk
    row = lax.broadcasted_iota(jnp.int32, (2 * tq, tk), 0)
    col = lax.broadcasted_iota(jnp.int32, (2 * tq, tk), 1)
    row = jnp.where(row >= tq, row - tq, row) + row0
    stage_a(j_last, (col + j_last * tk) < row)

    def body(jj, carry):
        j = j_last - 1 - jj
        stage_b(j + 1)
        stage_a(j, None)
        return carry

    lax.fori_loop(0, j_last, body, 0)
    stage_b(0)
    if tq < o_ref.shape[1]:
        o_ref[0, tq:, :] = jnp.zeros((o_ref.shape[1] - tq, o_ref.shape[2]), BF16)
    for hp in range(npairs):
        o_ref[0, 0:tq, hp * LANES:(hp + 1) * LANES] = jnp.where(
            lane_q < SB_HEAD_DIM, acc_ref[hp, 0:tq, :], acc_ref[hp, tq:2 * tq, :]).astype(BF16)


def _attention(q, k, v, tri, t_real):
    bsz, tp, width = q.shape
    bq, tk = ATT_TQ, ATT_TK
    assert bq <= tk and tk % bq == 0
    npairs = width // LANES
    n_full = t_real // bq
    rem = -(-(t_real - n_full * bq) // BF16_ROWS) * BF16_ROWS

    def call(tq, tile0, n_compute, n_tiles, prev):
        blk = lambda b, i: (b, i + tile0, 0)
        full = lambda b, i: (b, 0, 0)
        in_specs = [pl.BlockSpec((1, bq, width), blk), pl.BlockSpec((1, tp, width), full),
                    pl.BlockSpec((1, tp, width), full), pl.BlockSpec(tri.shape, lambda b, i: (0, 0))]
        args = [q, k, v, tri]
        if prev is not None:
            in_specs.insert(0, pl.BlockSpec(memory_space=pl.ANY))
            args.insert(0, prev)
        return pl.pallas_call(
            functools.partial(_attn_kernel, tq=tq, tile0=tile0, aliased=prev is not None, n_compute=n_compute),
            grid=(bsz, n_tiles),
            in_specs=in_specs,
            out_specs=pl.BlockSpec((1, bq, width), blk),
            out_shape=jax.ShapeDtypeStruct(q.shape, BF16),
            input_output_aliases={} if prev is None else {0: 0},
            scratch_shapes=[pltpu.VMEM((npairs, 2 * tq, LANES), BF16),
                            pltpu.VMEM((npairs, 2 * tq, tk), F32),
                            pltpu.VMEM((npairs, 2 * tq, LANES), F32),
                            pltpu.VMEM((npairs, 2 * tq, LANES), F32),
                            pltpu.VMEM((npairs, 2 * tq, LANES), F32)],
            compiler_params=_params("parallel", "arbitrary"),
            name="stick_breaking" if prev is None else "stick_breaking_tail",
        )(*args)

    out = call(bq, 0, n_full, tp // bq, None)
    if rem:
        out = call(rem, n_full, 1, 1, out)
    return out


def _even_ffn_kernel(h_ref, u_ref, o_ref, wo_ref, g_ref, wgu_ref, wd_ref, out_ref, xn_ref, *, tf):
    @pl.when(pl.program_id(1) == 0)
    def _():
        h1 = h_ref[...] + _dot(u_ref[...], wo_ref[0:CONV_CH, :]) + _dot(o_ref[...], wo_ref[CONV_CH:, :])
        out_ref[...] = h1
        xn_ref[...] = _rms(h1, g_ref[...]).astype(BF16)

    gu = _dot(xn_ref[...], wgu_ref[...])
    g = gu[:, 0:tf]
    mid = g * _sigmoid(g) * gu[:, tf:2 * tf]
    out_ref[...] += _dot(mid.astype(BF16), wd_ref[...])


def _even_ffn(h, u, o, wo, g, wgu, wd, tf):
    n, d = h.shape
    ff = wd.shape[0]
    tm = ROW_TILE
    row = lambda i, f: (i, 0)
    const = lambda i, f: (0, 0)
    return pl.pallas_call(
        functools.partial(_even_ffn_kernel, tf=tf),
        grid=(n // tm, ff // tf),
        in_specs=[pl.BlockSpec((tm, d), row), pl.BlockSpec((tm, CONV_CH), row),
                  pl.BlockSpec((tm, SB_WIDTH), row), pl.BlockSpec(wo.shape, const),
                  pl.BlockSpec((1, d), const),
                  pl.BlockSpec((d, 2 * tf), lambda i, f: (0, f)),
                  pl.BlockSpec((tf, d), lambda i, f: (f, 0))],
        out_specs=pl.BlockSpec((tm, d), row),
        out_shape=jax.ShapeDtypeStruct((n, d), F32),
        scratch_shapes=[pltpu.VMEM((tm, d), BF16)],
        compiler_params=_params("parallel", "arbitrary"),
        name="even_out_swiglu",
    )(h, u, o, wo, g, wgu, wd)


def _row_scatter_kernel(dest_ref, x_hbm, init_hbm, o_hbm, xbuf, in_sems, row_sems):
    del init_hbm
    i = pl.program_id(0)
    last = pl.num_programs(0) - 1
    tm = xbuf.shape[1]

    def tile_in(tile, slab):
        return pltpu.make_async_copy(x_hbm.at[pl.ds(tile * tm, tm)], xbuf.at[slab], in_sems.at[slab])

    def rows_out(sem):
        return pltpu.make_async_copy(xbuf.at[0], o_hbm.at[pl.ds(0, tm)], row_sems.at[sem])

    @pl.when(i == 0)
    def _():
        tile_in(0, 0).start()

    @pl.when(i < last)
    def _():
        tile_in(i + 1, (i + 1) % 3).start()

    slab = i % 3
    tile_in(i, slab).wait()
    for k in range(2):
        def issue(grp, c, k=k):
            for j in range(DMA_GROUP):
                r = grp * DMA_GROUP + j
                row = dest_ref[(i * 2 + k) * tm + r]
                pltpu.make_async_copy(xbuf.at[slab, r], o_hbm.at[row], row_sems.at[i % 2]).start(priority=j % 2)
            return c
        lax.fori_loop(0, tm // DMA_GROUP, issue, 0)

    @pl.when(i > 0)
    def _():
        for k in range(2):
            rows_out((i - 1) % 2).wait()

    @pl.when(i == last)
    def _():
        for k in range(2):
            rows_out(i % 2).wait()


def _row_scatter(dest, x3, n_rows_alloc):
    n = x3.shape[0]
    tm = COMBINE_TILE
    init = jnp.zeros((n_rows_alloc,) + x3.shape[1:], x3.dtype)
    return pl.pallas_call(
        _row_scatter_kernel,
        grid_spec=pltpu.PrefetchScalarGridSpec(
            num_scalar_prefetch=1, grid=(n // tm,),
            in_specs=[pl.BlockSpec(memory_space=pl.ANY), pl.BlockSpec(memory_space=pl.ANY)],
            out_specs=pl.BlockSpec(memory_space=pl.ANY),
            scratch_shapes=[pltpu.VMEM((3, tm) + x3.shape[1:], x3.dtype),
                            pltpu.SemaphoreType.DMA((3,)), pltpu.SemaphoreType.DMA((2,))]),
        out_shape=jax.ShapeDtypeStruct(init.shape, init.dtype),
        input_output_aliases={2: 0},
        compiler_params=_params("arbitrary"),
        name="moe_row_scatter",
    )(dest, x3, init)


def _gmm_kernel(te_ref, nu_ref, xs_ref, wg_ref, wu_ref, wd_ref, o_ref, acc_ref):
    p = pl.program_id(0)
    f = pl.program_id(1)
    last = pl.num_programs(1) - 1
    used = p < nu_ref[0]

    @pl.when(used)
    def _():
        x = xs_ref[...]
        g = _dot(x, wg_ref[0])
        mid = g * _sigmoid(g) * _dot(x, wu_ref[0])
        part = _dot(mid.astype(BF16), wd_ref[0])

        @pl.when(f == 0)
        def _():
            acc_ref[...] = part

        @pl.when(f > 0)
        def _():
            acc_ref[...] += part

        @pl.when(f == last)
        def _():
            o_ref[...] = acc_ref[...].astype(o_ref.dtype)

    @pl.when(jnp.logical_not(used) & (f == last))
    def _():
        o_ref[...] = jnp.zeros(o_ref.shape, o_ref.dtype)


def _grouped_ffn(tile_expert, n_used, xs, n_rows, wg, wu, wd):
    d = xs.shape[1]
    ff = wd.shape[1]
    tf = MOE_FF_TILE
    nf = ff // tf
    tg = MOE_TILE
    row = lambda p, f, te, nu: (p, 0)
    fidx = lambda p, f, nu: jnp.where(p < nu[0], f, nf - 1)
    up = pl.BlockSpec((1, d, tf), lambda p, f, te, nu: (te[p], 0, fidx(p, f, nu)))
    return pl.pallas_call(
        _gmm_kernel,
        grid_spec=pltpu.PrefetchScalarGridSpec(
            num_scalar_prefetch=2, grid=(n_rows // tg, nf),
            in_specs=[pl.BlockSpec((tg, d), row), up, up,
                      pl.BlockSpec((1, tf, d), lambda p, f, te, nu: (te[p], fidx(p, f, nu), 0))],
            out_specs=pl.BlockSpec((tg, d), row),
            scratch_shapes=[pltpu.VMEM((tg, d), F32)]),
        out_shape=jax.ShapeDtypeStruct((n_rows, d), BF16),
        compiler_params=_params("arbitrary", "arbitrary"),
        name="moe_grouped_ffn",
    )(tile_expert, n_used, xs, wg, wu, wd)


def _combine_kernel(dest_ref, h_ref, w1_ref, w2_ref, ys_hbm, o_ref, ybuf, sems):
    i = pl.program_id(0)
    tm = h_ref.shape[0]

    def fetch(tile, slot):
        for k in range(2):
            def issue(grp, c, k=k):
                for j in range(DMA_GROUP):
                    r = grp * DMA_GROUP + j
                    row = dest_ref[(tile * 2 + k) * tm + r]
                    pltpu.make_async_copy(ys_hbm.at[row], ybuf.at[slot, k, r], sems.at[slot]).start(priority=j % 2)
                return c
            lax.fori_loop(0, tm // DMA_GROUP, issue, 0)

    @pl.when(i == 0)
    def _():
        fetch(0, 0)

    @pl.when(i + 1 < pl.num_programs(0))
    def _():
        fetch(i + 1, (i + 1) % 2)

    slot = i % 2
    for k in range(2):
        pltpu.make_async_copy(ys_hbm.at[pl.ds(0, tm)], ybuf.at[slot, k], sems.at[slot]).wait()
    w1 = w1_ref[...][:, None, :]
    w2 = w2_ref[...][:, None, :]
    o_ref[...] = h_ref[...] + w1 * ybuf[slot, 0].astype(F32) + w2 * ybuf[slot, 1].astype(F32)


def _combine(dest, h3, w1b, w2b, ys3):
    n = h3.shape[0]
    tm = COMBINE_TILE
    blk = lambda i, dest: (i, 0, 0)
    return pl.pallas_call(
        _combine_kernel,
        grid_spec=pltpu.PrefetchScalarGridSpec(
            num_scalar_prefetch=1, grid=(n // tm,),
            in_specs=[pl.BlockSpec((tm,) + h3.shape[1:], blk), pl.BlockSpec((tm, LANES), lambda i, dest: (i, 0)),
                      pl.BlockSpec((tm, LANES), lambda i, dest: (i, 0)), pl.BlockSpec(memory_space=pl.ANY)],
            out_specs=pl.BlockSpec((tm,) + h3.shape[1:], blk),
            scratch_shapes=[pltpu.VMEM((2, 2, tm) + ys3.shape[1:], ys3.dtype), pltpu.SemaphoreType.DMA((2,))]),
        out_shape=jax.ShapeDtypeStruct(h3.shape, F32),
        compiler_params=_params("arbitrary"),
        name="moe_combine",
    )(dest, h3, w1b, w2b, ys3)


def _gate_windows(lw, bd):
    out = []
    for c0 in range(0, lw, LRU_GATE_TILE):
        nc = min(LRU_GATE_TILE, lw - c0)
        r0 = (c0 // bd) * bd
        r1 = ((c0 + nc - 1) // bd + 1) * bd
        r0 = (r0 // LANES) * LANES
        r1 = min(-(-r1 // LANES) * LANES, lw)
        out.append((c0, nc, r0, r1 - r0))
    return out


def _lru_kernel(h_ref, ng_ref, win_ref, cw_ref, cb_ref, wg_ref, br_ref, bi_ref, lam_ref, wo_ref,
                o_ref, xbuf_ref, hs_ref, a_ref, b_ref, y_ref, gate_ref, *, windows):
    ti = pl.program_id(1)
    tt, lw = a_ref.shape

    @pl.when(ti == 0)
    def _():
        xbuf_ref[0:8, :] = jnp.zeros((8, lw), F32)
        hs_ref[...] = jnp.zeros((8, lw), F32)

    @pl.when(ti > 0)
    def _():
        xbuf_ref[0:8, :] = xbuf_ref[tt:tt + 8, :]

    proj = _dot(_rms(h_ref[0], ng_ref[...]).astype(BF16), win_ref[...])
    gate_ref[...] = jax.nn.gelu(proj[:, 0:lw], approximate=True)
    xbuf_ref[8:8 + tt, :] = proj[:, lw:2 * lw]
    off = 8 - (LRU_CONV_WIDTH - 1)
    xc = jnp.broadcast_to(cb_ref[...], (tt, lw))
    for k in range(LRU_CONV_WIDTH):
        xc = xc + cw_ref[k:k + 1, :] * xbuf_ref[off + k:off + k + tt, :]
    xcb = xc.astype(BF16)
    sp_lam = jnp.log(1.0 + jnp.exp(-lam_ref[...]))
    log2_a_per_r = (-LRU_C * LOG2E) * sp_lam
    wcol = 0
    for (c0, nc, r0, nr) in windows:
        rg = _dot(xcb[:, r0:r0 + nr], wg_ref[r0:r0 + nr, wcol:wcol + 2 * nc])
        wcol += 2 * nc
        r = _sigmoid(rg[:, 0:nc] + br_ref[:, c0:c0 + nc])
        ig = _sigmoid(rg[:, nc:2 * nc] + bi_ref[:, c0:c0 + nc])
        a = jnp.exp2(r * log2_a_per_r[:, c0:c0 + nc])
        a_ref[:, c0:c0 + nc] = a
        one_m_a2 = 1.0 - a * a
        root = jnp.where(one_m_a2 > 0.0, one_m_a2 * lax.rsqrt(one_m_a2), 0.0)
        b_ref[:, c0:c0 + nc] = root * (ig * xc[:, c0:c0 + nc])

    rowi = lax.broadcasted_iota(jnp.int32, (8, lw), 0)

    def group(gi, hprev):
        r0 = pl.multiple_of(gi * 8, 8)
        a = a_ref[pl.ds(r0, 8), :]
        b = b_ref[pl.ds(r0, 8), :]
        for s in (1, 2, 4):
            m = rowi >= s
            a_sh = jnp.where(m, pltpu.roll(a, s, axis=0), 1.0)
            b_sh = jnp.where(m, pltpu.roll(b, s, axis=0), 0.0)
            b = a * b_sh + b
            a = a * a_sh
        hrows = a * hprev + b
        y_ref[pl.ds(r0, 8), :] = hrows
        return jnp.broadcast_to(hrows[7:8, :], (8, lw))

    hs_ref[...] = lax.fori_loop(0, tt // 8, group, hs_ref[...])
    gy = (gate_ref[...] * y_ref[...]).astype(BF16)
    o_ref[0] = h_ref[0] + _dot(gy, wo_ref[...])


def _lru_mixer(h, ng, win, cw, cb, wg, br, bi, lam, wo, windows):
    bsz, tp, d = h.shape
    lw = lam.shape[-1]
    tt = LRU_TT
    const = lambda b, t: (0, 0)
    blk = lambda b, t: (b, t, 0)
    resident = lambda w: pl.BlockSpec(w.shape, const, pipeline_mode=pl.Buffered(1))
    return pl.pallas_call(
        functools.partial(_lru_kernel, windows=windows),
        grid=(bsz, tp // tt),
        in_specs=[pl.BlockSpec((1, tt, d), blk), pl.BlockSpec((1, d), const), resident(win),
                  pl.BlockSpec(cw.shape, const), pl.BlockSpec((1, lw), const), resident(wg),
                  pl.BlockSpec((1, lw), const), pl.BlockSpec((1, lw), const),
                  pl.BlockSpec((1, lw), const), resident(wo)],
        out_specs=pl.BlockSpec((1, tt, d), blk),
        out_shape=jax.ShapeDtypeStruct(h.shape, F32),
        scratch_shapes=[pltpu.VMEM((8 + tt, lw), F32), pltpu.VMEM((8, lw), F32),
                        pltpu.VMEM((tt, lw), F32), pltpu.VMEM((tt, lw), F32), pltpu.VMEM((tt, lw), F32),
                        pltpu.VMEM((tt, lw), F32)],
        compiler_params=_params("arbitrary", "arbitrary"),
        name="rglru_mixer",
    )(h, ng, win, cw, cb, wg, br, bi, lam, wo)


ROUTE_E1, ROUTE_E2, ROUTE_R1, ROUTE_R2, ROUTE_VALID = range(5)


def _router_kernel(h_ref, g_ref, rw_ref, ltri_ref, xn_ref, info_ref, w1_ref, w2_ref, cnt_ref, run_ref,
                   *, tp, t_real):
    i = pl.program_id(0)
    tm = h_ref.shape[0]

    @pl.when(i == 0)
    def _():
        run_ref[...] = jnp.zeros(run_ref.shape, F32)

    xn = _rms(h_ref[...], g_ref[...])
    xn_ref[...] = xn.astype(BF16)
    split = lambda a: (a.astype(BF16), (a - a.astype(BF16).astype(F32)).astype(BF16))
    (x_hi, x_lo), (r_hi, r_lo) = split(xn), split(rw_ref[...])
    logits = _dot(x_hi, r_hi) + (_dot(x_hi, r_lo) + _dot(x_lo, r_hi))
    lane = lax.broadcasted_iota(jnp.int32, logits.shape, 1)
    neg = jnp.float32(-jnp.inf)
    logits = jnp.where(lane < N_EXPERTS, logits, neg)
    top1 = jnp.max(logits, axis=1, keepdims=True)
    idx1 = jnp.min(jnp.where(logits == top1, lane, LANES), axis=1, keepdims=True)
    rest = jnp.where(lane == idx1, neg, logits)
    top2 = jnp.max(rest, axis=1, keepdims=True)
    idx2 = jnp.min(jnp.where(rest == top2, lane, LANES), axis=1, keepdims=True)
    w1 = 1.0 / (1.0 + jnp.exp(top2 - top1))

    rowg = (lax.broadcasted_iota(jnp.int32, (tm, 1), 0) + i * tm).astype(F32)
    pos = rowg - jnp.floor((rowg + 0.5) / tp) * tp
    valid = jnp.where(pos < t_real, 1.0, 0.0)
    onehot = (jnp.where(lane == idx1, 1.0, 0.0) + jnp.where(lane == idx2, 1.0, 0.0)) * valid
    before = _dot(ltri_ref[...], onehot.astype(BF16)) + run_ref[...]
    rank1 = jnp.sum(jnp.where(lane == idx1, before, 0.0), axis=1, keepdims=True)
    rank2 = jnp.sum(jnp.where(lane == idx2, before, 0.0), axis=1, keepdims=True)
    run_ref[...] += jnp.sum(onehot, axis=0, keepdims=True)
    cnt_ref[...] = run_ref[...]
    w1_ref[...] = jnp.broadcast_to(w1, (tm, LANES))
    w2_ref[...] = jnp.broadcast_to(1.0 - w1, (tm, LANES))
    fields = {ROUTE_E1: idx1.astype(F32), ROUTE_E2: idx2.astype(F32),
              ROUTE_R1: rank1, ROUTE_R2: rank2, ROUTE_VALID: valid}
    info = jnp.zeros(logits.shape, F32)
    for col, val in fields.items():
        info = jnp.where(lane == col, val, info)
    info_ref[...] = info.T[0:SUBLANES, :]


def _router(h, g, rw, ltri, tp, t_real):
    n, d = h.shape
    tm = ROW_TILE
    row = lambda i: (i, 0)
    const = lambda i: (0, 0)
    return pl.pallas_call(
        functools.partial(_router_kernel, tp=tp, t_real=t_real),
        grid=(n // tm,),
        in_specs=[pl.BlockSpec((tm, d), row), pl.BlockSpec((1, d), const), pl.BlockSpec(rw.shape, const),
                  pl.BlockSpec(ltri.shape, const)],
        out_specs=[pl.BlockSpec((tm, d), row), pl.BlockSpec((SUBLANES, tm), row),
                   pl.BlockSpec((tm, LANES), row), pl.BlockSpec((tm, LANES), row),
                   pl.BlockSpec((1, LANES), const)],
        out_shape=[jax.ShapeDtypeStruct((n, d), BF16), jax.ShapeDtypeStruct((n // tm * SUBLANES, tm), F32),
                   jax.ShapeDtypeStruct((n, LANES), F32), jax.ShapeDtypeStruct((n, LANES), F32),
                   jax.ShapeDtypeStruct((1, LANES), F32)],
        scratch_shapes=[pltpu.VMEM((1, LANES), F32)],
        compiler_params=_params("arbitrary"),
        name="router",
    )(h, g, rw, ltri)


def _routing_tables(info, counts, n_rows):
    tm = ROW_TILE
    assert tm == COMBINE_TILE
    tg = MOE_TILE
    info = info.reshape(-1, SUBLANES, tm)
    field = lambda r: info[:, r, :].astype(jnp.int32)
    valid = info[:, ROUTE_VALID, :] > 0.5
    cnt = counts[0, :N_EXPERTS].astype(jnp.int32)
    padded = (cnt + tg - 1) // tg * tg
    ends = jnp.cumsum(padded)
    starts = ends - padded
    experts = jnp.arange(N_EXPERTS, dtype=jnp.int32)
    start_of = lambda e: jnp.sum(jnp.where(e[..., None] == experts, starts, 0), axis=-1)
    d1 = start_of(field(ROUTE_E1)) + field(ROUTE_R1)
    d2 = start_of(field(ROUTE_E2)) + field(ROUTE_R2)
    tile_start = jnp.arange(n_rows // tg, dtype=jnp.int32) * tg
    tile_expert = jnp.minimum(jnp.sum(tile_start[:, None] >= ends[None, :], axis=1), N_EXPERTS - 1)
    n_used = (ends[-1] // tg).reshape(1)
    per_tile = lambda a, b: jnp.stack([a, b], axis=1).reshape(-1)
    dump = n_rows + jnp.arange(tm, dtype=jnp.int32)[None, :]
    scatter_dest = per_tile(jnp.where(valid, d1, dump), jnp.where(valid, d2, dump + tm))
    gather_dest = per_tile(jnp.where(valid, d1, 0), jnp.where(valid, d2, 0))
    return scatter_dest, gather_dest, tile_expert.astype(jnp.int32), n_used.astype(jnp.int32)


def _interleave_gate_up(wg, wu, tf):
    e, d, ff = wg.shape
    g = wg.astype(BF16).reshape(e, d, ff // tf, tf)
    u = wu.astype(BF16).reshape(e, d, ff // tf, tf)
    return jnp.concatenate([g, u], axis=3).reshape(e, d, 2 * ff)


def _dense_block_diag(w):
    nb, bd, _ = w.shape
    eye = jnp.eye(nb, dtype=w.dtype)
    return (w[:, :, None, :] * eye[:, None, :, None]).reshape(nb * bd, nb * bd)


def _pack_lru_gates(wr, wi, windows):
    dr = _dense_block_diag(wr).astype(BF16)
    di = _dense_block_diag(wi).astype(BF16)
    cols = []
    for (c0, nc, _, _) in windows:
        cols += [dr[:, c0:c0 + nc], di[:, c0:c0 + nc]]
    return jnp.concatenate(cols, axis=1)


def _ffn_chunk(ff):
    for tf in (896, 512, 384, 256, 128):
        if ff % tf == 0:
            return tf
    raise ValueError(f"unsupported d_ff {ff}")


def kernel(x, meta_tokens, mix_norm_even, w_in_even, conv_w, conv_b, conv_ln_g, conv_ln_b, q_norm_g, k_norm_g, w_out_even, ffn_norm_even, ffn_w_gate, ffn_w_up, ffn_w_down, mix_norm_odd, w_in_odd, lru_conv_w, lru_conv_b, gate_r_w, gate_r_b, gate_i_w, gate_i_b, lru_lambda, w_out_odd, ffn_norm_odd, router_w, moe_w_gate, moe_w_up, moe_w_down):
    bsz, seq, d = x.shape
    t_real = N_META + seq
    tp = -(-t_real // TIME_TILE) * TIME_TILE
    n = bsz * tp
    assert n % ROW_TILE == 0
    depth = mix_norm_even.shape[0] + mix_norm_odd.shape[0]

    meta = jnp.broadcast_to(meta_tokens[None].astype(x.dtype), (bsz, N_META, d))
    h = jnp.concatenate([meta, x, jnp.zeros((bsz, tp - t_real, d), x.dtype)], axis=1).reshape(n, d)

    head_mean = jnp.kron(jnp.eye(MXU_DIM // SB_HEAD_DIM, dtype=F32),
                         jnp.full((SB_HEAD_DIM, SB_HEAD_DIM), 1.0 / SB_HEAD_DIM, F32)).astype(BF16)
    kk = jnp.arange(ATT_TK)
    neg_tri = -(kk[:, None] >= kk[None, :]).astype(BF16)
    row2 = lambda a: a.reshape(1, -1)

    for layer in range(depth):
        p = layer // 2
        if layer % 2 == 0:
            u, q, k, v = _even_in_proj(h, row2(mix_norm_even[p]), w_in_even[p].astype(BF16),
                                       row2(jnp.tile(q_norm_g[p], SB_HEADS)),
                                       row2(jnp.tile(k_norm_g[p], SB_HEADS)), head_mean)
            u = _conv_module(u.reshape(bsz, tp, CONV_CH), conv_w[p], row2(conv_b[p]),
                             row2(conv_ln_g[p]), row2(conv_ln_b[p]))
            o = _attention(q.reshape(bsz, tp, SB_WIDTH), k.reshape(bsz, tp, SB_WIDTH),
                           v.reshape(bsz, tp, SB_WIDTH), neg_tri, t_real)
            tf = _ffn_chunk(ffn_w_gate.shape[-1])
            h = _even_ffn(h, u.reshape(n, CONV_CH), o.reshape(n, SB_WIDTH), w_out_even[p].astype(BF16),
                          row2(ffn_norm_even[p]),
                          _interleave_gate_up(ffn_w_gate[p][None], ffn_w_up[p][None], tf)[0],
                          ffn_w_down[p].astype(BF16), tf)
        else:
            lw = lru_lambda.shape[-1]
            windows = _gate_windows(lw, lw // LRU_BLOCKS)
            h = _lru_mixer(h.reshape(bsz, tp, d), row2(mix_norm_odd[p]), w_in_odd[p].astype(BF16),
                           lru_conv_w[p], row2(lru_conv_b[p]),
                           _pack_lru_gates(gate_r_w[p], gate_i_w[p], windows),
                           row2(gate_r_b[p]), row2(gate_i_b[p]), row2(lru_lambda[p]),
                           w_out_odd[p].astype(BF16), windows).reshape(n, d)
            rw = jnp.pad(router_w[p], ((0, 0), (0, LANES - N_EXPERTS)))
            kk = jnp.arange(ROW_TILE)
            ltri = (kk[:, None] > kk[None, :]).astype(BF16)
            xn, info, w1b, w2b, counts = _router(h, row2(ffn_norm_odd[p]), rw, ltri, tp, t_real)
            tg = MOE_TILE
            n_rows = -(-(2 * bsz * t_real + N_EXPERTS * (tg - 1)) // tg) * tg
            sdest, gdest, tile_expert, n_used = _routing_tables(info, counts, n_rows)
            sub = d // LANES
            assert sub == SUBLANES
            n_alloc = n_rows + 2 * COMBINE_TILE
            xs = _row_scatter(sdest, xn.reshape(n, sub, LANES), n_alloc).reshape(n_alloc, d)
            ys = _grouped_ffn(tile_expert, n_used, xs, n_rows, moe_w_gate[p].astype(BF16),
                              moe_w_up[p].astype(BF16), moe_w_down[p].astype(BF16))
            h = _combine(gdest, h.reshape(n, sub, LANES), w1b, w2b,
                         ys.reshape(n_rows, sub, LANES)).reshape(n, d)
    return h.reshape(bsz, tp, d)[:, N_META:t_real]
```

```python
import functools
import math

import jax
import jax.numpy as jnp
from jax import lax
from jax.experimental import pallas as pl
from jax.experimental.pallas import tpu as pltpu

F32 = jnp.float32
BF16 = jnp.bfloat16

EPS = 1e-6
N_META = 16
CONV_CH = 512
CONV_WIDTH = 31
SB_HEADS = 8
SB_HEAD_DIM = 64
SB_WIDTH = SB_HEADS * SB_HEAD_DIM
LRU_BLOCKS = 16
LRU_CONV_WIDTH = 4
LRU_C = 8.0
N_EXPERTS = 8

LANES = 128
MXU_DIM = 256
TIME_TILE = 256
VMEM_LIMIT = 50 * 1024 * 1024

ROW_TILE = 512
CONV_TT = 128
CONV_HALO = 32
CONV_CHUNK = 32
ATT_TQ = 256
ATT_TK = 256
LOG2E = 1.4426950408889634
MASKED_EXPONENT = -1e30
SOFTPLUS_CUTOFF = 126.0
LRU_TT = 256
LRU_GATE_TILE = 256
MOE_TILE = 512
MOE_FF_TILE = 1792
COMBINE_TILE = 512
COMBINE_WIN = 16
COMBINE_NWIN = 2 * COMBINE_TILE // COMBINE_WIN + 2 * N_EXPERTS
SUBLANES = 8
BF16_ROWS = 16
DMA_GROUP = 8


def _dot(a, b):
    return jnp.dot(a, b, preferred_element_type=F32)


def _params(*sem):
    return pltpu.CompilerParams(dimension_semantics=sem, vmem_limit_bytes=VMEM_LIMIT)


def _sigmoid(x):
    return 0.5 * jnp.tanh(0.5 * x) + 0.5


def _rms(x, g):
    ms = jnp.mean(x * x, axis=-1, keepdims=True)
    return x * lax.rsqrt(ms + EPS) * g


def _even_in_kernel(h_ref, g_ref, w_ref, qg_ref, kg_ref, hm_ref, u_ref, q_ref, k_ref, v_ref):
    xn = _rms(h_ref[...], g_ref[...]).astype(BF16)
    a = _dot(xn, w_ref[:, 0:CONV_CH])
    gate = _dot(xn, w_ref[:, CONV_CH:2 * CONV_CH])
    u_ref[...] = (a * _sigmoid(gate)).astype(BF16)

    def head_norm(y, gain):
        yy = (y * y).astype(BF16)
        parts = [_dot(yy[:, c:c + MXU_DIM], hm_ref[...]) for c in range(0, SB_WIDTH, MXU_DIM)]
        ms = jnp.concatenate(parts, axis=1)
        return y * lax.rsqrt(ms + EPS) * gain

    c0 = 2 * CONV_CH
    q = head_norm(_dot(xn, w_ref[:, c0:c0 + SB_WIDTH]), qg_ref[...])
    q_ref[...] = (q * (LOG2E / math.sqrt(SB_HEAD_DIM))).astype(BF16)
    k = head_norm(_dot(xn, w_ref[:, c0 + SB_WIDTH:c0 + 2 * SB_WIDTH]), kg_ref[...])
    k_ref[...] = k.astype(BF16)
    v_ref[...] = _dot(xn, w_ref[:, c0 + 2 * SB_WIDTH:c0 + 3 * SB_WIDTH]).astype(BF16)


def _even_in_proj(h, g, w, qg, kg, hm):
    n, d = h.shape
    tm = ROW_TILE
    row = lambda i: (i, 0)
    const = lambda i: (0, 0)
    out = jax.ShapeDtypeStruct((n, SB_WIDTH), BF16)
    return pl.pallas_call(
        _even_in_kernel,
        grid=(n // tm,),
        in_specs=[pl.BlockSpec((tm, d), row), pl.BlockSpec((1, d), const),
                  pl.BlockSpec(w.shape, const), pl.BlockSpec((1, SB_WIDTH), const),
                  pl.BlockSpec((1, SB_WIDTH), const), pl.BlockSpec(hm.shape, const)],
        out_specs=[pl.BlockSpec((tm, SB_WIDTH), row)] * 4,
        out_shape=[out] * 4,
        compiler_params=_params("parallel"),
        name="even_in_proj",
    )(h, g, w, qg, kg, hm)


def _conv_kernel(cur_ref, halo_ref, w_ref, b_ref, lg_ref, lb_ref, o_ref, buf_ref, sh_ref):
    i = pl.program_id(1)
    tt = cur_ref.shape[1]
    rows = CONV_HALO + tt
    halo = halo_ref[0].astype(F32)
    buf_ref[0:CONV_HALO, :] = jnp.where(i > 0, halo, 0.0)
    buf_ref[CONV_HALO:rows, :] = cur_ref[0].astype(F32)
    for s in range(1, SUBLANES):
        sh_ref[s - 1, 0:rows - SUBLANES, :] = buf_ref[s:s + rows - SUBLANES, :]
    off = CONV_HALO - (CONV_WIDTH - 1)
    for c in range(tt // CONV_CHUNK):
        r0 = c * CONV_CHUNK
        acc = jnp.broadcast_to(b_ref[...], (CONV_CHUNK, CONV_CH))
        for k in range(CONV_WIDTH):
            shift = (off + k) % SUBLANES
            base = r0 + off + k - shift
            if shift == 0:
                tap = buf_ref[base:base + CONV_CHUNK, :]
            else:
                tap = sh_ref[shift - 1, base:base + CONV_CHUNK, :]
            acc = acc + w_ref[k:k + 1, :] * tap
        mu = jnp.mean(acc, axis=-1, keepdims=True)
        xc = acc - mu
        var = jnp.mean(xc * xc, axis=-1, keepdims=True)
        y = xc * lax.rsqrt(var + EPS) * lg_ref[...] + lb_ref[...]
        o_ref[0, r0:r0 + CONV_CHUNK, :] = (y * _sigmoid(y)).astype(BF16)


def _conv_module(u, w, b, lg, lb):
    bsz, tp, c = u.shape
    tt = CONV_TT
    per = tt // CONV_HALO
    const = lambda bi, i: (0, 0)
    return pl.pallas_call(
        _conv_kernel,
        grid=(bsz, tp // tt),
        in_specs=[pl.BlockSpec((1, tt, c), lambda bi, i: (bi, i, 0)),
                  pl.BlockSpec((1, CONV_HALO, c), lambda bi, i: (bi, jnp.maximum(i * per - 1, 0), 0)),
                  pl.BlockSpec(w.shape, const), pl.BlockSpec((1, c), const),
                  pl.BlockSpec((1, c), const), pl.BlockSpec((1, c), const)],
        out_specs=pl.BlockSpec((1, tt, c), lambda bi, i: (bi, i, 0)),
        out_shape=jax.ShapeDtypeStruct(u.shape, BF16),
        scratch_shapes=[pltpu.VMEM((CONV_HALO + tt, c), F32),
                        pltpu.VMEM((SUBLANES - 1, CONV_HALO + tt - SUBLANES, c), F32)],
        compiler_params=_params("parallel", "parallel"),
        name="conv_module",
    )(u, u, w, b, lg, lb)


def _attn_kernel(*refs, tq, tile0, aliased, n_compute):
    if aliased:
        refs = refs[1:]
    o_ref = refs[4]
    step = pl.program_id(1)

    @pl.when(step < n_compute)
    def _():
        _attn_tile(*refs, tq=tq, row0=(step + tile0) * ATT_TQ)

    @pl.when(step >= n_compute)
    def _():
        o_ref[...] = jnp.zeros(o_ref.shape, BF16)


def _attn_tile(q_ref, k_ref, v_ref, tri_ref, o_ref, q2_ref, e_ref, tot_ref, acc_ref, car_ref, *, tq, row0):
    tk = ATT_TK
    npairs = q_ref.shape[2] // LANES
    lane_q = lax.broadcasted_iota(jnp.int32, (tq, LANES), 1)

    for hp in range(npairs):
        q = q_ref[0, 0:tq, hp * LANES:(hp + 1) * LANES]
        zero = jnp.zeros_like(q)
        q2_ref[hp, 0:tq, :] = jnp.where(lane_q < SB_HEAD_DIM, q, zero)
        q2_ref[hp, tq:2 * tq, :] = jnp.where(lane_q >= SB_HEAD_DIM, q, zero)
        car_ref[hp] = jnp.zeros((2 * tq, LANES), F32)
        acc_ref[hp] = jnp.zeros((2 * tq, LANES), F32)

    def stage_a(j, mask):
        k0 = pl.multiple_of(j * tk, tk)
        for hp in range(npairs):
            kt = k_ref[0, pl.ds(k0, tk), hp * LANES:(hp + 1) * LANES]
            z = lax.dot_general(q2_ref[hp], kt, (((1,), (1,)), ((), ())), preferred_element_type=F32)
            sp = jnp.where(z > SOFTPLUS_CUTOFF, z, jnp.log(1.0 + jnp.exp2(z)) * LOG2E)
            if mask is not None:
                sp = jnp.where(mask, sp, 0.0)
            e = z + _dot(sp.astype(BF16), tri_ref[...])
            if mask is not None:
                e = jnp.where(mask, e, MASKED_EXPONENT)
            e_ref[hp] = e
            tot_ref[hp] = jnp.broadcast_to(jnp.sum(sp, axis=1, keepdims=True), (2 * tq, LANES))

    def stage_b(j):
        k0 = pl.multiple_of(j * tk, tk)
        for hp in range(npairs):
            car = car_ref[hp]
            w = jnp.exp2(e_ref[hp] + jnp.concatenate([car] * (tk // LANES), axis=1)).astype(BF16)
            vt = v_ref[0, pl.ds(k0, tk), hp * LANES:(hp + 1) * LANES]
            acc_ref[hp] += _dot(w, vt)
            car_ref[hp] = car - tot_ref[hp]

    j_last = row0 // tk
    row = lax.broadcasted_iota(jnp.int32, (2 * tq, tk), 0)
    col = lax.broadcasted_iota(jnp.int32, (2 * tq, tk), 1)
    row = jnp.where(row >= tq, row - tq, row) + row0
    stage_a(j_last, (col + j_last * tk) < row)

    def body(jj, carry):
        j = j_last - 1 - jj
        stage_b(j + 1)
        stage_a(j, None)
        return carry

    lax.fori_loop(0, j_last, body, 0)
    stage_b(0)
    if tq < o_ref.shape[1]:
        o_ref[0, tq:, :] = jnp.zeros((o_ref.shape[1] - tq, o_ref.shape[2]), BF16)
    for hp in range(npairs):
        o_ref[0, 0:tq, hp * LANES:(hp + 1) * LANES] = jnp.where(
            lane_q < SB_HEAD_DIM, acc_ref[hp, 0:tq, :], acc_ref[hp, tq:2 * tq, :]).astype(BF16)


def _attention(q, k, v, tri, t_real):
    bsz, tp, width = q.shape
    bq, tk = ATT_TQ, ATT_TK
    assert bq <= tk and tk % bq == 0
    npairs = width // LANES
    n_full = t_real // bq
    rem = -(-(t_real - n_full * bq) // BF16_ROWS) * BF16_ROWS

    def call(tq, tile0, n_compute, n_tiles, prev):
        blk = lambda b, i: (b, i + tile0, 0)
        full = lambda b, i: (b, 0, 0)
        in_specs = [pl.BlockSpec((1, bq, width), blk), pl.BlockSpec((1, tp, width), full),
                    pl.BlockSpec((1, tp, width), full), pl.BlockSpec(tri.shape, lambda b, i: (0, 0))]
        args = [q, k, v, tri]
        if prev is not None:
            in_specs.insert(0, pl.BlockSpec(memory_space=pl.ANY))
            args.insert(0, prev)
        return pl.pallas_call(
            functools.partial(_attn_kernel, tq=tq, tile0=tile0, aliased=prev is not None, n_compute=n_compute),
            grid=(bsz, n_tiles),
            in_specs=in_specs,
            out_specs=pl.BlockSpec((1, bq, width), blk),
            out_shape=jax.ShapeDtypeStruct(q.shape, BF16),
            input_output_aliases={} if prev is None else {0: 0},
            scratch_shapes=[pltpu.VMEM((npairs, 2 * tq, LANES), BF16),
                            pltpu.VMEM((npairs, 2 * tq, tk), F32),
                            pltpu.VMEM((npairs, 2 * tq, LANES), F32),
                            pltpu.VMEM((npairs, 2 * tq, LANES), F32),
                            pltpu.VMEM((npairs, 2 * tq, LANES), F32)],
            compiler_params=_params("parallel", "arbitrary"),
            name="stick_breaking" if prev is None else "stick_breaking_tail",
        )(*args)

    out = call(bq, 0, n_full, tp // bq, None)
    if rem:
        out = call(rem, n_full, 1, 1, out)
    return out


def _even_ffn_kernel(h_ref, u_ref, o_ref, wo_ref, g_ref, wgu_ref, wd_ref, out_ref, xn_ref, *, tf):
    @pl.when(pl.program_id(1) == 0)
    def _():
        h1 = h_ref[...] + _dot(u_ref[...], wo_ref[0:CONV_CH, :]) + _dot(o_ref[...], wo_ref[CONV_CH:, :])
        out_ref[...] = h1
        xn_ref[...] = _rms(h1, g_ref[...]).astype(BF16)

    gu = _dot(xn_ref[...], wgu_ref[...])
    g = gu[:, 0:tf]
    mid = g * _sigmoid(g) * gu[:, tf:2 * tf]
    out_ref[...] += _dot(mid.astype(BF16), wd_ref[...])


def _even_ffn(h, u, o, wo, g, wgu, wd, tf):
    n, d = h.shape
    ff = wd.shape[0]
    tm = ROW_TILE
    row = lambda i, f: (i, 0)
    const = lambda i, f: (0, 0)
    return pl.pallas_call(
        functools.partial(_even_ffn_kernel, tf=tf),
        grid=(n // tm, ff // tf),
        in_specs=[pl.BlockSpec((tm, d), row), pl.BlockSpec((tm, CONV_CH), row),
                  pl.BlockSpec((tm, SB_WIDTH), row), pl.BlockSpec(wo.shape, const),
                  pl.BlockSpec((1, d), const),
                  pl.BlockSpec((d, 2 * tf), lambda i, f: (0, f)),
                  pl.BlockSpec((tf, d), lambda i, f: (f, 0))],
        out_specs=pl.BlockSpec((tm, d), row),
        out_shape=jax.ShapeDtypeStruct((n, d), F32),
        scratch_shapes=[pltpu.VMEM((tm, d), BF16)],
        compiler_params=_params("parallel", "arbitrary"),
        name="even_out_swiglu",
    )(h, u, o, wo, g, wgu, wd)


def _row_scatter_kernel(dest_ref, x_hbm, init_hbm, o_hbm, xbuf, in_sems, row_sems):
    del init_hbm
    i = pl.program_id(0)
    last = pl.num_programs(0) - 1
    tm = xbuf.shape[1]

    def tile_in(tile, slab):
        return pltpu.make_async_copy(x_hbm.at[pl.ds(tile * tm, tm)], xbuf.at[slab], in_sems.at[slab])

    def rows_out(sem):
        return pltpu.make_async_copy(xbuf.at[0], o_hbm.at[pl.ds(0, tm)], row_sems.at[sem])

    @pl.when(i == 0)
    def _():
        tile_in(0, 0).start()

    @pl.when(i < last)
    def _():
        tile_in(i + 1, (i + 1) % 3).start()

    slab = i % 3
    tile_in(i, slab).wait()
    for k in range(2):
        def issue(grp, c, k=k):
            for j in range(DMA_GROUP):
                r = grp * DMA_GROUP + j
                row = dest_ref[(i * 2 + k) * tm + r]
                pltpu.make_async_copy(xbuf.at[slab, r], o_hbm.at[row], row_sems.at[i % 2]).start(priority=j % 2)
            return c
        lax.fori_loop(0, tm // DMA_GROUP, issue, 0)

    @pl.when(i > 0)
    def _():
        for k in range(2):
            rows_out((i - 1) % 2).wait()

    @pl.when(i == last)
    def _():
        for k in range(2):
            rows_out(i % 2).wait()


def _row_scatter(dest, x3, n_rows_alloc):
    n = x3.shape[0]
    tm = COMBINE_TILE
    init = jnp.zeros((n_rows_alloc,) + x3.shape[1:], x3.dtype)
    return pl.pallas_call(
        _row_scatter_kernel,
        grid_spec=pltpu.PrefetchScalarGridSpec(
            num_scalar_prefetch=1, grid=(n // tm,),
            in_specs=[pl.BlockSpec(memory_space=pl.ANY), pl.BlockSpec(memory_space=pl.ANY)],
            out_specs=pl.BlockSpec(memory_space=pl.ANY),
            scratch_shapes=[pltpu.VMEM((3, tm) + x3.shape[1:], x3.dtype),
                            pltpu.SemaphoreType.DMA((3,)), pltpu.SemaphoreType.DMA((2,))]),
        out_shape=jax.ShapeDtypeStruct(init.shape, init.dtype),
        input_output_aliases={2: 0},
        compiler_params=_params("arbitrary"),
        name="moe_row_scatter",
    )(dest, x3, init)


def _gmm_kernel(te_ref, nu_ref, xs_ref, wg_ref, wu_ref, wd_ref, o_ref, acc_ref):
    p = pl.program_id(0)
    f = pl.program_id(1)
    last = pl.num_programs(1) - 1
    used = p < nu_ref[0]

    @pl.when(used)
    def _():
        x = xs_ref[...]
        g = _dot(x, wg_ref[0])
        mid = g * _sigmoid(g) * _dot(x, wu_ref[0])
        part = _dot(mid.astype(BF16), wd_ref[0])

        @pl.when(f == 0)
        def _():
            acc_ref[...] = part

        @pl.when(f > 0)
        def _():
            acc_ref[...] += part

        @pl.when(f == last)
        def _():
            o_ref[...] = acc_ref[...].astype(o_ref.dtype)

    @pl.when(jnp.logical_not(used) & (f == last))
    def _():
        o_ref[...] = jnp.zeros(o_ref.shape, o_ref.dtype)


def _grouped_ffn(tile_expert, n_used, xs, n_rows, wg, wu, wd):
    d = xs.shape[1]
    ff = wd.shape[1]
    tf = MOE_FF_TILE
    nf = ff // tf
    tg = MOE_TILE
    row = lambda p, f, te, nu: (p, 0)
    fidx = lambda p, f, nu: jnp.where(p < nu[0], f, nf - 1)
    up = pl.BlockSpec((1, d, tf), lambda p, f, te, nu: (te[p], 0, fidx(p, f, nu)))
    return pl.pallas_call(
        _gmm_kernel,
        grid_spec=pltpu.PrefetchScalarGridSpec(
            num_scalar_prefetch=2, grid=(n_rows // tg, nf),
            in_specs=[pl.BlockSpec((tg, d), row), up, up,
                      pl.BlockSpec((1, tf, d), lambda p, f, te, nu: (te[p], fidx(p, f, nu), 0))],
            out_specs=pl.BlockSpec((tg, d), row),
            scratch_shapes=[pltpu.VMEM((tg, d), F32)]),
        out_shape=jax.ShapeDtypeStruct((n_rows, d), BF16),
        compiler_params=_params("arbitrary", "arbitrary"),
        name="moe_grouped_ffn",
    )(tile_expert, n_used, xs, wg, wu, wd)


def _combine_kernel(win_ref, h_ref, tok_ref, w1_ref, w2_ref, starts_ref, rows_ref, ys_hbm, o_ref, gbuf, sems):
    i = pl.program_id(0)
    tm = h_ref.shape[0]
    win = COMBINE_WIN

    def fetch(tile, slot):
        for s in range(COMBINE_NWIN):
            start = pl.multiple_of(win_ref[tile * COMBINE_NWIN + s], win)
            pltpu.make_async_copy(ys_hbm.at[pl.ds(start, win)], gbuf.at[slot, pl.ds(s * win, win)],
                                  sems.at[slot]).start(priority=s % 2)

    @pl.when(i == 0)
    def _():
        fetch(0, 0)

    @pl.when(i + 1 < pl.num_programs(0))
    def _():
        fetch(i + 1, (i + 1) % 2)

    slot = i % 2
    pltpu.make_async_copy(ys_hbm.at[pl.ds(0, COMBINE_NWIN * win)], gbuf.at[slot], sems.at[slot]).wait()

    tok = tok_ref[...]
    lane = lax.broadcasted_iota(jnp.int32, tok.shape, 1)
    field = lambda c: jnp.sum(jnp.where(lane == c, tok, 0.0), axis=1, keepdims=True)
    valid = field(ROUTE_VALID) > 0.5
    rows = rows_ref[0]
    reps = h_ref.shape[1] // LANES
    out = h_ref[...]
    for e_col, r_col, w_ref in ((ROUTE_E1, ROUTE_R1, w1_ref), (ROUTE_E2, ROUTE_R2, w2_ref)):
        start = jnp.sum(jnp.where(lane == field(e_col).astype(jnp.int32), starts_ref[...], 0.0),
                        axis=1, keepdims=True)
        dest = jnp.where(valid, start + field(r_col), -2.0)
        pick = jnp.where(dest == rows, 1.0, 0.0).astype(BF16)
        out = out + jnp.concatenate([w_ref[...]] * reps, axis=1) * _dot(pick, gbuf[slot])
    o_ref[...] = out


def _combine(win_start, h, tok, w1b, w2b, starts_row, slot_rows, ys):
    n, d = h.shape
    tm = COMBINE_TILE
    n_slots = COMBINE_NWIN * COMBINE_WIN
    row = lambda i, w: (i, 0)
    return pl.pallas_call(
        _combine_kernel,
        grid_spec=pltpu.PrefetchScalarGridSpec(
            num_scalar_prefetch=1, grid=(n // tm,),
            in_specs=[pl.BlockSpec((tm, d), row), pl.BlockSpec((tm, LANES), row), pl.BlockSpec((tm, LANES), row),
                      pl.BlockSpec((tm, LANES), row), pl.BlockSpec((1, LANES), lambda i, w: (0, 0)),
                      pl.BlockSpec((1, 1, n_slots), lambda i, w: (i, 0, 0)),
                      pl.BlockSpec(memory_space=pl.ANY)],
            out_specs=pl.BlockSpec((tm, d), row),
            scratch_shapes=[pltpu.VMEM((2, n_slots, d), ys.dtype), pltpu.SemaphoreType.DMA((2,))]),
        out_shape=jax.ShapeDtypeStruct(h.shape, F32),
        compiler_params=_params("arbitrary"),
        name="moe_combine",
    )(win_start, h, tok, w1b, w2b, starts_row, slot_rows, ys)


def _gate_windows(lw, bd):
    out = []
    for c0 in range(0, lw, LRU_GATE_TILE):
        nc = min(LRU_GATE_TILE, lw - c0)
        r0 = (c0 // bd) * bd
        r1 = ((c0 + nc - 1) // bd + 1) * bd
        r0 = (r0 // LANES) * LANES
        r1 = min(-(-r1 // LANES) * LANES, lw)
        out.append((c0, nc, r0, r1 - r0))
    return out


def _lru_kernel(h_ref, ng_ref, win_ref, cw_ref, cb_ref, wg_ref, br_ref, bi_ref, lam_ref, wo_ref,
                o_ref, xbuf_ref, hs_ref, a_ref, b_ref, y_ref, gate_ref, *, windows):
    ti = pl.program_id(1)
    tt, lw = a_ref.shape

    @pl.when(ti == 0)
    def _():
        xbuf_ref[0:8, :] = jnp.zeros((8, lw), F32)
        hs_ref[...] = jnp.zeros((8, lw), F32)

    @pl.when(ti > 0)
    def _():
        xbuf_ref[0:8, :] = xbuf_ref[tt:tt + 8, :]

    proj = _dot(_rms(h_ref[0], ng_ref[...]).astype(BF16), win_ref[...])
    gate_ref[...] = jax.nn.gelu(proj[:, 0:lw], approximate=True)
    xbuf_ref[8:8 + tt, :] = proj[:, lw:2 * lw]
    off = 8 - (LRU_CONV_WIDTH - 1)
    xc = jnp.broadcast_to(cb_ref[...], (tt, lw))
    for k in range(LRU_CONV_WIDTH):
        xc = xc + cw_ref[k:k + 1, :] * xbuf_ref[off + k:off + k + tt, :]
    xcb = xc.astype(BF16)
    sp_lam = jnp.log(1.0 + jnp.exp(-lam_ref[...]))
    log2_a_per_r = (-LRU_C * LOG2E) * sp_lam
    wcol = 0
    for (c0, nc, r0, nr) in windows:
        rg = _dot(xcb[:, r0:r0 + nr], wg_ref[r0:r0 + nr, wcol:wcol + 2 * nc])
        wcol += 2 * nc
        r = _sigmoid(rg[:, 0:nc] + br_ref[:, c0:c0 + nc])
        ig = _sigmoid(rg[:, nc:2 * nc] + bi_ref[:, c0:c0 + nc])
        a = jnp.exp2(r * log2_a_per_r[:, c0:c0 + nc])
        a_ref[:, c0:c0 + nc] = a
        one_m_a2 = 1.0 - a * a
        root = jnp.where(one_m_a2 > 0.0, one_m_a2 * lax.rsqrt(one_m_a2), 0.0)
        b_ref[:, c0:c0 + nc] = root * (ig * xc[:, c0:c0 + nc])

    rowi = lax.broadcasted_iota(jnp.int32, (8, lw), 0)

    def group(gi, hprev):
        r0 = pl.multiple_of(gi * 8, 8)
        a = a_ref[pl.ds(r0, 8), :]
        b = b_ref[pl.ds(r0, 8), :]
        for s in (1, 2, 4):
            m = rowi >= s
            a_sh = jnp.where(m, pltpu.roll(a, s, axis=0), 1.0)
            b_sh = jnp.where(m, pltpu.roll(b, s, axis=0), 0.0)
            b = a * b_sh + b
            a = a * a_sh
        hrows = a * hprev + b
        y_ref[pl.ds(r0, 8), :] = hrows
        return jnp.broadcast_to(hrows[7:8, :], (8, lw))

    hs_ref[...] = lax.fori_loop(0, tt // 8, group, hs_ref[...])
    gy = (gate_ref[...] * y_ref[...]).astype(BF16)
    o_ref[0] = h_ref[0] + _dot(gy, wo_ref[...])


def _lru_mixer(h, ng, win, cw, cb, wg, br, bi, lam, wo, windows):
    bsz, tp, d = h.shape
    lw = lam.shape[-1]
    tt = LRU_TT
    const = lambda b, t: (0, 0)
    blk = lambda b, t: (b, t, 0)
    resident = lambda w: pl.BlockSpec(w.shape, const, pipeline_mode=pl.Buffered(1))
    return pl.pallas_call(
        functools.partial(_lru_kernel, windows=windows),
        grid=(bsz, tp // tt),
        in_specs=[pl.BlockSpec((1, tt, d), blk), pl.BlockSpec((1, d), const), resident(win),
                  pl.BlockSpec(cw.shape, const), pl.BlockSpec((1, lw), const), resident(wg),
                  pl.BlockSpec((1, lw), const), pl.BlockSpec((1, lw), const),
                  pl.BlockSpec((1, lw), const), resident(wo)],
        out_specs=pl.BlockSpec((1, tt, d), blk),
        out_shape=jax.ShapeDtypeStruct(h.shape, F32),
        scratch_shapes=[pltpu.VMEM((8 + tt, lw), F32), pltpu.VMEM((8, lw), F32),
                        pltpu.VMEM((tt, lw), F32), pltpu.VMEM((tt, lw), F32), pltpu.VMEM((tt, lw), F32),
                        pltpu.VMEM((tt, lw), F32)],
        compiler_params=_params("arbitrary", "arbitrary"),
        name="rglru_mixer",
    )(h, ng, win, cw, cb, wg, br, bi, lam, wo)


ROUTE_E1, ROUTE_E2, ROUTE_R1, ROUTE_R2, ROUTE_VALID = range(5)


def _router_kernel(h_ref, g_ref, rw_ref, ltri_ref, xn_ref, info_ref, tok_ref, w1_ref, w2_ref, before_ref, cnt_ref,
                   run_ref, *, tp, t_real):
    i = pl.program_id(0)
    tm = h_ref.shape[0]

    @pl.when(i == 0)
    def _():
        run_ref[...] = jnp.zeros(run_ref.shape, F32)

    xn = _rms(h_ref[...], g_ref[...])
    xn_ref[...] = xn.astype(BF16)
    split = lambda a: (a.astype(BF16), (a - a.astype(BF16).astype(F32)).astype(BF16))
    (x_hi, x_lo), (r_hi, r_lo) = split(xn), split(rw_ref[...])
    logits = _dot(x_hi, r_hi) + (_dot(x_hi, r_lo) + _dot(x_lo, r_hi))
    lane = lax.broadcasted_iota(jnp.int32, logits.shape, 1)
    neg = jnp.float32(-jnp.inf)
    logits = jnp.where(lane < N_EXPERTS, logits, neg)
    top1 = jnp.max(logits, axis=1, keepdims=True)
    idx1 = jnp.min(jnp.where(logits == top1, lane, LANES), axis=1, keepdims=True)
    rest = jnp.where(lane == idx1, neg, logits)
    top2 = jnp.max(rest, axis=1, keepdims=True)
    idx2 = jnp.min(jnp.where(rest == top2, lane, LANES), axis=1, keepdims=True)
    w1 = 1.0 / (1.0 + jnp.exp(top2 - top1))

    rowg = (lax.broadcasted_iota(jnp.int32, (tm, 1), 0) + i * tm).astype(F32)
    pos = rowg - jnp.floor((rowg + 0.5) / tp) * tp
    valid = jnp.where(pos < t_real, 1.0, 0.0)
    onehot = (jnp.where(lane == idx1, 1.0, 0.0) + jnp.where(lane == idx2, 1.0, 0.0)) * valid
    before_ref[...] = jnp.broadcast_to(run_ref[...], before_ref.shape)
    before = _dot(ltri_ref[...], onehot.astype(BF16)) + run_ref[...]
    rank1 = jnp.sum(jnp.where(lane == idx1, before, 0.0), axis=1, keepdims=True)
    rank2 = jnp.sum(jnp.where(lane == idx2, before, 0.0), axis=1, keepdims=True)
    run_ref[...] += jnp.sum(onehot, axis=0, keepdims=True)
    cnt_ref[...] = run_ref[...]
    w1_ref[...] = jnp.broadcast_to(w1, (tm, LANES))
    w2_ref[...] = jnp.broadcast_to(1.0 - w1, (tm, LANES))
    fields = {ROUTE_E1: idx1.astype(F32), ROUTE_E2: idx2.astype(F32),
              ROUTE_R1: rank1, ROUTE_R2: rank2, ROUTE_VALID: valid}
    info = jnp.zeros(logits.shape, F32)
    for col, val in fields.items():
        info = jnp.where(lane == col, val, info)
    tok_ref[...] = info
    info_ref[...] = info.T[0:SUBLANES, :]


def _router(h, g, rw, ltri, tp, t_real):
    n, d = h.shape
    tm = ROW_TILE
    row = lambda i: (i, 0)
    const = lambda i: (0, 0)
    return pl.pallas_call(
        functools.partial(_router_kernel, tp=tp, t_real=t_real),
        grid=(n // tm,),
        in_specs=[pl.BlockSpec((tm, d), row), pl.BlockSpec((1, d), const), pl.BlockSpec(rw.shape, const),
                  pl.BlockSpec(ltri.shape, const)],
        out_specs=[pl.BlockSpec((tm, d), row), pl.BlockSpec((SUBLANES, tm), row),
                   pl.BlockSpec((tm, LANES), row), pl.BlockSpec((tm, LANES), row),
                   pl.BlockSpec((tm, LANES), row), pl.BlockSpec((SUBLANES, LANES), row),
                   pl.BlockSpec((1, LANES), const)],
        out_shape=[jax.ShapeDtypeStruct((n, d), BF16), jax.ShapeDtypeStruct((n // tm * SUBLANES, tm), F32),
                   jax.ShapeDtypeStruct((n, LANES), F32), jax.ShapeDtypeStruct((n, LANES), F32),
                   jax.ShapeDtypeStruct((n, LANES), F32), jax.ShapeDtypeStruct((n // tm * SUBLANES, LANES), F32),
                   jax.ShapeDtypeStruct((1, LANES), F32)],
        scratch_shapes=[pltpu.VMEM((1, LANES), F32)],
        compiler_params=_params("arbitrary"),
        name="router",
    )(h, g, rw, ltri)


def _routing_tables(info, before, counts, n_rows):
    tm = ROW_TILE
    assert tm == COMBINE_TILE
    tg = MOE_TILE
    info = info.reshape(-1, SUBLANES, tm)
    field = lambda r: info[:, r, :].astype(jnp.int32)
    valid = info[:, ROUTE_VALID, :] > 0.5
    cnt = counts[0, :N_EXPERTS].astype(jnp.int32)
    padded = (cnt + tg - 1) // tg * tg
    ends = jnp.cumsum(padded)
    starts = ends - padded
    experts = jnp.arange(N_EXPERTS, dtype=jnp.int32)
    start_of = lambda e: jnp.sum(jnp.where(e[..., None] == experts, starts, 0), axis=-1)
    d1 = start_of(field(ROUTE_E1)) + field(ROUTE_R1)
    d2 = start_of(field(ROUTE_E2)) + field(ROUTE_R2)
    tile_start = jnp.arange(n_rows // tg, dtype=jnp.int32) * tg
    tile_expert = jnp.minimum(jnp.sum(tile_start[:, None] >= ends[None, :], axis=1), N_EXPERTS - 1)
    n_used = (ends[-1] // tg).reshape(1)
    per_tile = lambda a, b: jnp.stack([a, b], axis=1).reshape(-1)
    dump = n_rows + jnp.arange(tm, dtype=jnp.int32)[None, :]
    scatter_dest = per_tile(jnp.where(valid, d1, dump), jnp.where(valid, d2, dump + tm))
    win = COMBINE_WIN
    bef = before.reshape(-1, SUBLANES, LANES)[:, 0, :N_EXPERTS].astype(jnp.int32)
    cnt_tile = jnp.concatenate([bef[1:], cnt[None, :]], axis=0) - bef
    lo = starts[None, :] + bef
    first = lo // win * win
    nwin = jnp.where(cnt_tile > 0, (lo - first + cnt_tile + win - 1) // win, 0)
    base = jnp.cumsum(nwin, axis=1) - nwin
    slot = jnp.arange(COMBINE_NWIN, dtype=jnp.int32)[None, :, None]
    mine = (slot >= base[:, None, :]) & (slot < (base + nwin)[:, None, :])
    win_start = jnp.sum(jnp.where(mine, first[:, None, :] + win * (slot - base[:, None, :]), 0), axis=2)
    used = jnp.any(mine, axis=2)
    slot_rows = jnp.where(used[..., None], win_start[..., None] + jnp.arange(win, dtype=jnp.int32), -1)
    slot_rows = slot_rows.reshape(-1, 1, COMBINE_NWIN * win).astype(F32)
    starts_row = jnp.zeros((1, LANES), F32).at[0, :N_EXPERTS].set(starts.astype(F32))
    return (scatter_dest, tile_expert.astype(jnp.int32), n_used.astype(jnp.int32),
            win_start.reshape(-1).astype(jnp.int32), slot_rows, starts_row)


def _interleave_gate_up(wg, wu, tf):
    e, d, ff = wg.shape
    g = wg.astype(BF16).reshape(e, d, ff // tf, tf)
    u = wu.astype(BF16).reshape(e, d, ff // tf, tf)
    return jnp.concatenate([g, u], axis=3).reshape(e, d, 2 * ff)


def _dense_block_diag(w):
    nb, bd, _ = w.shape
    eye = jnp.eye(nb, dtype=w.dtype)
    return (w[:, :, None, :] * eye[:, None, :, None]).reshape(nb * bd, nb * bd)


def _pack_lru_gates(wr, wi, windows):
    dr = _dense_block_diag(wr).astype(BF16)
    di = _dense_block_diag(wi).astype(BF16)
    cols = []
    for (c0, nc, _, _) in windows:
        cols += [dr[:, c0:c0 + nc], di[:, c0:c0 + nc]]
    return jnp.concatenate(cols, axis=1)


def _ffn_chunk(ff):
    for tf in (896, 512, 384, 256, 128):
        if ff % tf == 0:
            return tf
    raise ValueError(f"unsupported d_ff {ff}")


def kernel(x, meta_tokens, mix_norm_even, w_in_even, conv_w, conv_b, conv_ln_g, conv_ln_b, q_norm_g, k_norm_g, w_out_even, ffn_norm_even, ffn_w_gate, ffn_w_up, ffn_w_down, mix_norm_odd, w_in_odd, lru_conv_w, lru_conv_b, gate_r_w, gate_r_b, gate_i_w, gate_i_b, lru_lambda, w_out_odd, ffn_norm_odd, router_w, moe_w_gate, moe_w_up, moe_w_down):
    bsz, seq, d = x.shape
    t_real = N_META + seq
    tp = -(-t_real // TIME_TILE) * TIME_TILE
    n = bsz * tp
    assert n % ROW_TILE == 0
    depth = mix_norm_even.shape[0] + mix_norm_odd.shape[0]

    meta = jnp.broadcast_to(meta_tokens[None].astype(x.dtype), (bsz, N_META, d))
    h = jnp.concatenate([meta, x, jnp.zeros((bsz, tp - t_real, d), x.dtype)], axis=1).reshape(n, d)

    head_mean = jnp.kron(jnp.eye(MXU_DIM // SB_HEAD_DIM, dtype=F32),
                         jnp.full((SB_HEAD_DIM, SB_HEAD_DIM), 1.0 / SB_HEAD_DIM, F32)).astype(BF16)
    kk = jnp.arange(ATT_TK)
    neg_tri = -(kk[:, None] >= kk[None, :]).astype(BF16)
    row2 = lambda a: a.reshape(1, -1)

    for layer in range(depth):
        p = layer // 2
        if layer % 2 == 0:
            u, q, k, v = _even_in_proj(h, row2(mix_norm_even[p]), w_in_even[p].astype(BF16),
                                       row2(jnp.tile(q_norm_g[p], SB_HEADS)),
                                       row2(jnp.tile(k_norm_g[p], SB_HEADS)), head_mean)
            u = _conv_module(u.reshape(bsz, tp, CONV_CH), conv_w[p], row2(conv_b[p]),
                             row2(conv_ln_g[p]), row2(conv_ln_b[p]))
            o = _attention(q.reshape(bsz, tp, SB_WIDTH), k.reshape(bsz, tp, SB_WIDTH),
                           v.reshape(bsz, tp, SB_WIDTH), neg_tri, t_real)
            tf = _ffn_chunk(ffn_w_gate.shape[-1])
            h = _even_ffn(h, u.reshape(n, CONV_CH), o.reshape(n, SB_WIDTH), w_out_even[p].astype(BF16),
                          row2(ffn_norm_even[p]),
                          _interleave_gate_up(ffn_w_gate[p][None], ffn_w_up[p][None], tf)[0],
                          ffn_w_down[p].astype(BF16), tf)
        else:
            lw = lru_lambda.shape[-1]
            windows = _gate_windows(lw, lw // LRU_BLOCKS)
            h = _lru_mixer(h.reshape(bsz, tp, d), row2(mix_norm_odd[p]), w_in_odd[p].astype(BF16),
                           lru_conv_w[p], row2(lru_conv_b[p]),
                           _pack_lru_gates(gate_r_w[p], gate_i_w[p], windows),
                           row2(gate_r_b[p]), row2(gate_i_b[p]), row2(lru_lambda[p]),
                           w_out_odd[p].astype(BF16), windows).reshape(n, d)
            rw = jnp.pad(router_w[p], ((0, 0), (0, LANES - N_EXPERTS)))
            kk = jnp.arange(ROW_TILE)
            ltri = (kk[:, None] > kk[None, :]).astype(BF16)
            xn, info, tok, w1b, w2b, before, counts = _router(h, row2(ffn_norm_odd[p]), rw, ltri, tp, t_real)
            tg = MOE_TILE
            n_rows = -(-(2 * bsz * t_real + N_EXPERTS * (tg - 1)) // tg) * tg
            sdest, tile_expert, n_used, win_start, slot_rows, starts_row = _routing_tables(
                info, before, counts, n_rows)
            sub = d // LANES
            assert sub == SUBLANES
            n_alloc = n_rows + 2 * COMBINE_TILE
            xs = _row_scatter(sdest, xn.reshape(n, sub, LANES), n_alloc).reshape(n_alloc, d)
            ys = _grouped_ffn(tile_expert, n_used, xs, n_rows, moe_w_gate[p].astype(BF16),
                              moe_w_up[p].astype(BF16), moe_w_down[p].astype(BF16))
            h = _combine(win_start, h, tok, w1b, w2b, starts_row, slot_rows, ys)
    return h.reshape(bsz, tp, d)[:, N_META:t_real]
```

```python
import functools
import math

import jax
import jax.numpy as jnp
from jax import lax
from jax.experimental import pallas as pl
from jax.experimental.pallas import tpu as pltpu

F32 = jnp.float32
BF16 = jnp.bfloat16

EPS = 1e-6
N_META = 16
CONV_CH = 512
CONV_WIDTH = 31
SB_HEADS = 8
SB_HEAD_DIM = 64
SB_WIDTH = SB_HEADS * SB_HEAD_DIM
LRU_BLOCKS = 16
LRU_CONV_WIDTH = 4
LRU_C = 8.0
N_EXPERTS = 8

LANES = 128
MXU_DIM = 256
TIME_TILE = 256
VMEM_LIMIT = 50 * 1024 * 1024

ROW_TILE = 512
CONV_TT = 128
CONV_HALO = 32
CONV_CHUNK = 32
ATT_TQ = 256
ATT_TK = 256
LOG2E = 1.4426950408889634
MASKED_EXPONENT = -1e30
SOFTPLUS_CUTOFF = 126.0
LRU_TT = 256
LRU_GATE_TILE = 256
MOE_TILE = 512
MOE_FF_TILE = 1792
COMBINE_TILE = 512
ROW_WIN = 16
ROW_NWIN = 2 * COMBINE_TILE // ROW_WIN + N_EXPERTS
SUBLANES = 8
BF16_ROWS = 16


def _dot(a, b):
    return jnp.dot(a, b, preferred_element_type=F32)


def _params(*sem):
    return pltpu.CompilerParams(dimension_semantics=sem, vmem_limit_bytes=VMEM_LIMIT)


def _sigmoid(x):
    return 0.5 * jnp.tanh(0.5 * x) + 0.5


def _rms(x, g):
    ms = jnp.mean(x * x, axis=-1, keepdims=True)
    return x * lax.rsqrt(ms + EPS) * g


def _even_in_kernel(h_ref, g_ref, w_ref, qg_ref, kg_ref, hm_ref, u_ref, q_ref, k_ref, v_ref):
    xn = _rms(h_ref[...], g_ref[...]).astype(BF16)
    a = _dot(xn, w_ref[:, 0:CONV_CH])
    gate = _dot(xn, w_ref[:, CONV_CH:2 * CONV_CH])
    u_ref[...] = (a * _sigmoid(gate)).astype(BF16)

    def head_norm(y, gain):
        yy = (y * y).astype(BF16)
        parts = [_dot(yy[:, c:c + MXU_DIM], hm_ref[...]) for c in range(0, SB_WIDTH, MXU_DIM)]
        ms = jnp.concatenate(parts, axis=1)
        return y * lax.rsqrt(ms + EPS) * gain

    c0 = 2 * CONV_CH
    q = head_norm(_dot(xn, w_ref[:, c0:c0 + SB_WIDTH]), qg_ref[...])
    q_ref[...] = (q * (LOG2E / math.sqrt(SB_HEAD_DIM))).astype(BF16)
    k = head_norm(_dot(xn, w_ref[:, c0 + SB_WIDTH:c0 + 2 * SB_WIDTH]), kg_ref[...])
    k_ref[...] = k.astype(BF16)
    v_ref[...] = _dot(xn, w_ref[:, c0 + 2 * SB_WIDTH:c0 + 3 * SB_WIDTH]).astype(BF16)


def _even_in_proj(h, g, w, qg, kg, hm):
    n, d = h.shape
    tm = ROW_TILE
    row = lambda i: (i, 0)
    const = lambda i: (0, 0)
    out = jax.ShapeDtypeStruct((n, SB_WIDTH), BF16)
    return pl.pallas_call(
        _even_in_kernel,
        grid=(n // tm,),
        in_specs=[pl.BlockSpec((tm, d), row), pl.BlockSpec((1, d), const),
                  pl.BlockSpec(w.shape, const), pl.BlockSpec((1, SB_WIDTH), const),
                  pl.BlockSpec((1, SB_WIDTH), const), pl.BlockSpec(hm.shape, const)],
        out_specs=[pl.BlockSpec((tm, SB_WIDTH), row)] * 4,
        out_shape=[out] * 4,
        compiler_params=_params("parallel"),
        name="even_in_proj",
    )(h, g, w, qg, kg, hm)


def _conv_kernel(cur_ref, halo_ref, w_ref, b_ref, lg_ref, lb_ref, o_ref, buf_ref, sh_ref):
    i = pl.program_id(1)
    tt = cur_ref.shape[1]
    rows = CONV_HALO + tt
    halo = halo_ref[0].astype(F32)
    buf_ref[0:CONV_HALO, :] = jnp.where(i > 0, halo, 0.0)
    buf_ref[CONV_HALO:rows, :] = cur_ref[0].astype(F32)
    for s in range(1, SUBLANES):
        sh_ref[s - 1, 0:rows - SUBLANES, :] = buf_ref[s:s + rows - SUBLANES, :]
    off = CONV_HALO - (CONV_WIDTH - 1)
    for c in range(tt // CONV_CHUNK):
        r0 = c * CONV_CHUNK
        acc = jnp.broadcast_to(b_ref[...], (CONV_CHUNK, CONV_CH))
        for k in range(CONV_WIDTH):
            shift = (off + k) % SUBLANES
            base = r0 + off + k - shift
            if shift == 0:
                tap = buf_ref[base:base + CONV_CHUNK, :]
            else:
                tap = sh_ref[shift - 1, base:base + CONV_CHUNK, :]
            acc = acc + w_ref[k:k + 1, :] * tap
        mu = jnp.mean(acc, axis=-1, keepdims=True)
        xc = acc - mu
        var = jnp.mean(xc * xc, axis=-1, keepdims=True)
        y = xc * lax.rsqrt(var + EPS) * lg_ref[...] + lb_ref[...]
        o_ref[0, r0:r0 + CONV_CHUNK, :] = (y * _sigmoid(y)).astype(BF16)


def _conv_module(u, w, b, lg, lb):
    bsz, tp, c = u.shape
    tt = CONV_TT
    per = tt // CONV_HALO
    const = lambda bi, i: (0, 0)
    return pl.pallas_call(
        _conv_kernel,
        grid=(bsz, tp // tt),
        in_specs=[pl.BlockSpec((1, tt, c), lambda bi, i: (bi, i, 0)),
                  pl.BlockSpec((1, CONV_HALO, c), lambda bi, i: (bi, jnp.maximum(i * per - 1, 0), 0)),
                  pl.BlockSpec(w.shape, const), pl.BlockSpec((1, c), const),
                  pl.BlockSpec((1, c), const), pl.BlockSpec((1, c), const)],
        out_specs=pl.BlockSpec((1, tt, c), lambda bi, i: (bi, i, 0)),
        out_shape=jax.ShapeDtypeStruct(u.shape, BF16),
        scratch_shapes=[pltpu.VMEM((CONV_HALO + tt, c), F32),
                        pltpu.VMEM((SUBLANES - 1, CONV_HALO + tt - SUBLANES, c), F32)],
        compiler_params=_params("parallel", "parallel"),
        name="conv_module",
    )(u, u, w, b, lg, lb)


def _attn_kernel(*refs, tq, tile0, aliased, n_compute):
    if aliased:
        refs = refs[1:]
    o_ref = refs[4]
    step = pl.program_id(1)

    @pl.when(step < n_compute)
    def _():
        _attn_tile(*refs, tq=tq, row0=(step + tile0) * ATT_TQ)

    @pl.when(step >= n_compute)
    def _():
        o_ref[...] = jnp.zeros(o_ref.shape, BF16)


def _attn_tile(q_ref, k_ref, v_ref, tri_ref, o_ref, q2_ref, e_ref, tot_ref, acc_ref, car_ref, *, tq, row0):
    tk = ATT_TK
    npairs = q_ref.shape[2] // LANES
    lane_q = lax.broadcasted_iota(jnp.int32, (tq, LANES), 1)

    for hp in range(npairs):
        q = q_ref[0, 0:tq, hp * LANES:(hp + 1) * LANES]
        zero = jnp.zeros_like(q)
        q2_ref[hp, 0:tq, :] = jnp.where(lane_q < SB_HEAD_DIM, q, zero)
        q2_ref[hp, tq:2 * tq, :] = jnp.where(lane_q >= SB_HEAD_DIM, q, zero)
        car_ref[hp] = jnp.zeros((2 * tq, LANES), F32)
        acc_ref[hp] = jnp.zeros((2 * tq, LANES), F32)

    def stage_a(j, mask):
        k0 = pl.multiple_of(j * tk, tk)
        for hp in range(npairs):
            kt = k_ref[0, pl.ds(k0, tk), hp * LANES:(hp + 1) * LANES]
            z = lax.dot_general(q2_ref[hp], kt, (((1,), (1,)), ((), ())), preferred_element_type=F32)
            sp = jnp.where(z > SOFTPLUS_CUTOFF, z, jnp.log(1.0 + jnp.exp2(z)) * LOG2E)
            if mask is not None:
                sp = jnp.where(mask, sp, 0.0)
            e = z + _dot(sp.astype(BF16), tri_ref[...])
            if mask is not None:
                e = jnp.where(mask, e, MASKED_EXPONENT)
            e_ref[hp] = e
            tot_ref[hp] = jnp.broadcast_to(jnp.sum(sp, axis=1, keepdims=True), (2 * tq, LANES))

    def stage_b(j):
        k0 = pl.multiple_of(j * tk, tk)
        for hp in range(npairs):
            car = car_ref[hp]
            w = jnp.exp2(e_ref[hp] + jnp.concatenate([car] * (tk // LANES), axis=1)).astype(BF16)
            vt = v_ref[0, pl.ds(k0, tk), hp * LANES:(hp + 1) * LANES]
            acc_ref[hp] += _dot(w, vt)
            car_ref[hp] = car - tot_ref[hp]

    j_last = row0 // tk
    row = lax.broadcasted_iota(jnp.int32, (2 * tq, tk), 0)
    col = lax.broadcasted_iota(jnp.int32, (2 * tq, tk), 1)
    row = jnp.where(row >= tq, row - tq, row) + row0
    stage_a(j_last, (col + j_last * tk) < row)

    def body(jj, carry):
        j = j_last - 1 - jj
        stage_b(j + 1)
        stage_a(j, None)
        return carry

    lax.fori_loop(0, j_last, body, 0)
    stage_b(0)
    if tq < o_ref.shape[1]:
        o_ref[0, tq:, :] = jnp.zeros((o_ref.shape[1] - tq, o_ref.shape[2]), BF16)
    for hp in range(npairs):
        o_ref[0, 0:tq, hp * LANES:(hp + 1) * LANES] = jnp.where(
            lane_q < SB_HEAD_DIM, acc_ref[hp, 0:tq, :], acc_ref[hp, tq:2 * tq, :]).astype(BF16)


def _attention(q, k, v, tri, t_real):
    bsz, tp, width = q.shape
    bq, tk = ATT_TQ, ATT_TK
    assert bq <= tk and tk % bq == 0
    npairs = width // LANES
    n_full = t_real // bq
    rem = -(-(t_real - n_full * bq) // BF16_ROWS) * BF16_ROWS

    def call(tq, tile0, n_compute, n_tiles, prev):
        blk = lambda b, i: (b, i + tile0, 0)
        full = lambda b, i: (b, 0, 0)
        in_specs = [pl.BlockSpec((1, bq, width), blk), pl.BlockSpec((1, tp, width), full),
                    pl.BlockSpec((1, tp, width), full), pl.BlockSpec(tri.shape, lambda b, i: (0, 0))]
        args = [q, k, v, tri]
        if prev is not None:
            in_specs.insert(0, pl.BlockSpec(memory_space=pl.ANY))
            args.insert(0, prev)
        return pl.pallas_call(
            functools.partial(_attn_kernel, tq=tq, tile0=tile0, aliased=prev is not None, n_compute=n_compute),
            grid=(bsz, n_tiles),
            in_specs=in_specs,
            out_specs=pl.BlockSpec((1, bq, width), blk),
            out_shape=jax.ShapeDtypeStruct(q.shape, BF16),
            input_output_aliases={} if prev is None else {0: 0},
            scratch_shapes=[pltpu.VMEM((npairs, 2 * tq, LANES), BF16),
                            pltpu.VMEM((npairs, 2 * tq, tk), F32),
                            pltpu.VMEM((npairs, 2 * tq, LANES), F32),
                            pltpu.VMEM((npairs, 2 * tq, LANES), F32),
                            pltpu.VMEM((npairs, 2 * tq, LANES), F32)],
            compiler_params=_params("parallel", "arbitrary"),
            name="stick_breaking" if prev is None else "stick_breaking_tail",
        )(*args)

    out = call(bq, 0, n_full, tp // bq, None)
    if rem:
        out = call(rem, n_full, 1, 1, out)
    return out


def _even_ffn_kernel(h_ref, u_ref, o_ref, wo_ref, g_ref, wgu_ref, wd_ref, out_ref, xn_ref, *, tf):
    @pl.when(pl.program_id(1) == 0)
    def _():
        h1 = h_ref[...] + _dot(u_ref[...], wo_ref[0:CONV_CH, :]) + _dot(o_ref[...], wo_ref[CONV_CH:, :])
        out_ref[...] = h1
        xn_ref[...] = _rms(h1, g_ref[...]).astype(BF16)

    gu = _dot(xn_ref[...], wgu_ref[...])
    g = gu[:, 0:tf]
    mid = g * _sigmoid(g) * gu[:, tf:2 * tf]
    out_ref[...] += _dot(mid.astype(BF16), wd_ref[...])


def _even_ffn(h, u, o, wo, g, wgu, wd, tf):
    n, d = h.shape
    ff = wd.shape[0]
    tm = ROW_TILE
    row = lambda i, f: (i, 0)
    const = lambda i, f: (0, 0)
    return pl.pallas_call(
        functools.partial(_even_ffn_kernel, tf=tf),
        grid=(n // tm, ff // tf),
        in_specs=[pl.BlockSpec((tm, d), row), pl.BlockSpec((tm, CONV_CH), row),
                  pl.BlockSpec((tm, SB_WIDTH), row), pl.BlockSpec(wo.shape, const),
                  pl.BlockSpec((1, d), const),
                  pl.BlockSpec((d, 2 * tf), lambda i, f: (0, f)),
                  pl.BlockSpec((tf, d), lambda i, f: (f, 0))],
        out_specs=pl.BlockSpec((tm, d), row),
        out_shape=jax.ShapeDtypeStruct((n, d), F32),
        scratch_shapes=[pltpu.VMEM((tm, d), BF16)],
        compiler_params=_params("parallel", "arbitrary"),
        name="even_out_swiglu",
    )(h, u, o, wo, g, wgu, wd)


def _row_scatter_kernel(win_ref, x_ref, info_ref, off_ref, init_hbm, o_hbm, cbuf, sems):
    del init_hbm
    i = pl.program_id(0)
    last = pl.num_programs(0) - 1
    n_slots = cbuf.shape[1]
    buf = i % 2

    def drained(b):
        return pltpu.make_async_copy(cbuf.at[b], o_hbm.at[pl.ds(0, n_slots)], sems.at[b])

    info = info_ref[...]
    sub = lax.broadcasted_iota(jnp.int32, info.shape, 0)
    valid = info[ROUTE_VALID:ROUTE_VALID + 1, :] > 0.5

    def local_slot(e_row, r_row):
        expert = info[e_row:e_row + 1, :].astype(jnp.int32)
        off = jnp.sum(jnp.where(sub == expert, off_ref[:, 0:1], 0.0), axis=0, keepdims=True)
        return jnp.where(valid, off + info[r_row:r_row + 1, :], -1.0)

    slot = lax.broadcasted_iota(jnp.int32, (n_slots, info.shape[1]), 0).astype(F32)
    pick = (jnp.where(slot == local_slot(ROUTE_E1, ROUTE_R1), 1.0, 0.0)
            + jnp.where(slot == local_slot(ROUTE_E2, ROUTE_R2), 1.0, 0.0)).astype(BF16)
    rows = _dot(pick, x_ref[...]).astype(cbuf.dtype)

    @pl.when(i >= 2)
    def _():
        drained(buf).wait()

    cbuf[buf] = rows
    for s in range(ROW_NWIN):
        dst = pl.multiple_of(win_ref[i * ROW_NWIN + s], ROW_WIN)
        pltpu.make_async_copy(cbuf.at[buf, pl.ds(s * ROW_WIN, ROW_WIN)], o_hbm.at[pl.ds(dst, ROW_WIN)],
                              sems.at[buf]).start(priority=s % 2)

    @pl.when(i == last)
    def _():
        drained(buf).wait()

        @pl.when(i >= 1)
        def _():
            drained(1 - buf).wait()


def _row_scatter(win_rows, x, info, local_off, n_rows_alloc):
    n, d = x.shape
    tm = COMBINE_TILE
    init = jnp.zeros((n_rows_alloc, d), x.dtype)
    return pl.pallas_call(
        _row_scatter_kernel,
        grid_spec=pltpu.PrefetchScalarGridSpec(
            num_scalar_prefetch=1, grid=(n // tm,),
            in_specs=[pl.BlockSpec((tm, d), lambda i, w: (i, 0)), pl.BlockSpec((SUBLANES, tm), lambda i, w: (i, 0)),
                      pl.BlockSpec((SUBLANES, LANES), lambda i, w: (i, 0)), pl.BlockSpec(memory_space=pl.ANY)],
            out_specs=pl.BlockSpec(memory_space=pl.ANY),
            scratch_shapes=[pltpu.VMEM((2, ROW_NWIN * ROW_WIN, d), x.dtype), pltpu.SemaphoreType.DMA((2,))]),
        out_shape=jax.ShapeDtypeStruct(init.shape, init.dtype),
        input_output_aliases={4: 0},
        compiler_params=_params("arbitrary"),
        name="moe_row_scatter",
    )(win_rows, x, info, local_off, init)


def _gmm_kernel(te_ref, nu_ref, xs_ref, wg_ref, wu_ref, wd_ref, o_ref, acc_ref):
    p = pl.program_id(0)
    f = pl.program_id(1)
    last = pl.num_programs(1) - 1
    used = p < nu_ref[0]

    @pl.when(used)
    def _():
        x = xs_ref[...]
        g = _dot(x, wg_ref[0])
        mid = g * _sigmoid(g) * _dot(x, wu_ref[0])
        part = _dot(mid.astype(BF16), wd_ref[0])

        @pl.when(f == 0)
        def _():
            acc_ref[...] = part

        @pl.when(f > 0)
        def _():
            acc_ref[...] += part

        @pl.when(f == last)
        def _():
            o_ref[...] = acc_ref[...].astype(o_ref.dtype)

    @pl.when(jnp.logical_not(used) & (f == last))
    def _():
        o_ref[...] = jnp.zeros(o_ref.shape, o_ref.dtype)


def _grouped_ffn(tile_expert, n_used, xs, n_rows, wg, wu, wd):
    d = xs.shape[1]
    ff = wd.shape[1]
    tf = MOE_FF_TILE
    nf = ff // tf
    tg = MOE_TILE
    row = lambda p, f, te, nu: (p, 0)
    fidx = lambda p, f, nu: jnp.where(p < nu[0], f, nf - 1)
    up = pl.BlockSpec((1, d, tf), lambda p, f, te, nu: (te[p], 0, fidx(p, f, nu)))
    return pl.pallas_call(
        _gmm_kernel,
        grid_spec=pltpu.PrefetchScalarGridSpec(
            num_scalar_prefetch=2, grid=(n_rows // tg, nf),
            in_specs=[pl.BlockSpec((tg, d), row), up, up,
                      pl.BlockSpec((1, tf, d), lambda p, f, te, nu: (te[p], fidx(p, f, nu), 0))],
            out_specs=pl.BlockSpec((tg, d), row),
            scratch_shapes=[pltpu.VMEM((tg, d), F32)]),
        out_shape=jax.ShapeDtypeStruct((n_rows, d), BF16),
        compiler_params=_params("arbitrary", "arbitrary"),
        name="moe_grouped_ffn",
    )(tile_expert, n_used, xs, wg, wu, wd)


def _combine_kernel(win_ref, h_ref, tok_ref, w1_ref, w2_ref, starts_ref, rows_ref, ys_hbm, o_ref, gbuf, sems):
    i = pl.program_id(0)
    tm = h_ref.shape[0]
    win = ROW_WIN

    def fetch(tile, slot):
        for s in range(ROW_NWIN):
            start = pl.multiple_of(win_ref[tile * ROW_NWIN + s], win)
            pltpu.make_async_copy(ys_hbm.at[pl.ds(start, win)], gbuf.at[slot, pl.ds(s * win, win)],
                                  sems.at[slot]).start(priority=s % 2)

    @pl.when(i == 0)
    def _():
        fetch(0, 0)

    @pl.when(i + 1 < pl.num_programs(0))
    def _():
        fetch(i + 1, (i + 1) % 2)

    slot = i % 2
    pltpu.make_async_copy(ys_hbm.at[pl.ds(0, ROW_NWIN * win)], gbuf.at[slot], sems.at[slot]).wait()

    tok = tok_ref[...]
    lane = lax.broadcasted_iota(jnp.int32, tok.shape, 1)
    field = lambda c: jnp.sum(jnp.where(lane == c, tok, 0.0), axis=1, keepdims=True)
    valid = field(ROUTE_VALID) > 0.5
    rows = rows_ref[0]
    reps = h_ref.shape[1] // LANES
    out = h_ref[...]
    for e_col, r_col, w_ref in ((ROUTE_E1, ROUTE_R1, w1_ref), (ROUTE_E2, ROUTE_R2, w2_ref)):
        start = jnp.sum(jnp.where(lane == field(e_col).astype(jnp.int32), starts_ref[...], 0.0),
                        axis=1, keepdims=True)
        dest = jnp.where(valid, start + field(r_col), -2.0)
        pick = jnp.where(dest == rows, 1.0, 0.0).astype(BF16)
        out = out + jnp.concatenate([w_ref[...]] * reps, axis=1) * _dot(pick, gbuf[slot])
    o_ref[...] = out


def _combine(win_start, h, tok, w1b, w2b, starts_row, slot_rows, ys):
    n, d = h.shape
    tm = COMBINE_TILE
    n_slots = ROW_NWIN * ROW_WIN
    row = lambda i, w: (i, 0)
    return pl.pallas_call(
        _combine_kernel,
        grid_spec=pltpu.PrefetchScalarGridSpec(
            num_scalar_prefetch=1, grid=(n // tm,),
            in_specs=[pl.BlockSpec((tm, d), row), pl.BlockSpec((tm, LANES), row), pl.BlockSpec((tm, LANES), row),
                      pl.BlockSpec((tm, LANES), row), pl.BlockSpec((1, LANES), lambda i, w: (0, 0)),
                      pl.BlockSpec((1, 1, n_slots), lambda i, w: (i, 0, 0)),
                      pl.BlockSpec(memory_space=pl.ANY)],
            out_specs=pl.BlockSpec((tm, d), row),
            scratch_shapes=[pltpu.VMEM((2, n_slots, d), ys.dtype), pltpu.SemaphoreType.DMA((2,))]),
        out_shape=jax.ShapeDtypeStruct(h.shape, F32),
        compiler_params=_params("arbitrary"),
        name="moe_combine",
    )(win_start, h, tok, w1b, w2b, starts_row, slot_rows, ys)


def _gate_windows(lw, bd):
    out = []
    for c0 in range(0, lw, LRU_GATE_TILE):
        nc = min(LRU_GATE_TILE, lw - c0)
        r0 = (c0 // bd) * bd
        r1 = ((c0 + nc - 1) // bd + 1) * bd
        r0 = (r0 // LANES) * LANES
        r1 = min(-(-r1 // LANES) * LANES, lw)
        out.append((c0, nc, r0, r1 - r0))
    return out


def _lru_kernel(h_ref, ng_ref, win_ref, cw_ref, cb_ref, wg_ref, br_ref, bi_ref, lam_ref, wo_ref,
                o_ref, xbuf_ref, hs_ref, a_ref, b_ref, y_ref, gate_ref, *, windows):
    ti = pl.program_id(1)
    tt, lw = a_ref.shape

    @pl.when(ti == 0)
    def _():
        xbuf_ref[0:8, :] = jnp.zeros((8, lw), F32)
        hs_ref[...] = jnp.zeros((8, lw), F32)

    @pl.when(ti > 0)
    def _():
        xbuf_ref[0:8, :] = xbuf_ref[tt:tt + 8, :]

    proj = _dot(_rms(h_ref[0], ng_ref[...]).astype(BF16), win_ref[...])
    gate_ref[...] = jax.nn.gelu(proj[:, 0:lw], approximate=True)
    xbuf_ref[8:8 + tt, :] = proj[:, lw:2 * lw]
    off = 8 - (LRU_CONV_WIDTH - 1)
    xc = jnp.broadcast_to(cb_ref[...], (tt, lw))
    for k in range(LRU_CONV_WIDTH):
        xc = xc + cw_ref[k:k + 1, :] * xbuf_ref[off + k:off + k + tt, :]
    xcb = xc.astype(BF16)
    sp_lam = jnp.log(1.0 + jnp.exp(-lam_ref[...]))
    log2_a_per_r = (-LRU_C * LOG2E) * sp_lam
    wcol = 0
    for (c0, nc, r0, nr) in windows:
        rg = _dot(xcb[:, r0:r0 + nr], wg_ref[r0:r0 + nr, wcol:wcol + 2 * nc])
        wcol += 2 * nc
        r = _sigmoid(rg[:, 0:nc] + br_ref[:, c0:c0 + nc])
        ig = _sigmoid(rg[:, nc:2 * nc] + bi_ref[:, c0:c0 + nc])
        a = jnp.exp2(r * log2_a_per_r[:, c0:c0 + nc])
        a_ref[:, c0:c0 + nc] = a
        one_m_a2 = 1.0 - a * a
        root = jnp.where(one_m_a2 > 0.0, one_m_a2 * lax.rsqrt(one_m_a2), 0.0)
        b_ref[:, c0:c0 + nc] = root * (ig * xc[:, c0:c0 + nc])

    rowi = lax.broadcasted_iota(jnp.int32, (8, lw), 0)

    def group(gi, hprev):
        r0 = pl.multiple_of(gi * 8, 8)
        a = a_ref[pl.ds(r0, 8), :]
        b = b_ref[pl.ds(r0, 8), :]
        for s in (1, 2, 4):
            m = rowi >= s
            a_sh = jnp.where(m, pltpu.roll(a, s, axis=0), 1.0)
            b_sh = jnp.where(m, pltpu.roll(b, s, axis=0), 0.0)
            b = a * b_sh + b
            a = a * a_sh
        hrows = a * hprev + b
        y_ref[pl.ds(r0, 8), :] = hrows
        return jnp.broadcast_to(hrows[7:8, :], (8, lw))

    hs_ref[...] = lax.fori_loop(0, tt // 8, group, hs_ref[...])
    gy = (gate_ref[...] * y_ref[...]).astype(BF16)
    o_ref[0] = h_ref[0] + _dot(gy, wo_ref[...])


def _lru_mixer(h, ng, win, cw, cb, wg, br, bi, lam, wo, windows):
    bsz, tp, d = h.shape
    lw = lam.shape[-1]
    tt = LRU_TT
    const = lambda b, t: (0, 0)
    blk = lambda b, t: (b, t, 0)
    resident = lambda w: pl.BlockSpec(w.shape, const, pipeline_mode=pl.Buffered(1))
    return pl.pallas_call(
        functools.partial(_lru_kernel, windows=windows),
        grid=(bsz, tp // tt),
        in_specs=[pl.BlockSpec((1, tt, d), blk), pl.BlockSpec((1, d), const), resident(win),
                  pl.BlockSpec(cw.shape, const), pl.BlockSpec((1, lw), const), resident(wg),
                  pl.BlockSpec((1, lw), const), pl.BlockSpec((1, lw), const),
                  pl.BlockSpec((1, lw), const), resident(wo)],
        out_specs=pl.BlockSpec((1, tt, d), blk),
        out_shape=jax.ShapeDtypeStruct(h.shape, F32),
        scratch_shapes=[pltpu.VMEM((8 + tt, lw), F32), pltpu.VMEM((8, lw), F32),
                        pltpu.VMEM((tt, lw), F32), pltpu.VMEM((tt, lw), F32), pltpu.VMEM((tt, lw), F32),
                        pltpu.VMEM((tt, lw), F32)],
        compiler_params=_params("arbitrary", "arbitrary"),
        name="rglru_mixer",
    )(h, ng, win, cw, cb, wg, br, bi, lam, wo)


ROUTE_E1, ROUTE_E2, ROUTE_R1, ROUTE_R2, ROUTE_VALID = range(5)


def _router_kernel(h_ref, g_ref, rw_ref, ltri_ref, xn_ref, info_ref, tok_ref, w1_ref, w2_ref, before_ref, cnt_ref,
                   run_ref, *, tp, t_real):
    i = pl.program_id(0)
    tm = h_ref.shape[0]

    @pl.when(i == 0)
    def _():
        run_ref[...] = jnp.zeros(run_ref.shape, F32)

    xn = _rms(h_ref[...], g_ref[...])
    xn_ref[...] = xn.astype(BF16)
    split = lambda a: (a.astype(BF16), (a - a.astype(BF16).astype(F32)).astype(BF16))
    (x_hi, x_lo), (r_hi, r_lo) = split(xn), split(rw_ref[...])
    logits = _dot(x_hi, r_hi) + (_dot(x_hi, r_lo) + _dot(x_lo, r_hi))
    lane = lax.broadcasted_iota(jnp.int32, logits.shape, 1)
    neg = jnp.float32(-jnp.inf)
    logits = jnp.where(lane < N_EXPERTS, logits, neg)
    top1 = jnp.max(logits, axis=1, keepdims=True)
    idx1 = jnp.min(jnp.where(logits == top1, lane, LANES), axis=1, keepdims=True)
    rest = jnp.where(lane == idx1, neg, logits)
    top2 = jnp.max(rest, axis=1, keepdims=True)
    idx2 = jnp.min(jnp.where(rest == top2, lane, LANES), axis=1, keepdims=True)
    w1 = 1.0 / (1.0 + jnp.exp(top2 - top1))

    rowg = (lax.broadcasted_iota(jnp.int32, (tm, 1), 0) + i * tm).astype(F32)
    pos = rowg - jnp.floor((rowg + 0.5) / tp) * tp
    valid = jnp.where(pos < t_real, 1.0, 0.0)
    onehot = (jnp.where(lane == idx1, 1.0, 0.0) + jnp.where(lane == idx2, 1.0, 0.0)) * valid
    before_ref[...] = jnp.broadcast_to(run_ref[...], before_ref.shape)
    before = _dot(ltri_ref[...], onehot.astype(BF16)) + run_ref[...]
    rank1 = jnp.sum(jnp.where(lane == idx1, before, 0.0), axis=1, keepdims=True)
    rank2 = jnp.sum(jnp.where(lane == idx2, before, 0.0), axis=1, keepdims=True)
    run_ref[...] += jnp.ceil(jnp.sum(onehot, axis=0, keepdims=True) / ROW_WIN) * ROW_WIN
    cnt_ref[...] = run_ref[...]
    w1_ref[...] = jnp.broadcast_to(w1, (tm, LANES))
    w2_ref[...] = jnp.broadcast_to(1.0 - w1, (tm, LANES))
    fields = {ROUTE_E1: idx1.astype(F32), ROUTE_E2: idx2.astype(F32),
              ROUTE_R1: rank1, ROUTE_R2: rank2, ROUTE_VALID: valid}
    info = jnp.zeros(logits.shape, F32)
    for col, val in fields.items():
        info = jnp.where(lane == col, val, info)
    tok_ref[...] = info
    info_ref[...] = info.T[0:SUBLANES, :]


def _router(h, g, rw, ltri, tp, t_real):
    n, d = h.shape
    tm = ROW_TILE
    row = lambda i: (i, 0)
    const = lambda i: (0, 0)
    return pl.pallas_call(
        functools.partial(_router_kernel, tp=tp, t_real=t_real),
        grid=(n // tm,),
        in_specs=[pl.BlockSpec((tm, d), row), pl.BlockSpec((1, d), const), pl.BlockSpec(rw.shape, const),
                  pl.BlockSpec(ltri.shape, const)],
        out_specs=[pl.BlockSpec((tm, d), row), pl.BlockSpec((SUBLANES, tm), row),
                   pl.BlockSpec((tm, LANES), row), pl.BlockSpec((tm, LANES), row),
                   pl.BlockSpec((tm, LANES), row), pl.BlockSpec((SUBLANES, LANES), row),
                   pl.BlockSpec((1, LANES), const)],
        out_shape=[jax.ShapeDtypeStruct((n, d), BF16), jax.ShapeDtypeStruct((n // tm * SUBLANES, tm), F32),
                   jax.ShapeDtypeStruct((n, LANES), F32), jax.ShapeDtypeStruct((n, LANES), F32),
                   jax.ShapeDtypeStruct((n, LANES), F32), jax.ShapeDtypeStruct((n // tm * SUBLANES, LANES), F32),
                   jax.ShapeDtypeStruct((1, LANES), F32)],
        scratch_shapes=[pltpu.VMEM((1, LANES), F32)],
        compiler_params=_params("arbitrary"),
        name="router",
    )(h, g, rw, ltri)


def _routing_tables(before, counts, n_rows):
    tg = MOE_TILE
    win = ROW_WIN
    cnt = counts[0, :N_EXPERTS].astype(jnp.int32)
    padded = (cnt + tg - 1) // tg * tg
    ends = jnp.cumsum(padded)
    starts = ends - padded
    tile_start = jnp.arange(n_rows // tg, dtype=jnp.int32) * tg
    tile_expert = jnp.minimum(jnp.sum(tile_start[:, None] >= ends[None, :], axis=1), N_EXPERTS - 1)
    n_used = (ends[-1] // tg).reshape(1)
    bef = before.reshape(-1, SUBLANES, LANES)[:, 0, :N_EXPERTS].astype(jnp.int32)
    nwin = (jnp.concatenate([bef[1:], cnt[None, :]], axis=0) - bef) // win
    base = jnp.cumsum(nwin, axis=1) - nwin
    slot = jnp.arange(ROW_NWIN, dtype=jnp.int32)[None, :, None]
    mine = (slot >= base[:, None, :]) & (slot < (base + nwin)[:, None, :])
    rows = jnp.sum(jnp.where(mine, (starts[None, :] + bef)[:, None, :] + win * (slot - base[:, None, :]), 0), axis=2)
    used = jnp.any(mine, axis=2)
    parity = (jnp.arange(bef.shape[0], dtype=jnp.int32) % 2)[:, None]
    scatter_rows = jnp.where(used, rows, n_rows + win * (parity * ROW_NWIN + slot[:, :, 0]))
    gather_rows = jnp.where(used, rows, 0)
    slot_rows = jnp.where(used[..., None], rows[..., None] + jnp.arange(win, dtype=jnp.int32), -1)
    slot_rows = slot_rows.reshape(-1, 1, ROW_NWIN * win).astype(F32)
    starts_row = jnp.zeros((1, LANES), F32).at[0, :N_EXPERTS].set(starts.astype(F32))
    local_off = jnp.zeros(bef.shape[:1] + (SUBLANES,), jnp.int32).at[:, :N_EXPERTS].set(win * base - bef)
    local_off = jnp.broadcast_to(local_off.astype(F32)[:, :, None], local_off.shape + (LANES,)).reshape(-1, LANES)
    return (tile_expert.astype(jnp.int32), n_used.astype(jnp.int32), scatter_rows.reshape(-1).astype(jnp.int32),
            gather_rows.reshape(-1).astype(jnp.int32), slot_rows, starts_row, local_off)


def _interleave_gate_up(wg, wu, tf):
    e, d, ff = wg.shape
    g = wg.astype(BF16).reshape(e, d, ff // tf, tf)
    u = wu.astype(BF16).reshape(e, d, ff // tf, tf)
    return jnp.concatenate([g, u], axis=3).reshape(e, d, 2 * ff)


def _dense_block_diag(w):
    nb, bd, _ = w.shape
    eye = jnp.eye(nb, dtype=w.dtype)
    return (w[:, :, None, :] * eye[:, None, :, None]).reshape(nb * bd, nb * bd)


def _pack_lru_gates(wr, wi, windows):
    dr = _dense_block_diag(wr).astype(BF16)
    di = _dense_block_diag(wi).astype(BF16)
    cols = []
    for (c0, nc, _, _) in windows:
        cols += [dr[:, c0:c0 + nc], di[:, c0:c0 + nc]]
    return jnp.concatenate(cols, axis=1)


def _ffn_chunk(ff):
    for tf in (896, 512, 384, 256, 128):
        if ff % tf == 0:
            return tf
    raise ValueError(f"unsupported d_ff {ff}")


def kernel(x, meta_tokens, mix_norm_even, w_in_even, conv_w, conv_b, conv_ln_g, conv_ln_b, q_norm_g, k_norm_g, w_out_even, ffn_norm_even, ffn_w_gate, ffn_w_up, ffn_w_down, mix_norm_odd, w_in_odd, lru_conv_w, lru_conv_b, gate_r_w, gate_r_b, gate_i_w, gate_i_b, lru_lambda, w_out_odd, ffn_norm_odd, router_w, moe_w_gate, moe_w_up, moe_w_down):
    bsz, seq, d = x.shape
    t_real = N_META + seq
    tp = -(-t_real // TIME_TILE) * TIME_TILE
    n = bsz * tp
    assert n % ROW_TILE == 0
    depth = mix_norm_even.shape[0] + mix_norm_odd.shape[0]

    meta = jnp.broadcast_to(meta_tokens[None].astype(x.dtype), (bsz, N_META, d))
    h = jnp.concatenate([meta, x, jnp.zeros((bsz, tp - t_real, d), x.dtype)], axis=1).reshape(n, d)

    head_mean = jnp.kron(jnp.eye(MXU_DIM // SB_HEAD_DIM, dtype=F32),
                         jnp.full((SB_HEAD_DIM, SB_HEAD_DIM), 1.0 / SB_HEAD_DIM, F32)).astype(BF16)
    kk = jnp.arange(ATT_TK)
    neg_tri = -(kk[:, None] >= kk[None, :]).astype(BF16)
    row2 = lambda a: a.reshape(1, -1)

    for layer in range(depth):
        p = layer // 2
        if layer % 2 == 0:
            u, q, k, v = _even_in_proj(h, row2(mix_norm_even[p]), w_in_even[p].astype(BF16),
                                       row2(jnp.tile(q_norm_g[p], SB_HEADS)),
                                       row2(jnp.tile(k_norm_g[p], SB_HEADS)), head_mean)
            u = _conv_module(u.reshape(bsz, tp, CONV_CH), conv_w[p], row2(conv_b[p]),
                             row2(conv_ln_g[p]), row2(conv_ln_b[p]))
            o = _attention(q.reshape(bsz, tp, SB_WIDTH), k.reshape(bsz, tp, SB_WIDTH),
                           v.reshape(bsz, tp, SB_WIDTH), neg_tri, t_real)
            tf = _ffn_chunk(ffn_w_gate.shape[-1])
            h = _even_ffn(h, u.reshape(n, CONV_CH), o.reshape(n, SB_WIDTH), w_out_even[p].astype(BF16),
                          row2(ffn_norm_even[p]),
                          _interleave_gate_up(ffn_w_gate[p][None], ffn_w_up[p][None], tf)[0],
                          ffn_w_down[p].astype(BF16), tf)
        else:
            lw = lru_lambda.shape[-1]
            windows = _gate_windows(lw, lw // LRU_BLOCKS)
            h = _lru_mixer(h.reshape(bsz, tp, d), row2(mix_norm_odd[p]), w_in_odd[p].astype(BF16),
                           lru_conv_w[p], row2(lru_conv_b[p]),
                           _pack_lru_gates(gate_r_w[p], gate_i_w[p], windows),
                           row2(gate_r_b[p]), row2(gate_i_b[p]), row2(lru_lambda[p]),
                           w_out_odd[p].astype(BF16), windows).reshape(n, d)
            rw = jnp.pad(router_w[p], ((0, 0), (0, LANES - N_EXPERTS)))
            kk = jnp.arange(ROW_TILE)
            ltri = (kk[:, None] > kk[None, :]).astype(BF16)
            xn, info, tok, w1b, w2b, before, counts = _router(h, row2(ffn_norm_odd[p]), rw, ltri, tp, t_real)
            tg = MOE_TILE
            assert ROW_TILE == COMBINE_TILE
            n_tiles = n // COMBINE_TILE
            n_rows = 2 * bsz * t_real + n_tiles * N_EXPERTS * (ROW_WIN - 1) + N_EXPERTS * (tg - 1)
            n_rows = -(-n_rows // tg) * tg
            tile_expert, n_used, scatter_rows, gather_rows, slot_rows, starts_row, local_off = _routing_tables(
                before, counts, n_rows)
            xs = _row_scatter(scatter_rows, xn, info, local_off, n_rows + 2 * ROW_NWIN * ROW_WIN)
            ys = _grouped_ffn(tile_expert, n_used, xs, n_rows, moe_w_gate[p].astype(BF16),
                              moe_w_up[p].astype(BF16), moe_w_down[p].astype(BF16))
            h = _combine(gather_rows, h, tok, w1b, w2b, starts_row, slot_rows, ys)
    return h.reshape(bsz, tp, d)[:, N_META:t_real]
```

```python
import functools
import math

import jax
import jax.numpy as jnp
from jax import lax
from jax.experimental import pallas as pl
from jax.experimental.pallas import tpu as pltpu

F32 = jnp.float32
BF16 = jnp.bfloat16

EPS = 1e-6
N_META = 16
CONV_CH = 512
CONV_WIDTH = 31
SB_HEADS = 8
SB_HEAD_DIM = 64
SB_WIDTH = SB_HEADS * SB_HEAD_DIM
LRU_BLOCKS = 16
LRU_CONV_WIDTH = 4
LRU_C = 8.0
N_EXPERTS = 8

LANES = 128
MXU_DIM = 256
TIME_TILE = 256
VMEM_LIMIT = 50 * 1024 * 1024

ROW_TILE = 512
CONV_TT = 128
CONV_HALO = 32
CONV_CHUNK = 32
ATT_TQ = 256
ATT_TK = 256
LOG2E = 1.4426950408889634
MASKED_EXPONENT = -1e30
SOFTPLUS_CUTOFF = 126.0
LRU_ROWS = 256
LRU_SCAN_UNROLL = 4
LRU_GATE_TILE = 256
MOE_TILE = 512
MOE_FF_TILE = 1792
COMBINE_TILE = 512
ROW_WIN = 16
ROW_NWIN = 2 * COMBINE_TILE // ROW_WIN + N_EXPERTS
SUBLANES = 8
BF16_ROWS = 16


def _dot(a, b):
    return jnp.dot(a, b, preferred_element_type=F32)


def _params(*sem):
    return pltpu.CompilerParams(dimension_semantics=sem, vmem_limit_bytes=VMEM_LIMIT)


def _sigmoid(x):
    return 0.5 * jnp.tanh(0.5 * x) + 0.5


def _rms(x, g):
    ms = jnp.mean(x * x, axis=-1, keepdims=True)
    return x * lax.rsqrt(ms + EPS) * g


def _even_in_kernel(h_ref, g_ref, w_ref, qg_ref, kg_ref, hm_ref, u_ref, q_ref, k_ref, v_ref):
    xn = _rms(h_ref[...], g_ref[...]).astype(BF16)
    a = _dot(xn, w_ref[:, 0:CONV_CH])
    gate = _dot(xn, w_ref[:, CONV_CH:2 * CONV_CH])
    u_ref[...] = (a * _sigmoid(gate)).astype(BF16)

    def head_norm(y, gain):
        yy = (y * y).astype(BF16)
        parts = [_dot(yy[:, c:c + MXU_DIM], hm_ref[...]) for c in range(0, SB_WIDTH, MXU_DIM)]
        ms = jnp.concatenate(parts, axis=1)
        return y * lax.rsqrt(ms + EPS) * gain

    c0 = 2 * CONV_CH
    q = head_norm(_dot(xn, w_ref[:, c0:c0 + SB_WIDTH]), qg_ref[...])
    q_ref[...] = (q * (LOG2E / math.sqrt(SB_HEAD_DIM))).astype(BF16)
    k = head_norm(_dot(xn, w_ref[:, c0 + SB_WIDTH:c0 + 2 * SB_WIDTH]), kg_ref[...])
    k_ref[...] = k.astype(BF16)
    v_ref[...] = _dot(xn, w_ref[:, c0 + 2 * SB_WIDTH:c0 + 3 * SB_WIDTH]).astype(BF16)


def _even_in_proj(h, g, w, qg, kg, hm):
    n, d = h.shape
    tm = ROW_TILE
    row = lambda i: (i, 0)
    const = lambda i: (0, 0)
    out = jax.ShapeDtypeStruct((n, SB_WIDTH), BF16)
    return pl.pallas_call(
        _even_in_kernel,
        grid=(n // tm,),
        in_specs=[pl.BlockSpec((tm, d), row), pl.BlockSpec((1, d), const),
                  pl.BlockSpec(w.shape, const), pl.BlockSpec((1, SB_WIDTH), const),
                  pl.BlockSpec((1, SB_WIDTH), const), pl.BlockSpec(hm.shape, const)],
        out_specs=[pl.BlockSpec((tm, SB_WIDTH), row)] * 4,
        out_shape=[out] * 4,
        compiler_params=_params("parallel"),
        name="even_in_proj",
    )(h, g, w, qg, kg, hm)


def _conv_kernel(cur_ref, halo_ref, w_ref, b_ref, lg_ref, lb_ref, o_ref, buf_ref, sh_ref):
    i = pl.program_id(1)
    tt = cur_ref.shape[1]
    rows = CONV_HALO + tt
    halo = halo_ref[0].astype(F32)
    buf_ref[0:CONV_HALO, :] = jnp.where(i > 0, halo, 0.0)
    buf_ref[CONV_HALO:rows, :] = cur_ref[0].astype(F32)
    for s in range(1, SUBLANES):
        sh_ref[s - 1, 0:rows - SUBLANES, :] = buf_ref[s:s + rows - SUBLANES, :]
    off = CONV_HALO - (CONV_WIDTH - 1)
    for c in range(tt // CONV_CHUNK):
        r0 = c * CONV_CHUNK
        acc = jnp.broadcast_to(b_ref[...], (CONV_CHUNK, CONV_CH))
        for k in range(CONV_WIDTH):
            shift = (off + k) % SUBLANES
            base = r0 + off + k - shift
            if shift == 0:
                tap = buf_ref[base:base + CONV_CHUNK, :]
            else:
                tap = sh_ref[shift - 1, base:base + CONV_CHUNK, :]
            acc = acc + w_ref[k:k + 1, :] * tap
        mu = jnp.mean(acc, axis=-1, keepdims=True)
        xc = acc - mu
        var = jnp.mean(xc * xc, axis=-1, keepdims=True)
        y = xc * lax.rsqrt(var + EPS) * lg_ref[...] + lb_ref[...]
        o_ref[0, r0:r0 + CONV_CHUNK, :] = (y * _sigmoid(y)).astype(BF16)


def _conv_module(u, w, b, lg, lb):
    bsz, tp, c = u.shape
    tt = CONV_TT
    per = tt // CONV_HALO
    const = lambda bi, i: (0, 0)
    return pl.pallas_call(
        _conv_kernel,
        grid=(bsz, tp // tt),
        in_specs=[pl.BlockSpec((1, tt, c), lambda bi, i: (bi, i, 0)),
                  pl.BlockSpec((1, CONV_HALO, c), lambda bi, i: (bi, jnp.maximum(i * per - 1, 0), 0)),
                  pl.BlockSpec(w.shape, const), pl.BlockSpec((1, c), const),
                  pl.BlockSpec((1, c), const), pl.BlockSpec((1, c), const)],
        out_specs=pl.BlockSpec((1, tt, c), lambda bi, i: (bi, i, 0)),
        out_shape=jax.ShapeDtypeStruct(u.shape, BF16),
        scratch_shapes=[pltpu.VMEM((CONV_HALO + tt, c), F32),
                        pltpu.VMEM((SUBLANES - 1, CONV_HALO + tt - SUBLANES, c), F32)],
        compiler_params=_params("parallel", "parallel"),
        name="conv_module",
    )(u, u, w, b, lg, lb)


def _attn_kernel(*refs, tq, tile0, aliased, n_compute):
    if aliased:
        refs = refs[1:]
    o_ref = refs[4]
    step = pl.program_id(1)

    @pl.when(step < n_compute)
    def _():
        _attn_tile(*refs, tq=tq, row0=(step + tile0) * ATT_TQ)

    @pl.when(step >= n_compute)
    def _():
        o_ref[...] = jnp.zeros(o_ref.shape, BF16)


def _attn_tile(q_ref, k_ref, v_ref, tri_ref, o_ref, q2_ref, e_ref, tot_ref, acc_ref, car_ref, *, tq, row0):
    tk = ATT_TK
    npairs = q_ref.shape[2] // LANES
    lane_q = lax.broadcasted_iota(jnp.int32, (tq, LANES), 1)

    for hp in range(npairs):
        q = q_ref[0, 0:tq, hp * LANES:(hp + 1) * LANES]
        zero = jnp.zeros_like(q)
        q2_ref[hp, 0:tq, :] = jnp.where(lane_q < SB_HEAD_DIM, q, zero)
        q2_ref[hp, tq:2 * tq, :] = jnp.where(lane_q >= SB_HEAD_DIM, q, zero)
        car_ref[hp] = jnp.zeros((2 * tq, LANES), F32)
        acc_ref[hp] = jnp.zeros((2 * tq, LANES), F32)

    def stage_a(j, mask):
        k0 = pl.multiple_of(j * tk, tk)
        for hp in range(npairs):
            kt = k_ref[0, pl.ds(k0, tk), hp * LANES:(hp + 1) * LANES]
            z = lax.dot_general(q2_ref[hp], kt, (((1,), (1,)), ((), ())), preferred_element_type=F32)
            sp = jnp.where(z > SOFTPLUS_CUTOFF, z, jnp.log(1.0 + jnp.exp2(z)) * LOG2E)
            if mask is not None:
                sp = jnp.where(mask, sp, 0.0)
            e = z + _dot(sp.astype(BF16), tri_ref[...])
            if mask is not None:
                e = jnp.where(mask, e, MASKED_EXPONENT)
            e_ref[hp] = e
            tot_ref[hp] = jnp.broadcast_to(jnp.sum(sp, axis=1, keepdims=True), (2 * tq, LANES))

    def stage_b(j):
        k0 = pl.multiple_of(j * tk, tk)
        for hp in range(npairs):
            car = car_ref[hp]
            w = jnp.exp2(e_ref[hp] + jnp.concatenate([car] * (tk // LANES), axis=1)).astype(BF16)
            vt = v_ref[0, pl.ds(k0, tk), hp * LANES:(hp + 1) * LANES]
            acc_ref[hp] += _dot(w, vt)
            car_ref[hp] = car - tot_ref[hp]

    j_last = row0 // tk
    row = lax.broadcasted_iota(jnp.int32, (2 * tq, tk), 0)
    col = lax.broadcasted_iota(jnp.int32, (2 * tq, tk), 1)
    row = jnp.where(row >= tq, row - tq, row) + row0
    stage_a(j_last, (col + j_last * tk) < row)

    def body(jj, carry):
        j = j_last - 1 - jj
        stage_b(j + 1)
        stage_a(j, None)
        return carry

    lax.fori_loop(0, j_last, body, 0)
    stage_b(0)
    if tq < o_ref.shape[1]:
        o_ref[0, tq:, :] = jnp.zeros((o_ref.shape[1] - tq, o_ref.shape[2]), BF16)
    for hp in range(npairs):
        o_ref[0, 0:tq, hp * LANES:(hp + 1) * LANES] = jnp.where(
            lane_q < SB_HEAD_DIM, acc_ref[hp, 0:tq, :], acc_ref[hp, tq:2 * tq, :]).astype(BF16)


def _attention(q, k, v, tri, t_real):
    bsz, tp, width = q.shape
    bq, tk = ATT_TQ, ATT_TK
    assert bq <= tk and tk % bq == 0
    npairs = width // LANES
    n_full = t_real // bq
    rem = -(-(t_real - n_full * bq) // BF16_ROWS) * BF16_ROWS

    def call(tq, tile0, n_compute, n_tiles, prev):
        blk = lambda b, i: (b, i + tile0, 0)
        full = lambda b, i: (b, 0, 0)
        in_specs = [pl.BlockSpec((1, bq, width), blk), pl.BlockSpec((1, tp, width), full),
                    pl.BlockSpec((1, tp, width), full), pl.BlockSpec(tri.shape, lambda b, i: (0, 0))]
        args = [q, k, v, tri]
        if prev is not None:
            in_specs.insert(0, pl.BlockSpec(memory_space=pl.ANY))
            args.insert(0, prev)
        return pl.pallas_call(
            functools.partial(_attn_kernel, tq=tq, tile0=tile0, aliased=prev is not None, n_compute=n_compute),
            grid=(bsz, n_tiles),
            in_specs=in_specs,
            out_specs=pl.BlockSpec((1, bq, width), blk),
            out_shape=jax.ShapeDtypeStruct(q.shape, BF16),
            input_output_aliases={} if prev is None else {0: 0},
            scratch_shapes=[pltpu.VMEM((npairs, 2 * tq, LANES), BF16),
                            pltpu.VMEM((npairs, 2 * tq, tk), F32),
                            pltpu.VMEM((npairs, 2 * tq, LANES), F32),
                            pltpu.VMEM((npairs, 2 * tq, LANES), F32),
                            pltpu.VMEM((npairs, 2 * tq, LANES), F32)],
            compiler_params=_params("parallel", "arbitrary"),
            name="stick_breaking" if prev is None else "stick_breaking_tail",
        )(*args)

    out = call(bq, 0, n_full, tp // bq, None)
    if rem:
        out = call(rem, n_full, 1, 1, out)
    return out


def _even_ffn_kernel(h_ref, u_ref, o_ref, wo_ref, g_ref, wgu_ref, wd_ref, out_ref, xn_ref, *, tf):
    @pl.when(pl.program_id(1) == 0)
    def _():
        h1 = h_ref[...] + _dot(u_ref[...], wo_ref[0:CONV_CH, :]) + _dot(o_ref[...], wo_ref[CONV_CH:, :])
        out_ref[...] = h1
        xn_ref[...] = _rms(h1, g_ref[...]).astype(BF16)

    gu = _dot(xn_ref[...], wgu_ref[...])
    g = gu[:, 0:tf]
    mid = g * _sigmoid(g) * gu[:, tf:2 * tf]
    out_ref[...] += _dot(mid.astype(BF16), wd_ref[...])


def _seq_row_tile(tp):
    for tm in range(ROW_TILE * 5 // 4 // BF16_ROWS * BF16_ROWS, 0, -BF16_ROWS):
        if tp % tm == 0:
            return tm
    raise ValueError(f"no row tile divides {tp}")


def _even_ffn(h, u, o, wo, g, wgu, wd, tf, bsz, time_major):
    n, d = h.shape
    ff = wd.shape[0]
    tp = n // bsz
    tm = _seq_row_tile(tp)
    per_seq = tp // tm
    row = lambda i, f: (i, 0)
    const = lambda i, f: (0, 0)
    if time_major:
        out_spec = pl.BlockSpec((tm, d), lambda i, f: (i % per_seq, i // per_seq))
        out_shape = jax.ShapeDtypeStruct((tp, bsz * d), F32)
    else:
        out_spec = pl.BlockSpec((tm, d), row)
        out_shape = jax.ShapeDtypeStruct((n, d), F32)
    out = pl.pallas_call(
        functools.partial(_even_ffn_kernel, tf=tf),
        grid=(n // tm, ff // tf),
        in_specs=[pl.BlockSpec((tm, d), row), pl.BlockSpec((tm, CONV_CH), row),
                  pl.BlockSpec((tm, SB_WIDTH), row), pl.BlockSpec(wo.shape, const),
                  pl.BlockSpec((1, d), const),
                  pl.BlockSpec((d, 2 * tf), lambda i, f: (0, f)),
                  pl.BlockSpec((tf, d), lambda i, f: (f, 0))],
        out_specs=out_spec,
        out_shape=out_shape,
        scratch_shapes=[pltpu.VMEM((tm, d), BF16)],
        compiler_params=_params("parallel", "arbitrary"),
        name="even_out_swiglu",
    )(h, u, o, wo, g, wgu, wd)
    return out.reshape(n, d)


def _row_scatter_kernel(win_ref, x_ref, info_ref, off_ref, init_hbm, o_hbm, cbuf, sems):
    del init_hbm
    i = pl.program_id(0)
    last = pl.num_programs(0) - 1
    n_slots = cbuf.shape[1]
    buf = i % 2

    def drained(b):
        return pltpu.make_async_copy(cbuf.at[b], o_hbm.at[pl.ds(0, n_slots)], sems.at[b])

    info = info_ref[...]
    sub = lax.broadcasted_iota(jnp.int32, info.shape, 0)
    valid = info[ROUTE_VALID:ROUTE_VALID + 1, :] > 0.5

    def local_slot(e_row, r_row):
        expert = info[e_row:e_row + 1, :].astype(jnp.int32)
        off = jnp.sum(jnp.where(sub == expert, off_ref[:, 0:1], 0.0), axis=0, keepdims=True)
        return jnp.where(valid, off + info[r_row:r_row + 1, :], -1.0)

    slot = lax.broadcasted_iota(jnp.int32, (n_slots, info.shape[1]), 0).astype(F32)
    pick = (jnp.where(slot == local_slot(ROUTE_E1, ROUTE_R1), 1.0, 0.0)
            + jnp.where(slot == local_slot(ROUTE_E2, ROUTE_R2), 1.0, 0.0)).astype(BF16)
    rows = _dot(pick, x_ref[...]).astype(cbuf.dtype)

    @pl.when(i >= 2)
    def _():
        drained(buf).wait()

    cbuf[buf] = rows
    for s in range(ROW_NWIN):
        dst = pl.multiple_of(win_ref[i * ROW_NWIN + s], ROW_WIN)
        pltpu.make_async_copy(cbuf.at[buf, pl.ds(s * ROW_WIN, ROW_WIN)], o_hbm.at[pl.ds(dst, ROW_WIN)],
                              sems.at[buf]).start(priority=s % 2)

    @pl.when(i == last)
    def _():
        drained(buf).wait()

        @pl.when(i >= 1)
        def _():
            drained(1 - buf).wait()


def _row_scatter(win_rows, x, info, local_off, n_rows_alloc):
    n, d = x.shape
    tm = COMBINE_TILE
    init = jnp.zeros((n_rows_alloc, d), x.dtype)
    return pl.pallas_call(
        _row_scatter_kernel,
        grid_spec=pltpu.PrefetchScalarGridSpec(
            num_scalar_prefetch=1, grid=(n // tm,),
            in_specs=[pl.BlockSpec((tm, d), lambda i, w: (i, 0)), pl.BlockSpec((SUBLANES, tm), lambda i, w: (i, 0)),
                      pl.BlockSpec((SUBLANES, LANES), lambda i, w: (i, 0)), pl.BlockSpec(memory_space=pl.ANY)],
            out_specs=pl.BlockSpec(memory_space=pl.ANY),
            scratch_shapes=[pltpu.VMEM((2, ROW_NWIN * ROW_WIN, d), x.dtype), pltpu.SemaphoreType.DMA((2,))]),
        out_shape=jax.ShapeDtypeStruct(init.shape, init.dtype),
        input_output_aliases={4: 0},
        compiler_params=_params("arbitrary"),
        name="moe_row_scatter",
    )(win_rows, x, info, local_off, init)


def _gmm_kernel(te_ref, nu_ref, xs_ref, wg_ref, wu_ref, wd_ref, o_ref, acc_ref):
    p = pl.program_id(0)
    f = pl.program_id(1)
    last = pl.num_programs(1) - 1
    used = p < nu_ref[0]

    @pl.when(used)
    def _():
        x = xs_ref[...]
        g = _dot(x, wg_ref[0])
        mid = g * _sigmoid(g) * _dot(x, wu_ref[0])
        part = _dot(mid.astype(BF16), wd_ref[0])

        @pl.when(f == 0)
        def _():
            acc_ref[...] = part

        @pl.when(f > 0)
        def _():
            acc_ref[...] += part

        @pl.when(f == last)
        def _():
            o_ref[...] = acc_ref[...].astype(o_ref.dtype)

    @pl.when(jnp.logical_not(used) & (f == last))
    def _():
        o_ref[...] = jnp.zeros(o_ref.shape, o_ref.dtype)


def _grouped_ffn(tile_expert, n_used, xs, n_rows, wg, wu, wd):
    d = xs.shape[1]
    ff = wd.shape[1]
    tf = MOE_FF_TILE
    nf = ff // tf
    tg = MOE_TILE
    row = lambda p, f, te, nu: (p, 0)
    fidx = lambda p, f, nu: jnp.where(p < nu[0], f, nf - 1)
    up = pl.BlockSpec((1, d, tf), lambda p, f, te, nu: (te[p], 0, fidx(p, f, nu)))
    return pl.pallas_call(
        _gmm_kernel,
        grid_spec=pltpu.PrefetchScalarGridSpec(
            num_scalar_prefetch=2, grid=(n_rows // tg, nf),
            in_specs=[pl.BlockSpec((tg, d), row), up, up,
                      pl.BlockSpec((1, tf, d), lambda p, f, te, nu: (te[p], fidx(p, f, nu), 0))],
            out_specs=pl.BlockSpec((tg, d), row),
            scratch_shapes=[pltpu.VMEM((tg, d), F32)]),
        out_shape=jax.ShapeDtypeStruct((n_rows, d), BF16),
        compiler_params=_params("arbitrary", "arbitrary"),
        name="moe_grouped_ffn",
    )(tile_expert, n_used, xs, wg, wu, wd)


def _combine_kernel(win_ref, h_ref, tok_ref, w1_ref, w2_ref, starts_ref, rows_ref, ys_hbm, o_ref, gbuf, sems):
    i = pl.program_id(0)
    tm = h_ref.shape[0]
    win = ROW_WIN

    def fetch(tile, slot):
        for s in range(ROW_NWIN):
            start = pl.multiple_of(win_ref[tile * ROW_NWIN + s], win)
            pltpu.make_async_copy(ys_hbm.at[pl.ds(start, win)], gbuf.at[slot, pl.ds(s * win, win)],
                                  sems.at[slot]).start(priority=s % 2)

    @pl.when(i == 0)
    def _():
        fetch(0, 0)

    @pl.when(i + 1 < pl.num_programs(0))
    def _():
        fetch(i + 1, (i + 1) % 2)

    slot = i % 2
    pltpu.make_async_copy(ys_hbm.at[pl.ds(0, ROW_NWIN * win)], gbuf.at[slot], sems.at[slot]).wait()

    tok = tok_ref[...]
    lane = lax.broadcasted_iota(jnp.int32, tok.shape, 1)
    field = lambda c: jnp.sum(jnp.where(lane == c, tok, 0.0), axis=1, keepdims=True)
    valid = field(ROUTE_VALID) > 0.5
    rows = rows_ref[0]
    reps = h_ref.shape[1] // LANES
    out = h_ref[...]
    for e_col, r_col, w_ref in ((ROUTE_E1, ROUTE_R1, w1_ref), (ROUTE_E2, ROUTE_R2, w2_ref)):
        start = jnp.sum(jnp.where(lane == field(e_col).astype(jnp.int32), starts_ref[...], 0.0),
                        axis=1, keepdims=True)
        dest = jnp.where(valid, start + field(r_col), -2.0)
        pick = jnp.where(dest == rows, 1.0, 0.0).astype(BF16)
        out = out + jnp.concatenate([w_ref[...]] * reps, axis=1) * _dot(pick, gbuf[slot])
    o_ref[...] = out


def _combine(win_start, h, tok, w1b, w2b, starts_row, slot_rows, ys):
    n, d = h.shape
    tm = COMBINE_TILE
    n_slots = ROW_NWIN * ROW_WIN
    row = lambda i, w: (i, 0)
    return pl.pallas_call(
        _combine_kernel,
        grid_spec=pltpu.PrefetchScalarGridSpec(
            num_scalar_prefetch=1, grid=(n // tm,),
            in_specs=[pl.BlockSpec((tm, d), row), pl.BlockSpec((tm, LANES), row), pl.BlockSpec((tm, LANES), row),
                      pl.BlockSpec((tm, LANES), row), pl.BlockSpec((1, LANES), lambda i, w: (0, 0)),
                      pl.BlockSpec((1, 1, n_slots), lambda i, w: (i, 0, 0)),
                      pl.BlockSpec(memory_space=pl.ANY)],
            out_specs=pl.BlockSpec((tm, d), row),
            scratch_shapes=[pltpu.VMEM((2, n_slots, d), ys.dtype), pltpu.SemaphoreType.DMA((2,))]),
        out_shape=jax.ShapeDtypeStruct(h.shape, F32),
        compiler_params=_params("arbitrary"),
        name="moe_combine",
    )(win_start, h, tok, w1b, w2b, starts_row, slot_rows, ys)


def _gate_windows(lw, bd):
    out = []
    for c0 in range(0, lw, LRU_GATE_TILE):
        nc = min(LRU_GATE_TILE, lw - c0)
        r0 = (c0 // bd) * bd
        r1 = ((c0 + nc - 1) // bd + 1) * bd
        r0 = (r0 // LANES) * LANES
        r1 = min(-(-r1 // LANES) * LANES, lw)
        out.append((c0, nc, r0, r1 - r0))
    return out


def _lru_kernel(h_ref, ng_ref, win_ref, cw_ref, cb_ref, wg_ref, br_ref, bi_ref, lam_ref, wo_ref,
                o_ref, xbuf_ref, hs_ref, a_ref, b_ref, y_ref, gate_ref, *, windows, bsz):
    ti = pl.program_id(0)
    rows, lw = a_ref.shape
    halo = (LRU_CONV_WIDTH - 1) * bsz

    @pl.when(ti == 0)
    def _():
        xbuf_ref[0:halo, :] = jnp.zeros((halo, lw), F32)
        hs_ref[...] = jnp.zeros(hs_ref.shape, F32)

    @pl.when(ti > 0)
    def _():
        xbuf_ref[0:halo, :] = xbuf_ref[rows:rows + halo, :]

    proj = _dot(_rms(h_ref[...], ng_ref[...]).astype(BF16), win_ref[...])
    gate_ref[...] = jax.nn.gelu(proj[:, 0:lw], approximate=True)
    xbuf_ref[halo:halo + rows, :] = proj[:, lw:2 * lw]
    xc = jnp.broadcast_to(cb_ref[...], (rows, lw))
    for k in range(LRU_CONV_WIDTH):
        xc = xc + cw_ref[k:k + 1, :] * xbuf_ref[k * bsz:k * bsz + rows, :]
    xcb = xc.astype(BF16)
    sp_lam = jnp.log(1.0 + jnp.exp(-lam_ref[...]))
    log2_a_per_r = (-LRU_C * LOG2E) * sp_lam
    wcol = 0
    for (c0, nc, r0, nr) in windows:
        rg = _dot(xcb[:, r0:r0 + nr], wg_ref[r0:r0 + nr, wcol:wcol + 2 * nc])
        wcol += 2 * nc
        r = _sigmoid(rg[:, 0:nc] + br_ref[:, c0:c0 + nc])
        ig = _sigmoid(rg[:, nc:2 * nc] + bi_ref[:, c0:c0 + nc])
        a = jnp.exp2(r * log2_a_per_r[:, c0:c0 + nc])
        a_ref[:, c0:c0 + nc] = a
        one_m_a2 = 1.0 - a * a
        root = jnp.where(one_m_a2 > 0.0, one_m_a2 * lax.rsqrt(one_m_a2), 0.0)
        b_ref[:, c0:c0 + nc] = root * (ig * xc[:, c0:c0 + nc])

    def step(t, hprev):
        r0 = pl.multiple_of(t * bsz, SUBLANES)
        hnew = a_ref[pl.ds(r0, bsz), :] * hprev + b_ref[pl.ds(r0, bsz), :]
        y_ref[pl.ds(r0, bsz), :] = hnew
        return hnew

    hs_ref[...] = lax.fori_loop(0, rows // bsz, step, hs_ref[...], unroll=LRU_SCAN_UNROLL)
    gy = (gate_ref[...] * y_ref[...]).astype(BF16)
    o_ref[...] = h_ref[...] + _dot(gy, wo_ref[...])


def _lru_mixer(h, ng, win, cw, cb, wg, br, bi, lam, wo, windows, bsz):
    n, d = h.shape
    lw = lam.shape[-1]
    assert bsz % SUBLANES == 0 and LRU_ROWS % bsz == 0 and n % LRU_ROWS == 0
    rows = LRU_ROWS
    halo = (LRU_CONV_WIDTH - 1) * bsz
    const = lambda t: (0, 0)
    blk = lambda t: (t, 0)
    resident = lambda w: pl.BlockSpec(w.shape, const, pipeline_mode=pl.Buffered(1))
    return pl.pallas_call(
        functools.partial(_lru_kernel, windows=windows, bsz=bsz),
        grid=(n // rows,),
        in_specs=[pl.BlockSpec((rows, d), blk), pl.BlockSpec((1, d), const), resident(win),
                  pl.BlockSpec(cw.shape, const), pl.BlockSpec((1, lw), const), resident(wg),
                  pl.BlockSpec((1, lw), const), pl.BlockSpec((1, lw), const),
                  pl.BlockSpec((1, lw), const), resident(wo)],
        out_specs=pl.BlockSpec((rows, d), blk),
        out_shape=jax.ShapeDtypeStruct(h.shape, F32),
        scratch_shapes=[pltpu.VMEM((halo + rows, lw), F32), pltpu.VMEM((bsz, lw), F32),
                        pltpu.VMEM((rows, lw), F32), pltpu.VMEM((rows, lw), F32), pltpu.VMEM((rows, lw), F32),
                        pltpu.VMEM((rows, lw), F32)],
        compiler_params=_params("arbitrary"),
        name="rglru_mixer",
    )(h, ng, win, cw, cb, wg, br, bi, lam, wo)


ROUTE_E1, ROUTE_E2, ROUTE_R1, ROUTE_R2, ROUTE_VALID = range(5)


def _router_kernel(h_ref, g_ref, rw_ref, ltri_ref, xn_ref, info_ref, tok_ref, w1_ref, w2_ref, before_ref, cnt_ref,
                   run_ref, *, n_real_rows):
    i = pl.program_id(0)
    tm = h_ref.shape[0]

    @pl.when(i == 0)
    def _():
        run_ref[...] = jnp.zeros(run_ref.shape, F32)

    xn = _rms(h_ref[...], g_ref[...])
    xn_ref[...] = xn.astype(BF16)
    split = lambda a: (a.astype(BF16), (a - a.astype(BF16).astype(F32)).astype(BF16))
    (x_hi, x_lo), (r_hi, r_lo) = split(xn), split(rw_ref[...])
    logits = _dot(x_hi, r_hi) + (_dot(x_hi, r_lo) + _dot(x_lo, r_hi))
    lane = lax.broadcasted_iota(jnp.int32, logits.shape, 1)
    neg = jnp.float32(-jnp.inf)
    logits = jnp.where(lane < N_EXPERTS, logits, neg)
    top1 = jnp.max(logits, axis=1, keepdims=True)
    idx1 = jnp.min(jnp.where(logits == top1, lane, LANES), axis=1, keepdims=True)
    rest = jnp.where(lane == idx1, neg, logits)
    top2 = jnp.max(rest, axis=1, keepdims=True)
    idx2 = jnp.min(jnp.where(rest == top2, lane, LANES), axis=1, keepdims=True)
    w1 = 1.0 / (1.0 + jnp.exp(top2 - top1))

    rowg = lax.broadcasted_iota(jnp.int32, (tm, 1), 0) + i * tm
    valid = jnp.where(rowg < n_real_rows, 1.0, 0.0)
    onehot = (jnp.where(lane == idx1, 1.0, 0.0) + jnp.where(lane == idx2, 1.0, 0.0)) * valid
    before_ref[...] = jnp.broadcast_to(run_ref[...], before_ref.shape)
    before = _dot(ltri_ref[...], onehot.astype(BF16)) + run_ref[...]
    rank1 = jnp.sum(jnp.where(lane == idx1, before, 0.0), axis=1, keepdims=True)
    rank2 = jnp.sum(jnp.where(lane == idx2, before, 0.0), axis=1, keepdims=True)
    run_ref[...] += jnp.ceil(jnp.sum(onehot, axis=0, keepdims=True) / ROW_WIN) * ROW_WIN
    cnt_ref[...] = run_ref[...]
    w1_ref[...] = jnp.broadcast_to(w1, (tm, LANES))
    w2_ref[...] = jnp.broadcast_to(1.0 - w1, (tm, LANES))
    fields = {ROUTE_E1: idx1.astype(F32), ROUTE_E2: idx2.astype(F32),
              ROUTE_R1: rank1, ROUTE_R2: rank2, ROUTE_VALID: valid}
    info = jnp.zeros(logits.shape, F32)
    for col, val in fields.items():
        info = jnp.where(lane == col, val, info)
    tok_ref[...] = info
    info_ref[...] = info.T[0:SUBLANES, :]


def _router(h, g, rw, ltri, n_real_rows):
    n, d = h.shape
    tm = ROW_TILE
    row = lambda i: (i, 0)
    const = lambda i: (0, 0)
    return pl.pallas_call(
        functools.partial(_router_kernel, n_real_rows=n_real_rows),
        grid=(n // tm,),
        in_specs=[pl.BlockSpec((tm, d), row), pl.BlockSpec((1, d), const), pl.BlockSpec(rw.shape, const),
                  pl.BlockSpec(ltri.shape, const)],
        out_specs=[pl.BlockSpec((tm, d), row), pl.BlockSpec((SUBLANES, tm), row),
                   pl.BlockSpec((tm, LANES), row), pl.BlockSpec((tm, LANES), row),
                   pl.BlockSpec((tm, LANES), row), pl.BlockSpec((SUBLANES, LANES), row),
                   pl.BlockSpec((1, LANES), const)],
        out_shape=[jax.ShapeDtypeStruct((n, d), BF16), jax.ShapeDtypeStruct((n // tm * SUBLANES, tm), F32),
                   jax.ShapeDtypeStruct((n, LANES), F32), jax.ShapeDtypeStruct((n, LANES), F32),
                   jax.ShapeDtypeStruct((n, LANES), F32), jax.ShapeDtypeStruct((n // tm * SUBLANES, LANES), F32),
                   jax.ShapeDtypeStruct((1, LANES), F32)],
        scratch_shapes=[pltpu.VMEM((1, LANES), F32)],
        compiler_params=_params("arbitrary"),
        name="router",
    )(h, g, rw, ltri)


def _routing_tables(before, counts, n_rows):
    tg = MOE_TILE
    win = ROW_WIN
    cnt = counts[0, :N_EXPERTS].astype(jnp.int32)
    padded = (cnt + tg - 1) // tg * tg
    ends = jnp.cumsum(padded)
    starts = ends - padded
    tile_start = jnp.arange(n_rows // tg, dtype=jnp.int32) * tg
    tile_expert = jnp.minimum(jnp.sum(tile_start[:, None] >= ends[None, :], axis=1), N_EXPERTS - 1)
    n_used = (ends[-1] // tg).reshape(1)
    bef = before.reshape(-1, SUBLANES, LANES)[:, 0, :N_EXPERTS].astype(jnp.int32)
    nwin = (jnp.concatenate([bef[1:], cnt[None, :]], axis=0) - bef) // win
    base = jnp.cumsum(nwin, axis=1) - nwin
    slot = jnp.arange(ROW_NWIN, dtype=jnp.int32)[None, :, None]
    mine = (slot >= base[:, None, :]) & (slot < (base + nwin)[:, None, :])
    rows = jnp.sum(jnp.where(mine, (starts[None, :] + bef)[:, None, :] + win * (slot - base[:, None, :]), 0), axis=2)
    used = jnp.any(mine, axis=2)
    parity = (jnp.arange(bef.shape[0], dtype=jnp.int32) % 2)[:, None]
    scatter_rows = jnp.where(used, rows, n_rows + win * (parity * ROW_NWIN + slot[:, :, 0]))
    gather_rows = jnp.where(used, rows, 0)
    slot_rows = jnp.where(used[..., None], rows[..., None] + jnp.arange(win, dtype=jnp.int32), -1)
    slot_rows = slot_rows.reshape(-1, 1, ROW_NWIN * win).astype(F32)
    starts_row = jnp.zeros((1, LANES), F32).at[0, :N_EXPERTS].set(starts.astype(F32))
    local_off = jnp.zeros(bef.shape[:1] + (SUBLANES,), jnp.int32).at[:, :N_EXPERTS].set(win * base - bef)
    local_off = jnp.broadcast_to(local_off.astype(F32)[:, :, None], local_off.shape + (LANES,)).reshape(-1, LANES)
    return (tile_expert.astype(jnp.int32), n_used.astype(jnp.int32), scatter_rows.reshape(-1).astype(jnp.int32),
            gather_rows.reshape(-1).astype(jnp.int32), slot_rows, starts_row, local_off)


def _interleave_gate_up(wg, wu, tf):
    e, d, ff = wg.shape
    g = wg.astype(BF16).reshape(e, d, ff // tf, tf)
    u = wu.astype(BF16).reshape(e, d, ff // tf, tf)
    return jnp.concatenate([g, u], axis=3).reshape(e, d, 2 * ff)


def _dense_block_diag(w):
    nb, bd, _ = w.shape
    eye = jnp.eye(nb, dtype=w.dtype)
    return (w[:, :, None, :] * eye[:, None, :, None]).reshape(nb * bd, nb * bd)


def _pack_lru_gates(wr, wi, windows):
    dr = _dense_block_diag(wr).astype(BF16)
    di = _dense_block_diag(wi).astype(BF16)
    cols = []
    for (c0, nc, _, _) in windows:
        cols += [dr[:, c0:c0 + nc], di[:, c0:c0 + nc]]
    return jnp.concatenate(cols, axis=1)


def _ffn_chunk(ff):
    for tf in (896, 512, 384, 256, 128):
        if ff % tf == 0:
            return tf
    raise ValueError(f"unsupported d_ff {ff}")


def kernel(x, meta_tokens, mix_norm_even, w_in_even, conv_w, conv_b, conv_ln_g, conv_ln_b, q_norm_g, k_norm_g, w_out_even, ffn_norm_even, ffn_w_gate, ffn_w_up, ffn_w_down, mix_norm_odd, w_in_odd, lru_conv_w, lru_conv_b, gate_r_w, gate_r_b, gate_i_w, gate_i_b, lru_lambda, w_out_odd, ffn_norm_odd, router_w, moe_w_gate, moe_w_up, moe_w_down):
    bsz, seq, d = x.shape
    t_real = N_META + seq
    tp = -(-t_real // TIME_TILE) * TIME_TILE
    n = bsz * tp
    assert n % ROW_TILE == 0
    depth = mix_norm_even.shape[0] + mix_norm_odd.shape[0]

    meta = jnp.broadcast_to(meta_tokens[None].astype(x.dtype), (bsz, N_META, d))
    h = jnp.concatenate([meta, x, jnp.zeros((bsz, tp - t_real, d), x.dtype)], axis=1).reshape(n, d)

    head_mean = jnp.kron(jnp.eye(MXU_DIM // SB_HEAD_DIM, dtype=F32),
                         jnp.full((SB_HEAD_DIM, SB_HEAD_DIM), 1.0 / SB_HEAD_DIM, F32)).astype(BF16)
    kk = jnp.arange(ATT_TK)
    neg_tri = -(kk[:, None] >= kk[None, :]).astype(BF16)
    row2 = lambda a: a.reshape(1, -1)

    time_major = False
    for layer in range(depth):
        p = layer // 2
        if layer % 2 == 0:
            if time_major:
                h = h.reshape(tp, bsz, d).transpose(1, 0, 2).reshape(n, d)
            u, q, k, v = _even_in_proj(h, row2(mix_norm_even[p]), w_in_even[p].astype(BF16),
                                       row2(jnp.tile(q_norm_g[p], SB_HEADS)),
                                       row2(jnp.tile(k_norm_g[p], SB_HEADS)), head_mean)
            u = _conv_module(u.reshape(bsz, tp, CONV_CH), conv_w[p], row2(conv_b[p]),
                             row2(conv_ln_g[p]), row2(conv_ln_b[p]))
            o = _attention(q.reshape(bsz, tp, SB_WIDTH), k.reshape(bsz, tp, SB_WIDTH),
                           v.reshape(bsz, tp, SB_WIDTH), neg_tri, t_real)
            tf = _ffn_chunk(ffn_w_gate.shape[-1])
            time_major = layer + 1 < depth
            h = _even_ffn(h, u.reshape(n, CONV_CH), o.reshape(n, SB_WIDTH), w_out_even[p].astype(BF16),
                          row2(ffn_norm_even[p]),
                          _interleave_gate_up(ffn_w_gate[p][None], ffn_w_up[p][None], tf)[0],
                          ffn_w_down[p].astype(BF16), tf, bsz, time_major)
        else:
            if not time_major:
                h = h.reshape(bsz, tp, d).transpose(1, 0, 2).reshape(n, d)
                time_major = True
            lw = lru_lambda.shape[-1]
            windows = _gate_windows(lw, lw // LRU_BLOCKS)
            h = _lru_mixer(h, row2(mix_norm_odd[p]), w_in_odd[p].astype(BF16),
                           lru_conv_w[p], row2(lru_conv_b[p]),
                           _pack_lru_gates(gate_r_w[p], gate_i_w[p], windows),
                           row2(gate_r_b[p]), row2(gate_i_b[p]), row2(lru_lambda[p]),
                           w_out_odd[p].astype(BF16), windows, bsz)
            rw = jnp.pad(router_w[p], ((0, 0), (0, LANES - N_EXPERTS)))
            kk = jnp.arange(ROW_TILE)
            ltri = (kk[:, None] > kk[None, :]).astype(BF16)
            xn, info, tok, w1b, w2b, before, counts = _router(h, row2(ffn_norm_odd[p]), rw, ltri, t_real * bsz)
            tg = MOE_TILE
            assert ROW_TILE == COMBINE_TILE
            n_tiles = n // COMBINE_TILE
            n_rows = 2 * bsz * t_real + n_tiles * N_EXPERTS * (ROW_WIN - 1) + N_EXPERTS * (tg - 1)
            n_rows = -(-n_rows // tg) * tg
            tile_expert, n_used, scatter_rows, gather_rows, slot_rows, starts_row, local_off = _routing_tables(
                before, counts, n_rows)
            xs = _row_scatter(scatter_rows, xn, info, local_off, n_rows + 2 * ROW_NWIN * ROW_WIN)
            ys = _grouped_ffn(tile_expert, n_used, xs, n_rows, moe_w_gate[p].astype(BF16),
                              moe_w_up[p].astype(BF16), moe_w_down[p].astype(BF16))
            h = _combine(gather_rows, h, tok, w1b, w2b, starts_row, slot_rows, ys)
    if time_major:
        return h.reshape(tp, bsz, d)[N_META:t_real].transpose(1, 0, 2)
    return h.reshape(bsz, tp, d)[:, N_META:t_real]
```

```python
import functools
import math

import jax
import jax.numpy as jnp
from jax import lax
from jax.experimental import pallas as pl
from jax.experimental.pallas import tpu as pltpu

F32 = jnp.float32
BF16 = jnp.bfloat16

EPS = 1e-6
N_META = 16
CONV_CH = 512
CONV_WIDTH = 31
SB_HEADS = 8
SB_HEAD_DIM = 64
SB_WIDTH = SB_HEADS * SB_HEAD_DIM
LRU_BLOCKS = 16
LRU_CONV_WIDTH = 4
LRU_C = 8.0
N_EXPERTS = 8

LANES = 128
MXU_DIM = 256
TIME_TILE = 256
VMEM_LIMIT = 50 * 1024 * 1024

ROW_TILE = 512
CONV_TT = 128
CONV_HALO = 32
CONV_CHUNK = 32
ATT_TQ = 256
ATT_TK = 256
LOG2E = 1.4426950408889634
MASKED_EXPONENT = -1e30
SOFTPLUS_CUTOFF = 126.0
LRU_ROWS = 256
LRU_SCAN_UNROLL = 4
LRU_GATE_TILE = 256
MOE_TILE = 512
MOE_FF_TILE = 1792
COMBINE_TILE = 512
ROW_WIN = 16
ROW_NWIN = 2 * COMBINE_TILE // ROW_WIN + N_EXPERTS
SUBLANES = 8
BF16_ROWS = 16


def _dot(a, b):
    return jnp.dot(a, b, preferred_element_type=F32)


def _params(*sem):
    return pltpu.CompilerParams(dimension_semantics=sem, vmem_limit_bytes=VMEM_LIMIT)


def _sigmoid(x):
    return 0.5 * jnp.tanh(0.5 * x) + 0.5


def _rms(x, g):
    ms = jnp.mean(x * x, axis=-1, keepdims=True)
    return x * lax.rsqrt(ms + EPS) * g


def _even_in_kernel(h_ref, g_ref, w_ref, qg_ref, kg_ref, hm_ref, u_ref, q_ref, k_ref, v_ref):
    xn = _rms(h_ref[...], g_ref[...]).astype(BF16)
    a = _dot(xn, w_ref[:, 0:CONV_CH])
    gate = _dot(xn, w_ref[:, CONV_CH:2 * CONV_CH])
    u_ref[...] = (a * _sigmoid(gate)).astype(BF16)

    def head_norm(y, gain):
        yy = (y * y).astype(BF16)
        parts = [_dot(yy[:, c:c + MXU_DIM], hm_ref[...]) for c in range(0, SB_WIDTH, MXU_DIM)]
        ms = jnp.concatenate(parts, axis=1)
        return y * lax.rsqrt(ms + EPS) * gain

    c0 = 2 * CONV_CH
    q = head_norm(_dot(xn, w_ref[:, c0:c0 + SB_WIDTH]), qg_ref[...])
    q_ref[...] = (q * (LOG2E / math.sqrt(SB_HEAD_DIM))).astype(BF16)
    k = head_norm(_dot(xn, w_ref[:, c0 + SB_WIDTH:c0 + 2 * SB_WIDTH]), kg_ref[...])
    k_ref[...] = k.astype(BF16)
    v_ref[...] = _dot(xn, w_ref[:, c0 + 2 * SB_WIDTH:c0 + 3 * SB_WIDTH]).astype(BF16)


def _even_in_proj(h, g, w, qg, kg, hm):
    n, d = h.shape
    tm = ROW_TILE
    row = lambda i: (i, 0)
    const = lambda i: (0, 0)
    out = jax.ShapeDtypeStruct((n, SB_WIDTH), BF16)
    return pl.pallas_call(
        _even_in_kernel,
        grid=(n // tm,),
        in_specs=[pl.BlockSpec((tm, d), row), pl.BlockSpec((1, d), const),
                  pl.BlockSpec(w.shape, const), pl.BlockSpec((1, SB_WIDTH), const),
                  pl.BlockSpec((1, SB_WIDTH), const), pl.BlockSpec(hm.shape, const)],
        out_specs=[pl.BlockSpec((tm, SB_WIDTH), row)] * 4,
        out_shape=[out] * 4,
        compiler_params=_params("parallel"),
        name="even_in_proj",
    )(h, g, w, qg, kg, hm)


def _conv_kernel(cur_ref, halo_ref, w_ref, b_ref, lg_ref, lb_ref, o_ref, buf_ref, sh_ref):
    i = pl.program_id(1)
    tt = cur_ref.shape[1]
    rows = CONV_HALO + tt
    halo = halo_ref[0].astype(F32)
    buf_ref[0:CONV_HALO, :] = jnp.where(i > 0, halo, 0.0)
    buf_ref[CONV_HALO:rows, :] = cur_ref[0].astype(F32)
    for s in range(1, SUBLANES):
        sh_ref[s - 1, 0:rows - SUBLANES, :] = buf_ref[s:s + rows - SUBLANES, :]
    off = CONV_HALO - (CONV_WIDTH - 1)
    for c in range(tt // CONV_CHUNK):
        r0 = c * CONV_CHUNK
        acc = jnp.broadcast_to(b_ref[...], (CONV_CHUNK, CONV_CH))
        for k in range(CONV_WIDTH):
            shift = (off + k) % SUBLANES
            base = r0 + off + k - shift
            if shift == 0:
                tap = buf_ref[base:base + CONV_CHUNK, :]
            else:
                tap = sh_ref[shift - 1, base:base + CONV_CHUNK, :]
            acc = acc + w_ref[k:k + 1, :] * tap
        mu = jnp.mean(acc, axis=-1, keepdims=True)
        xc = acc - mu
        var = jnp.mean(xc * xc, axis=-1, keepdims=True)
        y = xc * lax.rsqrt(var + EPS) * lg_ref[...] + lb_ref[...]
        o_ref[0, r0:r0 + CONV_CHUNK, :] = (y * _sigmoid(y)).astype(BF16)


def _conv_module(u, w, b, lg, lb):
    bsz, tp, c = u.shape
    tt = CONV_TT
    per = tt // CONV_HALO
    const = lambda bi, i: (0, 0)
    return pl.pallas_call(
        _conv_kernel,
        grid=(bsz, tp // tt),
        in_specs=[pl.BlockSpec((1, tt, c), lambda bi, i: (bi, i, 0)),
                  pl.BlockSpec((1, CONV_HALO, c), lambda bi, i: (bi, jnp.maximum(i * per - 1, 0), 0)),
                  pl.BlockSpec(w.shape, const), pl.BlockSpec((1, c), const),
                  pl.BlockSpec((1, c), const), pl.BlockSpec((1, c), const)],
        out_specs=pl.BlockSpec((1, tt, c), lambda bi, i: (bi, i, 0)),
        out_shape=jax.ShapeDtypeStruct(u.shape, BF16),
        scratch_shapes=[pltpu.VMEM((CONV_HALO + tt, c), F32),
                        pltpu.VMEM((SUBLANES - 1, CONV_HALO + tt - SUBLANES, c), F32)],
        compiler_params=_params("parallel", "parallel"),
        name="conv_module",
    )(u, u, w, b, lg, lb)


def _attn_kernel(*refs, tq, tile0, aliased, n_compute):
    if aliased:
        refs = refs[1:]
    o_ref = refs[4]
    step = pl.program_id(1)

    @pl.when(step < n_compute)
    def _():
        _attn_tile(*refs, tq=tq, row0=(step + tile0) * ATT_TQ)

    @pl.when(step >= n_compute)
    def _():
        o_ref[...] = jnp.zeros(o_ref.shape, BF16)


def _attn_tile(q_ref, k_ref, v_ref, tri_ref, o_ref, q2_ref, e_ref, tot_ref, acc_ref, car_ref, *, tq, row0):
    tk = ATT_TK
    npairs = q_ref.shape[2] // LANES
    lane_q = lax.broadcasted_iota(jnp.int32, (tq, LANES), 1)

    for hp in range(npairs):
        q = q_ref[0, 0:tq, hp * LANES:(hp + 1) * LANES]
        zero = jnp.zeros_like(q)
        q2_ref[hp, 0:tq, :] = jnp.where(lane_q < SB_HEAD_DIM, q, zero)
        q2_ref[hp, tq:2 * tq, :] = jnp.where(lane_q >= SB_HEAD_DIM, q, zero)
        car_ref[hp] = jnp.zeros((2 * tq, LANES), F32)
        acc_ref[hp] = jnp.zeros((2 * tq, LANES), F32)

    def stage_a(j, mask):
        k0 = pl.multiple_of(j * tk, tk)
        for hp in range(npairs):
            kt = k_ref[0, pl.ds(k0, tk), hp * LANES:(hp + 1) * LANES]
            z = lax.dot_general(q2_ref[hp], kt, (((1,), (1,)), ((), ())), preferred_element_type=F32)
            sp = jnp.where(z > SOFTPLUS_CUTOFF, z, jnp.log(1.0 + jnp.exp2(z)) * LOG2E)
            if mask is not None:
                sp = jnp.where(mask, sp, 0.0)
            e = z + _dot(sp.astype(BF16), tri_ref[...])
            if mask is not None:
                e = jnp.where(mask, e, MASKED_EXPONENT)
            e_ref[hp] = e
            tot_ref[hp] = jnp.broadcast_to(jnp.sum(sp, axis=1, keepdims=True), (2 * tq, LANES))

    def stage_b(j):
        k0 = pl.multiple_of(j * tk, tk)
        for hp in range(npairs):
            car = car_ref[hp]
            w = jnp.exp2(e_ref[hp] + jnp.concatenate([car] * (tk // LANES), axis=1)).astype(BF16)
            vt = v_ref[0, pl.ds(k0, tk), hp * LANES:(hp + 1) * LANES]
            acc_ref[hp] += _dot(w, vt)
            car_ref[hp] = car - tot_ref[hp]

    j_last = row0 // tk
    row = lax.broadcasted_iota(jnp.int32, (2 * tq, tk), 0)
    col = lax.broadcasted_iota(jnp.int32, (2 * tq, tk), 1)
    row = jnp.where(row >= tq, row - tq, row) + row0
    stage_a(j_last, (col + j_last * tk) < row)

    def body(jj, carry):
        j = j_last - 1 - jj
        stage_b(j + 1)
        stage_a(j, None)
        return carry

    lax.fori_loop(0, j_last, body, 0)
    stage_b(0)
    if tq < o_ref.shape[1]:
        o_ref[0, tq:, :] = jnp.zeros((o_ref.shape[1] - tq, o_ref.shape[2]), BF16)
    for hp in range(npairs):
        o_ref[0, 0:tq, hp * LANES:(hp + 1) * LANES] = jnp.where(
            lane_q < SB_HEAD_DIM, acc_ref[hp, 0:tq, :], acc_ref[hp, tq:2 * tq, :]).astype(BF16)


def _attention(q, k, v, tri, t_real):
    bsz, tp, width = q.shape
    bq, tk = ATT_TQ, ATT_TK
    assert bq <= tk and tk % bq == 0
    npairs = width // LANES
    n_full = t_real // bq
    rem = -(-(t_real - n_full * bq) // BF16_ROWS) * BF16_ROWS

    def call(tq, tile0, n_compute, n_tiles, prev):
        blk = lambda b, i: (b, i + tile0, 0)
        full = lambda b, i: (b, 0, 0)
        in_specs = [pl.BlockSpec((1, bq, width), blk), pl.BlockSpec((1, tp, width), full),
                    pl.BlockSpec((1, tp, width), full), pl.BlockSpec(tri.shape, lambda b, i: (0, 0))]
        args = [q, k, v, tri]
        if prev is not None:
            in_specs.insert(0, pl.BlockSpec(memory_space=pl.ANY))
            args.insert(0, prev)
        return pl.pallas_call(
            functools.partial(_attn_kernel, tq=tq, tile0=tile0, aliased=prev is not None, n_compute=n_compute),
            grid=(bsz, n_tiles),
            in_specs=in_specs,
            out_specs=pl.BlockSpec((1, bq, width), blk),
            out_shape=jax.ShapeDtypeStruct(q.shape, BF16),
            input_output_aliases={} if prev is None else {0: 0},
            scratch_shapes=[pltpu.VMEM((npairs, 2 * tq, LANES), BF16),
                            pltpu.VMEM((npairs, 2 * tq, tk), F32),
                            pltpu.VMEM((npairs, 2 * tq, LANES), F32),
                            pltpu.VMEM((npairs, 2 * tq, LANES), F32),
                            pltpu.VMEM((npairs, 2 * tq, LANES), F32)],
            compiler_params=_params("parallel", "arbitrary"),
            name="stick_breaking" if prev is None else "stick_breaking_tail",
        )(*args)

    out = call(bq, 0, n_full, tp // bq, None)
    if rem:
        out = call(rem, n_full, 1, 1, out)
    return out


def _even_ffn_kernel(h_ref, u_ref, o_ref, wo_ref, g_ref, wgu_ref, wd_ref, out_ref, xn_ref, *, tf):
    @pl.when(pl.program_id(1) == 0)
    def _():
        h1 = h_ref[...] + _dot(u_ref[...], wo_ref[0:CONV_CH, :]) + _dot(o_ref[...], wo_ref[CONV_CH:, :])
        out_ref[...] = h1
        xn_ref[...] = _rms(h1, g_ref[...]).astype(BF16)

    gu = _dot(xn_ref[...], wgu_ref[...])
    g = gu[:, 0:tf]
    mid = g * _sigmoid(g) * gu[:, tf:2 * tf]
    out_ref[...] += _dot(mid.astype(BF16), wd_ref[...])


def _even_ffn(h, u, o, wo, g, wgu, wd, tf):
    n, d = h.shape
    ff = wd.shape[0]
    tm = ROW_TILE
    row = lambda i, f: (i, 0)
    const = lambda i, f: (0, 0)
    return pl.pallas_call(
        functools.partial(_even_ffn_kernel, tf=tf),
        grid=(n // tm, ff // tf),
        in_specs=[pl.BlockSpec((tm, d), row), pl.BlockSpec((tm, CONV_CH), row),
                  pl.BlockSpec((tm, SB_WIDTH), row), pl.BlockSpec(wo.shape, const),
                  pl.BlockSpec((1, d), const),
                  pl.BlockSpec((d, 2 * tf), lambda i, f: (0, f)),
                  pl.BlockSpec((tf, d), lambda i, f: (f, 0))],
        out_specs=pl.BlockSpec((tm, d), row),
        out_shape=jax.ShapeDtypeStruct((n, d), F32),
        scratch_shapes=[pltpu.VMEM((tm, d), BF16)],
        compiler_params=_params("parallel", "arbitrary"),
        name="even_out_swiglu",
    )(h, u, o, wo, g, wgu, wd)


def _row_scatter_kernel(win_ref, x_ref, info_ref, off_ref, init_hbm, o_hbm, cbuf, sems):
    del init_hbm
    i = pl.program_id(0)
    last = pl.num_programs(0) - 1
    n_slots = cbuf.shape[1]
    buf = i % 2

    def drained(b):
        return pltpu.make_async_copy(cbuf.at[b], o_hbm.at[pl.ds(0, n_slots)], sems.at[b])

    info = info_ref[...]
    sub = lax.broadcasted_iota(jnp.int32, info.shape, 0)
    valid = info[ROUTE_VALID:ROUTE_VALID + 1, :] > 0.5

    def local_slot(e_row, r_row):
        expert = info[e_row:e_row + 1, :].astype(jnp.int32)
        off = jnp.sum(jnp.where(sub == expert, off_ref[:, 0:1], 0.0), axis=0, keepdims=True)
        return jnp.where(valid, off + info[r_row:r_row + 1, :], -1.0)

    slot = lax.broadcasted_iota(jnp.int32, (n_slots, info.shape[1]), 0).astype(F32)
    pick = (jnp.where(slot == local_slot(ROUTE_E1, ROUTE_R1), 1.0, 0.0)
            + jnp.where(slot == local_slot(ROUTE_E2, ROUTE_R2), 1.0, 0.0)).astype(BF16)
    rows = _dot(pick, x_ref[...]).astype(cbuf.dtype)

    @pl.when(i >= 2)
    def _():
        drained(buf).wait()

    cbuf[buf] = rows
    for s in range(ROW_NWIN):
        dst = pl.multiple_of(win_ref[i * ROW_NWIN + s], ROW_WIN)
        pltpu.make_async_copy(cbuf.at[buf, pl.ds(s * ROW_WIN, ROW_WIN)], o_hbm.at[pl.ds(dst, ROW_WIN)],
                              sems.at[buf]).start(priority=s % 2)

    @pl.when(i == last)
    def _():
        drained(buf).wait()

        @pl.when(i >= 1)
        def _():
            drained(1 - buf).wait()


def _row_scatter(win_rows, x, info, local_off, n_rows_alloc):
    n, d = x.shape
    tm = COMBINE_TILE
    init = jnp.zeros((n_rows_alloc, d), x.dtype)
    return pl.pallas_call(
        _row_scatter_kernel,
        grid_spec=pltpu.PrefetchScalarGridSpec(
            num_scalar_prefetch=1, grid=(n // tm,),
            in_specs=[pl.BlockSpec((tm, d), lambda i, w: (i, 0)), pl.BlockSpec((SUBLANES, tm), lambda i, w: (i, 0)),
                      pl.BlockSpec((SUBLANES, LANES), lambda i, w: (i, 0)), pl.BlockSpec(memory_space=pl.ANY)],
            out_specs=pl.BlockSpec(memory_space=pl.ANY),
            scratch_shapes=[pltpu.VMEM((2, ROW_NWIN * ROW_WIN, d), x.dtype), pltpu.SemaphoreType.DMA((2,))]),
        out_shape=jax.ShapeDtypeStruct(init.shape, init.dtype),
        input_output_aliases={4: 0},
        compiler_params=_params("arbitrary"),
        name="moe_row_scatter",
    )(win_rows, x, info, local_off, init)


def _gmm_kernel(te_ref, nu_ref, xs_ref, wg_ref, wu_ref, wd_ref, o_ref, acc_ref):
    p = pl.program_id(0)
    f = pl.program_id(1)
    last = pl.num_programs(1) - 1
    used = p < nu_ref[0]

    @pl.when(used)
    def _():
        x = xs_ref[...]
        g = _dot(x, wg_ref[0])
        mid = g * _sigmoid(g) * _dot(x, wu_ref[0])
        part = _dot(mid.astype(BF16), wd_ref[0])

        @pl.when(f == 0)
        def _():
            acc_ref[...] = part

        @pl.when(f > 0)
        def _():
            acc_ref[...] += part

        @pl.when(f == last)
        def _():
            o_ref[...] = acc_ref[...].astype(o_ref.dtype)

    @pl.when(jnp.logical_not(used) & (f == last))
    def _():
        o_ref[...] = jnp.zeros(o_ref.shape, o_ref.dtype)


def _grouped_ffn(tile_expert, n_used, xs, n_rows, wg, wu, wd):
    d = xs.shape[1]
    ff = wd.shape[1]
    tf = MOE_FF_TILE
    nf = ff // tf
    tg = MOE_TILE
    row = lambda p, f, te, nu: (p, 0)
    fidx = lambda p, f, nu: jnp.where(p < nu[0], f, nf - 1)
    up = pl.BlockSpec((1, d, tf), lambda p, f, te, nu: (te[p], 0, fidx(p, f, nu)))
    return pl.pallas_call(
        _gmm_kernel,
        grid_spec=pltpu.PrefetchScalarGridSpec(
            num_scalar_prefetch=2, grid=(n_rows // tg, nf),
            in_specs=[pl.BlockSpec((tg, d), row), up, up,
                      pl.BlockSpec((1, tf, d), lambda p, f, te, nu: (te[p], fidx(p, f, nu), 0))],
            out_specs=pl.BlockSpec((tg, d), row),
            scratch_shapes=[pltpu.VMEM((tg, d), F32)]),
        out_shape=jax.ShapeDtypeStruct((n_rows, d), BF16),
        compiler_params=_params("arbitrary", "arbitrary"),
        name="moe_grouped_ffn",
    )(tile_expert, n_used, xs, wg, wu, wd)


def _combine_kernel(win_ref, h_ref, tok_ref, w1_ref, w2_ref, starts_ref, rows_ref, ys_hbm, o_ref, gbuf, sems):
    i = pl.program_id(0)
    tm = h_ref.shape[0]
    win = ROW_WIN

    def fetch(tile, slot):
        for s in range(ROW_NWIN):
            start = pl.multiple_of(win_ref[tile * ROW_NWIN + s], win)
            pltpu.make_async_copy(ys_hbm.at[pl.ds(start, win)], gbuf.at[slot, pl.ds(s * win, win)],
                                  sems.at[slot]).start(priority=s % 2)

    @pl.when(i == 0)
    def _():
        fetch(0, 0)

    @pl.when(i + 1 < pl.num_programs(0))
    def _():
        fetch(i + 1, (i + 1) % 2)

    slot = i % 2
    pltpu.make_async_copy(ys_hbm.at[pl.ds(0, ROW_NWIN * win)], gbuf.at[slot], sems.at[slot]).wait()

    tok = tok_ref[...]
    lane = lax.broadcasted_iota(jnp.int32, tok.shape, 1)
    field = lambda c: jnp.sum(jnp.where(lane == c, tok, 0.0), axis=1, keepdims=True)
    valid = field(ROUTE_VALID) > 0.5
    rows = rows_ref[0]
    reps = h_ref.shape[1] // LANES
    out = h_ref[...]
    for e_col, r_col, w_ref in ((ROUTE_E1, ROUTE_R1, w1_ref), (ROUTE_E2, ROUTE_R2, w2_ref)):
        start = jnp.sum(jnp.where(lane == field(e_col).astype(jnp.int32), starts_ref[...], 0.0),
                        axis=1, keepdims=True)
        dest = jnp.where(valid, start + field(r_col), -2.0)
        pick = jnp.where(dest == rows, 1.0, 0.0).astype(BF16)
        out = out + jnp.concatenate([w_ref[...]] * reps, axis=1) * _dot(pick, gbuf[slot])
    o_ref[...] = out


def _combine(win_start, h, tok, w1b, w2b, starts_row, slot_rows, ys):
    n, d = h.shape
    tm = COMBINE_TILE
    n_slots = ROW_NWIN * ROW_WIN
    row = lambda i, w: (i, 0)
    return pl.pallas_call(
        _combine_kernel,
        grid_spec=pltpu.PrefetchScalarGridSpec(
            num_scalar_prefetch=1, grid=(n // tm,),
            in_specs=[pl.BlockSpec((tm, d), row), pl.BlockSpec((tm, LANES), row), pl.BlockSpec((tm, LANES), row),
                      pl.BlockSpec((tm, LANES), row), pl.BlockSpec((1, LANES), lambda i, w: (0, 0)),
                      pl.BlockSpec((1, 1, n_slots), lambda i, w: (i, 0, 0)),
                      pl.BlockSpec(memory_space=pl.ANY)],
            out_specs=pl.BlockSpec((tm, d), row),
            scratch_shapes=[pltpu.VMEM((2, n_slots, d), ys.dtype), pltpu.SemaphoreType.DMA((2,))]),
        out_shape=jax.ShapeDtypeStruct(h.shape, F32),
        compiler_params=_params("arbitrary"),
        name="moe_combine",
    )(win_start, h, tok, w1b, w2b, starts_row, slot_rows, ys)


def _gate_windows(lw, bd):
    out = []
    for c0 in range(0, lw, LRU_GATE_TILE):
        nc = min(LRU_GATE_TILE, lw - c0)
        r0 = (c0 // bd) * bd
        r1 = ((c0 + nc - 1) // bd + 1) * bd
        r0 = (r0 // LANES) * LANES
        r1 = min(-(-r1 // LANES) * LANES, lw)
        out.append((c0, nc, r0, r1 - r0))
    return out


def _lru_kernel(h_ref, ng_ref, win_ref, cw_ref, cb_ref, wg_ref, br_ref, bi_ref, lam_ref, wo_ref,
                o_ref, xbuf_ref, hs_ref, a_ref, b_ref, y_ref, gate_ref, res_ref, *, windows, bsz):
    ti = pl.program_id(0)
    rows, lw = a_ref.shape
    halo = (LRU_CONV_WIDTH - 1) * bsz

    @pl.when(ti == 0)
    def _():
        xbuf_ref[0:halo, :] = jnp.zeros((halo, lw), F32)
        hs_ref[...] = jnp.zeros(hs_ref.shape, F32)

    @pl.when(ti > 0)
    def _():
        xbuf_ref[0:halo, :] = xbuf_ref[rows:rows + halo, :]

    res_ref[...] = jnp.concatenate([h_ref[:, t, :] for t in range(rows // bsz)], axis=0)
    proj = _dot(_rms(res_ref[...], ng_ref[...]).astype(BF16), win_ref[...])
    gate_ref[...] = jax.nn.gelu(proj[:, 0:lw], approximate=True)
    xbuf_ref[halo:halo + rows, :] = proj[:, lw:2 * lw]
    xc = jnp.broadcast_to(cb_ref[...], (rows, lw))
    for k in range(LRU_CONV_WIDTH):
        xc = xc + cw_ref[k:k + 1, :] * xbuf_ref[k * bsz:k * bsz + rows, :]
    xcb = xc.astype(BF16)
    sp_lam = jnp.log(1.0 + jnp.exp(-lam_ref[...]))
    log2_a_per_r = (-LRU_C * LOG2E) * sp_lam
    wcol = 0
    for (c0, nc, r0, nr) in windows:
        rg = _dot(xcb[:, r0:r0 + nr], wg_ref[r0:r0 + nr, wcol:wcol + 2 * nc])
        wcol += 2 * nc
        r = _sigmoid(rg[:, 0:nc] + br_ref[:, c0:c0 + nc])
        ig = _sigmoid(rg[:, nc:2 * nc] + bi_ref[:, c0:c0 + nc])
        a = jnp.exp2(r * log2_a_per_r[:, c0:c0 + nc])
        a_ref[:, c0:c0 + nc] = a
        one_m_a2 = 1.0 - a * a
        root = jnp.where(one_m_a2 > 0.0, one_m_a2 * lax.rsqrt(one_m_a2), 0.0)
        b_ref[:, c0:c0 + nc] = root * (ig * xc[:, c0:c0 + nc])

    def step(t, hprev):
        r0 = pl.multiple_of(t * bsz, SUBLANES)
        hnew = a_ref[pl.ds(r0, bsz), :] * hprev + b_ref[pl.ds(r0, bsz), :]
        y_ref[pl.ds(r0, bsz), :] = hnew
        return hnew

    hs_ref[...] = lax.fori_loop(0, rows // bsz, step, hs_ref[...], unroll=LRU_SCAN_UNROLL)
    gy = (gate_ref[...] * y_ref[...]).astype(BF16)
    o_ref[...] = res_ref[...] + _dot(gy, wo_ref[...])


def _lru_mixer(h, ng, win, cw, cb, wg, br, bi, lam, wo, windows):
    bsz, tp, d = h.shape
    n = bsz * tp
    lw = lam.shape[-1]
    assert bsz % SUBLANES == 0 and LRU_ROWS % bsz == 0 and n % LRU_ROWS == 0
    rows = LRU_ROWS
    halo = (LRU_CONV_WIDTH - 1) * bsz
    const = lambda t: (0, 0)
    blk = lambda t: (t, 0)
    resident = lambda w: pl.BlockSpec(w.shape, const, pipeline_mode=pl.Buffered(1))
    return pl.pallas_call(
        functools.partial(_lru_kernel, windows=windows, bsz=bsz),
        grid=(n // rows,),
        in_specs=[pl.BlockSpec((bsz, rows // bsz, d), lambda t: (0, t, 0)), pl.BlockSpec((1, d), const), resident(win),
                  pl.BlockSpec(cw.shape, const), pl.BlockSpec((1, lw), const), resident(wg),
                  pl.BlockSpec((1, lw), const), pl.BlockSpec((1, lw), const),
                  pl.BlockSpec((1, lw), const), resident(wo)],
        out_specs=pl.BlockSpec((rows, d), blk),
        out_shape=jax.ShapeDtypeStruct((n, d), F32),
        scratch_shapes=[pltpu.VMEM((halo + rows, lw), F32), pltpu.VMEM((bsz, lw), F32),
                        pltpu.VMEM((rows, lw), F32), pltpu.VMEM((rows, lw), F32), pltpu.VMEM((rows, lw), F32),
                        pltpu.VMEM((rows, lw), F32), pltpu.VMEM((rows, d), F32)],
        compiler_params=_params("arbitrary"),
        name="rglru_mixer",
    )(h, ng, win, cw, cb, wg, br, bi, lam, wo)


ROUTE_E1, ROUTE_E2, ROUTE_R1, ROUTE_R2, ROUTE_VALID = range(5)


def _router_kernel(h_ref, g_ref, rw_ref, ltri_ref, xn_ref, info_ref, tok_ref, w1_ref, w2_ref, before_ref, cnt_ref,
                   run_ref, *, n_real_rows):
    i = pl.program_id(0)
    tm = h_ref.shape[0]

    @pl.when(i == 0)
    def _():
        run_ref[...] = jnp.zeros(run_ref.shape, F32)

    xn = _rms(h_ref[...], g_ref[...])
    xn_ref[...] = xn.astype(BF16)
    split = lambda a: (a.astype(BF16), (a - a.astype(BF16).astype(F32)).astype(BF16))
    (x_hi, x_lo), (r_hi, r_lo) = split(xn), split(rw_ref[...])
    logits = _dot(x_hi, r_hi) + (_dot(x_hi, r_lo) + _dot(x_lo, r_hi))
    lane = lax.broadcasted_iota(jnp.int32, logits.shape, 1)
    neg = jnp.float32(-jnp.inf)
    logits = jnp.where(lane < N_EXPERTS, logits, neg)
    top1 = jnp.max(logits, axis=1, keepdims=True)
    idx1 = jnp.min(jnp.where(logits == top1, lane, LANES), axis=1, keepdims=True)
    rest = jnp.where(lane == idx1, neg, logits)
    top2 = jnp.max(rest, axis=1, keepdims=True)
    idx2 = jnp.min(jnp.where(rest == top2, lane, LANES), axis=1, keepdims=True)
    w1 = 1.0 / (1.0 + jnp.exp(top2 - top1))

    rowg = lax.broadcasted_iota(jnp.int32, (tm, 1), 0) + i * tm
    valid = jnp.where(rowg < n_real_rows, 1.0, 0.0)
    onehot = (jnp.where(lane == idx1, 1.0, 0.0) + jnp.where(lane == idx2, 1.0, 0.0)) * valid
    before_ref[...] = jnp.broadcast_to(run_ref[...], before_ref.shape)
    before = _dot(ltri_ref[...], onehot.astype(BF16)) + run_ref[...]
    rank1 = jnp.sum(jnp.where(lane == idx1, before, 0.0), axis=1, keepdims=True)
    rank2 = jnp.sum(jnp.where(lane == idx2, before, 0.0), axis=1, keepdims=True)
    run_ref[...] += jnp.ceil(jnp.sum(onehot, axis=0, keepdims=True) / ROW_WIN) * ROW_WIN
    cnt_ref[...] = run_ref[...]
    w1_ref[...] = jnp.broadcast_to(w1, (tm, LANES))
    w2_ref[...] = jnp.broadcast_to(1.0 - w1, (tm, LANES))
    fields = {ROUTE_E1: idx1.astype(F32), ROUTE_E2: idx2.astype(F32),
              ROUTE_R1: rank1, ROUTE_R2: rank2, ROUTE_VALID: valid}
    info = jnp.zeros(logits.shape, F32)
    for col, val in fields.items():
        info = jnp.where(lane == col, val, info)
    tok_ref[...] = info
    info_ref[...] = info.T[0:SUBLANES, :]


def _router(h, g, rw, ltri, n_real_rows):
    n, d = h.shape
    tm = ROW_TILE
    row = lambda i: (i, 0)
    const = lambda i: (0, 0)
    return pl.pallas_call(
        functools.partial(_router_kernel, n_real_rows=n_real_rows),
        grid=(n // tm,),
        in_specs=[pl.BlockSpec((tm, d), row), pl.BlockSpec((1, d), const), pl.BlockSpec(rw.shape, const),
                  pl.BlockSpec(ltri.shape, const)],
        out_specs=[pl.BlockSpec((tm, d), row), pl.BlockSpec((SUBLANES, tm), row),
                   pl.BlockSpec((tm, LANES), row), pl.BlockSpec((tm, LANES), row),
                   pl.BlockSpec((tm, LANES), row), pl.BlockSpec((SUBLANES, LANES), row),
                   pl.BlockSpec((1, LANES), const)],
        out_shape=[jax.ShapeDtypeStruct((n, d), BF16), jax.ShapeDtypeStruct((n // tm * SUBLANES, tm), F32),
                   jax.ShapeDtypeStruct((n, LANES), F32), jax.ShapeDtypeStruct((n, LANES), F32),
                   jax.ShapeDtypeStruct((n, LANES), F32), jax.ShapeDtypeStruct((n // tm * SUBLANES, LANES), F32),
                   jax.ShapeDtypeStruct((1, LANES), F32)],
        scratch_shapes=[pltpu.VMEM((1, LANES), F32)],
        compiler_params=_params("arbitrary"),
        name="router",
    )(h, g, rw, ltri)


def _routing_tables(before, counts, n_rows):
    tg = MOE_TILE
    win = ROW_WIN
    cnt = counts[0, :N_EXPERTS].astype(jnp.int32)
    padded = (cnt + tg - 1) // tg * tg
    ends = jnp.cumsum(padded)
    starts = ends - padded
    tile_start = jnp.arange(n_rows // tg, dtype=jnp.int32) * tg
    tile_expert = jnp.minimum(jnp.sum(tile_start[:, None] >= ends[None, :], axis=1), N_EXPERTS - 1)
    n_used = (ends[-1] // tg).reshape(1)
    bef = before.reshape(-1, SUBLANES, LANES)[:, 0, :N_EXPERTS].astype(jnp.int32)
    nwin = (jnp.concatenate([bef[1:], cnt[None, :]], axis=0) - bef) // win
    base = jnp.cumsum(nwin, axis=1) - nwin
    slot = jnp.arange(ROW_NWIN, dtype=jnp.int32)[None, :, None]
    mine = (slot >= base[:, None, :]) & (slot < (base + nwin)[:, None, :])
    rows = jnp.sum(jnp.where(mine, (starts[None, :] + bef)[:, None, :] + win * (slot - base[:, None, :]), 0), axis=2)
    used = jnp.any(mine, axis=2)
    parity = (jnp.arange(bef.shape[0], dtype=jnp.int32) % 2)[:, None]
    scatter_rows = jnp.where(used, rows, n_rows + win * (parity * ROW_NWIN + slot[:, :, 0]))
    gather_rows = jnp.where(used, rows, 0)
    slot_rows = jnp.where(used[..., None], rows[..., None] + jnp.arange(win, dtype=jnp.int32), -1)
    slot_rows = slot_rows.reshape(-1, 1, ROW_NWIN * win).astype(F32)
    starts_row = jnp.zeros((1, LANES), F32).at[0, :N_EXPERTS].set(starts.astype(F32))
    local_off = jnp.zeros(bef.shape[:1] + (SUBLANES,), jnp.int32).at[:, :N_EXPERTS].set(win * base - bef)
    local_off = jnp.broadcast_to(local_off.astype(F32)[:, :, None], local_off.shape + (LANES,)).reshape(-1, LANES)
    return (tile_expert.astype(jnp.int32), n_used.astype(jnp.int32), scatter_rows.reshape(-1).astype(jnp.int32),
            gather_rows.reshape(-1).astype(jnp.int32), slot_rows, starts_row, local_off)


def _interleave_gate_up(wg, wu, tf):
    e, d, ff = wg.shape
    g = wg.astype(BF16).reshape(e, d, ff // tf, tf)
    u = wu.astype(BF16).reshape(e, d, ff // tf, tf)
    return jnp.concatenate([g, u], axis=3).reshape(e, d, 2 * ff)


def _dense_block_diag(w):
    nb, bd, _ = w.shape
    eye = jnp.eye(nb, dtype=w.dtype)
    return (w[:, :, None, :] * eye[:, None, :, None]).reshape(nb * bd, nb * bd)


def _pack_lru_gates(wr, wi, windows):
    dr = _dense_block_diag(wr).astype(BF16)
    di = _dense_block_diag(wi).astype(BF16)
    cols = []
    for (c0, nc, _, _) in windows:
        cols += [dr[:, c0:c0 + nc], di[:, c0:c0 + nc]]
    return jnp.concatenate(cols, axis=1)


def _ffn_chunk(ff):
    for tf in (896, 512, 384, 256, 128):
        if ff % tf == 0:
            return tf
    raise ValueError(f"unsupported d_ff {ff}")


def kernel(x, meta_tokens, mix_norm_even, w_in_even, conv_w, conv_b, conv_ln_g, conv_ln_b, q_norm_g, k_norm_g, w_out_even, ffn_norm_even, ffn_w_gate, ffn_w_up, ffn_w_down, mix_norm_odd, w_in_odd, lru_conv_w, lru_conv_b, gate_r_w, gate_r_b, gate_i_w, gate_i_b, lru_lambda, w_out_odd, ffn_norm_odd, router_w, moe_w_gate, moe_w_up, moe_w_down):
    bsz, seq, d = x.shape
    t_real = N_META + seq
    tp = -(-t_real // TIME_TILE) * TIME_TILE
    n = bsz * tp
    assert n % ROW_TILE == 0
    depth = mix_norm_even.shape[0] + mix_norm_odd.shape[0]

    meta = jnp.broadcast_to(meta_tokens[None].astype(x.dtype), (bsz, N_META, d))
    h = jnp.concatenate([meta, x, jnp.zeros((bsz, tp - t_real, d), x.dtype)], axis=1).reshape(n, d)

    head_mean = jnp.kron(jnp.eye(MXU_DIM // SB_HEAD_DIM, dtype=F32),
                         jnp.full((SB_HEAD_DIM, SB_HEAD_DIM), 1.0 / SB_HEAD_DIM, F32)).astype(BF16)
    kk = jnp.arange(ATT_TK)
    neg_tri = -(kk[:, None] >= kk[None, :]).astype(BF16)
    row2 = lambda a: a.reshape(1, -1)

    time_major = False
    for layer in range(depth):
        p = layer // 2
        if layer % 2 == 0:
            if time_major:
                h = h.reshape(tp, bsz, d).transpose(1, 0, 2).reshape(n, d)
                time_major = False
            u, q, k, v = _even_in_proj(h, row2(mix_norm_even[p]), w_in_even[p].astype(BF16),
                                       row2(jnp.tile(q_norm_g[p], SB_HEADS)),
                                       row2(jnp.tile(k_norm_g[p], SB_HEADS)), head_mean)
            u = _conv_module(u.reshape(bsz, tp, CONV_CH), conv_w[p], row2(conv_b[p]),
                             row2(conv_ln_g[p]), row2(conv_ln_b[p]))
            o = _attention(q.reshape(bsz, tp, SB_WIDTH), k.reshape(bsz, tp, SB_WIDTH),
                           v.reshape(bsz, tp, SB_WIDTH), neg_tri, t_real)
            tf = _ffn_chunk(ffn_w_gate.shape[-1])
            h = _even_ffn(h, u.reshape(n, CONV_CH), o.reshape(n, SB_WIDTH), w_out_even[p].astype(BF16),
                          row2(ffn_norm_even[p]),
                          _interleave_gate_up(ffn_w_gate[p][None], ffn_w_up[p][None], tf)[0],
                          ffn_w_down[p].astype(BF16), tf)
        else:
            if time_major:
                h = h.reshape(tp, bsz, d).transpose(1, 0, 2).reshape(n, d)
            time_major = True
            lw = lru_lambda.shape[-1]
            windows = _gate_windows(lw, lw // LRU_BLOCKS)
            h = _lru_mixer(h.reshape(bsz, tp, d), row2(mix_norm_odd[p]), w_in_odd[p].astype(BF16),
                           lru_conv_w[p], row2(lru_conv_b[p]),
                           _pack_lru_gates(gate_r_w[p], gate_i_w[p], windows),
                           row2(gate_r_b[p]), row2(gate_i_b[p]), row2(lru_lambda[p]),
                           w_out_odd[p].astype(BF16), windows)
            rw = jnp.pad(router_w[p], ((0, 0), (0, LANES - N_EXPERTS)))
            kk = jnp.arange(ROW_TILE)
            ltri = (kk[:, None] > kk[None, :]).astype(BF16)
            xn, info, tok, w1b, w2b, before, counts = _router(h, row2(ffn_norm_odd[p]), rw, ltri, t_real * bsz)
            tg = MOE_TILE
            assert ROW_TILE == COMBINE_TILE
            n_tiles = n // COMBINE_TILE
            n_rows = 2 * bsz * t_real + n_tiles * N_EXPERTS * (ROW_WIN - 1) + N_EXPERTS * (tg - 1)
            n_rows = -(-n_rows // tg) * tg
            tile_expert, n_used, scatter_rows, gather_rows, slot_rows, starts_row, local_off = _routing_tables(
                before, counts, n_rows)
            xs = _row_scatter(scatter_rows, xn, info, local_off, n_rows + 2 * ROW_NWIN * ROW_WIN)
            ys = _grouped_ffn(tile_expert, n_used, xs, n_rows, moe_w_gate[p].astype(BF16),
                              moe_w_up[p].astype(BF16), moe_w_down[p].astype(BF16))
            h = _combine(gather_rows, h, tok, w1b, w2b, starts_row, slot_rows, ys)
    if time_major:
        return h.reshape(tp, bsz, d)[N_META:t_real].transpose(1, 0, 2)
    return h.reshape(bsz, tp, d)[:, N_META:t_real]
```

```python
import functools
import math

import jax
import jax.numpy as jnp
from jax import lax
from jax.experimental import pallas as pl
from jax.experimental.pallas import tpu as pltpu

F32 = jnp.float32
BF16 = jnp.bfloat16

EPS = 1e-6
N_META = 16
CONV_CH = 512
CONV_WIDTH = 31
SB_HEADS = 8
SB_HEAD_DIM = 64
SB_WIDTH = SB_HEADS * SB_HEAD_DIM
LRU_BLOCKS = 16
LRU_CONV_WIDTH = 4
LRU_C = 8.0
N_EXPERTS = 8

LANES = 128
MXU_DIM = 256
TIME_TILE = 256
VMEM_LIMIT = 50 * 1024 * 1024

ROW_TILE = 512
CONV_TT = 128
CONV_HALO = 32
CONV_CHUNK = 32
ATT_TQ = 256
ATT_TK = 256
LOG2E = 1.4426950408889634
MASKED_EXPONENT = -1e30
SOFTPLUS_CUTOFF = 126.0
LRU_ROWS = 256
LRU_SCAN_UNROLL = 4
LRU_GATE_TILE = 256
MOE_TILE = 512
MOE_FF_TILE = 1792
COMBINE_TILE = 512
ROW_WIN = 16
ROW_NWIN = 2 * COMBINE_TILE // ROW_WIN + N_EXPERTS
SUBLANES = 8
BF16_ROWS = 16


def _dot(a, b):
    return jnp.dot(a, b, preferred_element_type=F32)


def _params(*sem):
    return pltpu.CompilerParams(dimension_semantics=sem, vmem_limit_bytes=VMEM_LIMIT)


def _sigmoid(x):
    return 0.5 * jnp.tanh(0.5 * x) + 0.5


def _rms(x, g):
    ms = jnp.mean(x * x, axis=-1, keepdims=True)
    return x * lax.rsqrt(ms + EPS) * g


def _even_in_kernel(h_ref, g_ref, w_ref, qg_ref, kg_ref, hm_ref, u_ref, q_ref, k_ref, v_ref):
    xn = _rms(h_ref[...], g_ref[...]).astype(BF16)
    a = _dot(xn, w_ref[:, 0:CONV_CH])
    gate = _dot(xn, w_ref[:, CONV_CH:2 * CONV_CH])
    u_ref[...] = (a * _sigmoid(gate)).astype(BF16)

    def head_norm(y, gain):
        yy = (y * y).astype(BF16)
        parts = [_dot(yy[:, c:c + MXU_DIM], hm_ref[...]) for c in range(0, SB_WIDTH, MXU_DIM)]
        ms = jnp.concatenate(parts, axis=1)
        return y * lax.rsqrt(ms + EPS) * gain

    c0 = 2 * CONV_CH
    q = head_norm(_dot(xn, w_ref[:, c0:c0 + SB_WIDTH]), qg_ref[...])
    q_ref[...] = (q * (LOG2E / math.sqrt(SB_HEAD_DIM))).astype(BF16)
    k = head_norm(_dot(xn, w_ref[:, c0 + SB_WIDTH:c0 + 2 * SB_WIDTH]), kg_ref[...])
    k_ref[...] = k.astype(BF16)
    v_ref[...] = _dot(xn, w_ref[:, c0 + 2 * SB_WIDTH:c0 + 3 * SB_WIDTH]).astype(BF16)


def _even_in_proj(h, g, w, qg, kg, hm):
    n, d = h.shape
    tm = ROW_TILE
    row = lambda i: (i, 0)
    const = lambda i: (0, 0)
    out = jax.ShapeDtypeStruct((n, SB_WIDTH), BF16)
    return pl.pallas_call(
        _even_in_kernel,
        grid=(n // tm,),
        in_specs=[pl.BlockSpec((tm, d), row), pl.BlockSpec((1, d), const),
                  pl.BlockSpec(w.shape, const), pl.BlockSpec((1, SB_WIDTH), const),
                  pl.BlockSpec((1, SB_WIDTH), const), pl.BlockSpec(hm.shape, const)],
        out_specs=[pl.BlockSpec((tm, SB_WIDTH), row)] * 4,
        out_shape=[out] * 4,
        compiler_params=_params("parallel"),
        name="even_in_proj",
    )(h, g, w, qg, kg, hm)


def _conv_kernel(cur_ref, halo_ref, w_ref, b_ref, lg_ref, lb_ref, o_ref, buf_ref, sh_ref):
    i = pl.program_id(1)
    tt = cur_ref.shape[1]
    rows = CONV_HALO + tt
    halo = halo_ref[0].astype(F32)
    buf_ref[0:CONV_HALO, :] = jnp.where(i > 0, halo, 0.0)
    buf_ref[CONV_HALO:rows, :] = cur_ref[0].astype(F32)
    for s in range(1, SUBLANES):
        sh_ref[s - 1, 0:rows - SUBLANES, :] = buf_ref[s:s + rows - SUBLANES, :]
    off = CONV_HALO - (CONV_WIDTH - 1)
    for c in range(tt // CONV_CHUNK):
        r0 = c * CONV_CHUNK
        acc = jnp.broadcast_to(b_ref[...], (CONV_CHUNK, CONV_CH))
        for k in range(CONV_WIDTH):
            shift = (off + k) % SUBLANES
            base = r0 + off + k - shift
            if shift == 0:
                tap = buf_ref[base:base + CONV_CHUNK, :]
            else:
                tap = sh_ref[shift - 1, base:base + CONV_CHUNK, :]
            acc = acc + w_ref[k:k + 1, :] * tap
        mu = jnp.mean(acc, axis=-1, keepdims=True)
        xc = acc - mu
        var = jnp.mean(xc * xc, axis=-1, keepdims=True)
        y = xc * lax.rsqrt(var + EPS) * lg_ref[...] + lb_ref[...]
        o_ref[0, r0:r0 + CONV_CHUNK, :] = (y * _sigmoid(y)).astype(BF16)


def _conv_module(u, w, b, lg, lb):
    bsz, tp, c = u.shape
    tt = CONV_TT
    per = tt // CONV_HALO
    const = lambda bi, i: (0, 0)
    return pl.pallas_call(
        _conv_kernel,
        grid=(bsz, tp // tt),
        in_specs=[pl.BlockSpec((1, tt, c), lambda bi, i: (bi, i, 0)),
                  pl.BlockSpec((1, CONV_HALO, c), lambda bi, i: (bi, jnp.maximum(i * per - 1, 0), 0)),
                  pl.BlockSpec(w.shape, const), pl.BlockSpec((1, c), const),
                  pl.BlockSpec((1, c), const), pl.BlockSpec((1, c), const)],
        out_specs=pl.BlockSpec((1, tt, c), lambda bi, i: (bi, i, 0)),
        out_shape=jax.ShapeDtypeStruct(u.shape, BF16),
        scratch_shapes=[pltpu.VMEM((CONV_HALO + tt, c), F32),
                        pltpu.VMEM((SUBLANES - 1, CONV_HALO + tt - SUBLANES, c), F32)],
        compiler_params=_params("parallel", "parallel"),
        name="conv_module",
    )(u, u, w, b, lg, lb)


def _attn_kernel(*refs, tq, tile0, aliased, n_compute):
    if aliased:
        refs = refs[1:]
    o_ref = refs[4]
    step = pl.program_id(1)

    @pl.when(step < n_compute)
    def _():
        _attn_tile(*refs, tq=tq, row0=(step + tile0) * ATT_TQ)

    @pl.when(step >= n_compute)
    def _():
        o_ref[...] = jnp.zeros(o_ref.shape, BF16)


def _attn_tile(q_ref, k_ref, v_ref, tri_ref, o_ref, q2_ref, e_ref, tot_ref, acc_ref, car_ref, *, tq, row0):
    tk = ATT_TK
    npairs = q_ref.shape[2] // LANES
    lane_q = lax.broadcasted_iota(jnp.int32, (tq, LANES), 1)

    for hp in range(npairs):
        q = q_ref[0, 0:tq, hp * LANES:(hp + 1) * LANES]
        zero = jnp.zeros_like(q)
        q2_ref[hp, 0:tq, :] = jnp.where(lane_q < SB_HEAD_DIM, q, zero)
        q2_ref[hp, tq:2 * tq, :] = jnp.where(lane_q >= SB_HEAD_DIM, q, zero)
        car_ref[hp] = jnp.zeros((2 * tq, LANES), F32)
        acc_ref[hp] = jnp.zeros((2 * tq, LANES), F32)

    def stage_a(j, mask):
        k0 = pl.multiple_of(j * tk, tk)
        for hp in range(npairs):
            kt = k_ref[0, pl.ds(k0, tk), hp * LANES:(hp + 1) * LANES]
            z = lax.dot_general(q2_ref[hp], kt, (((1,), (1,)), ((), ())), preferred_element_type=F32)
            sp = jnp.where(z > SOFTPLUS_CUTOFF, z, jnp.log(1.0 + jnp.exp2(z)) * LOG2E)
            if mask is not None:
                sp = jnp.where(mask, sp, 0.0)
            e = z + _dot(sp.astype(BF16), tri_ref[...])
            if mask is not None:
                e = jnp.where(mask, e, MASKED_EXPONENT)
            e_ref[hp] = e
            tot_ref[hp] = jnp.broadcast_to(jnp.sum(sp, axis=1, keepdims=True), (2 * tq, LANES))

    def stage_b(j):
        k0 = pl.multiple_of(j * tk, tk)
        for hp in range(npairs):
            car = car_ref[hp]
            w = jnp.exp2(e_ref[hp] + jnp.concatenate([car] * (tk // LANES), axis=1)).astype(BF16)
            vt = v_ref[0, pl.ds(k0, tk), hp * LANES:(hp + 1) * LANES]
            acc_ref[hp] += _dot(w, vt)
            car_ref[hp] = car - tot_ref[hp]

    j_last = row0 // tk
    row = lax.broadcasted_iota(jnp.int32, (2 * tq, tk), 0)
    col = lax.broadcasted_iota(jnp.int32, (2 * tq, tk), 1)
    row = jnp.where(row >= tq, row - tq, row) + row0
    stage_a(j_last, (col + j_last * tk) < row)

    def body(jj, carry):
        j = j_last - 1 - jj
        stage_b(j + 1)
        stage_a(j, None)
        return carry

    lax.fori_loop(0, j_last, body, 0)
    stage_b(0)
    if tq < o_ref.shape[1]:
        o_ref[0, tq:, :] = jnp.zeros((o_ref.shape[1] - tq, o_ref.shape[2]), BF16)
    for hp in range(npairs):
        o_ref[0, 0:tq, hp * LANES:(hp + 1) * LANES] = jnp.where(
            lane_q < SB_HEAD_DIM, acc_ref[hp, 0:tq, :], acc_ref[hp, tq:2 * tq, :]).astype(BF16)


def _attention(q, k, v, tri, t_real):
    bsz, tp, width = q.shape
    bq, tk = ATT_TQ, ATT_TK
    assert bq <= tk and tk % bq == 0
    npairs = width // LANES
    n_full = t_real // bq
    rem = -(-(t_real - n_full * bq) // BF16_ROWS) * BF16_ROWS

    def call(tq, tile0, n_compute, n_tiles, prev):
        blk = lambda b, i: (b, i + tile0, 0)
        full = lambda b, i: (b, 0, 0)
        in_specs = [pl.BlockSpec((1, bq, width), blk), pl.BlockSpec((1, tp, width), full),
                    pl.BlockSpec((1, tp, width), full), pl.BlockSpec(tri.shape, lambda b, i: (0, 0))]
        args = [q, k, v, tri]
        if prev is not None:
            in_specs.insert(0, pl.BlockSpec(memory_space=pl.ANY))
            args.insert(0, prev)
        return pl.pallas_call(
            functools.partial(_attn_kernel, tq=tq, tile0=tile0, aliased=prev is not None, n_compute=n_compute),
            grid=(bsz, n_tiles),
            in_specs=in_specs,
            out_specs=pl.BlockSpec((1, bq, width), blk),
            out_shape=jax.ShapeDtypeStruct(q.shape, BF16),
            input_output_aliases={} if prev is None else {0: 0},
            scratch_shapes=[pltpu.VMEM((npairs, 2 * tq, LANES), BF16),
                            pltpu.VMEM((npairs, 2 * tq, tk), F32),
                            pltpu.VMEM((npairs, 2 * tq, LANES), F32),
                            pltpu.VMEM((npairs, 2 * tq, LANES), F32),
                            pltpu.VMEM((npairs, 2 * tq, LANES), F32)],
            compiler_params=_params("parallel", "arbitrary"),
            name="stick_breaking" if prev is None else "stick_breaking_tail",
        )(*args)

    out = call(bq, 0, n_full, tp // bq, None)
    if rem:
        out = call(rem, n_full, 1, 1, out)
    return out


def _even_ffn_kernel(h_ref, u_ref, o_ref, wo_ref, g_ref, wgu_ref, wd_ref, out_ref, xn_ref, *, tf):
    @pl.when(pl.program_id(1) == 0)
    def _():
        h1 = h_ref[...] + _dot(u_ref[...], wo_ref[0:CONV_CH, :]) + _dot(o_ref[...], wo_ref[CONV_CH:, :])
        out_ref[...] = h1
        xn_ref[...] = _rms(h1, g_ref[...]).astype(BF16)

    gu = _dot(xn_ref[...], wgu_ref[...])
    g = gu[:, 0:tf]
    mid = g * _sigmoid(g) * gu[:, tf:2 * tf]
    out_ref[...] += _dot(mid.astype(BF16), wd_ref[...])


def _even_ffn(h, u, o, wo, g, wgu, wd, tf):
    n, d = h.shape
    ff = wd.shape[0]
    tm = ROW_TILE
    row = lambda i, f: (i, 0)
    const = lambda i, f: (0, 0)
    return pl.pallas_call(
        functools.partial(_even_ffn_kernel, tf=tf),
        grid=(n // tm, ff // tf),
        in_specs=[pl.BlockSpec((tm, d), row), pl.BlockSpec((tm, CONV_CH), row),
                  pl.BlockSpec((tm, SB_WIDTH), row), pl.BlockSpec(wo.shape, const),
                  pl.BlockSpec((1, d), const),
                  pl.BlockSpec((d, 2 * tf), lambda i, f: (0, f)),
                  pl.BlockSpec((tf, d), lambda i, f: (f, 0))],
        out_specs=pl.BlockSpec((tm, d), row),
        out_shape=jax.ShapeDtypeStruct((n, d), F32),
        scratch_shapes=[pltpu.VMEM((tm, d), BF16)],
        compiler_params=_params("parallel", "arbitrary"),
        name="even_out_swiglu",
    )(h, u, o, wo, g, wgu, wd)


def _row_scatter_kernel(win_ref, x_ref, info_ref, off_ref, init_hbm, o_hbm, cbuf, sems):
    del init_hbm
    i = pl.program_id(0)
    last = pl.num_programs(0) - 1
    n_slots = cbuf.shape[1]
    buf = i % 2

    def drained(b):
        return pltpu.make_async_copy(cbuf.at[b], o_hbm.at[pl.ds(0, n_slots)], sems.at[b])

    info = info_ref[...]
    sub = lax.broadcasted_iota(jnp.int32, info.shape, 0)
    valid = info[ROUTE_VALID:ROUTE_VALID + 1, :] > 0.5

    def local_slot(e_row, r_row):
        expert = info[e_row:e_row + 1, :].astype(jnp.int32)
        off = jnp.sum(jnp.where(sub == expert, off_ref[:, 0:1], 0.0), axis=0, keepdims=True)
        return jnp.where(valid, off + info[r_row:r_row + 1, :], -1.0)

    slot = lax.broadcasted_iota(jnp.int32, (n_slots, info.shape[1]), 0).astype(F32)
    pick = (jnp.where(slot == local_slot(ROUTE_E1, ROUTE_R1), 1.0, 0.0)
            + jnp.where(slot == local_slot(ROUTE_E2, ROUTE_R2), 1.0, 0.0)).astype(BF16)
    rows = _dot(pick, x_ref[...]).astype(cbuf.dtype)

    @pl.when(i >= 2)
    def _():
        drained(buf).wait()

    cbuf[buf] = rows
    for s in range(ROW_NWIN):
        dst = pl.multiple_of(win_ref[i * ROW_NWIN + s], ROW_WIN)
        pltpu.make_async_copy(cbuf.at[buf, pl.ds(s * ROW_WIN, ROW_WIN)], o_hbm.at[pl.ds(dst, ROW_WIN)],
                              sems.at[buf]).start(priority=s % 2)

    @pl.when(i == last)
    def _():
        drained(buf).wait()

        @pl.when(i >= 1)
        def _():
            drained(1 - buf).wait()


def _row_scatter(win_rows, x, info, local_off, n_rows_alloc):
    n, d = x.shape
    tm = COMBINE_TILE
    init = jnp.zeros((n_rows_alloc, d), x.dtype)
    return pl.pallas_call(
        _row_scatter_kernel,
        grid_spec=pltpu.PrefetchScalarGridSpec(
            num_scalar_prefetch=1, grid=(n // tm,),
            in_specs=[pl.BlockSpec((tm, d), lambda i, w: (i, 0)), pl.BlockSpec((SUBLANES, tm), lambda i, w: (i, 0)),
                      pl.BlockSpec((SUBLANES, LANES), lambda i, w: (i, 0)), pl.BlockSpec(memory_space=pl.ANY)],
            out_specs=pl.BlockSpec(memory_space=pl.ANY),
            scratch_shapes=[pltpu.VMEM((2, ROW_NWIN * ROW_WIN, d), x.dtype), pltpu.SemaphoreType.DMA((2,))]),
        out_shape=jax.ShapeDtypeStruct(init.shape, init.dtype),
        input_output_aliases={4: 0},
        compiler_params=_params("arbitrary"),
        name="moe_row_scatter",
    )(win_rows, x, info, local_off, init)


def _gmm_kernel(te_ref, nu_ref, xs_ref, wg_ref, wu_ref, wd_ref, o_ref, acc_ref):
    p = pl.program_id(0)
    f = pl.program_id(1)
    last = pl.num_programs(1) - 1
    used = p < nu_ref[0]

    @pl.when(used)
    def _():
        x = xs_ref[...]
        g = _dot(x, wg_ref[0])
        mid = g * _sigmoid(g) * _dot(x, wu_ref[0])
        part = _dot(mid.astype(BF16), wd_ref[0])

        @pl.when(f == 0)
        def _():
            acc_ref[...] = part

        @pl.when(f > 0)
        def _():
            acc_ref[...] += part

        @pl.when(f == last)
        def _():
            o_ref[...] = acc_ref[...].astype(o_ref.dtype)

    @pl.when(jnp.logical_not(used) & (f == last))
    def _():
        o_ref[...] = jnp.zeros(o_ref.shape, o_ref.dtype)


def _grouped_ffn(tile_expert, n_used, xs, n_rows, wg, wu, wd):
    d = xs.shape[1]
    ff = wd.shape[1]
    tf = MOE_FF_TILE
    nf = ff // tf
    tg = MOE_TILE
    row = lambda p, f, te, nu: (p, 0)
    fidx = lambda p, f, nu: jnp.where(p < nu[0], f, nf - 1)
    up = pl.BlockSpec((1, d, tf), lambda p, f, te, nu: (te[p], 0, fidx(p, f, nu)))
    return pl.pallas_call(
        _gmm_kernel,
        grid_spec=pltpu.PrefetchScalarGridSpec(
            num_scalar_prefetch=2, grid=(n_rows // tg, nf),
            in_specs=[pl.BlockSpec((tg, d), row), up, up,
                      pl.BlockSpec((1, tf, d), lambda p, f, te, nu: (te[p], fidx(p, f, nu), 0))],
            out_specs=pl.BlockSpec((tg, d), row),
            scratch_shapes=[pltpu.VMEM((tg, d), F32)]),
        out_shape=jax.ShapeDtypeStruct((n_rows, d), BF16),
        compiler_params=_params("arbitrary", "arbitrary"),
        name="moe_grouped_ffn",
    )(tile_expert, n_used, xs, wg, wu, wd)


def _combine_kernel(win_ref, h_ref, tok_ref, w1_ref, w2_ref, starts_ref, rows_ref, ys_hbm, o_ref, gbuf, sems):
    i = pl.program_id(0)
    tm = h_ref.shape[0]
    win = ROW_WIN

    def fetch(tile, slot):
        for s in range(ROW_NWIN):
            start = pl.multiple_of(win_ref[tile * ROW_NWIN + s], win)
            pltpu.make_async_copy(ys_hbm.at[pl.ds(start, win)], gbuf.at[slot, pl.ds(s * win, win)],
                                  sems.at[slot]).start(priority=s % 2)

    @pl.when(i == 0)
    def _():
        fetch(0, 0)

    @pl.when(i + 1 < pl.num_programs(0))
    def _():
        fetch(i + 1, (i + 1) % 2)

    slot = i % 2
    pltpu.make_async_copy(ys_hbm.at[pl.ds(0, ROW_NWIN * win)], gbuf.at[slot], sems.at[slot]).wait()

    tok = tok_ref[...]
    lane = lax.broadcasted_iota(jnp.int32, tok.shape, 1)
    field = lambda c: jnp.sum(jnp.where(lane == c, tok, 0.0), axis=1, keepdims=True)
    valid = field(ROUTE_VALID) > 0.5
    rows = rows_ref[0]
    reps = h_ref.shape[1] // LANES
    out = h_ref[...]
    for e_col, r_col, w_ref in ((ROUTE_E1, ROUTE_R1, w1_ref), (ROUTE_E2, ROUTE_R2, w2_ref)):
        start = jnp.sum(jnp.where(lane == field(e_col).astype(jnp.int32), starts_ref[...], 0.0),
                        axis=1, keepdims=True)
        dest = jnp.where(valid, start + field(r_col), -2.0)
        pick = jnp.where(dest == rows, 1.0, 0.0).astype(BF16)
        out = out + jnp.concatenate([w_ref[...]] * reps, axis=1) * _dot(pick, gbuf[slot])
    if len(o_ref.shape) == 2:
        o_ref[...] = out
    else:
        bsz = o_ref.shape[0]
        for t in range(o_ref.shape[1]):
            o_ref[:, t, :] = out[t * bsz:(t + 1) * bsz, :]


def _combine(win_start, h, tok, w1b, w2b, starts_row, slot_rows, ys, batch_major_bsz=None):
    n, d = h.shape
    tm = COMBINE_TILE
    n_slots = ROW_NWIN * ROW_WIN
    row = lambda i, w: (i, 0)
    if batch_major_bsz is None:
        out_spec = pl.BlockSpec((tm, d), row)
        out_shape = jax.ShapeDtypeStruct(h.shape, F32)
    else:
        bsz = batch_major_bsz
        assert tm % bsz == 0 and bsz % SUBLANES == 0
        out_spec = pl.BlockSpec((bsz, tm // bsz, d), lambda i, w: (0, i, 0))
        out_shape = jax.ShapeDtypeStruct((bsz, n // bsz, d), F32)
    return pl.pallas_call(
        _combine_kernel,
        grid_spec=pltpu.PrefetchScalarGridSpec(
            num_scalar_prefetch=1, grid=(n // tm,),
            in_specs=[pl.BlockSpec((tm, d), row), pl.BlockSpec((tm, LANES), row), pl.BlockSpec((tm, LANES), row),
                      pl.BlockSpec((tm, LANES), row), pl.BlockSpec((1, LANES), lambda i, w: (0, 0)),
                      pl.BlockSpec((1, 1, n_slots), lambda i, w: (i, 0, 0)),
                      pl.BlockSpec(memory_space=pl.ANY)],
            out_specs=out_spec,
            scratch_shapes=[pltpu.VMEM((2, n_slots, d), ys.dtype), pltpu.SemaphoreType.DMA((2,))]),
        out_shape=out_shape,
        compiler_params=_params("arbitrary"),
        name="moe_combine",
    )(win_start, h, tok, w1b, w2b, starts_row, slot_rows, ys)


def _gate_windows(lw, bd):
    out = []
    for c0 in range(0, lw, LRU_GATE_TILE):
        nc = min(LRU_GATE_TILE, lw - c0)
        r0 = (c0 // bd) * bd
        r1 = ((c0 + nc - 1) // bd + 1) * bd
        r0 = (r0 // LANES) * LANES
        r1 = min(-(-r1 // LANES) * LANES, lw)
        out.append((c0, nc, r0, r1 - r0))
    return out


def _lru_kernel(h_ref, ng_ref, win_ref, cw_ref, cb_ref, wg_ref, br_ref, bi_ref, lam_ref, wo_ref,
                o_ref, xbuf_ref, hs_ref, a_ref, b_ref, y_ref, gate_ref, res_ref, *, windows, bsz):
    ti = pl.program_id(0)
    rows, lw = a_ref.shape
    halo = (LRU_CONV_WIDTH - 1) * bsz

    @pl.when(ti == 0)
    def _():
        xbuf_ref[0:halo, :] = jnp.zeros((halo, lw), F32)
        hs_ref[...] = jnp.zeros(hs_ref.shape, F32)

    @pl.when(ti > 0)
    def _():
        xbuf_ref[0:halo, :] = xbuf_ref[rows:rows + halo, :]

    res_ref[...] = jnp.concatenate([h_ref[:, t, :] for t in range(rows // bsz)], axis=0)
    proj = _dot(_rms(res_ref[...], ng_ref[...]).astype(BF16), win_ref[...])
    gate_ref[...] = jax.nn.gelu(proj[:, 0:lw], approximate=True)
    xbuf_ref[halo:halo + rows, :] = proj[:, lw:2 * lw]
    xc = jnp.broadcast_to(cb_ref[...], (rows, lw))
    for k in range(LRU_CONV_WIDTH):
        xc = xc + cw_ref[k:k + 1, :] * xbuf_ref[k * bsz:k * bsz + rows, :]
    xcb = xc.astype(BF16)
    sp_lam = jnp.log(1.0 + jnp.exp(-lam_ref[...]))
    log2_a_per_r = (-LRU_C * LOG2E) * sp_lam
    wcol = 0
    for (c0, nc, r0, nr) in windows:
        rg = _dot(xcb[:, r0:r0 + nr], wg_ref[r0:r0 + nr, wcol:wcol + 2 * nc])
        wcol += 2 * nc
        r = _sigmoid(rg[:, 0:nc] + br_ref[:, c0:c0 + nc])
        ig = _sigmoid(rg[:, nc:2 * nc] + bi_ref[:, c0:c0 + nc])
        a = jnp.exp2(r * log2_a_per_r[:, c0:c0 + nc])
        a_ref[:, c0:c0 + nc] = a
        one_m_a2 = 1.0 - a * a
        root = jnp.where(one_m_a2 > 0.0, one_m_a2 * lax.rsqrt(one_m_a2), 0.0)
        b_ref[:, c0:c0 + nc] = root * (ig * xc[:, c0:c0 + nc])

    def step(t, hprev):
        r0 = pl.multiple_of(t * bsz, SUBLANES)
        hnew = a_ref[pl.ds(r0, bsz), :] * hprev + b_ref[pl.ds(r0, bsz), :]
        y_ref[pl.ds(r0, bsz), :] = hnew
        return hnew

    hs_ref[...] = lax.fori_loop(0, rows // bsz, step, hs_ref[...], unroll=LRU_SCAN_UNROLL)
    gy = (gate_ref[...] * y_ref[...]).astype(BF16)
    o_ref[...] = res_ref[...] + _dot(gy, wo_ref[...])


def _lru_mixer(h, ng, win, cw, cb, wg, br, bi, lam, wo, windows):
    bsz, tp, d = h.shape
    n = bsz * tp
    lw = lam.shape[-1]
    assert bsz % SUBLANES == 0 and LRU_ROWS % bsz == 0 and n % LRU_ROWS == 0
    rows = LRU_ROWS
    halo = (LRU_CONV_WIDTH - 1) * bsz
    const = lambda t: (0, 0)
    blk = lambda t: (t, 0)
    resident = lambda w: pl.BlockSpec(w.shape, const, pipeline_mode=pl.Buffered(1))
    return pl.pallas_call(
        functools.partial(_lru_kernel, windows=windows, bsz=bsz),
        grid=(n // rows,),
        in_specs=[pl.BlockSpec((bsz, rows // bsz, d), lambda t: (0, t, 0)), pl.BlockSpec((1, d), const), resident(win),
                  pl.BlockSpec(cw.shape, const), pl.BlockSpec((1, lw), const), resident(wg),
                  pl.BlockSpec((1, lw), const), pl.BlockSpec((1, lw), const),
                  pl.BlockSpec((1, lw), const), resident(wo)],
        out_specs=pl.BlockSpec((rows, d), blk),
        out_shape=jax.ShapeDtypeStruct((n, d), F32),
        scratch_shapes=[pltpu.VMEM((halo + rows, lw), F32), pltpu.VMEM((bsz, lw), F32),
                        pltpu.VMEM((rows, lw), F32), pltpu.VMEM((rows, lw), F32), pltpu.VMEM((rows, lw), F32),
                        pltpu.VMEM((rows, lw), F32), pltpu.VMEM((rows, d), F32)],
        compiler_params=_params("arbitrary"),
        name="rglru_mixer",
    )(h, ng, win, cw, cb, wg, br, bi, lam, wo)


ROUTE_E1, ROUTE_E2, ROUTE_R1, ROUTE_R2, ROUTE_VALID = range(5)


def _router_kernel(h_ref, g_ref, rw_ref, ltri_ref, xn_ref, info_ref, tok_ref, w1_ref, w2_ref, before_ref, cnt_ref,
                   run_ref, *, n_real_rows):
    i = pl.program_id(0)
    tm = h_ref.shape[0]

    @pl.when(i == 0)
    def _():
        run_ref[...] = jnp.zeros(run_ref.shape, F32)

    xn = _rms(h_ref[...], g_ref[...])
    xn_ref[...] = xn.astype(BF16)
    split = lambda a: (a.astype(BF16), (a - a.astype(BF16).astype(F32)).astype(BF16))
    (x_hi, x_lo), (r_hi, r_lo) = split(xn), split(rw_ref[...])
    logits = _dot(x_hi, r_hi) + (_dot(x_hi, r_lo) + _dot(x_lo, r_hi))
    lane = lax.broadcasted_iota(jnp.int32, logits.shape, 1)
    neg = jnp.float32(-jnp.inf)
    logits = jnp.where(lane < N_EXPERTS, logits, neg)
    top1 = jnp.max(logits, axis=1, keepdims=True)
    idx1 = jnp.min(jnp.where(logits == top1, lane, LANES), axis=1, keepdims=True)
    rest = jnp.where(lane == idx1, neg, logits)
    top2 = jnp.max(rest, axis=1, keepdims=True)
    idx2 = jnp.min(jnp.where(rest == top2, lane, LANES), axis=1, keepdims=True)
    w1 = 1.0 / (1.0 + jnp.exp(top2 - top1))

    rowg = lax.broadcasted_iota(jnp.int32, (tm, 1), 0) + i * tm
    valid = jnp.where(rowg < n_real_rows, 1.0, 0.0)
    onehot = (jnp.where(lane == idx1, 1.0, 0.0) + jnp.where(lane == idx2, 1.0, 0.0)) * valid
    before_ref[...] = jnp.broadcast_to(run_ref[...], before_ref.shape)
    before = _dot(ltri_ref[...], onehot.astype(BF16)) + run_ref[...]
    rank1 = jnp.sum(jnp.where(lane == idx1, before, 0.0), axis=1, keepdims=True)
    rank2 = jnp.sum(jnp.where(lane == idx2, before, 0.0), axis=1, keepdims=True)
    run_ref[...] += jnp.ceil(jnp.sum(onehot, axis=0, keepdims=True) / ROW_WIN) * ROW_WIN
    cnt_ref[...] = run_ref[...]
    w1_ref[...] = jnp.broadcast_to(w1, (tm, LANES))
    w2_ref[...] = jnp.broadcast_to(1.0 - w1, (tm, LANES))
    fields = {ROUTE_E1: idx1.astype(F32), ROUTE_E2: idx2.astype(F32),
              ROUTE_R1: rank1, ROUTE_R2: rank2, ROUTE_VALID: valid}
    info = jnp.zeros(logits.shape, F32)
    for col, val in fields.items():
        info = jnp.where(lane == col, val, info)
    tok_ref[...] = info
    info_ref[...] = info.T[0:SUBLANES, :]


def _router(h, g, rw, ltri, n_real_rows):
    n, d = h.shape
    tm = ROW_TILE
    row = lambda i: (i, 0)
    const = lambda i: (0, 0)
    return pl.pallas_call(
        functools.partial(_router_kernel, n_real_rows=n_real_rows),
        grid=(n // tm,),
        in_specs=[pl.BlockSpec((tm, d), row), pl.BlockSpec((1, d), const), pl.BlockSpec(rw.shape, const),
                  pl.BlockSpec(ltri.shape, const)],
        out_specs=[pl.BlockSpec((tm, d), row), pl.BlockSpec((SUBLANES, tm), row),
                   pl.BlockSpec((tm, LANES), row), pl.BlockSpec((tm, LANES), row),
                   pl.BlockSpec((tm, LANES), row), pl.BlockSpec((SUBLANES, LANES), row),
                   pl.BlockSpec((1, LANES), const)],
        out_shape=[jax.ShapeDtypeStruct((n, d), BF16), jax.ShapeDtypeStruct((n // tm * SUBLANES, tm), F32),
                   jax.ShapeDtypeStruct((n, LANES), F32), jax.ShapeDtypeStruct((n, LANES), F32),
                   jax.ShapeDtypeStruct((n, LANES), F32), jax.ShapeDtypeStruct((n // tm * SUBLANES, LANES), F32),
                   jax.ShapeDtypeStruct((1, LANES), F32)],
        scratch_shapes=[pltpu.VMEM((1, LANES), F32)],
        compiler_params=_params("arbitrary"),
        name="router",
    )(h, g, rw, ltri)


def _routing_tables(before, counts, n_rows):
    tg = MOE_TILE
    win = ROW_WIN
    cnt = counts[0, :N_EXPERTS].astype(jnp.int32)
    padded = (cnt + tg - 1) // tg * tg
    ends = jnp.cumsum(padded)
    starts = ends - padded
    tile_start = jnp.arange(n_rows // tg, dtype=jnp.int32) * tg
    tile_expert = jnp.minimum(jnp.sum(tile_start[:, None] >= ends[None, :], axis=1), N_EXPERTS - 1)
    n_used = (ends[-1] // tg).reshape(1)
    bef = before.reshape(-1, SUBLANES, LANES)[:, 0, :N_EXPERTS].astype(jnp.int32)
    nwin = (jnp.concatenate([bef[1:], cnt[None, :]], axis=0) - bef) // win
    base = jnp.cumsum(nwin, axis=1) - nwin
    slot = jnp.arange(ROW_NWIN, dtype=jnp.int32)[None, :, None]
    mine = (slot >= base[:, None, :]) & (slot < (base + nwin)[:, None, :])
    rows = jnp.sum(jnp.where(mine, (starts[None, :] + bef)[:, None, :] + win * (slot - base[:, None, :]), 0), axis=2)
    used = jnp.any(mine, axis=2)
    parity = (jnp.arange(bef.shape[0], dtype=jnp.int32) % 2)[:, None]
    scatter_rows = jnp.where(used, rows, n_rows + win * (parity * ROW_NWIN + slot[:, :, 0]))
    gather_rows = jnp.where(used, rows, 0)
    slot_rows = jnp.where(used[..., None], rows[..., None] + jnp.arange(win, dtype=jnp.int32), -1)
    slot_rows = slot_rows.reshape(-1, 1, ROW_NWIN * win).astype(F32)
    starts_row = jnp.zeros((1, LANES), F32).at[0, :N_EXPERTS].set(starts.astype(F32))
    local_off = jnp.zeros(bef.shape[:1] + (SUBLANES,), jnp.int32).at[:, :N_EXPERTS].set(win * base - bef)
    local_off = jnp.broadcast_to(local_off.astype(F32)[:, :, None], local_off.shape + (LANES,)).reshape(-1, LANES)
    return (tile_expert.astype(jnp.int32), n_used.astype(jnp.int32), scatter_rows.reshape(-1).astype(jnp.int32),
            gather_rows.reshape(-1).astype(jnp.int32), slot_rows, starts_row, local_off)


def _interleave_gate_up(wg, wu, tf):
    e, d, ff = wg.shape
    g = wg.astype(BF16).reshape(e, d, ff // tf, tf)
    u = wu.astype(BF16).reshape(e, d, ff // tf, tf)
    return jnp.concatenate([g, u], axis=3).reshape(e, d, 2 * ff)


def _dense_block_diag(w):
    nb, bd, _ = w.shape
    eye = jnp.eye(nb, dtype=w.dtype)
    return (w[:, :, None, :] * eye[:, None, :, None]).reshape(nb * bd, nb * bd)


def _pack_lru_gates(wr, wi, windows):
    dr = _dense_block_diag(wr).astype(BF16)
    di = _dense_block_diag(wi).astype(BF16)
    cols = []
    for (c0, nc, _, _) in windows:
        cols += [dr[:, c0:c0 + nc], di[:, c0:c0 + nc]]
    return jnp.concatenate(cols, axis=1)


def _ffn_chunk(ff):
    for tf in (896, 512, 384, 256, 128):
        if ff % tf == 0:
            return tf
    raise ValueError(f"unsupported d_ff {ff}")


def kernel(x, meta_tokens, mix_norm_even, w_in_even, conv_w, conv_b, conv_ln_g, conv_ln_b, q_norm_g, k_norm_g, w_out_even, ffn_norm_even, ffn_w_gate, ffn_w_up, ffn_w_down, mix_norm_odd, w_in_odd, lru_conv_w, lru_conv_b, gate_r_w, gate_r_b, gate_i_w, gate_i_b, lru_lambda, w_out_odd, ffn_norm_odd, router_w, moe_w_gate, moe_w_up, moe_w_down):
    bsz, seq, d = x.shape
    t_real = N_META + seq
    tp = -(-t_real // TIME_TILE) * TIME_TILE
    n = bsz * tp
    assert n % ROW_TILE == 0
    depth = mix_norm_even.shape[0] + mix_norm_odd.shape[0]

    meta = jnp.broadcast_to(meta_tokens[None].astype(x.dtype), (bsz, N_META, d))
    h = jnp.concatenate([meta, x, jnp.zeros((bsz, tp - t_real, d), x.dtype)], axis=1).reshape(n, d)

    head_mean = jnp.kron(jnp.eye(MXU_DIM // SB_HEAD_DIM, dtype=F32),
                         jnp.full((SB_HEAD_DIM, SB_HEAD_DIM), 1.0 / SB_HEAD_DIM, F32)).astype(BF16)
    kk = jnp.arange(ATT_TK)
    neg_tri = -(kk[:, None] >= kk[None, :]).astype(BF16)
    row2 = lambda a: a.reshape(1, -1)

    time_major = False
    for layer in range(depth):
        p = layer // 2
        if layer % 2 == 0:
            if time_major:
                h = h.reshape(tp, bsz, d).transpose(1, 0, 2).reshape(n, d)
                time_major = False
            u, q, k, v = _even_in_proj(h, row2(mix_norm_even[p]), w_in_even[p].astype(BF16),
                                       row2(jnp.tile(q_norm_g[p], SB_HEADS)),
                                       row2(jnp.tile(k_norm_g[p], SB_HEADS)), head_mean)
            u = _conv_module(u.reshape(bsz, tp, CONV_CH), conv_w[p], row2(conv_b[p]),
                             row2(conv_ln_g[p]), row2(conv_ln_b[p]))
            o = _attention(q.reshape(bsz, tp, SB_WIDTH), k.reshape(bsz, tp, SB_WIDTH),
                           v.reshape(bsz, tp, SB_WIDTH), neg_tri, t_real)
            tf = _ffn_chunk(ffn_w_gate.shape[-1])
            h = _even_ffn(h, u.reshape(n, CONV_CH), o.reshape(n, SB_WIDTH), w_out_even[p].astype(BF16),
                          row2(ffn_norm_even[p]),
                          _interleave_gate_up(ffn_w_gate[p][None], ffn_w_up[p][None], tf)[0],
                          ffn_w_down[p].astype(BF16), tf)
        else:
            if time_major:
                h = h.reshape(tp, bsz, d).transpose(1, 0, 2).reshape(n, d)
            time_major = True
            lw = lru_lambda.shape[-1]
            windows = _gate_windows(lw, lw // LRU_BLOCKS)
            h = _lru_mixer(h.reshape(bsz, tp, d), row2(mix_norm_odd[p]), w_in_odd[p].astype(BF16),
                           lru_conv_w[p], row2(lru_conv_b[p]),
                           _pack_lru_gates(gate_r_w[p], gate_i_w[p], windows),
                           row2(gate_r_b[p]), row2(gate_i_b[p]), row2(lru_lambda[p]),
                           w_out_odd[p].astype(BF16), windows)
            rw = jnp.pad(router_w[p], ((0, 0), (0, LANES - N_EXPERTS)))
            kk = jnp.arange(ROW_TILE)
            ltri = (kk[:, None] > kk[None, :]).astype(BF16)
            xn, info, tok, w1b, w2b, before, counts = _router(h, row2(ffn_norm_odd[p]), rw, ltri, t_real * bsz)
            tg = MOE_TILE
            assert ROW_TILE == COMBINE_TILE
            n_tiles = n // COMBINE_TILE
            n_rows = 2 * bsz * t_real + n_tiles * N_EXPERTS * (ROW_WIN - 1) + N_EXPERTS * (tg - 1)
            n_rows = -(-n_rows // tg) * tg
            tile_expert, n_used, scatter_rows, gather_rows, slot_rows, starts_row, local_off = _routing_tables(
                before, counts, n_rows)
            xs = _row_scatter(scatter_rows, xn, info, local_off, n_rows + 2 * ROW_NWIN * ROW_WIN)
            ys = _grouped_ffn(tile_expert, n_used, xs, n_rows, moe_w_gate[p].astype(BF16),
                              moe_w_up[p].astype(BF16), moe_w_down[p].astype(BF16))
            last = layer + 1 == depth
            h = _combine(gather_rows, h, tok, w1b, w2b, starts_row, slot_rows, ys, bsz if last else None)
            if last:
                h = h.reshape(n, d)
                time_major = False
    if time_major:
        return h.reshape(tp, bsz, d)[N_META:t_real].transpose(1, 0, 2)
    return h.reshape(bsz, tp, d)[:, N_META:t_real]
```

```python
import functools
import math

import jax
import jax.numpy as jnp
from jax import lax
from jax.experimental import pallas as pl
from jax.experimental.pallas import tpu as pltpu

F32 = jnp.float32
BF16 = jnp.bfloat16

EPS = 1e-6
N_META = 16
CONV_CH = 512
CONV_WIDTH = 31
SB_HEADS = 8
SB_HEAD_DIM = 64
SB_WIDTH = SB_HEADS * SB_HEAD_DIM
LRU_BLOCKS = 16
LRU_CONV_WIDTH = 4
LRU_C = 8.0
N_EXPERTS = 8

LANES = 128
MXU_DIM = 256
TIME_TILE = 256
VMEM_LIMIT = 50 * 1024 * 1024

ROW_TILE = 512
CONV_TT = 256
CONV_HALO = 32
CONV_CHUNK = 32
ATT_TQ = 256
ATT_TK = 256
LOG2E = 1.4426950408889634
MASKED_EXPONENT = -1e30
SOFTPLUS_CUTOFF = 126.0
LRU_ROWS = 256
LRU_SCAN_UNROLL = 4
LRU_GATE_TILE = 256
MOE_TILE = 512
MOE_FF_TILE = 1792
COMBINE_TILE = 512
ROW_WIN = 16
ROW_NWIN = 2 * COMBINE_TILE // ROW_WIN + N_EXPERTS
SUBLANES = 8
BF16_ROWS = 16


def _dot(a, b):
    return jnp.dot(a, b, preferred_element_type=F32)


def _params(*sem):
    return pltpu.CompilerParams(dimension_semantics=sem, vmem_limit_bytes=VMEM_LIMIT)


def _sigmoid(x):
    return 0.5 * jnp.tanh(0.5 * x) + 0.5


def _rms(x, g):
    ms = jnp.mean(x * x, axis=-1, keepdims=True)
    return x * lax.rsqrt(ms + EPS) * g


def _even_in_kernel(h_ref, g_ref, w_ref, qg_ref, kg_ref, hm_ref, u_ref, q_ref, k_ref, v_ref):
    xn = _rms(h_ref[...], g_ref[...]).astype(BF16)
    a = _dot(xn, w_ref[:, 0:CONV_CH])
    gate = _dot(xn, w_ref[:, CONV_CH:2 * CONV_CH])
    u_ref[...] = (a * _sigmoid(gate)).astype(BF16)

    def head_norm(y, gain):
        yy = (y * y).astype(BF16)
        parts = [_dot(yy[:, c:c + MXU_DIM], hm_ref[...]) for c in range(0, SB_WIDTH, MXU_DIM)]
        ms = jnp.concatenate(parts, axis=1)
        return y * lax.rsqrt(ms + EPS) * gain

    c0 = 2 * CONV_CH
    q = head_norm(_dot(xn, w_ref[:, c0:c0 + SB_WIDTH]), qg_ref[...])
    q_ref[...] = (q * (LOG2E / math.sqrt(SB_HEAD_DIM))).astype(BF16)
    k = head_norm(_dot(xn, w_ref[:, c0 + SB_WIDTH:c0 + 2 * SB_WIDTH]), kg_ref[...])
    k_ref[...] = k.astype(BF16)
    v_ref[...] = _dot(xn, w_ref[:, c0 + 2 * SB_WIDTH:c0 + 3 * SB_WIDTH]).astype(BF16)


def _even_in_proj(h, g, w, qg, kg, hm):
    n, d = h.shape
    tm = ROW_TILE
    row = lambda i: (i, 0)
    const = lambda i: (0, 0)
    out = jax.ShapeDtypeStruct((n, SB_WIDTH), BF16)
    return pl.pallas_call(
        _even_in_kernel,
        grid=(n // tm,),
        in_specs=[pl.BlockSpec((tm, d), row), pl.BlockSpec((1, d), const),
                  pl.BlockSpec(w.shape, const), pl.BlockSpec((1, SB_WIDTH), const),
                  pl.BlockSpec((1, SB_WIDTH), const), pl.BlockSpec(hm.shape, const)],
        out_specs=[pl.BlockSpec((tm, SB_WIDTH), row)] * 4,
        out_shape=[out] * 4,
        compiler_params=_params("parallel"),
        name="even_in_proj",
    )(h, g, w, qg, kg, hm)


def _conv_kernel(cur_ref, halo_ref, w_ref, b_ref, lg_ref, lb_ref, o_ref, buf_ref, sh_ref):
    i = pl.program_id(1)
    tt = cur_ref.shape[1]
    rows = CONV_HALO + tt
    halo = halo_ref[0].astype(F32)
    buf_ref[0:CONV_HALO, :] = jnp.where(i > 0, halo, 0.0)
    buf_ref[CONV_HALO:rows, :] = cur_ref[0].astype(F32)
    for s in range(1, SUBLANES):
        sh_ref[s - 1, 0:rows - SUBLANES, :] = buf_ref[s:s + rows - SUBLANES, :]
    off = CONV_HALO - (CONV_WIDTH - 1)
    for c in range(tt // CONV_CHUNK):
        r0 = c * CONV_CHUNK
        acc = jnp.broadcast_to(b_ref[...], (CONV_CHUNK, CONV_CH))
        for k in range(CONV_WIDTH):
            shift = (off + k) % SUBLANES
            base = r0 + off + k - shift
            if shift == 0:
                tap = buf_ref[base:base + CONV_CHUNK, :]
            else:
                tap = sh_ref[shift - 1, base:base + CONV_CHUNK, :]
            acc = acc + w_ref[k:k + 1, :] * tap
        mu = jnp.mean(acc, axis=-1, keepdims=True)
        xc = acc - mu
        var = jnp.mean(xc * xc, axis=-1, keepdims=True)
        y = xc * lax.rsqrt(var + EPS) * lg_ref[...] + lb_ref[...]
        o_ref[0, r0:r0 + CONV_CHUNK, :] = (y * _sigmoid(y)).astype(BF16)


def _conv_module(u, w, b, lg, lb):
    bsz, tp, c = u.shape
    tt = CONV_TT
    per = tt // CONV_HALO
    const = lambda bi, i: (0, 0)
    return pl.pallas_call(
        _conv_kernel,
        grid=(bsz, tp // tt),
        in_specs=[pl.BlockSpec((1, tt, c), lambda bi, i: (bi, i, 0)),
                  pl.BlockSpec((1, CONV_HALO, c), lambda bi, i: (bi, jnp.maximum(i * per - 1, 0), 0)),
                  pl.BlockSpec(w.shape, const), pl.BlockSpec((1, c), const),
                  pl.BlockSpec((1, c), const), pl.BlockSpec((1, c), const)],
        out_specs=pl.BlockSpec((1, tt, c), lambda bi, i: (bi, i, 0)),
        out_shape=jax.ShapeDtypeStruct(u.shape, BF16),
        scratch_shapes=[pltpu.VMEM((CONV_HALO + tt, c), F32),
                        pltpu.VMEM((SUBLANES - 1, CONV_HALO + tt - SUBLANES, c), F32)],
        compiler_params=_params("parallel", "parallel"),
        name="conv_module",
    )(u, u, w, b, lg, lb)


def _attn_kernel(*refs, tq, tile0, aliased, n_compute):
    if aliased:
        refs = refs[1:]
    o_ref = refs[4]
    step = pl.program_id(1)

    @pl.when(step < n_compute)
    def _():
        _attn_tile(*refs, tq=tq, row0=(step + tile0) * ATT_TQ)

    @pl.when(step >= n_compute)
    def _():
        o_ref[...] = jnp.zeros(o_ref.shape, BF16)


def _attn_tile(q_ref, k_ref, v_ref, tri_ref, o_ref, q2_ref, e_ref, tot_ref, acc_ref, car_ref, *, tq, row0):
    tk = ATT_TK
    npairs = q_ref.shape[2] // LANES
    lane_q = lax.broadcasted_iota(jnp.int32, (tq, LANES), 1)

    for hp in range(npairs):
        q = q_ref[0, 0:tq, hp * LANES:(hp + 1) * LANES]
        zero = jnp.zeros_like(q)
        q2_ref[hp, 0:tq, :] = jnp.where(lane_q < SB_HEAD_DIM, q, zero)
        q2_ref[hp, tq:2 * tq, :] = jnp.where(lane_q >= SB_HEAD_DIM, q, zero)
        car_ref[hp] = jnp.zeros((2 * tq, LANES), F32)
        acc_ref[hp] = jnp.zeros((2 * tq, LANES), F32)

    def stage_a(j, mask):
        k0 = pl.multiple_of(j * tk, tk)
        for hp in range(npairs):
            kt = k_ref[0, pl.ds(k0, tk), hp * LANES:(hp + 1) * LANES]
            z = lax.dot_general(q2_ref[hp], kt, (((1,), (1,)), ((), ())), preferred_element_type=F32)
            sp = jnp.where(z > SOFTPLUS_CUTOFF, z, jnp.log(1.0 + jnp.exp2(z)) * LOG2E)
            if mask is not None:
                sp = jnp.where(mask, sp, 0.0)
            e = z + _dot(sp.astype(BF16), tri_ref[...])
            if mask is not None:
                e = jnp.where(mask, e, MASKED_EXPONENT)
            e_ref[hp] = e
            tot_ref[hp] = jnp.broadcast_to(jnp.sum(sp, axis=1, keepdims=True), (2 * tq, LANES))

    def stage_b(j):
        k0 = pl.multiple_of(j * tk, tk)
        for hp in range(npairs):
            car = car_ref[hp]
            w = jnp.exp2(e_ref[hp] + jnp.concatenate([car] * (tk // LANES), axis=1)).astype(BF16)
            vt = v_ref[0, pl.ds(k0, tk), hp * LANES:(hp + 1) * LANES]
            acc_ref[hp] += _dot(w, vt)
            car_ref[hp] = car - tot_ref[hp]

    j_last = row0 // tk
    row = lax.broadcasted_iota(jnp.int32, (2 * tq, tk), 0)
    col = lax.broadcasted_iota(jnp.int32, (2 * tq, tk), 1)
    row = jnp.where(row >= tq, row - tq, row) + row0
    stage_a(j_last, (col + j_last * tk) < row)

    def body(jj, carry):
        j = j_last - 1 - jj
        stage_b(j + 1)
        stage_a(j, None)
        return carry

    lax.fori_loop(0, j_last, body, 0)
    stage_b(0)
    if tq < o_ref.shape[1]:
        o_ref[0, tq:, :] = jnp.zeros((o_ref.shape[1] - tq, o_ref.shape[2]), BF16)
    for hp in range(npairs):
        o_ref[0, 0:tq, hp * LANES:(hp + 1) * LANES] = jnp.where(
            lane_q < SB_HEAD_DIM, acc_ref[hp, 0:tq, :], acc_ref[hp, tq:2 * tq, :]).astype(BF16)


def _attention(q, k, v, tri, t_real):
    bsz, tp, width = q.shape
    bq, tk = ATT_TQ, ATT_TK
    assert bq <= tk and tk % bq == 0
    npairs = width // LANES
    n_full = t_real // bq
    rem = -(-(t_real - n_full * bq) // BF16_ROWS) * BF16_ROWS

    def call(tq, tile0, n_compute, n_tiles, prev):
        blk = lambda b, i: (b, i + tile0, 0)
        full = lambda b, i: (b, 0, 0)
        in_specs = [pl.BlockSpec((1, bq, width), blk), pl.BlockSpec((1, tp, width), full),
                    pl.BlockSpec((1, tp, width), full), pl.BlockSpec(tri.shape, lambda b, i: (0, 0))]
        args = [q, k, v, tri]
        if prev is not None:
            in_specs.insert(0, pl.BlockSpec(memory_space=pl.ANY))
            args.insert(0, prev)
        return pl.pallas_call(
            functools.partial(_attn_kernel, tq=tq, tile0=tile0, aliased=prev is not None, n_compute=n_compute),
            grid=(bsz, n_tiles),
            in_specs=in_specs,
            out_specs=pl.BlockSpec((1, bq, width), blk),
            out_shape=jax.ShapeDtypeStruct(q.shape, BF16),
            input_output_aliases={} if prev is None else {0: 0},
            scratch_shapes=[pltpu.VMEM((npairs, 2 * tq, LANES), BF16),
                            pltpu.VMEM((npairs, 2 * tq, tk), F32),
                            pltpu.VMEM((npairs, 2 * tq, LANES), F32),
                            pltpu.VMEM((npairs, 2 * tq, LANES), F32),
                            pltpu.VMEM((npairs, 2 * tq, LANES), F32)],
            compiler_params=_params("parallel", "arbitrary"),
            name="stick_breaking" if prev is None else "stick_breaking_tail",
        )(*args)

    out = call(bq, 0, n_full, tp // bq, None)
    if rem:
        out = call(rem, n_full, 1, 1, out)
    return out


def _even_ffn_kernel(h_ref, u_ref, o_ref, wo_ref, g_ref, wgu_ref, wd_ref, out_ref, xn_ref, *, tf):
    @pl.when(pl.program_id(1) == 0)
    def _():
        h1 = h_ref[...] + _dot(u_ref[...], wo_ref[0:CONV_CH, :]) + _dot(o_ref[...], wo_ref[CONV_CH:, :])
        out_ref[...] = h1
        xn_ref[...] = _rms(h1, g_ref[...]).astype(BF16)

    gu = _dot(xn_ref[...], wgu_ref[...])
    g = gu[:, 0:tf]
    mid = g * _sigmoid(g) * gu[:, tf:2 * tf]
    out_ref[...] += _dot(mid.astype(BF16), wd_ref[...])


def _even_ffn(h, u, o, wo, g, wgu, wd, tf):
    n, d = h.shape
    ff = wd.shape[0]
    tm = ROW_TILE
    row = lambda i, f: (i, 0)
    const = lambda i, f: (0, 0)
    return pl.pallas_call(
        functools.partial(_even_ffn_kernel, tf=tf),
        grid=(n // tm, ff // tf),
        in_specs=[pl.BlockSpec((tm, d), row), pl.BlockSpec((tm, CONV_CH), row),
                  pl.BlockSpec((tm, SB_WIDTH), row), pl.BlockSpec(wo.shape, const),
                  pl.BlockSpec((1, d), const),
                  pl.BlockSpec((d, 2 * tf), lambda i, f: (0, f)),
                  pl.BlockSpec((tf, d), lambda i, f: (f, 0))],
        out_specs=pl.BlockSpec((tm, d), row),
        out_shape=jax.ShapeDtypeStruct((n, d), F32),
        scratch_shapes=[pltpu.VMEM((tm, d), BF16)],
        compiler_params=_params("parallel", "arbitrary"),
        name="even_out_swiglu",
    )(h, u, o, wo, g, wgu, wd)


def _row_scatter_kernel(win_ref, x_ref, info_ref, off_ref, init_hbm, o_hbm, cbuf, sems):
    del init_hbm
    i = pl.program_id(0)
    last = pl.num_programs(0) - 1
    n_slots = cbuf.shape[1]
    buf = i % 2

    def drained(b):
        return pltpu.make_async_copy(cbuf.at[b], o_hbm.at[pl.ds(0, n_slots)], sems.at[b])

    info = info_ref[...]
    sub = lax.broadcasted_iota(jnp.int32, info.shape, 0)
    valid = info[ROUTE_VALID:ROUTE_VALID + 1, :] > 0.5

    def local_slot(e_row, r_row):
        expert = info[e_row:e_row + 1, :].astype(jnp.int32)
        off = jnp.sum(jnp.where(sub == expert, off_ref[:, 0:1], 0.0), axis=0, keepdims=True)
        return jnp.where(valid, off + info[r_row:r_row + 1, :], -1.0)

    slot = lax.broadcasted_iota(jnp.int32, (n_slots, info.shape[1]), 0).astype(F32)
    pick = (jnp.where(slot == local_slot(ROUTE_E1, ROUTE_R1), 1.0, 0.0)
            + jnp.where(slot == local_slot(ROUTE_E2, ROUTE_R2), 1.0, 0.0)).astype(BF16)
    rows = _dot(pick, x_ref[...]).astype(cbuf.dtype)

    @pl.when(i >= 2)
    def _():
        drained(buf).wait()

    cbuf[buf] = rows
    for s in range(ROW_NWIN):
        dst = pl.multiple_of(win_ref[i * ROW_NWIN + s], ROW_WIN)
        pltpu.make_async_copy(cbuf.at[buf, pl.ds(s * ROW_WIN, ROW_WIN)], o_hbm.at[pl.ds(dst, ROW_WIN)],
                              sems.at[buf]).start(priority=s % 2)

    @pl.when(i == last)
    def _():
        drained(buf).wait()

        @pl.when(i >= 1)
        def _():
            drained(1 - buf).wait()


def _row_scatter(win_rows, x, info, local_off, n_rows_alloc):
    n, d = x.shape
    tm = COMBINE_TILE
    init = jnp.zeros((n_rows_alloc, d), x.dtype)
    return pl.pallas_call(
        _row_scatter_kernel,
        grid_spec=pltpu.PrefetchScalarGridSpec(
            num_scalar_prefetch=1, grid=(n // tm,),
            in_specs=[pl.BlockSpec((tm, d), lambda i, w: (i, 0)), pl.BlockSpec((SUBLANES, tm), lambda i, w: (i, 0)),
                      pl.BlockSpec((SUBLANES, LANES), lambda i, w: (i, 0)), pl.BlockSpec(memory_space=pl.ANY)],
            out_specs=pl.BlockSpec(memory_space=pl.ANY),
            scratch_shapes=[pltpu.VMEM((2, ROW_NWIN * ROW_WIN, d), x.dtype), pltpu.SemaphoreType.DMA((2,))]),
        out_shape=jax.ShapeDtypeStruct(init.shape, init.dtype),
        input_output_aliases={4: 0},
        compiler_params=_params("arbitrary"),
        name="moe_row_scatter",
    )(win_rows, x, info, local_off, init)


def _gmm_kernel(te_ref, nu_ref, xs_ref, wg_ref, wu_ref, wd_ref, o_ref, acc_ref):
    p = pl.program_id(0)
    f = pl.program_id(1)
    last = pl.num_programs(1) - 1
    used = p < nu_ref[0]

    @pl.when(used)
    def _():
        x = xs_ref[...]
        g = _dot(x, wg_ref[0])
        mid = g * _sigmoid(g) * _dot(x, wu_ref[0])
        part = _dot(mid.astype(BF16), wd_ref[0])

        @pl.when(f == 0)
        def _():
            acc_ref[...] = part

        @pl.when(f > 0)
        def _():
            acc_ref[...] += part

        @pl.when(f == last)
        def _():
            o_ref[...] = acc_ref[...].astype(o_ref.dtype)

    @pl.when(jnp.logical_not(used) & (f == last))
    def _():
        o_ref[...] = jnp.zeros(o_ref.shape, o_ref.dtype)


def _grouped_ffn(tile_expert, n_used, xs, n_rows, wg, wu, wd):
    d = xs.shape[1]
    ff = wd.shape[1]
    tf = MOE_FF_TILE
    nf = ff // tf
    tg = MOE_TILE
    row = lambda p, f, te, nu: (p, 0)
    fidx = lambda p, f, nu: jnp.where(p < nu[0], f, nf - 1)
    up = pl.BlockSpec((1, d, tf), lambda p, f, te, nu: (te[p], 0, fidx(p, f, nu)))
    return pl.pallas_call(
        _gmm_kernel,
        grid_spec=pltpu.PrefetchScalarGridSpec(
            num_scalar_prefetch=2, grid=(n_rows // tg, nf),
            in_specs=[pl.BlockSpec((tg, d), row), up, up,
                      pl.BlockSpec((1, tf, d), lambda p, f, te, nu: (te[p], fidx(p, f, nu), 0))],
            out_specs=pl.BlockSpec((tg, d), row),
            scratch_shapes=[pltpu.VMEM((tg, d), F32)]),
        out_shape=jax.ShapeDtypeStruct((n_rows, d), BF16),
        compiler_params=_params("arbitrary", "arbitrary"),
        name="moe_grouped_ffn",
    )(tile_expert, n_used, xs, wg, wu, wd)


def _combine_kernel(win_ref, h_ref, tok_ref, w1_ref, w2_ref, starts_ref, rows_ref, ys_hbm, o_ref, gbuf, sems,
                    *scratch, final_rows=None):
    i = pl.program_id(0)
    tm = h_ref.shape[0]
    win = ROW_WIN

    def fetch(tile, slot):
        for s in range(ROW_NWIN):
            start = pl.multiple_of(win_ref[tile * ROW_NWIN + s], win)
            pltpu.make_async_copy(ys_hbm.at[pl.ds(start, win)], gbuf.at[slot, pl.ds(s * win, win)],
                                  sems.at[slot]).start(priority=s % 2)

    @pl.when(i == 0)
    def _():
        fetch(0, 0)

    @pl.when(i + 1 < pl.num_programs(0))
    def _():
        fetch(i + 1, (i + 1) % 2)

    slot = i % 2
    pltpu.make_async_copy(ys_hbm.at[pl.ds(0, ROW_NWIN * win)], gbuf.at[slot], sems.at[slot]).wait()

    tok = tok_ref[...]
    lane = lax.broadcasted_iota(jnp.int32, tok.shape, 1)
    field = lambda c: jnp.sum(jnp.where(lane == c, tok, 0.0), axis=1, keepdims=True)
    valid = field(ROUTE_VALID) > 0.5
    rows = rows_ref[0]
    reps = h_ref.shape[1] // LANES
    out = h_ref[...]
    for e_col, r_col, w_ref in ((ROUTE_E1, ROUTE_R1, w1_ref), (ROUTE_E2, ROUTE_R2, w2_ref)):
        start = jnp.sum(jnp.where(lane == field(e_col).astype(jnp.int32), starts_ref[...], 0.0),
                        axis=1, keepdims=True)
        dest = jnp.where(valid, start + field(r_col), -2.0)
        pick = jnp.where(dest == rows, 1.0, 0.0).astype(BF16)
        out = out + jnp.concatenate([w_ref[...]] * reps, axis=1) * _dot(pick, gbuf[slot])
    if final_rows is None:
        o_ref[...] = out
        return

    obuf, osems = scratch
    n_meta, t_real = final_rows
    bsz, ts = obuf.shape[1], obuf.shape[2]
    steps = h_ref.shape[0] // (bsz * ts)
    assert steps == 1 and n_meta % SUBLANES == 0 and n_meta < ts
    i_last = t_real // ts
    rem = t_real - i_last * ts
    n_steps = pl.num_programs(0)
    oslot = i % 2

    def full_copy(step, s):
        dst = pl.multiple_of(step * ts - n_meta, SUBLANES)
        return pltpu.make_async_copy(obuf.at[s], o_ref.at[:, pl.ds(dst, ts), :], osems.at[s])

    is_full = lambda step: (step >= 1) & (step < i_last)

    @pl.when(is_full(i - 2))
    def _():
        full_copy(i - 2, oslot).wait()

    for t in range(ts):
        obuf[oslot, :, t, :] = out[t * bsz:(t + 1) * bsz, :]

    @pl.when(i == 0)
    def _():
        c = pltpu.make_async_copy(obuf.at[oslot, :, pl.ds(n_meta, ts - n_meta), :],
                                  o_ref.at[:, pl.ds(0, ts - n_meta), :], osems.at[oslot])
        c.start()
        c.wait()

    @pl.when(is_full(i))
    def _():
        full_copy(i, oslot).start()

    if rem:
        @pl.when(i == i_last)
        def _():
            c = pltpu.make_async_copy(obuf.at[oslot, :, pl.ds(0, rem), :],
                                      o_ref.at[:, pl.ds(i_last * ts - n_meta, rem), :], osems.at[oslot])
            c.start()
            c.wait()

    @pl.when(i == n_steps - 1)
    def _():
        @pl.when(is_full(i - 1))
        def _():
            full_copy(i - 1, 1 - oslot).wait()

        @pl.when(is_full(i))
        def _():
            full_copy(i, oslot).wait()


def _combine(win_start, h, tok, w1b, w2b, starts_row, slot_rows, ys, final=None):
    n, d = h.shape
    tm = COMBINE_TILE
    n_slots = ROW_NWIN * ROW_WIN
    row = lambda i, w: (i, 0)
    scratch = [pltpu.VMEM((2, n_slots, d), ys.dtype), pltpu.SemaphoreType.DMA((2,))]
    if final is None:
        kern = _combine_kernel
        out_spec = pl.BlockSpec((tm, d), row)
        out_shape = jax.ShapeDtypeStruct(h.shape, F32)
    else:
        bsz, n_meta, t_real = final
        assert tm % bsz == 0 and bsz % SUBLANES == 0
        kern = functools.partial(_combine_kernel, final_rows=(n_meta, t_real))
        out_spec = pl.BlockSpec(memory_space=pl.ANY)
        out_shape = jax.ShapeDtypeStruct((bsz, t_real - n_meta, d), F32)
        scratch += [pltpu.VMEM((2, bsz, tm // bsz, d), F32), pltpu.SemaphoreType.DMA((2,))]
    return pl.pallas_call(
        kern,
        grid_spec=pltpu.PrefetchScalarGridSpec(
            num_scalar_prefetch=1, grid=(n // tm,),
            in_specs=[pl.BlockSpec((tm, d), row), pl.BlockSpec((tm, LANES), row), pl.BlockSpec((tm, LANES), row),
                      pl.BlockSpec((tm, LANES), row), pl.BlockSpec((1, LANES), lambda i, w: (0, 0)),
                      pl.BlockSpec((1, 1, n_slots), lambda i, w: (i, 0, 0)),
                      pl.BlockSpec(memory_space=pl.ANY)],
            out_specs=out_spec,
            scratch_shapes=scratch),
        out_shape=out_shape,
        compiler_params=_params("arbitrary"),
        name="moe_combine",
    )(win_start, h, tok, w1b, w2b, starts_row, slot_rows, ys)


def _gate_windows(lw, bd):
    out = []
    for c0 in range(0, lw, LRU_GATE_TILE):
        nc = min(LRU_GATE_TILE, lw - c0)
        r0 = (c0 // bd) * bd
        r1 = ((c0 + nc - 1) // bd + 1) * bd
        r0 = (r0 // LANES) * LANES
        r1 = min(-(-r1 // LANES) * LANES, lw)
        out.append((c0, nc, r0, r1 - r0))
    return out


def _lru_kernel(h_ref, ng_ref, win_ref, cw_ref, cb_ref, wg_ref, br_ref, bi_ref, lam_ref, wo_ref,
                o_ref, xbuf_ref, hs_ref, a_ref, b_ref, y_ref, gate_ref, res_ref, *, windows, bsz):
    ti = pl.program_id(0)
    rows, lw = a_ref.shape
    halo = (LRU_CONV_WIDTH - 1) * bsz

    @pl.when(ti == 0)
    def _():
        xbuf_ref[0:halo, :] = jnp.zeros((halo, lw), F32)
        hs_ref[...] = jnp.zeros(hs_ref.shape, F32)

    @pl.when(ti > 0)
    def _():
        xbuf_ref[0:halo, :] = xbuf_ref[rows:rows + halo, :]

    res_ref[...] = jnp.concatenate([h_ref[:, t, :] for t in range(rows // bsz)], axis=0)
    proj = _dot(_rms(res_ref[...], ng_ref[...]).astype(BF16), win_ref[...])
    gate_ref[...] = jax.nn.gelu(proj[:, 0:lw], approximate=True)
    xbuf_ref[halo:halo + rows, :] = proj[:, lw:2 * lw]
    xc = jnp.broadcast_to(cb_ref[...], (rows, lw))
    for k in range(LRU_CONV_WIDTH):
        xc = xc + cw_ref[k:k + 1, :] * xbuf_ref[k * bsz:k * bsz + rows, :]
    xcb = xc.astype(BF16)
    sp_lam = jnp.log(1.0 + jnp.exp(-lam_ref[...]))
    log2_a_per_r = (-LRU_C * LOG2E) * sp_lam
    wcol = 0
    for (c0, nc, r0, nr) in windows:
        rg = _dot(xcb[:, r0:r0 + nr], wg_ref[r0:r0 + nr, wcol:wcol + 2 * nc])
        wcol += 2 * nc
        r = _sigmoid(rg[:, 0:nc] + br_ref[:, c0:c0 + nc])
        ig = _sigmoid(rg[:, nc:2 * nc] + bi_ref[:, c0:c0 + nc])
        a = jnp.exp2(r * log2_a_per_r[:, c0:c0 + nc])
        a_ref[:, c0:c0 + nc] = a
        one_m_a2 = 1.0 - a * a
        root = jnp.where(one_m_a2 > 0.0, one_m_a2 * lax.rsqrt(one_m_a2), 0.0)
        b_ref[:, c0:c0 + nc] = root * (ig * xc[:, c0:c0 + nc])

    def step(t, hprev):
        r0 = pl.multiple_of(t * bsz, SUBLANES)
        hnew = a_ref[pl.ds(r0, bsz), :] * hprev + b_ref[pl.ds(r0, bsz), :]
        y_ref[pl.ds(r0, bsz), :] = hnew
        return hnew

    hs_ref[...] = lax.fori_loop(0, rows // bsz, step, hs_ref[...], unroll=LRU_SCAN_UNROLL)
    gy = (gate_ref[...] * y_ref[...]).astype(BF16)
    o_ref[...] = res_ref[...] + _dot(gy, wo_ref[...])


def _lru_mixer(h, ng, win, cw, cb, wg, br, bi, lam, wo, windows):
    bsz, tp, d = h.shape
    n = bsz * tp
    lw = lam.shape[-1]
    assert bsz % SUBLANES == 0 and LRU_ROWS % bsz == 0 and n % LRU_ROWS == 0
    rows = LRU_ROWS
    halo = (LRU_CONV_WIDTH - 1) * bsz
    const = lambda t: (0, 0)
    blk = lambda t: (t, 0)
    resident = lambda w: pl.BlockSpec(w.shape, const, pipeline_mode=pl.Buffered(1))
    return pl.pallas_call(
        functools.partial(_lru_kernel, windows=windows, bsz=bsz),
        grid=(n // rows,),
        in_specs=[pl.BlockSpec((bsz, rows // bsz, d), lambda t: (0, t, 0)), pl.BlockSpec((1, d), const), resident(win),
                  pl.BlockSpec(cw.shape, const), pl.BlockSpec((1, lw), const), resident(wg),
                  pl.BlockSpec((1, lw), const), pl.BlockSpec((1, lw), const),
                  pl.BlockSpec((1, lw), const), resident(wo)],
        out_specs=pl.BlockSpec((rows, d), blk),
        out_shape=jax.ShapeDtypeStruct((n, d), F32),
        scratch_shapes=[pltpu.VMEM((halo + rows, lw), F32), pltpu.VMEM((bsz, lw), F32),
                        pltpu.VMEM((rows, lw), F32), pltpu.VMEM((rows, lw), F32), pltpu.VMEM((rows, lw), F32),
                        pltpu.VMEM((rows, lw), F32), pltpu.VMEM((rows, d), F32)],
        compiler_params=_params("arbitrary"),
        name="rglru_mixer",
    )(h, ng, win, cw, cb, wg, br, bi, lam, wo)


ROUTE_E1, ROUTE_E2, ROUTE_R1, ROUTE_R2, ROUTE_VALID = range(5)


def _router_kernel(h_ref, g_ref, rw_ref, ltri_ref, xn_ref, info_ref, tok_ref, w1_ref, w2_ref, before_ref, cnt_ref,
                   run_ref, *, n_real_rows):
    i = pl.program_id(0)
    tm = h_ref.shape[0]

    @pl.when(i == 0)
    def _():
        run_ref[...] = jnp.zeros(run_ref.shape, F32)

    xn = _rms(h_ref[...], g_ref[...])
    xn_ref[...] = xn.astype(BF16)
    split = lambda a: (a.astype(BF16), (a - a.astype(BF16).astype(F32)).astype(BF16))
    (x_hi, x_lo), (r_hi, r_lo) = split(xn), split(rw_ref[...])
    logits = _dot(x_hi, r_hi) + (_dot(x_hi, r_lo) + _dot(x_lo, r_hi))
    lane = lax.broadcasted_iota(jnp.int32, logits.shape, 1)
    neg = jnp.float32(-jnp.inf)
    logits = jnp.where(lane < N_EXPERTS, logits, neg)
    top1 = jnp.max(logits, axis=1, keepdims=True)
    idx1 = jnp.min(jnp.where(logits == top1, lane, LANES), axis=1, keepdims=True)
    rest = jnp.where(lane == idx1, neg, logits)
    top2 = jnp.max(rest, axis=1, keepdims=True)
    idx2 = jnp.min(jnp.where(rest == top2, lane, LANES), axis=1, keepdims=True)
    w1 = 1.0 / (1.0 + jnp.exp(top2 - top1))

    rowg = lax.broadcasted_iota(jnp.int32, (tm, 1), 0) + i * tm
    valid = jnp.where(rowg < n_real_rows, 1.0, 0.0)
    onehot = (jnp.where(lane == idx1, 1.0, 0.0) + jnp.where(lane == idx2, 1.0, 0.0)) * valid
    before_ref[...] = jnp.broadcast_to(run_ref[...], before_ref.shape)
    before = _dot(ltri_ref[...], onehot.astype(BF16)) + run_ref[...]
    rank1 = jnp.sum(jnp.where(lane == idx1, before, 0.0), axis=1, keepdims=True)
    rank2 = jnp.sum(jnp.where(lane == idx2, before, 0.0), axis=1, keepdims=True)
    run_ref[...] += jnp.ceil(jnp.sum(onehot, axis=0, keepdims=True) / ROW_WIN) * ROW_WIN
    cnt_ref[...] = run_ref[...]
    w1_ref[...] = jnp.broadcast_to(w1, (tm, LANES))
    w2_ref[...] = jnp.broadcast_to(1.0 - w1, (tm, LANES))
    fields = {ROUTE_E1: idx1.astype(F32), ROUTE_E2: idx2.astype(F32),
              ROUTE_R1: rank1, ROUTE_R2: rank2, ROUTE_VALID: valid}
    info = jnp.zeros(logits.shape, F32)
    for col, val in fields.items():
        info = jnp.where(lane == col, val, info)
    tok_ref[...] = info
    info_ref[...] = info.T[0:SUBLANES, :]


def _router(h, g, rw, ltri, n_real_rows):
    n, d = h.shape
    tm = ROW_TILE
    row = lambda i: (i, 0)
    const = lambda i: (0, 0)
    return pl.pallas_call(
        functools.partial(_router_kernel, n_real_rows=n_real_rows),
        grid=(n // tm,),
        in_specs=[pl.BlockSpec((tm, d), row), pl.BlockSpec((1, d), const), pl.BlockSpec(rw.shape, const),
                  pl.BlockSpec(ltri.shape, const)],
        out_specs=[pl.BlockSpec((tm, d), row), pl.BlockSpec((SUBLANES, tm), row),
                   pl.BlockSpec((tm, LANES), row), pl.BlockSpec((tm, LANES), row),
                   pl.BlockSpec((tm, LANES), row), pl.BlockSpec((SUBLANES, LANES), row),
                   pl.BlockSpec((1, LANES), const)],
        out_shape=[jax.ShapeDtypeStruct((n, d), BF16), jax.ShapeDtypeStruct((n // tm * SUBLANES, tm), F32),
                   jax.ShapeDtypeStruct((n, LANES), F32), jax.ShapeDtypeStruct((n, LANES), F32),
                   jax.ShapeDtypeStruct((n, LANES), F32), jax.ShapeDtypeStruct((n // tm * SUBLANES, LANES), F32),
                   jax.ShapeDtypeStruct((1, LANES), F32)],
        scratch_shapes=[pltpu.VMEM((1, LANES), F32)],
        compiler_params=_params("arbitrary"),
        name="router",
    )(h, g, rw, ltri)


def _routing_tables(before, counts, n_rows):
    tg = MOE_TILE
    win = ROW_WIN
    cnt = counts[0, :N_EXPERTS].astype(jnp.int32)
    padded = (cnt + tg - 1) // tg * tg
    ends = jnp.cumsum(padded)
    starts = ends - padded
    tile_start = jnp.arange(n_rows // tg, dtype=jnp.int32) * tg
    tile_expert = jnp.minimum(jnp.sum(tile_start[:, None] >= ends[None, :], axis=1), N_EXPERTS - 1)
    n_used = (ends[-1] // tg).reshape(1)
    bef = before.reshape(-1, SUBLANES, LANES)[:, 0, :N_EXPERTS].astype(jnp.int32)
    nwin = (jnp.concatenate([bef[1:], cnt[None, :]], axis=0) - bef) // win
    base = jnp.cumsum(nwin, axis=1) - nwin
    slot = jnp.arange(ROW_NWIN, dtype=jnp.int32)[None, :, None]
    mine = (slot >= base[:, None, :]) & (slot < (base + nwin)[:, None, :])
    rows = jnp.sum(jnp.where(mine, (starts[None, :] + bef)[:, None, :] + win * (slot - base[:, None, :]), 0), axis=2)
    used = jnp.any(mine, axis=2)
    parity = (jnp.arange(bef.shape[0], dtype=jnp.int32) % 2)[:, None]
    scatter_rows = jnp.where(used, rows, n_rows + win * (parity * ROW_NWIN + slot[:, :, 0]))
    gather_rows = jnp.where(used, rows, 0)
    slot_rows = jnp.where(used[..., None], rows[..., None] + jnp.arange(win, dtype=jnp.int32), -1)
    slot_rows = slot_rows.reshape(-1, 1, ROW_NWIN * win).astype(F32)
    starts_row = jnp.zeros((1, LANES), F32).at[0, :N_EXPERTS].set(starts.astype(F32))
    local_off = jnp.zeros(bef.shape[:1] + (SUBLANES,), jnp.int32).at[:, :N_EXPERTS].set(win * base - bef)
    local_off = jnp.broadcast_to(local_off.astype(F32)[:, :, None], local_off.shape + (LANES,)).reshape(-1, LANES)
    return (tile_expert.astype(jnp.int32), n_used.astype(jnp.int32), scatter_rows.reshape(-1).astype(jnp.int32),
            gather_rows.reshape(-1).astype(jnp.int32), slot_rows, starts_row, local_off)


def _interleave_gate_up(wg, wu, tf):
    e, d, ff = wg.shape
    g = wg.astype(BF16).reshape(e, d, ff // tf, tf)
    u = wu.astype(BF16).reshape(e, d, ff // tf, tf)
    return jnp.concatenate([g, u], axis=3).reshape(e, d, 2 * ff)


def _dense_block_diag(w):
    nb, bd, _ = w.shape
    eye = jnp.eye(nb, dtype=w.dtype)
    return (w[:, :, None, :] * eye[:, None, :, None]).reshape(nb * bd, nb * bd)


def _pack_lru_gates(wr, wi, windows):
    dr = _dense_block_diag(wr).astype(BF16)
    di = _dense_block_diag(wi).astype(BF16)
    cols = []
    for (c0, nc, _, _) in windows:
        cols += [dr[:, c0:c0 + nc], di[:, c0:c0 + nc]]
    return jnp.concatenate(cols, axis=1)


def _ffn_chunk(ff):
    for tf in (896, 512, 384, 256, 128):
        if ff % tf == 0:
            return tf
    raise ValueError(f"unsupported d_ff {ff}")


def kernel(x, meta_tokens, mix_norm_even, w_in_even, conv_w, conv_b, conv_ln_g, conv_ln_b, q_norm_g, k_norm_g, w_out_even, ffn_norm_even, ffn_w_gate, ffn_w_up, ffn_w_down, mix_norm_odd, w_in_odd, lru_conv_w, lru_conv_b, gate_r_w, gate_r_b, gate_i_w, gate_i_b, lru_lambda, w_out_odd, ffn_norm_odd, router_w, moe_w_gate, moe_w_up, moe_w_down):
    bsz, seq, d = x.shape
    t_real = N_META + seq
    tp = -(-t_real // TIME_TILE) * TIME_TILE
    n = bsz * tp
    assert n % ROW_TILE == 0
    depth = mix_norm_even.shape[0] + mix_norm_odd.shape[0]

    meta = jnp.broadcast_to(meta_tokens[None].astype(x.dtype), (bsz, N_META, d))
    h = jnp.concatenate([meta, x, jnp.zeros((bsz, tp - t_real, d), x.dtype)], axis=1).reshape(n, d)

    head_mean = jnp.kron(jnp.eye(MXU_DIM // SB_HEAD_DIM, dtype=F32),
                         jnp.full((SB_HEAD_DIM, SB_HEAD_DIM), 1.0 / SB_HEAD_DIM, F32)).astype(BF16)
    kk = jnp.arange(ATT_TK)
    neg_tri = -(kk[:, None] >= kk[None, :]).astype(BF16)
    row2 = lambda a: a.reshape(1, -1)

    time_major = False
    for layer in range(depth):
        p = layer // 2
        if layer % 2 == 0:
            if time_major:
                h = h.reshape(tp, bsz, d).transpose(1, 0, 2).reshape(n, d)
                time_major = False
            u, q, k, v = _even_in_proj(h, row2(mix_norm_even[p]), w_in_even[p].astype(BF16),
                                       row2(jnp.tile(q_norm_g[p], SB_HEADS)),
                                       row2(jnp.tile(k_norm_g[p], SB_HEADS)), head_mean)
            u = _conv_module(u.reshape(bsz, tp, CONV_CH), conv_w[p], row2(conv_b[p]),
                             row2(conv_ln_g[p]), row2(conv_ln_b[p]))
            o = _attention(q.reshape(bsz, tp, SB_WIDTH), k.reshape(bsz, tp, SB_WIDTH),
                           v.reshape(bsz, tp, SB_WIDTH), neg_tri, t_real)
            tf = _ffn_chunk(ffn_w_gate.shape[-1])
            h = _even_ffn(h, u.reshape(n, CONV_CH), o.reshape(n, SB_WIDTH), w_out_even[p].astype(BF16),
                          row2(ffn_norm_even[p]),
                          _interleave_gate_up(ffn_w_gate[p][None], ffn_w_up[p][None], tf)[0],
                          ffn_w_down[p].astype(BF16), tf)
        else:
            if time_major:
                h = h.reshape(tp, bsz, d).transpose(1, 0, 2).reshape(n, d)
            time_major = True
            lw = lru_lambda.shape[-1]
            windows = _gate_windows(lw, lw // LRU_BLOCKS)
            h = _lru_mixer(h.reshape(bsz, tp, d), row2(mix_norm_odd[p]), w_in_odd[p].astype(BF16),
                           lru_conv_w[p], row2(lru_conv_b[p]),
                           _pack_lru_gates(gate_r_w[p], gate_i_w[p], windows),
                           row2(gate_r_b[p]), row2(gate_i_b[p]), row2(lru_lambda[p]),
                           w_out_odd[p].astype(BF16), windows)
            rw = jnp.pad(router_w[p], ((0, 0), (0, LANES - N_EXPERTS)))
            kk = jnp.arange(ROW_TILE)
            ltri = (kk[:, None] > kk[None, :]).astype(BF16)
            xn, info, tok, w1b, w2b, before, counts = _router(h, row2(ffn_norm_odd[p]), rw, ltri, t_real * bsz)
            tg = MOE_TILE
            assert ROW_TILE == COMBINE_TILE
            n_tiles = n // COMBINE_TILE
            n_rows = 2 * bsz * t_real + n_tiles * N_EXPERTS * (ROW_WIN - 1) + N_EXPERTS * (tg - 1)
            n_rows = -(-n_rows // tg) * tg
            tile_expert, n_used, scatter_rows, gather_rows, slot_rows, starts_row, local_off = _routing_tables(
                before, counts, n_rows)
            xs = _row_scatter(scatter_rows, xn, info, local_off, n_rows + 2 * ROW_NWIN * ROW_WIN)
            ys = _grouped_ffn(tile_expert, n_used, xs, n_rows, moe_w_gate[p].astype(BF16),
                              moe_w_up[p].astype(BF16), moe_w_down[p].astype(BF16))
            if layer + 1 == depth:
                return _combine(gather_rows, h, tok, w1b, w2b, starts_row, slot_rows, ys, (bsz, N_META, t_real))
            h = _combine(gather_rows, h, tok, w1b, w2b, starts_row, slot_rows, ys)
    if time_major:
        return h.reshape(tp, bsz, d)[N_META:t_real].transpose(1, 0, 2)
    return h.reshape(bsz, tp, d)[:, N_META:t_real]
```

```python
import functools
import math

import jax
import jax.numpy as jnp
from jax import lax
from jax.experimental import pallas as pl
from jax.experimental.pallas import tpu as pltpu

F32 = jnp.float32
BF16 = jnp.bfloat16

EPS = 1e-6
N_META = 16
CONV_CH = 512
CONV_WIDTH = 31
SB_HEADS = 8
SB_HEAD_DIM = 64
SB_WIDTH = SB_HEADS * SB_HEAD_DIM
LRU_BLOCKS = 16
LRU_CONV_WIDTH = 4
LRU_C = 8.0
N_EXPERTS = 8

LANES = 128
MXU_DIM = 256
TIME_TILE = 256
VMEM_LIMIT = 50 * 1024 * 1024

ROW_TILE = 512
CONV_TT = 256
CONV_HALO = 32
CONV_CHUNK = 32
ATT_TQ = 256
ATT_TK = 256
LOG2E = 1.4426950408889634
MASKED_EXPONENT = -1e30
SOFTPLUS_CUTOFF = 126.0
LRU_ROWS = 256
LRU_SCAN_UNROLL = 4
LRU_GATE_TILE = 256
MOE_TILE = 512
MOE_FF_TILE = 1792
COMBINE_TILE = 512
ROW_WIN = 16
ROW_NWIN = 2 * COMBINE_TILE // ROW_WIN + N_EXPERTS
SUBLANES = 8
BF16_ROWS = 16


def _dot(a, b):
    return jnp.dot(a, b, preferred_element_type=F32)


def _params(*sem):
    return pltpu.CompilerParams(dimension_semantics=sem, vmem_limit_bytes=VMEM_LIMIT)


def _sigmoid(x):
    return 0.5 * jnp.tanh(0.5 * x) + 0.5


def _rms(x, g):
    ms = jnp.mean(x * x, axis=-1, keepdims=True)
    return x * lax.rsqrt(ms + EPS) * g


def _even_in_kernel(h_ref, g_ref, w_ref, qg_ref, kg_ref, hm_ref, u_ref, q_ref, k_ref, v_ref):
    xn = _rms(h_ref[...], g_ref[...]).astype(BF16)
    a = _dot(xn, w_ref[:, 0:CONV_CH])
    gate = _dot(xn, w_ref[:, CONV_CH:2 * CONV_CH])
    u_ref[...] = (a * _sigmoid(gate)).astype(BF16)

    def head_norm(y, gain):
        yy = (y * y).astype(BF16)
        parts = [_dot(yy[:, c:c + MXU_DIM], hm_ref[...]) for c in range(0, SB_WIDTH, MXU_DIM)]
        ms = jnp.concatenate(parts, axis=1)
        return y * lax.rsqrt(ms + EPS) * gain

    c0 = 2 * CONV_CH
    q = head_norm(_dot(xn, w_ref[:, c0:c0 + SB_WIDTH]), qg_ref[...])
    q_ref[...] = (q * (LOG2E / math.sqrt(SB_HEAD_DIM))).astype(BF16)
    k = head_norm(_dot(xn, w_ref[:, c0 + SB_WIDTH:c0 + 2 * SB_WIDTH]), kg_ref[...])
    k_ref[...] = k.astype(BF16)
    v_ref[...] = _dot(xn, w_ref[:, c0 + 2 * SB_WIDTH:c0 + 3 * SB_WIDTH]).astype(BF16)


def _even_in_proj(h, g, w, qg, kg, hm):
    n, d = h.shape
    tm = ROW_TILE
    row = lambda i: (i, 0)
    const = lambda i: (0, 0)
    out = jax.ShapeDtypeStruct((n, SB_WIDTH), BF16)
    return pl.pallas_call(
        _even_in_kernel,
        grid=(n // tm,),
        in_specs=[pl.BlockSpec((tm, d), row), pl.BlockSpec((1, d), const),
                  pl.BlockSpec(w.shape, const), pl.BlockSpec((1, SB_WIDTH), const),
                  pl.BlockSpec((1, SB_WIDTH), const), pl.BlockSpec(hm.shape, const)],
        out_specs=[pl.BlockSpec((tm, SB_WIDTH), row)] * 4,
        out_shape=[out] * 4,
        compiler_params=_params("parallel"),
        name="even_in_proj",
    )(h, g, w, qg, kg, hm)


def _conv_kernel(cur_ref, halo_ref, w_ref, b_ref, lg_ref, lb_ref, o_ref, buf_ref, sh_ref):
    i = pl.program_id(1)
    tt = cur_ref.shape[1]
    rows = CONV_HALO + tt
    halo = halo_ref[0].astype(F32)
    buf_ref[0:CONV_HALO, :] = jnp.where(i > 0, halo, 0.0)
    buf_ref[CONV_HALO:rows, :] = cur_ref[0].astype(F32)
    for s in range(1, SUBLANES):
        sh_ref[s - 1, 0:rows - SUBLANES, :] = buf_ref[s:s + rows - SUBLANES, :]
    off = CONV_HALO - (CONV_WIDTH - 1)
    for c in range(tt // CONV_CHUNK):
        r0 = c * CONV_CHUNK
        acc = jnp.broadcast_to(b_ref[...], (CONV_CHUNK, CONV_CH))
        for k in range(CONV_WIDTH):
            shift = (off + k) % SUBLANES
            base = r0 + off + k - shift
            if shift == 0:
                tap = buf_ref[base:base + CONV_CHUNK, :]
            else:
                tap = sh_ref[shift - 1, base:base + CONV_CHUNK, :]
            acc = acc + w_ref[k:k + 1, :] * tap
        mu = jnp.mean(acc, axis=-1, keepdims=True)
        xc = acc - mu
        var = jnp.mean(xc * xc, axis=-1, keepdims=True)
        y = xc * lax.rsqrt(var + EPS) * lg_ref[...] + lb_ref[...]
        o_ref[0, r0:r0 + CONV_CHUNK, :] = (y * _sigmoid(y)).astype(BF16)


def _conv_module(u, w, b, lg, lb):
    bsz, tp, c = u.shape
    tt = CONV_TT
    per = tt // CONV_HALO
    const = lambda bi, i: (0, 0)
    return pl.pallas_call(
        _conv_kernel,
        grid=(bsz, tp // tt),
        in_specs=[pl.BlockSpec((1, tt, c), lambda bi, i: (bi, i, 0)),
                  pl.BlockSpec((1, CONV_HALO, c), lambda bi, i: (bi, jnp.maximum(i * per - 1, 0), 0)),
                  pl.BlockSpec(w.shape, const), pl.BlockSpec((1, c), const),
                  pl.BlockSpec((1, c), const), pl.BlockSpec((1, c), const)],
        out_specs=pl.BlockSpec((1, tt, c), lambda bi, i: (bi, i, 0)),
        out_shape=jax.ShapeDtypeStruct(u.shape, BF16),
        scratch_shapes=[pltpu.VMEM((CONV_HALO + tt, c), F32),
                        pltpu.VMEM((SUBLANES - 1, CONV_HALO + tt - SUBLANES, c), F32)],
        compiler_params=_params("parallel", "parallel"),
        name="conv_module",
    )(u, u, w, b, lg, lb)


def _attn_kernel(*refs, tq, tile0, aliased, n_compute):
    if aliased:
        refs = refs[1:]
    o_ref = refs[4]
    step = pl.program_id(1)

    @pl.when(step < n_compute)
    def _():
        _attn_tile(*refs, tq=tq, row0=(step + tile0) * ATT_TQ)

    @pl.when(step >= n_compute)
    def _():
        o_ref[...] = jnp.zeros(o_ref.shape, BF16)


def _attn_tile(q_ref, k_ref, v_ref, tri_ref, o_ref, q2_ref, e_ref, tot_ref, acc_ref, car_ref, *, tq, row0):
    tk = ATT_TK
    npairs = q_ref.shape[2] // LANES
    lane_q = lax.broadcasted_iota(jnp.int32, (tq, LANES), 1)

    for hp in range(npairs):
        q = q_ref[0, 0:tq, hp * LANES:(hp + 1) * LANES]
        zero = jnp.zeros_like(q)
        q2_ref[hp, 0:tq, :] = jnp.where(lane_q < SB_HEAD_DIM, q, zero)
        q2_ref[hp, tq:2 * tq, :] = jnp.where(lane_q >= SB_HEAD_DIM, q, zero)
        car_ref[hp] = jnp.zeros((2 * tq, LANES), F32)
        acc_ref[hp] = jnp.zeros((2 * tq, LANES), F32)

    def stage_a(j, mask, slot):
        k0 = pl.multiple_of(j * tk, tk)
        for hp in range(npairs):
            kt = k_ref[0, pl.ds(k0, tk), hp * LANES:(hp + 1) * LANES]
            z = lax.dot_general(q2_ref[hp], kt, (((1,), (1,)), ((), ())), preferred_element_type=F32)
            sp = jnp.where(z > SOFTPLUS_CUTOFF, z, jnp.log(1.0 + jnp.exp2(z)) * LOG2E)
            if mask is not None:
                sp = jnp.where(mask, sp, 0.0)
            e = z + _dot(sp.astype(BF16), tri_ref[...])
            if mask is not None:
                e = jnp.where(mask, e, MASKED_EXPONENT)
            e_ref[slot, hp] = e
            tot_ref[slot, hp] = jnp.broadcast_to(jnp.sum(sp, axis=1, keepdims=True), (2 * tq, LANES))

    def stage_b(j, slot):
        k0 = pl.multiple_of(j * tk, tk)
        for hp in range(npairs):
            car = car_ref[hp]
            w = jnp.exp2(e_ref[slot, hp] + jnp.concatenate([car] * (tk // LANES), axis=1)).astype(BF16)
            vt = v_ref[0, pl.ds(k0, tk), hp * LANES:(hp + 1) * LANES]
            acc_ref[hp] += _dot(w, vt)
            car_ref[hp] = car - tot_ref[slot, hp]

    j_last = row0 // tk
    row = lax.broadcasted_iota(jnp.int32, (2 * tq, tk), 0)
    col = lax.broadcasted_iota(jnp.int32, (2 * tq, tk), 1)
    row = jnp.where(row >= tq, row - tq, row) + row0
    odd = j_last % 2
    stage_a(j_last, (col + j_last * tk) < row, odd)

    @pl.when(odd == 1)
    def _():
        stage_b(j_last, 1)
        stage_a(j_last - 1, None, 0)

    def body(jj, carry):
        j = j_last - odd - 1 - 2 * jj
        stage_b(j + 1, 0)
        stage_a(j, None, 1)
        stage_b(j, 1)
        stage_a(j - 1, None, 0)
        return carry

    lax.fori_loop(0, (j_last - odd) // 2, body, 0)
    stage_b(0, 0)
    if tq < o_ref.shape[1]:
        o_ref[0, tq:, :] = jnp.zeros((o_ref.shape[1] - tq, o_ref.shape[2]), BF16)
    for hp in range(npairs):
        o_ref[0, 0:tq, hp * LANES:(hp + 1) * LANES] = jnp.where(
            lane_q < SB_HEAD_DIM, acc_ref[hp, 0:tq, :], acc_ref[hp, tq:2 * tq, :]).astype(BF16)


def _attention(q, k, v, tri, t_real):
    bsz, tp, width = q.shape
    bq, tk = ATT_TQ, ATT_TK
    assert bq <= tk and tk % bq == 0
    npairs = width // LANES
    n_full = t_real // bq
    rem = -(-(t_real - n_full * bq) // BF16_ROWS) * BF16_ROWS

    def call(tq, tile0, n_compute, n_tiles, prev):
        blk = lambda b, i: (b, i + tile0, 0)
        full = lambda b, i: (b, 0, 0)
        in_specs = [pl.BlockSpec((1, bq, width), blk), pl.BlockSpec((1, tp, width), full),
                    pl.BlockSpec((1, tp, width), full), pl.BlockSpec(tri.shape, lambda b, i: (0, 0))]
        args = [q, k, v, tri]
        if prev is not None:
            in_specs.insert(0, pl.BlockSpec(memory_space=pl.ANY))
            args.insert(0, prev)
        return pl.pallas_call(
            functools.partial(_attn_kernel, tq=tq, tile0=tile0, aliased=prev is not None, n_compute=n_compute),
            grid=(bsz, n_tiles),
            in_specs=in_specs,
            out_specs=pl.BlockSpec((1, bq, width), blk),
            out_shape=jax.ShapeDtypeStruct(q.shape, BF16),
            input_output_aliases={} if prev is None else {0: 0},
            scratch_shapes=[pltpu.VMEM((npairs, 2 * tq, LANES), BF16),
                            pltpu.VMEM((2, npairs, 2 * tq, tk), F32),
                            pltpu.VMEM((2, npairs, 2 * tq, LANES), F32),
                            pltpu.VMEM((npairs, 2 * tq, LANES), F32),
                            pltpu.VMEM((npairs, 2 * tq, LANES), F32)],
            compiler_params=_params("parallel", "arbitrary"),
            name="stick_breaking" if prev is None else "stick_breaking_tail",
        )(*args)

    out = call(bq, 0, n_full, tp // bq, None)
    if rem:
        out = call(rem, n_full, 1, 1, out)
    return out


def _even_ffn_kernel(h_ref, u_ref, o_ref, wo_ref, g_ref, wgu_ref, wd_ref, out_ref, xn_ref, *, tf):
    @pl.when(pl.program_id(1) == 0)
    def _():
        h1 = h_ref[...] + _dot(u_ref[...], wo_ref[0:CONV_CH, :]) + _dot(o_ref[...], wo_ref[CONV_CH:, :])
        out_ref[...] = h1
        xn_ref[...] = _rms(h1, g_ref[...]).astype(BF16)

    gu = _dot(xn_ref[...], wgu_ref[...])
    g = gu[:, 0:tf]
    mid = g * _sigmoid(g) * gu[:, tf:2 * tf]
    out_ref[...] += _dot(mid.astype(BF16), wd_ref[...])


def _even_ffn(h, u, o, wo, g, wgu, wd, tf):
    n, d = h.shape
    ff = wd.shape[0]
    tm = ROW_TILE
    row = lambda i, f: (i, 0)
    const = lambda i, f: (0, 0)
    return pl.pallas_call(
        functools.partial(_even_ffn_kernel, tf=tf),
        grid=(n // tm, ff // tf),
        in_specs=[pl.BlockSpec((tm, d), row), pl.BlockSpec((tm, CONV_CH), row),
                  pl.BlockSpec((tm, SB_WIDTH), row), pl.BlockSpec(wo.shape, const),
                  pl.BlockSpec((1, d), const),
                  pl.BlockSpec((d, 2 * tf), lambda i, f: (0, f)),
                  pl.BlockSpec((tf, d), lambda i, f: (f, 0))],
        out_specs=pl.BlockSpec((tm, d), row),
        out_shape=jax.ShapeDtypeStruct((n, d), F32),
        scratch_shapes=[pltpu.VMEM((tm, d), BF16)],
        compiler_params=_params("parallel", "arbitrary"),
        name="even_out_swiglu",
    )(h, u, o, wo, g, wgu, wd)


def _row_scatter_kernel(win_ref, x_ref, info_ref, off_ref, init_hbm, o_hbm, cbuf, sems):
    del init_hbm
    i = pl.program_id(0)
    last = pl.num_programs(0) - 1
    n_slots = cbuf.shape[1]
    buf = i % 2

    def drained(b):
        return pltpu.make_async_copy(cbuf.at[b], o_hbm.at[pl.ds(0, n_slots)], sems.at[b])

    info = info_ref[...]
    sub = lax.broadcasted_iota(jnp.int32, info.shape, 0)
    valid = info[ROUTE_VALID:ROUTE_VALID + 1, :] > 0.5

    def local_slot(e_row, r_row):
        expert = info[e_row:e_row + 1, :].astype(jnp.int32)
        off = jnp.sum(jnp.where(sub == expert, off_ref[:, 0:1], 0.0), axis=0, keepdims=True)
        return jnp.where(valid, off + info[r_row:r_row + 1, :], -1.0)

    slot = lax.broadcasted_iota(jnp.int32, (n_slots, info.shape[1]), 0).astype(F32)
    pick = (jnp.where(slot == local_slot(ROUTE_E1, ROUTE_R1), 1.0, 0.0)
            + jnp.where(slot == local_slot(ROUTE_E2, ROUTE_R2), 1.0, 0.0)).astype(BF16)
    rows = _dot(pick, x_ref[...]).astype(cbuf.dtype)

    @pl.when(i >= 2)
    def _():
        drained(buf).wait()

    cbuf[buf] = rows
    for s in range(ROW_NWIN):
        dst = pl.multiple_of(win_ref[i * ROW_NWIN + s], ROW_WIN)
        pltpu.make_async_copy(cbuf.at[buf, pl.ds(s * ROW_WIN, ROW_WIN)], o_hbm.at[pl.ds(dst, ROW_WIN)],
                              sems.at[buf]).start(priority=s % 2)

    @pl.when(i == last)
    def _():
        drained(buf).wait()

        @pl.when(i >= 1)
        def _():
            drained(1 - buf).wait()


def _row_scatter(win_rows, x, info, local_off, n_rows_alloc):
    n, d = x.shape
    tm = COMBINE_TILE
    init = jnp.zeros((n_rows_alloc, d), x.dtype)
    return pl.pallas_call(
        _row_scatter_kernel,
        grid_spec=pltpu.PrefetchScalarGridSpec(
            num_scalar_prefetch=1, grid=(n // tm,),
            in_specs=[pl.BlockSpec((tm, d), lambda i, w: (i, 0)), pl.BlockSpec((SUBLANES, tm), lambda i, w: (i, 0)),
                      pl.BlockSpec((SUBLANES, LANES), lambda i, w: (i, 0)), pl.BlockSpec(memory_space=pl.ANY)],
            out_specs=pl.BlockSpec(memory_space=pl.ANY),
            scratch_shapes=[pltpu.VMEM((2, ROW_NWIN * ROW_WIN, d), x.dtype), pltpu.SemaphoreType.DMA((2,))]),
        out_shape=jax.ShapeDtypeStruct(init.shape, init.dtype),
        input_output_aliases={4: 0},
        compiler_params=_params("arbitrary"),
        name="moe_row_scatter",
    )(win_rows, x, info, local_off, init)


def _gmm_kernel(te_ref, nu_ref, xs_ref, wg_ref, wu_ref, wd_ref, o_ref, acc_ref):
    p = pl.program_id(0)
    f = pl.program_id(1)
    last = pl.num_programs(1) - 1
    used = p < nu_ref[0]

    @pl.when(used)
    def _():
        x = xs_ref[...]
        g = _dot(x, wg_ref[0])
        mid = g * _sigmoid(g) * _dot(x, wu_ref[0])
        part = _dot(mid.astype(BF16), wd_ref[0])

        @pl.when(f == 0)
        def _():
            acc_ref[...] = part

        @pl.when(f > 0)
        def _():
            acc_ref[...] += part

        @pl.when(f == last)
        def _():
            o_ref[...] = acc_ref[...].astype(o_ref.dtype)

    @pl.when(jnp.logical_not(used) & (f == last))
    def _():
        o_ref[...] = jnp.zeros(o_ref.shape, o_ref.dtype)


def _grouped_ffn(tile_expert, n_used, xs, n_rows, wg, wu, wd):
    d = xs.shape[1]
    ff = wd.shape[1]
    tf = MOE_FF_TILE
    nf = ff // tf
    tg = MOE_TILE
    row = lambda p, f, te, nu: (p, 0)
    fidx = lambda p, f, nu: jnp.where(p < nu[0], f, nf - 1)
    up = pl.BlockSpec((1, d, tf), lambda p, f, te, nu: (te[p], 0, fidx(p, f, nu)))
    return pl.pallas_call(
        _gmm_kernel,
        grid_spec=pltpu.PrefetchScalarGridSpec(
            num_scalar_prefetch=2, grid=(n_rows // tg, nf),
            in_specs=[pl.BlockSpec((tg, d), row), up, up,
                      pl.BlockSpec((1, tf, d), lambda p, f, te, nu: (te[p], fidx(p, f, nu), 0))],
            out_specs=pl.BlockSpec((tg, d), row),
            scratch_shapes=[pltpu.VMEM((tg, d), F32)]),
        out_shape=jax.ShapeDtypeStruct((n_rows, d), BF16),
        compiler_params=_params("arbitrary", "arbitrary"),
        name="moe_grouped_ffn",
    )(tile_expert, n_used, xs, wg, wu, wd)


def _combine_kernel(win_ref, h_ref, tok_ref, w1_ref, w2_ref, starts_ref, rows_ref, ys_hbm, o_ref, gbuf, sems,
                    *scratch, final_rows=None):
    i = pl.program_id(0)
    tm = h_ref.shape[0]
    win = ROW_WIN

    def fetch(tile, slot):
        for s in range(ROW_NWIN):
            start = pl.multiple_of(win_ref[tile * ROW_NWIN + s], win)
            pltpu.make_async_copy(ys_hbm.at[pl.ds(start, win)], gbuf.at[slot, pl.ds(s * win, win)],
                                  sems.at[slot]).start(priority=s % 2)

    @pl.when(i == 0)
    def _():
        fetch(0, 0)

    @pl.when(i + 1 < pl.num_programs(0))
    def _():
        fetch(i + 1, (i + 1) % 2)

    slot = i % 2
    pltpu.make_async_copy(ys_hbm.at[pl.ds(0, ROW_NWIN * win)], gbuf.at[slot], sems.at[slot]).wait()

    tok = tok_ref[...]
    lane = lax.broadcasted_iota(jnp.int32, tok.shape, 1)
    field = lambda c: jnp.sum(jnp.where(lane == c, tok, 0.0), axis=1, keepdims=True)
    valid = field(ROUTE_VALID) > 0.5
    rows = rows_ref[0]
    reps = h_ref.shape[1] // LANES
    out = h_ref[...]
    for e_col, r_col, w_ref in ((ROUTE_E1, ROUTE_R1, w1_ref), (ROUTE_E2, ROUTE_R2, w2_ref)):
        start = jnp.sum(jnp.where(lane == field(e_col).astype(jnp.int32), starts_ref[...], 0.0),
                        axis=1, keepdims=True)
        dest = jnp.where(valid, start + field(r_col), -2.0)
        pick = jnp.where(dest == rows, 1.0, 0.0).astype(BF16)
        out = out + jnp.concatenate([w_ref[...]] * reps, axis=1) * _dot(pick, gbuf[slot])
    if final_rows is None:
        o_ref[...] = out
        return

    obuf, osems = scratch
    n_meta, t_real = final_rows
    bsz, ts = obuf.shape[1], obuf.shape[2]
    steps = h_ref.shape[0] // (bsz * ts)
    assert steps == 1 and n_meta % SUBLANES == 0 and n_meta < ts
    i_last = t_real // ts
    rem = t_real - i_last * ts
    n_steps = pl.num_programs(0)
    oslot = i % 2

    def full_copy(step, s):
        dst = pl.multiple_of(step * ts - n_meta, SUBLANES)
        return pltpu.make_async_copy(obuf.at[s], o_ref.at[:, pl.ds(dst, ts), :], osems.at[s])

    is_full = lambda step: (step >= 1) & (step < i_last)

    @pl.when(is_full(i - 2))
    def _():
        full_copy(i - 2, oslot).wait()

    for t in range(ts):
        obuf[oslot, :, t, :] = out[t * bsz:(t + 1) * bsz, :]

    @pl.when(i == 0)
    def _():
        c = pltpu.make_async_copy(obuf.at[oslot, :, pl.ds(n_meta, ts - n_meta), :],
                                  o_ref.at[:, pl.ds(0, ts - n_meta), :], osems.at[oslot])
        c.start()
        c.wait()

    @pl.when(is_full(i))
    def _():
        full_copy(i, oslot).start()

    if rem:
        @pl.when(i == i_last)
        def _():
            c = pltpu.make_async_copy(obuf.at[oslot, :, pl.ds(0, rem), :],
                                      o_ref.at[:, pl.ds(i_last * ts - n_meta, rem), :], osems.at[oslot])
            c.start()
            c.wait()

    @pl.when(i == n_steps - 1)
    def _():
        @pl.when(is_full(i - 1))
        def _():
            full_copy(i - 1, 1 - oslot).wait()

        @pl.when(is_full(i))
        def _():
            full_copy(i, oslot).wait()


def _combine(win_start, h, tok, w1b, w2b, starts_row, slot_rows, ys, final=None):
    n, d = h.shape
    tm = COMBINE_TILE
    n_slots = ROW_NWIN * ROW_WIN
    row = lambda i, w: (i, 0)
    scratch = [pltpu.VMEM((2, n_slots, d), ys.dtype), pltpu.SemaphoreType.DMA((2,))]
    if final is None:
        kern = _combine_kernel
        out_spec = pl.BlockSpec((tm, d), row)
        out_shape = jax.ShapeDtypeStruct(h.shape, F32)
    else:
        bsz, n_meta, t_real = final
        assert tm % bsz == 0 and bsz % SUBLANES == 0
        kern = functools.partial(_combine_kernel, final_rows=(n_meta, t_real))
        out_spec = pl.BlockSpec(memory_space=pl.ANY)
        out_shape = jax.ShapeDtypeStruct((bsz, t_real - n_meta, d), F32)
        scratch += [pltpu.VMEM((2, bsz, tm // bsz, d), F32), pltpu.SemaphoreType.DMA((2,))]
    return pl.pallas_call(
        kern,
        grid_spec=pltpu.PrefetchScalarGridSpec(
            num_scalar_prefetch=1, grid=(n // tm,),
            in_specs=[pl.BlockSpec((tm, d), row), pl.BlockSpec((tm, LANES), row), pl.BlockSpec((tm, LANES), row),
                      pl.BlockSpec((tm, LANES), row), pl.BlockSpec((1, LANES), lambda i, w: (0, 0)),
                      pl.BlockSpec((1, 1, n_slots), lambda i, w: (i, 0, 0)),
                      pl.BlockSpec(memory_space=pl.ANY)],
            out_specs=out_spec,
            scratch_shapes=scratch),
        out_shape=out_shape,
        compiler_params=_params("arbitrary"),
        name="moe_combine",
    )(win_start, h, tok, w1b, w2b, starts_row, slot_rows, ys)


def _gate_windows(lw, bd):
    out = []
    for c0 in range(0, lw, LRU_GATE_TILE):
        nc = min(LRU_GATE_TILE, lw - c0)
        r0 = (c0 // bd) * bd
        r1 = ((c0 + nc - 1) // bd + 1) * bd
        r0 = (r0 // LANES) * LANES
        r1 = min(-(-r1 // LANES) * LANES, lw)
        out.append((c0, nc, r0, r1 - r0))
    return out


def _lru_kernel(h_ref, ng_ref, win_ref, cw_ref, cb_ref, wg_ref, br_ref, bi_ref, lam_ref, wo_ref,
                o_ref, xbuf_ref, hs_ref, a_ref, b_ref, y_ref, gate_ref, res_ref, *, windows, bsz):
    ti = pl.program_id(0)
    rows, lw = a_ref.shape
    halo = (LRU_CONV_WIDTH - 1) * bsz

    @pl.when(ti == 0)
    def _():
        xbuf_ref[0:halo, :] = jnp.zeros((halo, lw), F32)
        hs_ref[...] = jnp.zeros(hs_ref.shape, F32)

    @pl.when(ti > 0)
    def _():
        xbuf_ref[0:halo, :] = xbuf_ref[rows:rows + halo, :]

    res_ref[...] = jnp.concatenate([h_ref[:, t, :] for t in range(rows // bsz)], axis=0)
    proj = _dot(_rms(res_ref[...], ng_ref[...]).astype(BF16), win_ref[...])
    gate_ref[...] = jax.nn.gelu(proj[:, 0:lw], approximate=True)
    xbuf_ref[halo:halo + rows, :] = proj[:, lw:2 * lw]
    xc = jnp.broadcast_to(cb_ref[...], (rows, lw))
    for k in range(LRU_CONV_WIDTH):
        xc = xc + cw_ref[k:k + 1, :] * xbuf_ref[k * bsz:k * bsz + rows, :]
    xcb = xc.astype(BF16)
    sp_lam = jnp.log(1.0 + jnp.exp(-lam_ref[...]))
    log2_a_per_r = (-LRU_C * LOG2E) * sp_lam
    wcol = 0
    for (c0, nc, r0, nr) in windows:
        rg = _dot(xcb[:, r0:r0 + nr], wg_ref[r0:r0 + nr, wcol:wcol + 2 * nc])
        wcol += 2 * nc
        r = _sigmoid(rg[:, 0:nc] + br_ref[:, c0:c0 + nc])
        ig = _sigmoid(rg[:, nc:2 * nc] + bi_ref[:, c0:c0 + nc])
        a = jnp.exp2(r * log2_a_per_r[:, c0:c0 + nc])
        a_ref[:, c0:c0 + nc] = a
        one_m_a2 = 1.0 - a * a
        root = jnp.where(one_m_a2 > 0.0, one_m_a2 * lax.rsqrt(one_m_a2), 0.0)
        b_ref[:, c0:c0 + nc] = root * (ig * xc[:, c0:c0 + nc])

    def step(t, hprev):
        r0 = pl.multiple_of(t * bsz, SUBLANES)
        hnew = a_ref[pl.ds(r0, bsz), :] * hprev + b_ref[pl.ds(r0, bsz), :]
        y_ref[pl.ds(r0, bsz), :] = hnew
        return hnew

    hs_ref[...] = lax.fori_loop(0, rows // bsz, step, hs_ref[...], unroll=LRU_SCAN_UNROLL)
    gy = (gate_ref[...] * y_ref[...]).astype(BF16)
    o_ref[...] = res_ref[...] + _dot(gy, wo_ref[...])


def _lru_mixer(h, ng, win, cw, cb, wg, br, bi, lam, wo, windows):
    bsz, tp, d = h.shape
    n = bsz * tp
    lw = lam.shape[-1]
    assert bsz % SUBLANES == 0 and LRU_ROWS % bsz == 0 and n % LRU_ROWS == 0
    rows = LRU_ROWS
    halo = (LRU_CONV_WIDTH - 1) * bsz
    const = lambda t: (0, 0)
    blk = lambda t: (t, 0)
    resident = lambda w: pl.BlockSpec(w.shape, const, pipeline_mode=pl.Buffered(1))
    return pl.pallas_call(
        functools.partial(_lru_kernel, windows=windows, bsz=bsz),
        grid=(n // rows,),
        in_specs=[pl.BlockSpec((bsz, rows // bsz, d), lambda t: (0, t, 0)), pl.BlockSpec((1, d), const), resident(win),
                  pl.BlockSpec(cw.shape, const), pl.BlockSpec((1, lw), const), resident(wg),
                  pl.BlockSpec((1, lw), const), pl.BlockSpec((1, lw), const),
                  pl.BlockSpec((1, lw), const), resident(wo)],
        out_specs=pl.BlockSpec((rows, d), blk),
        out_shape=jax.ShapeDtypeStruct((n, d), F32),
        scratch_shapes=[pltpu.VMEM((halo + rows, lw), F32), pltpu.VMEM((bsz, lw), F32),
                        pltpu.VMEM((rows, lw), F32), pltpu.VMEM((rows, lw), F32), pltpu.VMEM((rows, lw), F32),
                        pltpu.VMEM((rows, lw), F32), pltpu.VMEM((rows, d), F32)],
        compiler_params=_params("arbitrary"),
        name="rglru_mixer",
    )(h, ng, win, cw, cb, wg, br, bi, lam, wo)


ROUTE_E1, ROUTE_E2, ROUTE_R1, ROUTE_R2, ROUTE_VALID = range(5)


def _router_kernel(h_ref, g_ref, rw_ref, ltri_ref, xn_ref, info_ref, tok_ref, w1_ref, w2_ref, before_ref, cnt_ref,
                   run_ref, *, n_real_rows):
    i = pl.program_id(0)
    tm = h_ref.shape[0]

    @pl.when(i == 0)
    def _():
        run_ref[...] = jnp.zeros(run_ref.shape, F32)

    xn = _rms(h_ref[...], g_ref[...])
    xn_ref[...] = xn.astype(BF16)
    split = lambda a: (a.astype(BF16), (a - a.astype(BF16).astype(F32)).astype(BF16))
    (x_hi, x_lo), (r_hi, r_lo) = split(xn), split(rw_ref[...])
    logits = _dot(x_hi, r_hi) + (_dot(x_hi, r_lo) + _dot(x_lo, r_hi))
    lane = lax.broadcasted_iota(jnp.int32, logits.shape, 1)
    neg = jnp.float32(-jnp.inf)
    logits = jnp.where(lane < N_EXPERTS, logits, neg)
    top1 = jnp.max(logits, axis=1, keepdims=True)
    idx1 = jnp.min(jnp.where(logits == top1, lane, LANES), axis=1, keepdims=True)
    rest = jnp.where(lane == idx1, neg, logits)
    top2 = jnp.max(rest, axis=1, keepdims=True)
    idx2 = jnp.min(jnp.where(rest == top2, lane, LANES), axis=1, keepdims=True)
    w1 = 1.0 / (1.0 + jnp.exp(top2 - top1))

    rowg = lax.broadcasted_iota(jnp.int32, (tm, 1), 0) + i * tm
    valid = jnp.where(rowg < n_real_rows, 1.0, 0.0)
    onehot = (jnp.where(lane == idx1, 1.0, 0.0) + jnp.where(lane == idx2, 1.0, 0.0)) * valid
    before_ref[...] = jnp.broadcast_to(run_ref[...], before_ref.shape)
    before = _dot(ltri_ref[...], onehot.astype(BF16)) + run_ref[...]
    rank1 = jnp.sum(jnp.where(lane == idx1, before, 0.0), axis=1, keepdims=True)
    rank2 = jnp.sum(jnp.where(lane == idx2, before, 0.0), axis=1, keepdims=True)
    run_ref[...] += jnp.ceil(jnp.sum(onehot, axis=0, keepdims=True) / ROW_WIN) * ROW_WIN
    cnt_ref[...] = run_ref[...]
    w1_ref[...] = jnp.broadcast_to(w1, (tm, LANES))
    w2_ref[...] = jnp.broadcast_to(1.0 - w1, (tm, LANES))
    fields = {ROUTE_E1: idx1.astype(F32), ROUTE_E2: idx2.astype(F32),
              ROUTE_R1: rank1, ROUTE_R2: rank2, ROUTE_VALID: valid}
    info = jnp.zeros(logits.shape, F32)
    for col, val in fields.items():
        info = jnp.where(lane == col, val, info)
    tok_ref[...] = info
    info_ref[...] = info.T[0:SUBLANES, :]


def _router(h, g, rw, ltri, n_real_rows):
    n, d = h.shape
    tm = ROW_TILE
    row = lambda i: (i, 0)
    const = lambda i: (0, 0)
    return pl.pallas_call(
        functools.partial(_router_kernel, n_real_rows=n_real_rows),
        grid=(n // tm,),
        in_specs=[pl.BlockSpec((tm, d), row), pl.BlockSpec((1, d), const), pl.BlockSpec(rw.shape, const),
                  pl.BlockSpec(ltri.shape, const)],
        out_specs=[pl.BlockSpec((tm, d), row), pl.BlockSpec((SUBLANES, tm), row),
                   pl.BlockSpec((tm, LANES), row), pl.BlockSpec((tm, LANES), row),
                   pl.BlockSpec((tm, LANES), row), pl.BlockSpec((SUBLANES, LANES), row),
                   pl.BlockSpec((1, LANES), const)],
        out_shape=[jax.ShapeDtypeStruct((n, d), BF16), jax.ShapeDtypeStruct((n // tm * SUBLANES, tm), F32),
                   jax.ShapeDtypeStruct((n, LANES), F32), jax.ShapeDtypeStruct((n, LANES), F32),
                   jax.ShapeDtypeStruct((n, LANES), F32), jax.ShapeDtypeStruct((n // tm * SUBLANES, LANES), F32),
                   jax.ShapeDtypeStruct((1, LANES), F32)],
        scratch_shapes=[pltpu.VMEM((1, LANES), F32)],
        compiler_params=_params("arbitrary"),
        name="router",
    )(h, g, rw, ltri)


def _routing_tables(before, counts, n_rows):
    tg = MOE_TILE
    win = ROW_WIN
    cnt = counts[0, :N_EXPERTS].astype(jnp.int32)
    padded = (cnt + tg - 1) // tg * tg
    ends = jnp.cumsum(padded)
    starts = ends - padded
    tile_start = jnp.arange(n_rows // tg, dtype=jnp.int32) * tg
    tile_expert = jnp.minimum(jnp.sum(tile_start[:, None] >= ends[None, :], axis=1), N_EXPERTS - 1)
    n_used = (ends[-1] // tg).reshape(1)
    bef = before.reshape(-1, SUBLANES, LANES)[:, 0, :N_EXPERTS].astype(jnp.int32)
    nwin = (jnp.concatenate([bef[1:], cnt[None, :]], axis=0) - bef) // win
    base = jnp.cumsum(nwin, axis=1) - nwin
    slot = jnp.arange(ROW_NWIN, dtype=jnp.int32)[None, :, None]
    mine = (slot >= base[:, None, :]) & (slot < (base + nwin)[:, None, :])
    rows = jnp.sum(jnp.where(mine, (starts[None, :] + bef)[:, None, :] + win * (slot - base[:, None, :]), 0), axis=2)
    used = jnp.any(mine, axis=2)
    parity = (jnp.arange(bef.shape[0], dtype=jnp.int32) % 2)[:, None]
    scatter_rows = jnp.where(used, rows, n_rows + win * (parity * ROW_NWIN + slot[:, :, 0]))
    gather_rows = jnp.where(used, rows, 0)
    slot_rows = jnp.where(used[..., None], rows[..., None] + jnp.arange(win, dtype=jnp.int32), -1)
    slot_rows = slot_rows.reshape(-1, 1, ROW_NWIN * win).astype(F32)
    starts_row = jnp.zeros((1, LANES), F32).at[0, :N_EXPERTS].set(starts.astype(F32))
    local_off = jnp.zeros(bef.shape[:1] + (SUBLANES,), jnp.int32).at[:, :N_EXPERTS].set(win * base - bef)
    local_off = jnp.broadcast_to(local_off.astype(F32)[:, :, None], local_off.shape + (LANES,)).reshape(-1, LANES)
    return (tile_expert.astype(jnp.int32), n_used.astype(jnp.int32), scatter_rows.reshape(-1).astype(jnp.int32),
            gather_rows.reshape(-1).astype(jnp.int32), slot_rows, starts_row, local_off)


def _interleave_gate_up(wg, wu, tf):
    e, d, ff = wg.shape
    g = wg.astype(BF16).reshape(e, d, ff // tf, tf)
    u = wu.astype(BF16).reshape(e, d, ff // tf, tf)
    return jnp.concatenate([g, u], axis=3).reshape(e, d, 2 * ff)


def _dense_block_diag(w):
    nb, bd, _ = w.shape
    eye = jnp.eye(nb, dtype=w.dtype)
    return (w[:, :, None, :] * eye[:, None, :, None]).reshape(nb * bd, nb * bd)


def _pack_lru_gates(wr, wi, windows):
    dr = _dense_block_diag(wr).astype(BF16)
    di = _dense_block_diag(wi).astype(BF16)
    cols = []
    for (c0, nc, _, _) in windows:
        cols += [dr[:, c0:c0 + nc], di[:, c0:c0 + nc]]
    return jnp.concatenate(cols, axis=1)


def _ffn_chunk(ff):
    for tf in (896, 512, 384, 256, 128):
        if ff % tf == 0:
            return tf
    raise ValueError(f"unsupported d_ff {ff}")


def kernel(x, meta_tokens, mix_norm_even, w_in_even, conv_w, conv_b, conv_ln_g, conv_ln_b, q_norm_g, k_norm_g, w_out_even, ffn_norm_even, ffn_w_gate, ffn_w_up, ffn_w_down, mix_norm_odd, w_in_odd, lru_conv_w, lru_conv_b, gate_r_w, gate_r_b, gate_i_w, gate_i_b, lru_lambda, w_out_odd, ffn_norm_odd, router_w, moe_w_gate, moe_w_up, moe_w_down):
    bsz, seq, d = x.shape
    t_real = N_META + seq
    tp = -(-t_real // TIME_TILE) * TIME_TILE
    n = bsz * tp
    assert n % ROW_TILE == 0
    depth = mix_norm_even.shape[0] + mix_norm_odd.shape[0]

    meta = jnp.broadcast_to(meta_tokens[None].astype(x.dtype), (bsz, N_META, d))
    h = jnp.concatenate([meta, x, jnp.zeros((bsz, tp - t_real, d), x.dtype)], axis=1).reshape(n, d)

    head_mean = jnp.kron(jnp.eye(MXU_DIM // SB_HEAD_DIM, dtype=F32),
                         jnp.full((SB_HEAD_DIM, SB_HEAD_DIM), 1.0 / SB_HEAD_DIM, F32)).astype(BF16)
    kk = jnp.arange(ATT_TK)
    neg_tri = -(kk[:, None] >= kk[None, :]).astype(BF16)
    row2 = lambda a: a.reshape(1, -1)

    time_major = False
    for layer in range(depth):
        p = layer // 2
        if layer % 2 == 0:
            if time_major:
                h = h.reshape(tp, bsz, d).transpose(1, 0, 2).reshape(n, d)
                time_major = False
            u, q, k, v = _even_in_proj(h, row2(mix_norm_even[p]), w_in_even[p].astype(BF16),
                                       row2(jnp.tile(q_norm_g[p], SB_HEADS)),
                                       row2(jnp.tile(k_norm_g[p], SB_HEADS)), head_mean)
            u = _conv_module(u.reshape(bsz, tp, CONV_CH), conv_w[p], row2(conv_b[p]),
                             row2(conv_ln_g[p]), row2(conv_ln_b[p]))
            o = _attention(q.reshape(bsz, tp, SB_WIDTH), k.reshape(bsz, tp, SB_WIDTH),
                           v.reshape(bsz, tp, SB_WIDTH), neg_tri, t_real)
            tf = _ffn_chunk(ffn_w_gate.shape[-1])
            h = _even_ffn(h, u.reshape(n, CONV_CH), o.reshape(n, SB_WIDTH), w_out_even[p].astype(BF16),
                          row2(ffn_norm_even[p]),
                          _interleave_gate_up(ffn_w_gate[p][None], ffn_w_up[p][None], tf)[0],
                          ffn_w_down[p].astype(BF16), tf)
        else:
            if time_major:
                h = h.reshape(tp, bsz, d).transpose(1, 0, 2).reshape(n, d)
            time_major = True
            lw = lru_lambda.shape[-1]
            windows = _gate_windows(lw, lw // LRU_BLOCKS)
            h = _lru_mixer(h.reshape(bsz, tp, d), row2(mix_norm_odd[p]), w_in_odd[p].astype(BF16),
                           lru_conv_w[p], row2(lru_conv_b[p]),
                           _pack_lru_gates(gate_r_w[p], gate_i_w[p], windows),
                           row2(gate_r_b[p]), row2(gate_i_b[p]), row2(lru_lambda[p]),
                           w_out_odd[p].astype(BF16), windows)
            rw = jnp.pad(router_w[p], ((0, 0), (0, LANES - N_EXPERTS)))
            kk = jnp.arange(ROW_TILE)
            ltri = (kk[:, None] > kk[None, :]).astype(BF16)
            xn, info, tok, w1b, w2b, before, counts = _router(h, row2(ffn_norm_odd[p]), rw, ltri, t_real * bsz)
            tg = MOE_TILE
            assert ROW_TILE == COMBINE_TILE
            n_tiles = n // COMBINE_TILE
            n_rows = 2 * bsz * t_real + n_tiles * N_EXPERTS * (ROW_WIN - 1) + N_EXPERTS * (tg - 1)
            n_rows = -(-n_rows // tg) * tg
            tile_expert, n_used, scatter_rows, gather_rows, slot_rows, starts_row, local_off = _routing_tables(
                before, counts, n_rows)
            xs = _row_scatter(scatter_rows, xn, info, local_off, n_rows + 2 * ROW_NWIN * ROW_WIN)
            ys = _grouped_ffn(tile_expert, n_used, xs, n_rows, moe_w_gate[p].astype(BF16),
                              moe_w_up[p].astype(BF16), moe_w_down[p].astype(BF16))
            if layer + 1 == depth:
                return _combine(gather_rows, h, tok, w1b, w2b, starts_row, slot_rows, ys, (bsz, N_META, t_real))
            h = _combine(gather_rows, h, tok, w1b, w2b, starts_row, slot_rows, ys)
    if time_major:
        return h.reshape(tp, bsz, d)[N_META:t_real].transpose(1, 0, 2)
    return h.reshape(bsz, tp, d)[:, N_META:t_real]
```

```python
import functools
import math

import jax
import jax.numpy as jnp
from jax import lax
from jax.experimental import pallas as pl
from jax.experimental.pallas import tpu as pltpu

F32 = jnp.float32
BF16 = jnp.bfloat16

EPS = 1e-6
N_META = 16
CONV_CH = 512
CONV_WIDTH = 31
SB_HEADS = 8
SB_HEAD_DIM = 64
SB_WIDTH = SB_HEADS * SB_HEAD_DIM
LRU_BLOCKS = 16
LRU_CONV_WIDTH = 4
LRU_C = 8.0
N_EXPERTS = 8

LANES = 128
MXU_DIM = 256
TIME_TILE = 256
VMEM_LIMIT = 50 * 1024 * 1024

ROW_TILE = 512
CONV_TT = 256
CONV_HALO = 32
CONV_CHUNK = 32
ATT_TQ = 256
ATT_TK = 256
ATT_UNROLL = 4
LOG2E = 1.4426950408889634
MASKED_EXPONENT = -1e30
SOFTPLUS_CUTOFF = 126.0
LRU_ROWS = 512
LRU_SCAN_UNROLL = 4
LRU_GATE_TILE = 256
MOE_TILE = 512
MOE_FF_TILE = 1792
COMBINE_TILE = 512
ROW_WIN = 16
ROW_NWIN = 2 * COMBINE_TILE // ROW_WIN + N_EXPERTS
SUBLANES = 8
BF16_ROWS = 16


def _dot(a, b):
    return jnp.dot(a, b, preferred_element_type=F32)


def _params(*sem):
    return pltpu.CompilerParams(dimension_semantics=sem, vmem_limit_bytes=VMEM_LIMIT)


def _sigmoid(x):
    return 0.5 * jnp.tanh(0.5 * x) + 0.5


def _rms(x, g):
    ms = jnp.mean(x * x, axis=-1, keepdims=True)
    return x * lax.rsqrt(ms + EPS) * g


def _even_in_kernel(h_ref, g_ref, w_ref, qg_ref, kg_ref, hm_ref, u_ref, q_ref, k_ref, v_ref):
    xn = _rms(h_ref[...], g_ref[...]).astype(BF16)
    a = _dot(xn, w_ref[:, 0:CONV_CH])
    gate = _dot(xn, w_ref[:, CONV_CH:2 * CONV_CH])
    u_ref[...] = (a * _sigmoid(gate)).astype(BF16)

    def head_norm(y, gain):
        yy = (y * y).astype(BF16)
        parts = [_dot(yy[:, c:c + MXU_DIM], hm_ref[...]) for c in range(0, SB_WIDTH, MXU_DIM)]
        ms = jnp.concatenate(parts, axis=1)
        return y * lax.rsqrt(ms + EPS) * gain

    c0 = 2 * CONV_CH
    q = head_norm(_dot(xn, w_ref[:, c0:c0 + SB_WIDTH]), qg_ref[...])
    q_ref[...] = (q * (LOG2E / math.sqrt(SB_HEAD_DIM))).astype(BF16)
    k = head_norm(_dot(xn, w_ref[:, c0 + SB_WIDTH:c0 + 2 * SB_WIDTH]), kg_ref[...])
    k_ref[...] = k.astype(BF16)
    v_ref[...] = _dot(xn, w_ref[:, c0 + 2 * SB_WIDTH:c0 + 3 * SB_WIDTH]).astype(BF16)


def _even_in_proj(h, g, w, qg, kg, hm):
    n, d = h.shape
    tm = ROW_TILE
    row = lambda i: (i, 0)
    const = lambda i: (0, 0)
    out = jax.ShapeDtypeStruct((n, SB_WIDTH), BF16)
    return pl.pallas_call(
        _even_in_kernel,
        grid=(n // tm,),
        in_specs=[pl.BlockSpec((tm, d), row), pl.BlockSpec((1, d), const),
                  pl.BlockSpec(w.shape, const), pl.BlockSpec((1, SB_WIDTH), const),
                  pl.BlockSpec((1, SB_WIDTH), const), pl.BlockSpec(hm.shape, const)],
        out_specs=[pl.BlockSpec((tm, SB_WIDTH), row)] * 4,
        out_shape=[out] * 4,
        compiler_params=_params("parallel"),
        name="even_in_proj",
    )(h, g, w, qg, kg, hm)


def _conv_kernel(cur_ref, halo_ref, w_ref, b_ref, lg_ref, lb_ref, o_ref, buf_ref, sh_ref):
    i = pl.program_id(1)
    tt = cur_ref.shape[1]
    rows = CONV_HALO + tt
    halo = halo_ref[0].astype(F32)
    buf_ref[0:CONV_HALO, :] = jnp.where(i > 0, halo, 0.0)
    buf_ref[CONV_HALO:rows, :] = cur_ref[0].astype(F32)
    for s in range(1, SUBLANES):
        sh_ref[s - 1, 0:rows - SUBLANES, :] = buf_ref[s:s + rows - SUBLANES, :]
    off = CONV_HALO - (CONV_WIDTH - 1)
    for c in range(tt // CONV_CHUNK):
        r0 = c * CONV_CHUNK
        acc = jnp.broadcast_to(b_ref[...], (CONV_CHUNK, CONV_CH))
        for k in range(CONV_WIDTH):
            shift = (off + k) % SUBLANES
            base = r0 + off + k - shift
            if shift == 0:
                tap = buf_ref[base:base + CONV_CHUNK, :]
            else:
                tap = sh_ref[shift - 1, base:base + CONV_CHUNK, :]
            acc = acc + w_ref[k:k + 1, :] * tap
        mu = jnp.mean(acc, axis=-1, keepdims=True)
        xc = acc - mu
        var = jnp.mean(xc * xc, axis=-1, keepdims=True)
        y = xc * lax.rsqrt(var + EPS) * lg_ref[...] + lb_ref[...]
        o_ref[0, r0:r0 + CONV_CHUNK, :] = (y * _sigmoid(y)).astype(BF16)


def _conv_module(u, w, b, lg, lb):
    bsz, tp, c = u.shape
    tt = CONV_TT
    per = tt // CONV_HALO
    const = lambda bi, i: (0, 0)
    return pl.pallas_call(
        _conv_kernel,
        grid=(bsz, tp // tt),
        in_specs=[pl.BlockSpec((1, tt, c), lambda bi, i: (bi, i, 0)),
                  pl.BlockSpec((1, CONV_HALO, c), lambda bi, i: (bi, jnp.maximum(i * per - 1, 0), 0)),
                  pl.BlockSpec(w.shape, const), pl.BlockSpec((1, c), const),
                  pl.BlockSpec((1, c), const), pl.BlockSpec((1, c), const)],
        out_specs=pl.BlockSpec((1, tt, c), lambda bi, i: (bi, i, 0)),
        out_shape=jax.ShapeDtypeStruct(u.shape, BF16),
        scratch_shapes=[pltpu.VMEM((CONV_HALO + tt, c), F32),
                        pltpu.VMEM((SUBLANES - 1, CONV_HALO + tt - SUBLANES, c), F32)],
        compiler_params=_params("parallel", "parallel"),
        name="conv_module",
    )(u, u, w, b, lg, lb)


def _attn_kernel(*refs, tq, tile0, aliased, n_compute):
    if aliased:
        refs = refs[1:]
    o_ref = refs[4]
    step = pl.program_id(1)

    @pl.when(step < n_compute)
    def _():
        _attn_tile(*refs, tq=tq, row0=(step + tile0) * ATT_TQ)

    @pl.when(step >= n_compute)
    def _():
        o_ref[...] = jnp.zeros(o_ref.shape, BF16)


def _attn_tile(q_ref, k_ref, v_ref, tri_ref, o_ref, q2_ref, e_ref, tot_ref, acc_ref, car_ref, *, tq, row0):
    tk = ATT_TK
    npairs = q_ref.shape[2] // LANES
    lane_q = lax.broadcasted_iota(jnp.int32, (tq, LANES), 1)

    for hp in range(npairs):
        q = q_ref[0, 0:tq, hp * LANES:(hp + 1) * LANES]
        zero = jnp.zeros_like(q)
        q2_ref[hp, 0:tq, :] = jnp.where(lane_q < SB_HEAD_DIM, q, zero)
        q2_ref[hp, tq:2 * tq, :] = jnp.where(lane_q >= SB_HEAD_DIM, q, zero)
        car_ref[hp] = jnp.zeros((2 * tq, LANES), F32)
        acc_ref[hp] = jnp.zeros((2 * tq, LANES), F32)

    def stage_a(j, mask, slot):
        k0 = pl.multiple_of(j * tk, tk)
        for hp in range(npairs):
            kt = k_ref[0, pl.ds(k0, tk), hp * LANES:(hp + 1) * LANES]
            z = lax.dot_general(q2_ref[hp], kt, (((1,), (1,)), ((), ())), preferred_element_type=F32)
            sp = jnp.where(z > SOFTPLUS_CUTOFF, z, jnp.log(1.0 + jnp.exp2(z)) * LOG2E)
            if mask is not None:
                sp = jnp.where(mask, sp, 0.0)
            e = z + _dot(sp.astype(BF16), tri_ref[...])
            if mask is not None:
                e = jnp.where(mask, e, MASKED_EXPONENT)
            e_ref[slot, hp] = e
            tot_ref[slot, hp] = jnp.broadcast_to(jnp.sum(sp, axis=1, keepdims=True), (2 * tq, LANES))

    def stage_b(j, slot):
        k0 = pl.multiple_of(j * tk, tk)
        for hp in range(npairs):
            car = car_ref[hp]
            w = jnp.exp2(e_ref[slot, hp] + jnp.concatenate([car] * (tk // LANES), axis=1)).astype(BF16)
            vt = v_ref[0, pl.ds(k0, tk), hp * LANES:(hp + 1) * LANES]
            acc_ref[hp] += _dot(w, vt)
            car_ref[hp] = car - tot_ref[slot, hp]

    j_last = row0 // tk
    row = lax.broadcasted_iota(jnp.int32, (2 * tq, tk), 0)
    col = lax.broadcasted_iota(jnp.int32, (2 * tq, tk), 1)
    row = jnp.where(row >= tq, row - tq, row) + row0
    stage_a(j_last, (col + j_last * tk) < row, j_last % 2)
    lead = j_last % ATT_UNROLL
    for s in range(ATT_UNROLL - 1):
        @pl.when(lead > s)
        def _(s=s):
            staged = j_last - s
            stage_b(staged, staged % 2)
            stage_a(staged - 1, None, (staged - 1) % 2)

    def body(jj, carry):
        staged = j_last - lead - ATT_UNROLL * jj
        for s in range(ATT_UNROLL):
            stage_b(staged - s, s % 2)
            stage_a(staged - s - 1, None, (s + 1) % 2)
        return carry

    lax.fori_loop(0, (j_last - lead) // ATT_UNROLL, body, 0)
    stage_b(0, 0)
    if tq < o_ref.shape[1]:
        o_ref[0, tq:, :] = jnp.zeros((o_ref.shape[1] - tq, o_ref.shape[2]), BF16)
    for hp in range(npairs):
        o_ref[0, 0:tq, hp * LANES:(hp + 1) * LANES] = jnp.where(
            lane_q < SB_HEAD_DIM, acc_ref[hp, 0:tq, :], acc_ref[hp, tq:2 * tq, :]).astype(BF16)


def _attention(q, k, v, tri, t_real):
    bsz, tp, width = q.shape
    bq, tk = ATT_TQ, ATT_TK
    assert bq <= tk and tk % bq == 0
    npairs = width // LANES
    n_full = t_real // bq
    rem = -(-(t_real - n_full * bq) // BF16_ROWS) * BF16_ROWS

    def call(tq, tile0, n_compute, n_tiles, prev):
        blk = lambda b, i: (b, i + tile0, 0)
        full = lambda b, i: (b, 0, 0)
        in_specs = [pl.BlockSpec((1, bq, width), blk), pl.BlockSpec((1, tp, width), full),
                    pl.BlockSpec((1, tp, width), full), pl.BlockSpec(tri.shape, lambda b, i: (0, 0))]
        args = [q, k, v, tri]
        if prev is not None:
            in_specs.insert(0, pl.BlockSpec(memory_space=pl.ANY))
            args.insert(0, prev)
        return pl.pallas_call(
            functools.partial(_attn_kernel, tq=tq, tile0=tile0, aliased=prev is not None, n_compute=n_compute),
            grid=(bsz, n_tiles),
            in_specs=in_specs,
            out_specs=pl.BlockSpec((1, bq, width), blk),
            out_shape=jax.ShapeDtypeStruct(q.shape, BF16),
            input_output_aliases={} if prev is None else {0: 0},
            scratch_shapes=[pltpu.VMEM((npairs, 2 * tq, LANES), BF16),
                            pltpu.VMEM((2, npairs, 2 * tq, tk), F32),
                            pltpu.VMEM((2, npairs, 2 * tq, LANES), F32),
                            pltpu.VMEM((npairs, 2 * tq, LANES), F32),
                            pltpu.VMEM((npairs, 2 * tq, LANES), F32)],
            compiler_params=_params("parallel", "arbitrary"),
            name="stick_breaking" if prev is None else "stick_breaking_tail",
        )(*args)

    out = call(bq, 0, n_full, tp // bq, None)
    if rem:
        out = call(rem, n_full, 1, 1, out)
    return out


def _even_ffn_kernel(h_ref, u_ref, o_ref, wo_ref, g_ref, wgu_ref, wd_ref, out_ref, xn_ref, *, tf):
    @pl.when(pl.program_id(1) == 0)
    def _():
        h1 = h_ref[...] + _dot(u_ref[...], wo_ref[0:CONV_CH, :]) + _dot(o_ref[...], wo_ref[CONV_CH:, :])
        out_ref[...] = h1
        xn_ref[...] = _rms(h1, g_ref[...]).astype(BF16)

    gu = _dot(xn_ref[...], wgu_ref[...])
    g = gu[:, 0:tf]
    mid = g * _sigmoid(g) * gu[:, tf:2 * tf]
    out_ref[...] += _dot(mid.astype(BF16), wd_ref[...])


def _even_ffn(h, u, o, wo, g, wgu, wd, tf):
    n, d = h.shape
    ff = wd.shape[0]
    tm = ROW_TILE
    row = lambda i, f: (i, 0)
    const = lambda i, f: (0, 0)
    return pl.pallas_call(
        functools.partial(_even_ffn_kernel, tf=tf),
        grid=(n // tm, ff // tf),
        in_specs=[pl.BlockSpec((tm, d), row), pl.BlockSpec((tm, CONV_CH), row),
                  pl.BlockSpec((tm, SB_WIDTH), row), pl.BlockSpec(wo.shape, const),
                  pl.BlockSpec((1, d), const),
                  pl.BlockSpec((d, 2 * tf), lambda i, f: (0, f)),
                  pl.BlockSpec((tf, d), lambda i, f: (f, 0))],
        out_specs=pl.BlockSpec((tm, d), row),
        out_shape=jax.ShapeDtypeStruct((n, d), F32),
        scratch_shapes=[pltpu.VMEM((tm, d), BF16)],
        compiler_params=_params("parallel", "arbitrary"),
        name="even_out_swiglu",
    )(h, u, o, wo, g, wgu, wd)


def _row_scatter_kernel(win_ref, x_ref, info_ref, off_ref, init_hbm, o_hbm, cbuf, sems):
    del init_hbm
    i = pl.program_id(0)
    last = pl.num_programs(0) - 1
    n_slots = cbuf.shape[1]
    buf = i % 2

    def drained(b):
        return pltpu.make_async_copy(cbuf.at[b], o_hbm.at[pl.ds(0, n_slots)], sems.at[b])

    info = info_ref[...]
    sub = lax.broadcasted_iota(jnp.int32, info.shape, 0)
    valid = info[ROUTE_VALID:ROUTE_VALID + 1, :] > 0.5

    def local_slot(e_row, r_row):
        expert = info[e_row:e_row + 1, :].astype(jnp.int32)
        off = jnp.sum(jnp.where(sub == expert, off_ref[:, 0:1], 0.0), axis=0, keepdims=True)
        return jnp.where(valid, off + info[r_row:r_row + 1, :], -1.0)

    slot = lax.broadcasted_iota(jnp.int32, (n_slots, info.shape[1]), 0).astype(F32)
    pick = (jnp.where(slot == local_slot(ROUTE_E1, ROUTE_R1), 1.0, 0.0)
            + jnp.where(slot == local_slot(ROUTE_E2, ROUTE_R2), 1.0, 0.0)).astype(BF16)
    rows = _dot(pick, x_ref[...]).astype(cbuf.dtype)

    @pl.when(i >= 2)
    def _():
        drained(buf).wait()

    cbuf[buf] = rows
    for s in range(ROW_NWIN):
        dst = pl.multiple_of(win_ref[i * ROW_NWIN + s], ROW_WIN)
        pltpu.make_async_copy(cbuf.at[buf, pl.ds(s * ROW_WIN, ROW_WIN)], o_hbm.at[pl.ds(dst, ROW_WIN)],
                              sems.at[buf]).start(priority=s % 2)

    @pl.when(i == last)
    def _():
        drained(buf).wait()

        @pl.when(i >= 1)
        def _():
            drained(1 - buf).wait()


def _row_scatter(win_rows, x, info, local_off, n_rows_alloc):
    n, d = x.shape
    tm = COMBINE_TILE
    init = jnp.zeros((n_rows_alloc, d), x.dtype)
    return pl.pallas_call(
        _row_scatter_kernel,
        grid_spec=pltpu.PrefetchScalarGridSpec(
            num_scalar_prefetch=1, grid=(n // tm,),
            in_specs=[pl.BlockSpec((tm, d), lambda i, w: (i, 0)), pl.BlockSpec((SUBLANES, tm), lambda i, w: (i, 0)),
                      pl.BlockSpec((SUBLANES, LANES), lambda i, w: (i, 0)), pl.BlockSpec(memory_space=pl.ANY)],
            out_specs=pl.BlockSpec(memory_space=pl.ANY),
            scratch_shapes=[pltpu.VMEM((2, ROW_NWIN * ROW_WIN, d), x.dtype), pltpu.SemaphoreType.DMA((2,))]),
        out_shape=jax.ShapeDtypeStruct(init.shape, init.dtype),
        input_output_aliases={4: 0},
        compiler_params=_params("arbitrary"),
        name="moe_row_scatter",
    )(win_rows, x, info, local_off, init)


def _gmm_kernel(te_ref, nu_ref, xs_ref, wg_ref, wu_ref, wd_ref, o_ref, acc_ref):
    p = pl.program_id(0)
    f = pl.program_id(1)
    last = pl.num_programs(1) - 1
    used = p < nu_ref[0]

    @pl.when(used)
    def _():
        x = xs_ref[...]
        g = _dot(x, wg_ref[0])
        mid = g * _sigmoid(g) * _dot(x, wu_ref[0])
        part = _dot(mid.astype(BF16), wd_ref[0])

        @pl.when(f == 0)
        def _():
            acc_ref[...] = part

        @pl.when(f > 0)
        def _():
            acc_ref[...] += part

        @pl.when(f == last)
        def _():
            o_ref[...] = acc_ref[...].astype(o_ref.dtype)

    @pl.when(jnp.logical_not(used) & (f == last))
    def _():
        o_ref[...] = jnp.zeros(o_ref.shape, o_ref.dtype)


def _grouped_ffn(tile_expert, n_used, xs, n_rows, wg, wu, wd):
    d = xs.shape[1]
    ff = wd.shape[1]
    tf = MOE_FF_TILE
    nf = ff // tf
    tg = MOE_TILE
    row = lambda p, f, te, nu: (p, 0)
    fidx = lambda p, f, nu: jnp.where(p < nu[0], f, nf - 1)
    up = pl.BlockSpec((1, d, tf), lambda p, f, te, nu: (te[p], 0, fidx(p, f, nu)))
    return pl.pallas_call(
        _gmm_kernel,
        grid_spec=pltpu.PrefetchScalarGridSpec(
            num_scalar_prefetch=2, grid=(n_rows // tg, nf),
            in_specs=[pl.BlockSpec((tg, d), row), up, up,
                      pl.BlockSpec((1, tf, d), lambda p, f, te, nu: (te[p], fidx(p, f, nu), 0))],
            out_specs=pl.BlockSpec((tg, d), row),
            scratch_shapes=[pltpu.VMEM((tg, d), F32)]),
        out_shape=jax.ShapeDtypeStruct((n_rows, d), BF16),
        compiler_params=_params("arbitrary", "arbitrary"),
        name="moe_grouped_ffn",
    )(tile_expert, n_used, xs, wg, wu, wd)


def _combine_kernel(win_ref, h_ref, tok_ref, w1_ref, w2_ref, starts_ref, rows_ref, ys_hbm, o_ref, gbuf, sems,
                    *scratch, final_rows=None):
    i = pl.program_id(0)
    tm = h_ref.shape[0]
    win = ROW_WIN

    def fetch(tile, slot):
        for s in range(ROW_NWIN):
            start = pl.multiple_of(win_ref[tile * ROW_NWIN + s], win)
            pltpu.make_async_copy(ys_hbm.at[pl.ds(start, win)], gbuf.at[slot, pl.ds(s * win, win)],
                                  sems.at[slot]).start(priority=s % 2)

    @pl.when(i == 0)
    def _():
        fetch(0, 0)

    @pl.when(i + 1 < pl.num_programs(0))
    def _():
        fetch(i + 1, (i + 1) % 2)

    slot = i % 2
    pltpu.make_async_copy(ys_hbm.at[pl.ds(0, ROW_NWIN * win)], gbuf.at[slot], sems.at[slot]).wait()

    tok = tok_ref[...]
    lane = lax.broadcasted_iota(jnp.int32, tok.shape, 1)
    field = lambda c: jnp.sum(jnp.where(lane == c, tok, 0.0), axis=1, keepdims=True)
    valid = field(ROUTE_VALID) > 0.5
    rows = rows_ref[0]
    reps = h_ref.shape[1] // LANES
    out = h_ref[...]
    for e_col, r_col, w_ref in ((ROUTE_E1, ROUTE_R1, w1_ref), (ROUTE_E2, ROUTE_R2, w2_ref)):
        start = jnp.sum(jnp.where(lane == field(e_col).astype(jnp.int32), starts_ref[...], 0.0),
                        axis=1, keepdims=True)
        dest = jnp.where(valid, start + field(r_col), -2.0)
        pick = jnp.where(dest == rows, 1.0, 0.0).astype(BF16)
        out = out + jnp.concatenate([w_ref[...]] * reps, axis=1) * _dot(pick, gbuf[slot])
    if final_rows is None:
        o_ref[...] = out
        return

    obuf, osems = scratch
    n_meta, t_real = final_rows
    bsz, ts = obuf.shape[1], obuf.shape[2]
    steps = h_ref.shape[0] // (bsz * ts)
    assert steps == 1 and n_meta % SUBLANES == 0 and n_meta < ts
    i_last = t_real // ts
    rem = t_real - i_last * ts
    n_steps = pl.num_programs(0)
    oslot = i % 2

    def full_copy(step, s):
        dst = pl.multiple_of(step * ts - n_meta, SUBLANES)
        return pltpu.make_async_copy(obuf.at[s], o_ref.at[:, pl.ds(dst, ts), :], osems.at[s])

    is_full = lambda step: (step >= 1) & (step < i_last)

    @pl.when(is_full(i - 2))
    def _():
        full_copy(i - 2, oslot).wait()

    for t in range(ts):
        obuf[oslot, :, t, :] = out[t * bsz:(t + 1) * bsz, :]

    @pl.when(i == 0)
    def _():
        c = pltpu.make_async_copy(obuf.at[oslot, :, pl.ds(n_meta, ts - n_meta), :],
                                  o_ref.at[:, pl.ds(0, ts - n_meta), :], osems.at[oslot])
        c.start()
        c.wait()

    @pl.when(is_full(i))
    def _():
        full_copy(i, oslot).start()

    if rem:
        @pl.when(i == i_last)
        def _():
            c = pltpu.make_async_copy(obuf.at[oslot, :, pl.ds(0, rem), :],
                                      o_ref.at[:, pl.ds(i_last * ts - n_meta, rem), :], osems.at[oslot])
            c.start()
            c.wait()

    @pl.when(i == n_steps - 1)
    def _():
        @pl.when(is_full(i - 1))
        def _():
            full_copy(i - 1, 1 - oslot).wait()

        @pl.when(is_full(i))
        def _():
            full_copy(i, oslot).wait()


def _combine(win_start, h, tok, w1b, w2b, starts_row, slot_rows, ys, final=None):
    n, d = h.shape
    tm = COMBINE_TILE
    n_slots = ROW_NWIN * ROW_WIN
    row = lambda i, w: (i, 0)
    scratch = [pltpu.VMEM((2, n_slots, d), ys.dtype), pltpu.SemaphoreType.DMA((2,))]
    if final is None:
        kern = _combine_kernel
        out_spec = pl.BlockSpec((tm, d), row)
        out_shape = jax.ShapeDtypeStruct(h.shape, F32)
    else:
        bsz, n_meta, t_real = final
        assert tm % bsz == 0 and bsz % SUBLANES == 0
        kern = functools.partial(_combine_kernel, final_rows=(n_meta, t_real))
        out_spec = pl.BlockSpec(memory_space=pl.ANY)
        out_shape = jax.ShapeDtypeStruct((bsz, t_real - n_meta, d), F32)
        scratch += [pltpu.VMEM((2, bsz, tm // bsz, d), F32), pltpu.SemaphoreType.DMA((2,))]
    return pl.pallas_call(
        kern,
        grid_spec=pltpu.PrefetchScalarGridSpec(
            num_scalar_prefetch=1, grid=(n // tm,),
            in_specs=[pl.BlockSpec((tm, d), row), pl.BlockSpec((tm, LANES), row), pl.BlockSpec((tm, LANES), row),
                      pl.BlockSpec((tm, LANES), row), pl.BlockSpec((1, LANES), lambda i, w: (0, 0)),
                      pl.BlockSpec((1, 1, n_slots), lambda i, w: (i, 0, 0)),
                      pl.BlockSpec(memory_space=pl.ANY)],
            out_specs=out_spec,
            scratch_shapes=scratch),
        out_shape=out_shape,
        compiler_params=_params("arbitrary"),
        name="moe_combine",
    )(win_start, h, tok, w1b, w2b, starts_row, slot_rows, ys)


def _gate_windows(lw, bd):
    out = []
    for c0 in range(0, lw, LRU_GATE_TILE):
        nc = min(LRU_GATE_TILE, lw - c0)
        r0 = (c0 // bd) * bd
        r1 = ((c0 + nc - 1) // bd + 1) * bd
        r0 = (r0 // LANES) * LANES
        r1 = min(-(-r1 // LANES) * LANES, lw)
        out.append((c0, nc, r0, r1 - r0))
    return out


def _lru_kernel(h_ref, ng_ref, win_ref, cw_ref, cb_ref, wg_ref, br_ref, bi_ref, lam_ref, wo_ref,
                o_ref, xbuf_ref, hs_ref, a_ref, b_ref, y_ref, gate_ref, res_ref, *, windows, bsz):
    ti = pl.program_id(0)
    rows, lw = a_ref.shape
    halo = (LRU_CONV_WIDTH - 1) * bsz

    @pl.when(ti == 0)
    def _():
        xbuf_ref[0:halo, :] = jnp.zeros((halo, lw), F32)
        hs_ref[...] = jnp.zeros(hs_ref.shape, F32)

    @pl.when(ti > 0)
    def _():
        xbuf_ref[0:halo, :] = xbuf_ref[rows:rows + halo, :]

    res_ref[...] = jnp.concatenate([h_ref[:, t, :] for t in range(rows // bsz)], axis=0)
    proj = _dot(_rms(res_ref[...], ng_ref[...]).astype(BF16), win_ref[...])
    gate_ref[...] = jax.nn.gelu(proj[:, 0:lw], approximate=True)
    xbuf_ref[halo:halo + rows, :] = proj[:, lw:2 * lw]
    xc = jnp.broadcast_to(cb_ref[...], (rows, lw))
    for k in range(LRU_CONV_WIDTH):
        xc = xc + cw_ref[k:k + 1, :] * xbuf_ref[k * bsz:k * bsz + rows, :]
    xcb = xc.astype(BF16)
    sp_lam = jnp.log(1.0 + jnp.exp(-lam_ref[...]))
    log2_a_per_r = (-LRU_C * LOG2E) * sp_lam
    wcol = 0
    for (c0, nc, r0, nr) in windows:
        rg = _dot(xcb[:, r0:r0 + nr], wg_ref[r0:r0 + nr, wcol:wcol + 2 * nc])
        wcol += 2 * nc
        r = _sigmoid(rg[:, 0:nc] + br_ref[:, c0:c0 + nc])
        ig = _sigmoid(rg[:, nc:2 * nc] + bi_ref[:, c0:c0 + nc])
        a = jnp.exp2(r * log2_a_per_r[:, c0:c0 + nc])
        a_ref[:, c0:c0 + nc] = a
        one_m_a2 = 1.0 - a * a
        root = jnp.where(one_m_a2 > 0.0, one_m_a2 * lax.rsqrt(one_m_a2), 0.0)
        b_ref[:, c0:c0 + nc] = root * (ig * xc[:, c0:c0 + nc])

    def step(t, hprev):
        r0 = pl.multiple_of(t * bsz, SUBLANES)
        hnew = a_ref[pl.ds(r0, bsz), :] * hprev + b_ref[pl.ds(r0, bsz), :]
        y_ref[pl.ds(r0, bsz), :] = hnew
        return hnew

    hs_ref[...] = lax.fori_loop(0, rows // bsz, step, hs_ref[...], unroll=LRU_SCAN_UNROLL)
    gy = (gate_ref[...] * y_ref[...]).astype(BF16)
    o_ref[...] = res_ref[...] + _dot(gy, wo_ref[...])


def _lru_mixer(h, ng, win, cw, cb, wg, br, bi, lam, wo, windows):
    bsz, tp, d = h.shape
    n = bsz * tp
    lw = lam.shape[-1]
    assert bsz % SUBLANES == 0 and LRU_ROWS % bsz == 0 and n % LRU_ROWS == 0
    rows = LRU_ROWS
    halo = (LRU_CONV_WIDTH - 1) * bsz
    const = lambda t: (0, 0)
    blk = lambda t: (t, 0)
    resident = lambda w: pl.BlockSpec(w.shape, const, pipeline_mode=pl.Buffered(1))
    return pl.pallas_call(
        functools.partial(_lru_kernel, windows=windows, bsz=bsz),
        grid=(n // rows,),
        in_specs=[pl.BlockSpec((bsz, rows // bsz, d), lambda t: (0, t, 0)), pl.BlockSpec((1, d), const), resident(win),
                  pl.BlockSpec(cw.shape, const), pl.BlockSpec((1, lw), const), resident(wg),
                  pl.BlockSpec((1, lw), const), pl.BlockSpec((1, lw), const),
                  pl.BlockSpec((1, lw), const), resident(wo)],
        out_specs=pl.BlockSpec((rows, d), blk),
        out_shape=jax.ShapeDtypeStruct((n, d), F32),
        scratch_shapes=[pltpu.VMEM((halo + rows, lw), F32), pltpu.VMEM((bsz, lw), F32),
                        pltpu.VMEM((rows, lw), F32), pltpu.VMEM((rows, lw), F32), pltpu.VMEM((rows, lw), F32),
                        pltpu.VMEM((rows, lw), F32), pltpu.VMEM((rows, d), F32)],
        compiler_params=_params("arbitrary"),
        name="rglru_mixer",
    )(h, ng, win, cw, cb, wg, br, bi, lam, wo)


ROUTE_E1, ROUTE_E2, ROUTE_R1, ROUTE_R2, ROUTE_VALID = range(5)


def _router_kernel(h_ref, g_ref, rw_ref, ltri_ref, xn_ref, info_ref, tok_ref, w1_ref, w2_ref, before_ref, cnt_ref,
                   run_ref, *, n_real_rows):
    i = pl.program_id(0)
    tm = h_ref.shape[0]

    @pl.when(i == 0)
    def _():
        run_ref[...] = jnp.zeros(run_ref.shape, F32)

    xn = _rms(h_ref[...], g_ref[...])
    xn_ref[...] = xn.astype(BF16)
    split = lambda a: (a.astype(BF16), (a - a.astype(BF16).astype(F32)).astype(BF16))
    (x_hi, x_lo), (r_hi, r_lo) = split(xn), split(rw_ref[...])
    logits = _dot(x_hi, r_hi) + (_dot(x_hi, r_lo) + _dot(x_lo, r_hi))
    lane = lax.broadcasted_iota(jnp.int32, logits.shape, 1)
    neg = jnp.float32(-jnp.inf)
    logits = jnp.where(lane < N_EXPERTS, logits, neg)
    top1 = jnp.max(logits, axis=1, keepdims=True)
    idx1 = jnp.min(jnp.where(logits == top1, lane, LANES), axis=1, keepdims=True)
    rest = jnp.where(lane == idx1, neg, logits)
    top2 = jnp.max(rest, axis=1, keepdims=True)
    idx2 = jnp.min(jnp.where(rest == top2, lane, LANES), axis=1, keepdims=True)
    w1 = 1.0 / (1.0 + jnp.exp(top2 - top1))

    rowg = lax.broadcasted_iota(jnp.int32, (tm, 1), 0) + i * tm
    valid = jnp.where(rowg < n_real_rows, 1.0, 0.0)
    onehot = (jnp.where(lane == idx1, 1.0, 0.0) + jnp.where(lane == idx2, 1.0, 0.0)) * valid
    before_ref[...] = jnp.broadcast_to(run_ref[...], before_ref.shape)
    before = _dot(ltri_ref[...], onehot.astype(BF16)) + run_ref[...]
    rank1 = jnp.sum(jnp.where(lane == idx1, before, 0.0), axis=1, keepdims=True)
    rank2 = jnp.sum(jnp.where(lane == idx2, before, 0.0), axis=1, keepdims=True)
    run_ref[...] += jnp.ceil(jnp.sum(onehot, axis=0, keepdims=True) / ROW_WIN) * ROW_WIN
    cnt_ref[...] = run_ref[...]
    w1_ref[...] = jnp.broadcast_to(w1, (tm, LANES))
    w2_ref[...] = jnp.broadcast_to(1.0 - w1, (tm, LANES))
    fields = {ROUTE_E1: idx1.astype(F32), ROUTE_E2: idx2.astype(F32),
              ROUTE_R1: rank1, ROUTE_R2: rank2, ROUTE_VALID: valid}
    info = jnp.zeros(logits.shape, F32)
    for col, val in fields.items():
        info = jnp.where(lane == col, val, info)
    tok_ref[...] = info
    info_ref[...] = info.T[0:SUBLANES, :]


def _router(h, g, rw, ltri, n_real_rows):
    n, d = h.shape
    tm = ROW_TILE
    row = lambda i: (i, 0)
    const = lambda i: (0, 0)
    return pl.pallas_call(
        functools.partial(_router_kernel, n_real_rows=n_real_rows),
        grid=(n // tm,),
        in_specs=[pl.BlockSpec((tm, d), row), pl.BlockSpec((1, d), const), pl.BlockSpec(rw.shape, const),
                  pl.BlockSpec(ltri.shape, const)],
        out_specs=[pl.BlockSpec((tm, d), row), pl.BlockSpec((SUBLANES, tm), row),
                   pl.BlockSpec((tm, LANES), row), pl.BlockSpec((tm, LANES), row),
                   pl.BlockSpec((tm, LANES), row), pl.BlockSpec((SUBLANES, LANES), row),
                   pl.BlockSpec((1, LANES), const)],
        out_shape=[jax.ShapeDtypeStruct((n, d), BF16), jax.ShapeDtypeStruct((n // tm * SUBLANES, tm), F32),
                   jax.ShapeDtypeStruct((n, LANES), F32), jax.ShapeDtypeStruct((n, LANES), F32),
                   jax.ShapeDtypeStruct((n, LANES), F32), jax.ShapeDtypeStruct((n // tm * SUBLANES, LANES), F32),
                   jax.ShapeDtypeStruct((1, LANES), F32)],
        scratch_shapes=[pltpu.VMEM((1, LANES), F32)],
        compiler_params=_params("arbitrary"),
        name="router",
    )(h, g, rw, ltri)


def _routing_tables(before, counts, n_rows):
    tg = MOE_TILE
    win = ROW_WIN
    cnt = counts[0, :N_EXPERTS].astype(jnp.int32)
    padded = (cnt + tg - 1) // tg * tg
    ends = jnp.cumsum(padded)
    starts = ends - padded
    tile_start = jnp.arange(n_rows // tg, dtype=jnp.int32) * tg
    tile_expert = jnp.minimum(jnp.sum(tile_start[:, None] >= ends[None, :], axis=1), N_EXPERTS - 1)
    n_used = (ends[-1] // tg).reshape(1)
    bef = before.reshape(-1, SUBLANES, LANES)[:, 0, :N_EXPERTS].astype(jnp.int32)
    nwin = (jnp.concatenate([bef[1:], cnt[None, :]], axis=0) - bef) // win
    base = jnp.cumsum(nwin, axis=1) - nwin
    slot = jnp.arange(ROW_NWIN, dtype=jnp.int32)[None, :, None]
    mine = (slot >= base[:, None, :]) & (slot < (base + nwin)[:, None, :])
    rows = jnp.sum(jnp.where(mine, (starts[None, :] + bef)[:, None, :] + win * (slot - base[:, None, :]), 0), axis=2)
    used = jnp.any(mine, axis=2)
    parity = (jnp.arange(bef.shape[0], dtype=jnp.int32) % 2)[:, None]
    scatter_rows = jnp.where(used, rows, n_rows + win * (parity * ROW_NWIN + slot[:, :, 0]))
    gather_rows = jnp.where(used, rows, 0)
    slot_rows = jnp.where(used[..., None], rows[..., None] + jnp.arange(win, dtype=jnp.int32), -1)
    slot_rows = slot_rows.reshape(-1, 1, ROW_NWIN * win).astype(F32)
    starts_row = jnp.zeros((1, LANES), F32).at[0, :N_EXPERTS].set(starts.astype(F32))
    local_off = jnp.zeros(bef.shape[:1] + (SUBLANES,), jnp.int32).at[:, :N_EXPERTS].set(win * base - bef)
    local_off = jnp.broadcast_to(local_off.astype(F32)[:, :, None], local_off.shape + (LANES,)).reshape(-1, LANES)
    return (tile_expert.astype(jnp.int32), n_used.astype(jnp.int32), scatter_rows.reshape(-1).astype(jnp.int32),
            gather_rows.reshape(-1).astype(jnp.int32), slot_rows, starts_row, local_off)


def _interleave_gate_up(wg, wu, tf):
    e, d, ff = wg.shape
    g = wg.astype(BF16).reshape(e, d, ff // tf, tf)
    u = wu.astype(BF16).reshape(e, d, ff // tf, tf)
    return jnp.concatenate([g, u], axis=3).reshape(e, d, 2 * ff)


def _dense_block_diag(w):
    nb, bd, _ = w.shape
    eye = jnp.eye(nb, dtype=w.dtype)
    return (w[:, :, None, :] * eye[:, None, :, None]).reshape(nb * bd, nb * bd)


def _pack_lru_gates(wr, wi, windows):
    dr = _dense_block_diag(wr).astype(BF16)
    di = _dense_block_diag(wi).astype(BF16)
    cols = []
    for (c0, nc, _, _) in windows:
        cols += [dr[:, c0:c0 + nc], di[:, c0:c0 + nc]]
    return jnp.concatenate(cols, axis=1)


def _ffn_chunk(ff):
    for tf in (896, 512, 384, 256, 128):
        if ff % tf == 0:
            return tf
    raise ValueError(f"unsupported d_ff {ff}")


def kernel(x, meta_tokens, mix_norm_even, w_in_even, conv_w, conv_b, conv_ln_g, conv_ln_b, q_norm_g, k_norm_g, w_out_even, ffn_norm_even, ffn_w_gate, ffn_w_up, ffn_w_down, mix_norm_odd, w_in_odd, lru_conv_w, lru_conv_b, gate_r_w, gate_r_b, gate_i_w, gate_i_b, lru_lambda, w_out_odd, ffn_norm_odd, router_w, moe_w_gate, moe_w_up, moe_w_down):
    bsz, seq, d = x.shape
    t_real = N_META + seq
    tp = -(-t_real // TIME_TILE) * TIME_TILE
    n = bsz * tp
    assert n % ROW_TILE == 0
    depth = mix_norm_even.shape[0] + mix_norm_odd.shape[0]

    meta = jnp.broadcast_to(meta_tokens[None].astype(x.dtype), (bsz, N_META, d))
    h = jnp.concatenate([meta, x, jnp.zeros((bsz, tp - t_real, d), x.dtype)], axis=1).reshape(n, d)

    head_mean = jnp.kron(jnp.eye(MXU_DIM // SB_HEAD_DIM, dtype=F32),
                         jnp.full((SB_HEAD_DIM, SB_HEAD_DIM), 1.0 / SB_HEAD_DIM, F32)).astype(BF16)
    kk = jnp.arange(ATT_TK)
    neg_tri = -(kk[:, None] >= kk[None, :]).astype(BF16)
    row2 = lambda a: a.reshape(1, -1)

    time_major = False
    for layer in range(depth):
        p = layer // 2
        if layer % 2 == 0:
            if time_major:
                h = h.reshape(tp, bsz, d).transpose(1, 0, 2).reshape(n, d)
                time_major = False
            u, q, k, v = _even_in_proj(h, row2(mix_norm_even[p]), w_in_even[p].astype(BF16),
                                       row2(jnp.tile(q_norm_g[p], SB_HEADS)),
                                       row2(jnp.tile(k_norm_g[p], SB_HEADS)), head_mean)
            u = _conv_module(u.reshape(bsz, tp, CONV_CH), conv_w[p], row2(conv_b[p]),
                             row2(conv_ln_g[p]), row2(conv_ln_b[p]))
            o = _attention(q.reshape(bsz, tp, SB_WIDTH), k.reshape(bsz, tp, SB_WIDTH),
                           v.reshape(bsz, tp, SB_WIDTH), neg_tri, t_real)
            tf = _ffn_chunk(ffn_w_gate.shape[-1])
            h = _even_ffn(h, u.reshape(n, CONV_CH), o.reshape(n, SB_WIDTH), w_out_even[p].astype(BF16),
                          row2(ffn_norm_even[p]),
                          _interleave_gate_up(ffn_w_gate[p][None], ffn_w_up[p][None], tf)[0],
                          ffn_w_down[p].astype(BF16), tf)
        else:
            if time_major:
                h = h.reshape(tp, bsz, d).transpose(1, 0, 2).reshape(n, d)
            time_major = True
            lw = lru_lambda.shape[-1]
            windows = _gate_windows(lw, lw // LRU_BLOCKS)
            h = _lru_mixer(h.reshape(bsz, tp, d), row2(mix_norm_odd[p]), w_in_odd[p].astype(BF16),
                           lru_conv_w[p], row2(lru_conv_b[p]),
                           _pack_lru_gates(gate_r_w[p], gate_i_w[p], windows),
                           row2(gate_r_b[p]), row2(gate_i_b[p]), row2(lru_lambda[p]),
                           w_out_odd[p].astype(BF16), windows)
            rw = jnp.pad(router_w[p], ((0, 0), (0, LANES - N_EXPERTS)))
            kk = jnp.arange(ROW_TILE)
            ltri = (kk[:, None] > kk[None, :]).astype(BF16)
            xn, info, tok, w1b, w2b, before, counts = _router(h, row2(ffn_norm_odd[p]), rw, ltri, t_real * bsz)
            tg = MOE_TILE
            assert ROW_TILE == COMBINE_TILE
            n_tiles = n // COMBINE_TILE
            n_rows = 2 * bsz * t_real + n_tiles * N_EXPERTS * (ROW_WIN - 1) + N_EXPERTS * (tg - 1)
            n_rows = -(-n_rows // tg) * tg
            tile_expert, n_used, scatter_rows, gather_rows, slot_rows, starts_row, local_off = _routing_tables(
                before, counts, n_rows)
            xs = _row_scatter(scatter_rows, xn, info, local_off, n_rows + 2 * ROW_NWIN * ROW_WIN)
            ys = _grouped_ffn(tile_expert, n_used, xs, n_rows, moe_w_gate[p].astype(BF16),
                              moe_w_up[p].astype(BF16), moe_w_down[p].astype(BF16))
            if layer + 1 == depth:
                return _combine(gather_rows, h, tok, w1b, w2b, starts_row, slot_rows, ys, (bsz, N_META, t_real))
            h = _combine(gather_rows, h, tok, w1b, w2b, starts_row, slot_rows, ys)
    if time_major:
        return h.reshape(tp, bsz, d)[N_META:t_real].transpose(1, 0, 2)
    return h.reshape(bsz, tp, d)[:, N_META:t_real]
```

```python
import functools
import math

import jax
import jax.numpy as jnp
from jax import lax
from jax.experimental import pallas as pl
from jax.experimental.pallas import tpu as pltpu

F32 = jnp.float32
BF16 = jnp.bfloat16

EPS = 1e-6
N_META = 16
CONV_CH = 512
CONV_WIDTH = 31
SB_HEADS = 8
SB_HEAD_DIM = 64
SB_WIDTH = SB_HEADS * SB_HEAD_DIM
LRU_BLOCKS = 16
LRU_CONV_WIDTH = 4
LRU_C = 8.0
N_EXPERTS = 8

LANES = 128
MXU_DIM = 256
TIME_TILE = 256
VMEM_LIMIT = 50 * 1024 * 1024

ROW_TILE = 512
CONV_TT = 256
CONV_HALO = 32
CONV_CHUNK = 32
ATT_TQ = 256
ATT_TK = 256
ATT_UNROLL = 2
LOG2E = 1.4426950408889634
MASKED_EXPONENT = -1e30
SOFTPLUS_CUTOFF = 126.0
LRU_ROWS = 512
LRU_SCAN_UNROLL = 4
LRU_GATE_TILE = 256
MOE_TILE = 512
MOE_FF_TILE = 1792
COMBINE_TILE = 512
ROW_WIN = 16
ROW_NWIN = 2 * COMBINE_TILE // ROW_WIN + N_EXPERTS
SUBLANES = 8
BF16_ROWS = 16


def _dot(a, b):
    return jnp.dot(a, b, preferred_element_type=F32)


def _params(*sem):
    return pltpu.CompilerParams(dimension_semantics=sem, vmem_limit_bytes=VMEM_LIMIT)


def _sigmoid(x):
    return 0.5 * jnp.tanh(0.5 * x) + 0.5


def _rms(x, g):
    ms = jnp.mean(x * x, axis=-1, keepdims=True)
    return x * lax.rsqrt(ms + EPS) * g


def _even_in_kernel(h_ref, g_ref, w_ref, qg_ref, kg_ref, hm_ref, u_ref, q_ref, k_ref, v_ref):
    xn = _rms(h_ref[...], g_ref[...]).astype(BF16)
    a = _dot(xn, w_ref[:, 0:CONV_CH])
    gate = _dot(xn, w_ref[:, CONV_CH:2 * CONV_CH])
    u_ref[...] = (a * _sigmoid(gate)).astype(BF16)

    def head_norm(y, gain):
        yy = (y * y).astype(BF16)
        parts = [_dot(yy[:, c:c + MXU_DIM], hm_ref[...]) for c in range(0, SB_WIDTH, MXU_DIM)]
        ms = jnp.concatenate(parts, axis=1)
        return y * lax.rsqrt(ms + EPS) * gain

    c0 = 2 * CONV_CH
    q = head_norm(_dot(xn, w_ref[:, c0:c0 + SB_WIDTH]), qg_ref[...])
    q_ref[...] = (q * (LOG2E / math.sqrt(SB_HEAD_DIM))).astype(BF16)
    k = head_norm(_dot(xn, w_ref[:, c0 + SB_WIDTH:c0 + 2 * SB_WIDTH]), kg_ref[...])
    k_ref[...] = k.astype(BF16)
    v_ref[...] = _dot(xn, w_ref[:, c0 + 2 * SB_WIDTH:c0 + 3 * SB_WIDTH]).astype(BF16)


def _even_in_proj(h, g, w, qg, kg, hm):
    n, d = h.shape
    tm = ROW_TILE
    row = lambda i: (i, 0)
    const = lambda i: (0, 0)
    out = jax.ShapeDtypeStruct((n, SB_WIDTH), BF16)
    return pl.pallas_call(
        _even_in_kernel,
        grid=(n // tm,),
        in_specs=[pl.BlockSpec((tm, d), row), pl.BlockSpec((1, d), const),
                  pl.BlockSpec(w.shape, const), pl.BlockSpec((1, SB_WIDTH), const),
                  pl.BlockSpec((1, SB_WIDTH), const), pl.BlockSpec(hm.shape, const)],
        out_specs=[pl.BlockSpec((tm, SB_WIDTH), row)] * 4,
        out_shape=[out] * 4,
        compiler_params=_params("parallel"),
        name="even_in_proj",
    )(h, g, w, qg, kg, hm)


def _conv_kernel(cur_ref, halo_ref, w_ref, b_ref, lg_ref, lb_ref, o_ref, buf_ref, sh_ref):
    i = pl.program_id(1)
    tt = cur_ref.shape[1]
    rows = CONV_HALO + tt
    halo = halo_ref[0].astype(F32)
    buf_ref[0:CONV_HALO, :] = jnp.where(i > 0, halo, 0.0)
    buf_ref[CONV_HALO:rows, :] = cur_ref[0].astype(F32)
    for s in range(1, SUBLANES):
        sh_ref[s - 1, 0:rows - SUBLANES, :] = buf_ref[s:s + rows - SUBLANES, :]
    off = CONV_HALO - (CONV_WIDTH - 1)
    for c in range(tt // CONV_CHUNK):
        r0 = c * CONV_CHUNK
        acc = jnp.broadcast_to(b_ref[...], (CONV_CHUNK, CONV_CH))
        for k in range(CONV_WIDTH):
            shift = (off + k) % SUBLANES
            base = r0 + off + k - shift
            if shift == 0:
                tap = buf_ref[base:base + CONV_CHUNK, :]
            else:
                tap = sh_ref[shift - 1, base:base + CONV_CHUNK, :]
            acc = acc + w_ref[k:k + 1, :] * tap
        mu = jnp.mean(acc, axis=-1, keepdims=True)
        xc = acc - mu
        var = jnp.mean(xc * xc, axis=-1, keepdims=True)
        y = xc * lax.rsqrt(var + EPS) * lg_ref[...] + lb_ref[...]
        o_ref[0, r0:r0 + CONV_CHUNK, :] = (y * _sigmoid(y)).astype(BF16)


def _conv_module(u, w, b, lg, lb):
    bsz, tp, c = u.shape
    tt = CONV_TT
    per = tt // CONV_HALO
    const = lambda bi, i: (0, 0)
    return pl.pallas_call(
        _conv_kernel,
        grid=(bsz, tp // tt),
        in_specs=[pl.BlockSpec((1, tt, c), lambda bi, i: (bi, i, 0)),
                  pl.BlockSpec((1, CONV_HALO, c), lambda bi, i: (bi, jnp.maximum(i * per - 1, 0), 0)),
                  pl.BlockSpec(w.shape, const), pl.BlockSpec((1, c), const),
                  pl.BlockSpec((1, c), const), pl.BlockSpec((1, c), const)],
        out_specs=pl.BlockSpec((1, tt, c), lambda bi, i: (bi, i, 0)),
        out_shape=jax.ShapeDtypeStruct(u.shape, BF16),
        scratch_shapes=[pltpu.VMEM((CONV_HALO + tt, c), F32),
                        pltpu.VMEM((SUBLANES - 1, CONV_HALO + tt - SUBLANES, c), F32)],
        compiler_params=_params("parallel", "parallel"),
        name="conv_module",
    )(u, u, w, b, lg, lb)


def _attn_kernel(*refs, tq, tile0, aliased, n_compute):
    if aliased:
        refs = refs[1:]
    o_ref = refs[4]
    step = pl.program_id(1)

    @pl.when(step < n_compute)
    def _():
        _attn_tile(*refs, tq=tq, row0=(step + tile0) * ATT_TQ)

    @pl.when(step >= n_compute)
    def _():
        o_ref[...] = jnp.zeros(o_ref.shape, BF16)


def _attn_tile(q_ref, k_ref, v_ref, tri_ref, o_ref, q2_ref, e_ref, tot_ref, acc_ref, car_ref, *, tq, row0):
    tk = ATT_TK
    npairs = q_ref.shape[2] // LANES
    lane_q = lax.broadcasted_iota(jnp.int32, (tq, LANES), 1)

    for hp in range(npairs):
        q = q_ref[0, 0:tq, hp * LANES:(hp + 1) * LANES]
        zero = jnp.zeros_like(q)
        q2_ref[hp, 0:tq, :] = jnp.where(lane_q < SB_HEAD_DIM, q, zero)
        q2_ref[hp, tq:2 * tq, :] = jnp.where(lane_q >= SB_HEAD_DIM, q, zero)
        car_ref[hp] = jnp.zeros((2 * tq, LANES), F32)
        acc_ref[hp] = jnp.zeros((2 * tq, LANES), F32)

    def stage_a(j, mask, slot):
        k0 = pl.multiple_of(j * tk, tk)
        for hp in range(npairs):
            kt = k_ref[0, pl.ds(k0, tk), hp * LANES:(hp + 1) * LANES]
            z = lax.dot_general(q2_ref[hp], kt, (((1,), (1,)), ((), ())), preferred_element_type=F32)
            sp = jnp.where(z > SOFTPLUS_CUTOFF, z, jnp.log(1.0 + jnp.exp2(z)) * LOG2E)
            if mask is not None:
                sp = jnp.where(mask, sp, 0.0)
            e = z + _dot(sp.astype(BF16), tri_ref[...])
            if mask is not None:
                e = jnp.where(mask, e, MASKED_EXPONENT)
            e_ref[slot, hp] = e
            tot_ref[slot, hp] = jnp.broadcast_to(jnp.sum(sp, axis=1, keepdims=True), (2 * tq, LANES))

    def stage_b(j, slot):
        k0 = pl.multiple_of(j * tk, tk)
        for hp in range(npairs):
            car = car_ref[hp]
            w = jnp.exp2(e_ref[slot, hp] + jnp.concatenate([car] * (tk // LANES), axis=1)).astype(BF16)
            vt = v_ref[0, pl.ds(k0, tk), hp * LANES:(hp + 1) * LANES]
            acc_ref[hp] += _dot(w, vt)
            car_ref[hp] = car - tot_ref[slot, hp]

    j_last = row0 // tk
    row = lax.broadcasted_iota(jnp.int32, (2 * tq, tk), 0)
    col = lax.broadcasted_iota(jnp.int32, (2 * tq, tk), 1)
    row = jnp.where(row >= tq, row - tq, row) + row0
    stage_a(j_last, (col + j_last * tk) < row, j_last % 2)
    lead = j_last % ATT_UNROLL
    for s in range(ATT_UNROLL - 1):
        @pl.when(lead > s)
        def _(s=s):
            staged = j_last - s
            stage_b(staged, staged % 2)
            stage_a(staged - 1, None, (staged - 1) % 2)

    def body(jj, carry):
        staged = j_last - lead - ATT_UNROLL * jj
        for s in range(ATT_UNROLL):
            stage_b(staged - s, s % 2)
            stage_a(staged - s - 1, None, (s + 1) % 2)
        return carry

    lax.fori_loop(0, (j_last - lead) // ATT_UNROLL, body, 0)
    stage_b(0, 0)
    if tq < o_ref.shape[1]:
        o_ref[0, tq:, :] = jnp.zeros((o_ref.shape[1] - tq, o_ref.shape[2]), BF16)
    for hp in range(npairs):
        o_ref[0, 0:tq, hp * LANES:(hp + 1) * LANES] = jnp.where(
            lane_q < SB_HEAD_DIM, acc_ref[hp, 0:tq, :], acc_ref[hp, tq:2 * tq, :]).astype(BF16)


def _attention(q, k, v, tri, t_real):
    bsz, tp, width = q.shape
    bq, tk = ATT_TQ, ATT_TK
    assert bq <= tk and tk % bq == 0
    npairs = width // LANES
    n_full = t_real // bq
    rem = -(-(t_real - n_full * bq) // BF16_ROWS) * BF16_ROWS

    def call(tq, tile0, n_compute, n_tiles, prev):
        blk = lambda b, i: (b, i + tile0, 0)
        full = lambda b, i: (b, 0, 0)
        in_specs = [pl.BlockSpec((1, bq, width), blk), pl.BlockSpec((1, tp, width), full),
                    pl.BlockSpec((1, tp, width), full), pl.BlockSpec(tri.shape, lambda b, i: (0, 0))]
        args = [q, k, v, tri]
        if prev is not None:
            in_specs.insert(0, pl.BlockSpec(memory_space=pl.ANY))
            args.insert(0, prev)
        return pl.pallas_call(
            functools.partial(_attn_kernel, tq=tq, tile0=tile0, aliased=prev is not None, n_compute=n_compute),
            grid=(bsz, n_tiles),
            in_specs=in_specs,
            out_specs=pl.BlockSpec((1, bq, width), blk),
            out_shape=jax.ShapeDtypeStruct(q.shape, BF16),
            input_output_aliases={} if prev is None else {0: 0},
            scratch_shapes=[pltpu.VMEM((npairs, 2 * tq, LANES), BF16),
                            pltpu.VMEM((2, npairs, 2 * tq, tk), F32),
                            pltpu.VMEM((2, npairs, 2 * tq, LANES), F32),
                            pltpu.VMEM((npairs, 2 * tq, LANES), F32),
                            pltpu.VMEM((npairs, 2 * tq, LANES), F32)],
            compiler_params=_params("parallel", "arbitrary"),
            name="stick_breaking" if prev is None else "stick_breaking_tail",
        )(*args)

    out = call(bq, 0, n_full, tp // bq, None)
    if rem:
        out = call(rem, n_full, 1, 1, out)
    return out


def _even_ffn_kernel(h_ref, u_ref, o_ref, wo_ref, g_ref, wgu_ref, wd_ref, out_ref, xn_ref, *, tf):
    @pl.when(pl.program_id(1) == 0)
    def _():
        h1 = h_ref[...] + _dot(u_ref[...], wo_ref[0:CONV_CH, :]) + _dot(o_ref[...], wo_ref[CONV_CH:, :])
        out_ref[...] = h1
        xn_ref[...] = _rms(h1, g_ref[...]).astype(BF16)

    gu = _dot(xn_ref[...], wgu_ref[...])
    g = gu[:, 0:tf]
    mid = g * _sigmoid(g) * gu[:, tf:2 * tf]
    out_ref[...] += _dot(mid.astype(BF16), wd_ref[...])


def _even_ffn(h, u, o, wo, g, wgu, wd, tf):
    n, d = h.shape
    ff = wd.shape[0]
    tm = ROW_TILE
    row = lambda i, f: (i, 0)
    const = lambda i, f: (0, 0)
    return pl.pallas_call(
        functools.partial(_even_ffn_kernel, tf=tf),
        grid=(n // tm, ff // tf),
        in_specs=[pl.BlockSpec((tm, d), row), pl.BlockSpec((tm, CONV_CH), row),
                  pl.BlockSpec((tm, SB_WIDTH), row), pl.BlockSpec(wo.shape, const),
                  pl.BlockSpec((1, d), const),
                  pl.BlockSpec((d, 2 * tf), lambda i, f: (0, f)),
                  pl.BlockSpec((tf, d), lambda i, f: (f, 0))],
        out_specs=pl.BlockSpec((tm, d), row),
        out_shape=jax.ShapeDtypeStruct((n, d), F32),
        scratch_shapes=[pltpu.VMEM((tm, d), BF16)],
        compiler_params=_params("parallel", "arbitrary"),
        name="even_out_swiglu",
    )(h, u, o, wo, g, wgu, wd)


def _row_scatter_kernel(win_ref, x_ref, info_ref, off_ref, init_hbm, o_hbm, cbuf, sems):
    del init_hbm
    i = pl.program_id(0)
    last = pl.num_programs(0) - 1
    n_slots = cbuf.shape[1]
    buf = i % 2

    def drained(b):
        return pltpu.make_async_copy(cbuf.at[b], o_hbm.at[pl.ds(0, n_slots)], sems.at[b])

    info = info_ref[...]
    sub = lax.broadcasted_iota(jnp.int32, info.shape, 0)
    valid = info[ROUTE_VALID:ROUTE_VALID + 1, :] > 0.5

    def local_slot(e_row, r_row):
        expert = info[e_row:e_row + 1, :].astype(jnp.int32)
        off = jnp.sum(jnp.where(sub == expert, off_ref[:, 0:1], 0.0), axis=0, keepdims=True)
        return jnp.where(valid, off + info[r_row:r_row + 1, :], -1.0)

    slot = lax.broadcasted_iota(jnp.int32, (n_slots, info.shape[1]), 0).astype(F32)
    pick = (jnp.where(slot == local_slot(ROUTE_E1, ROUTE_R1), 1.0, 0.0)
            + jnp.where(slot == local_slot(ROUTE_E2, ROUTE_R2), 1.0, 0.0)).astype(BF16)
    rows = _dot(pick, x_ref[...]).astype(cbuf.dtype)

    @pl.when(i >= 2)
    def _():
        drained(buf).wait()

    cbuf[buf] = rows
    for s in range(ROW_NWIN):
        dst = pl.multiple_of(win_ref[i * ROW_NWIN + s], ROW_WIN)
        pltpu.make_async_copy(cbuf.at[buf, pl.ds(s * ROW_WIN, ROW_WIN)], o_hbm.at[pl.ds(dst, ROW_WIN)],
                              sems.at[buf]).start(priority=s % 2)

    @pl.when(i == last)
    def _():
        drained(buf).wait()

        @pl.when(i >= 1)
        def _():
            drained(1 - buf).wait()


def _row_scatter(win_rows, x, info, local_off, n_rows_alloc):
    n, d = x.shape
    tm = COMBINE_TILE
    init = jnp.zeros((n_rows_alloc, d), x.dtype)
    return pl.pallas_call(
        _row_scatter_kernel,
        grid_spec=pltpu.PrefetchScalarGridSpec(
            num_scalar_prefetch=1, grid=(n // tm,),
            in_specs=[pl.BlockSpec((tm, d), lambda i, w: (i, 0)), pl.BlockSpec((SUBLANES, tm), lambda i, w: (i, 0)),
                      pl.BlockSpec((SUBLANES, LANES), lambda i, w: (i, 0)), pl.BlockSpec(memory_space=pl.ANY)],
            out_specs=pl.BlockSpec(memory_space=pl.ANY),
            scratch_shapes=[pltpu.VMEM((2, ROW_NWIN * ROW_WIN, d), x.dtype), pltpu.SemaphoreType.DMA((2,))]),
        out_shape=jax.ShapeDtypeStruct(init.shape, init.dtype),
        input_output_aliases={4: 0},
        compiler_params=_params("arbitrary"),
        name="moe_row_scatter",
    )(win_rows, x, info, local_off, init)


def _gmm_kernel(te_ref, nu_ref, xs_ref, wg_ref, wu_ref, wd_ref, o_ref, acc_ref):
    p = pl.program_id(0)
    f = pl.program_id(1)
    last = pl.num_programs(1) - 1
    used = p < nu_ref[0]

    @pl.when(used)
    def _():
        x = xs_ref[...]
        g = _dot(x, wg_ref[0])
        mid = g * _sigmoid(g) * _dot(x, wu_ref[0])
        part = _dot(mid.astype(BF16), wd_ref[0])

        @pl.when(f == 0)
        def _():
            acc_ref[...] = part

        @pl.when(f > 0)
        def _():
            acc_ref[...] += part

        @pl.when(f == last)
        def _():
            o_ref[...] = acc_ref[...].astype(o_ref.dtype)

    @pl.when(jnp.logical_not(used) & (f == last))
    def _():
        o_ref[...] = jnp.zeros(o_ref.shape, o_ref.dtype)


def _grouped_ffn(tile_expert, n_used, xs, n_rows, wg, wu, wd):
    d = xs.shape[1]
    ff = wd.shape[1]
    tf = MOE_FF_TILE
    nf = ff // tf
    tg = MOE_TILE
    row = lambda p, f, te, nu: (p, 0)
    fidx = lambda p, f, nu: jnp.where(p < nu[0], f, nf - 1)
    up = pl.BlockSpec((1, d, tf), lambda p, f, te, nu: (te[p], 0, fidx(p, f, nu)))
    return pl.pallas_call(
        _gmm_kernel,
        grid_spec=pltpu.PrefetchScalarGridSpec(
            num_scalar_prefetch=2, grid=(n_rows // tg, nf),
            in_specs=[pl.BlockSpec((tg, d), row), up, up,
                      pl.BlockSpec((1, tf, d), lambda p, f, te, nu: (te[p], fidx(p, f, nu), 0))],
            out_specs=pl.BlockSpec((tg, d), row),
            scratch_shapes=[pltpu.VMEM((tg, d), F32)]),
        out_shape=jax.ShapeDtypeStruct((n_rows, d), BF16),
        compiler_params=_params("arbitrary", "arbitrary"),
        name="moe_grouped_ffn",
    )(tile_expert, n_used, xs, wg, wu, wd)


def _combine_kernel(win_ref, h_ref, tok_ref, w1_ref, w2_ref, starts_ref, rows_ref, ys_hbm, o_ref, gbuf, sems,
                    *scratch, final_rows=None):
    i = pl.program_id(0)
    tm = h_ref.shape[0]
    win = ROW_WIN

    def fetch(tile, slot):
        for s in range(ROW_NWIN):
            start = pl.multiple_of(win_ref[tile * ROW_NWIN + s], win)
            pltpu.make_async_copy(ys_hbm.at[pl.ds(start, win)], gbuf.at[slot, pl.ds(s * win, win)],
                                  sems.at[slot]).start(priority=s % 2)

    @pl.when(i == 0)
    def _():
        fetch(0, 0)

    @pl.when(i + 1 < pl.num_programs(0))
    def _():
        fetch(i + 1, (i + 1) % 2)

    slot = i % 2
    pltpu.make_async_copy(ys_hbm.at[pl.ds(0, ROW_NWIN * win)], gbuf.at[slot], sems.at[slot]).wait()

    tok = tok_ref[...]
    lane = lax.broadcasted_iota(jnp.int32, tok.shape, 1)
    field = lambda c: jnp.sum(jnp.where(lane == c, tok, 0.0), axis=1, keepdims=True)
    valid = field(ROUTE_VALID) > 0.5
    rows = rows_ref[0]
    reps = h_ref.shape[1] // LANES
    out = h_ref[...]
    for e_col, r_col, w_ref in ((ROUTE_E1, ROUTE_R1, w1_ref), (ROUTE_E2, ROUTE_R2, w2_ref)):
        start = jnp.sum(jnp.where(lane == field(e_col).astype(jnp.int32), starts_ref[...], 0.0),
                        axis=1, keepdims=True)
        dest = jnp.where(valid, start + field(r_col), -2.0)
        pick = jnp.where(dest == rows, 1.0, 0.0).astype(BF16)
        out = out + jnp.concatenate([w_ref[...]] * reps, axis=1) * _dot(pick, gbuf[slot])
    if final_rows is None:
        o_ref[...] = out
        return

    obuf, osems = scratch
    n_meta, t_real = final_rows
    bsz, ts = obuf.shape[1], obuf.shape[2]
    steps = h_ref.shape[0] // (bsz * ts)
    assert steps == 1 and n_meta % SUBLANES == 0 and n_meta < ts
    i_last = t_real // ts
    rem = t_real - i_last * ts
    n_steps = pl.num_programs(0)
    oslot = i % 2

    def full_copy(step, s):
        dst = pl.multiple_of(step * ts - n_meta, SUBLANES)
        return pltpu.make_async_copy(obuf.at[s], o_ref.at[:, pl.ds(dst, ts), :], osems.at[s])

    is_full = lambda step: (step >= 1) & (step < i_last)

    @pl.when(is_full(i - 2))
    def _():
        full_copy(i - 2, oslot).wait()

    for t in range(ts):
        obuf[oslot, :, t, :] = out[t * bsz:(t + 1) * bsz, :]

    @pl.when(i == 0)
    def _():
        c = pltpu.make_async_copy(obuf.at[oslot, :, pl.ds(n_meta, ts - n_meta), :],
                                  o_ref.at[:, pl.ds(0, ts - n_meta), :], osems.at[oslot])
        c.start()
        c.wait()

    @pl.when(is_full(i))
    def _():
        full_copy(i, oslot).start()

    if rem:
        @pl.when(i == i_last)
        def _():
            c = pltpu.make_async_copy(obuf.at[oslot, :, pl.ds(0, rem), :],
                                      o_ref.at[:, pl.ds(i_last * ts - n_meta, rem), :], osems.at[oslot])
            c.start()
            c.wait()

    @pl.when(i == n_steps - 1)
    def _():
        @pl.when(is_full(i - 1))
        def _():
            full_copy(i - 1, 1 - oslot).wait()

        @pl.when(is_full(i))
        def _():
            full_copy(i, oslot).wait()


def _combine(win_start, h, tok, w1b, w2b, starts_row, slot_rows, ys, final=None):
    n, d = h.shape
    tm = COMBINE_TILE
    n_slots = ROW_NWIN * ROW_WIN
    row = lambda i, w: (i, 0)
    scratch = [pltpu.VMEM((2, n_slots, d), ys.dtype), pltpu.SemaphoreType.DMA((2,))]
    if final is None:
        kern = _combine_kernel
        out_spec = pl.BlockSpec((tm, d), row)
        out_shape = jax.ShapeDtypeStruct(h.shape, F32)
    else:
        bsz, n_meta, t_real = final
        assert tm % bsz == 0 and bsz % SUBLANES == 0
        kern = functools.partial(_combine_kernel, final_rows=(n_meta, t_real))
        out_spec = pl.BlockSpec(memory_space=pl.ANY)
        out_shape = jax.ShapeDtypeStruct((bsz, t_real - n_meta, d), F32)
        scratch += [pltpu.VMEM((2, bsz, tm // bsz, d), F32), pltpu.SemaphoreType.DMA((2,))]
    return pl.pallas_call(
        kern,
        grid_spec=pltpu.PrefetchScalarGridSpec(
            num_scalar_prefetch=1, grid=(n // tm,),
            in_specs=[pl.BlockSpec((tm, d), row), pl.BlockSpec((tm, LANES), row), pl.BlockSpec((tm, LANES), row),
                      pl.BlockSpec((tm, LANES), row), pl.BlockSpec((1, LANES), lambda i, w: (0, 0)),
                      pl.BlockSpec((1, 1, n_slots), lambda i, w: (i, 0, 0)),
                      pl.BlockSpec(memory_space=pl.ANY)],
            out_specs=out_spec,
            scratch_shapes=scratch),
        out_shape=out_shape,
        compiler_params=_params("arbitrary"),
        name="moe_combine",
    )(win_start, h, tok, w1b, w2b, starts_row, slot_rows, ys)


def _gate_windows(lw, bd):
    out = []
    for c0 in range(0, lw, LRU_GATE_TILE):
        nc = min(LRU_GATE_TILE, lw - c0)
        r0 = (c0 // bd) * bd
        r1 = ((c0 + nc - 1) // bd + 1) * bd
        r0 = (r0 // LANES) * LANES
        r1 = min(-(-r1 // LANES) * LANES, lw)
        out.append((c0, nc, r0, r1 - r0))
    return out


def _lru_kernel(h_ref, ng_ref, win_ref, cw_ref, cb_ref, wg_ref, br_ref, bi_ref, lam_ref, wo_ref,
                o_ref, xbuf_ref, hs_ref, a_ref, b_ref, y_ref, gate_ref, res_ref, *, windows, bsz):
    ti = pl.program_id(0)
    rows, lw = a_ref.shape
    halo = (LRU_CONV_WIDTH - 1) * bsz

    @pl.when(ti == 0)
    def _():
        xbuf_ref[0:halo, :] = jnp.zeros((halo, lw), F32)
        hs_ref[...] = jnp.zeros(hs_ref.shape, F32)

    @pl.when(ti > 0)
    def _():
        xbuf_ref[0:halo, :] = xbuf_ref[rows:rows + halo, :]

    res_ref[...] = jnp.concatenate([h_ref[:, t, :] for t in range(rows // bsz)], axis=0)
    proj = _dot(_rms(res_ref[...], ng_ref[...]).astype(BF16), win_ref[...])
    gate_ref[...] = jax.nn.gelu(proj[:, 0:lw], approximate=True)
    xbuf_ref[halo:halo + rows, :] = proj[:, lw:2 * lw]
    xc = jnp.broadcast_to(cb_ref[...], (rows, lw))
    for k in range(LRU_CONV_WIDTH):
        xc = xc + cw_ref[k:k + 1, :] * xbuf_ref[k * bsz:k * bsz + rows, :]
    xcb = xc.astype(BF16)
    sp_lam = jnp.log(1.0 + jnp.exp(-lam_ref[...]))
    log2_a_per_r = (-LRU_C * LOG2E) * sp_lam
    wcol = 0
    for (c0, nc, r0, nr) in windows:
        rg = _dot(xcb[:, r0:r0 + nr], wg_ref[r0:r0 + nr, wcol:wcol + 2 * nc])
        wcol += 2 * nc
        r = _sigmoid(rg[:, 0:nc] + br_ref[:, c0:c0 + nc])
        ig = _sigmoid(rg[:, nc:2 * nc] + bi_ref[:, c0:c0 + nc])
        a = jnp.exp2(r * log2_a_per_r[:, c0:c0 + nc])
        a_ref[:, c0:c0 + nc] = a
        one_m_a2 = 1.0 - a * a
        root = jnp.where(one_m_a2 > 0.0, one_m_a2 * lax.rsqrt(one_m_a2), 0.0)
        b_ref[:, c0:c0 + nc] = root * (ig * xc[:, c0:c0 + nc])

    def step(t, hprev):
        r0 = pl.multiple_of(t * bsz, SUBLANES)
        hnew = a_ref[pl.ds(r0, bsz), :] * hprev + b_ref[pl.ds(r0, bsz), :]
        y_ref[pl.ds(r0, bsz), :] = hnew
        return hnew

    hs_ref[...] = lax.fori_loop(0, rows // bsz, step, hs_ref[...], unroll=LRU_SCAN_UNROLL)
    gy = (gate_ref[...] * y_ref[...]).astype(BF16)
    o_ref[...] = res_ref[...] + _dot(gy, wo_ref[...])


def _lru_mixer(h, ng, win, cw, cb, wg, br, bi, lam, wo, windows):
    bsz, tp, d = h.shape
    n = bsz * tp
    lw = lam.shape[-1]
    assert bsz % SUBLANES == 0 and LRU_ROWS % bsz == 0 and n % LRU_ROWS == 0
    rows = LRU_ROWS
    halo = (LRU_CONV_WIDTH - 1) * bsz
    const = lambda t: (0, 0)
    blk = lambda t: (t, 0)
    resident = lambda w: pl.BlockSpec(w.shape, const, pipeline_mode=pl.Buffered(1))
    return pl.pallas_call(
        functools.partial(_lru_kernel, windows=windows, bsz=bsz),
        grid=(n // rows,),
        in_specs=[pl.BlockSpec((bsz, rows // bsz, d), lambda t: (0, t, 0)), pl.BlockSpec((1, d), const), resident(win),
                  pl.BlockSpec(cw.shape, const), pl.BlockSpec((1, lw), const), resident(wg),
                  pl.BlockSpec((1, lw), const), pl.BlockSpec((1, lw), const),
                  pl.BlockSpec((1, lw), const), resident(wo)],
        out_specs=pl.BlockSpec((rows, d), blk),
        out_shape=jax.ShapeDtypeStruct((n, d), F32),
        scratch_shapes=[pltpu.VMEM((halo + rows, lw), F32), pltpu.VMEM((bsz, lw), F32),
                        pltpu.VMEM((rows, lw), F32), pltpu.VMEM((rows, lw), F32), pltpu.VMEM((rows, lw), F32),
                        pltpu.VMEM((rows, lw), F32), pltpu.VMEM((rows, d), F32)],
        compiler_params=_params("arbitrary"),
        name="rglru_mixer",
    )(h, ng, win, cw, cb, wg, br, bi, lam, wo)


ROUTE_E1, ROUTE_E2, ROUTE_R1, ROUTE_R2, ROUTE_VALID = range(5)


def _router_kernel(h_ref, g_ref, rw_ref, ltri_ref, xn_ref, info_ref, tok_ref, w1_ref, w2_ref, before_ref, cnt_ref,
                   run_ref, *, n_real_rows):
    i = pl.program_id(0)
    tm = h_ref.shape[0]

    @pl.when(i == 0)
    def _():
        run_ref[...] = jnp.zeros(run_ref.shape, F32)

    xn = _rms(h_ref[...], g_ref[...])
    xn_ref[...] = xn.astype(BF16)
    split = lambda a: (a.astype(BF16), (a - a.astype(BF16).astype(F32)).astype(BF16))
    (x_hi, x_lo), (r_hi, r_lo) = split(xn), split(rw_ref[...])
    logits = _dot(x_hi, r_hi) + (_dot(x_hi, r_lo) + _dot(x_lo, r_hi))
    lane = lax.broadcasted_iota(jnp.int32, logits.shape, 1)
    neg = jnp.float32(-jnp.inf)
    logits = jnp.where(lane < N_EXPERTS, logits, neg)
    top1 = jnp.max(logits, axis=1, keepdims=True)
    idx1 = jnp.min(jnp.where(logits == top1, lane, LANES), axis=1, keepdims=True)
    rest = jnp.where(lane == idx1, neg, logits)
    top2 = jnp.max(rest, axis=1, keepdims=True)
    idx2 = jnp.min(jnp.where(rest == top2, lane, LANES), axis=1, keepdims=True)
    w1 = 1.0 / (1.0 + jnp.exp(top2 - top1))

    rowg = lax.broadcasted_iota(jnp.int32, (tm, 1), 0) + i * tm
    valid = jnp.where(rowg < n_real_rows, 1.0, 0.0)
    onehot = (jnp.where(lane == idx1, 1.0, 0.0) + jnp.where(lane == idx2, 1.0, 0.0)) * valid
    before_ref[...] = jnp.broadcast_to(run_ref[...], before_ref.shape)
    before = _dot(ltri_ref[...], onehot.astype(BF16)) + run_ref[...]
    rank1 = jnp.sum(jnp.where(lane == idx1, before, 0.0), axis=1, keepdims=True)
    rank2 = jnp.sum(jnp.where(lane == idx2, before, 0.0), axis=1, keepdims=True)
    run_ref[...] += jnp.ceil(jnp.sum(onehot, axis=0, keepdims=True) / ROW_WIN) * ROW_WIN
    cnt_ref[...] = run_ref[...]
    w1_ref[...] = jnp.broadcast_to(w1, (tm, LANES))
    w2_ref[...] = jnp.broadcast_to(1.0 - w1, (tm, LANES))
    fields = {ROUTE_E1: idx1.astype(F32), ROUTE_E2: idx2.astype(F32),
              ROUTE_R1: rank1, ROUTE_R2: rank2, ROUTE_VALID: valid}
    info = jnp.zeros(logits.shape, F32)
    for col, val in fields.items():
        info = jnp.where(lane == col, val, info)
    tok_ref[...] = info
    info_ref[...] = info.T[0:SUBLANES, :]


def _router(h, g, rw, ltri, n_real_rows):
    n, d = h.shape
    tm = ROW_TILE
    row = lambda i: (i, 0)
    const = lambda i: (0, 0)
    return pl.pallas_call(
        functools.partial(_router_kernel, n_real_rows=n_real_rows),
        grid=(n // tm,),
        in_specs=[pl.BlockSpec((tm, d), row), pl.BlockSpec((1, d), const), pl.BlockSpec(rw.shape, const),
                  pl.BlockSpec(ltri.shape, const)],
        out_specs=[pl.BlockSpec((tm, d), row), pl.BlockSpec((SUBLANES, tm), row),
                   pl.BlockSpec((tm, LANES), row), pl.BlockSpec((tm, LANES), row),
                   pl.BlockSpec((tm, LANES), row), pl.BlockSpec((SUBLANES, LANES), row),
                   pl.BlockSpec((1, LANES), const)],
        out_shape=[jax.ShapeDtypeStruct((n, d), BF16), jax.ShapeDtypeStruct((n // tm * SUBLANES, tm), F32),
                   jax.ShapeDtypeStruct((n, LANES), F32), jax.ShapeDtypeStruct((n, LANES), F32),
                   jax.ShapeDtypeStruct((n, LANES), F32), jax.ShapeDtypeStruct((n // tm * SUBLANES, LANES), F32),
                   jax.ShapeDtypeStruct((1, LANES), F32)],
        scratch_shapes=[pltpu.VMEM((1, LANES), F32)],
        compiler_params=_params("arbitrary"),
        name="router",
    )(h, g, rw, ltri)


def _routing_tables(before, counts, n_rows):
    tg = MOE_TILE
    win = ROW_WIN
    cnt = counts[0, :N_EXPERTS].astype(jnp.int32)
    padded = (cnt + tg - 1) // tg * tg
    ends = jnp.cumsum(padded)
    starts = ends - padded
    tile_start = jnp.arange(n_rows // tg, dtype=jnp.int32) * tg
    tile_expert = jnp.minimum(jnp.sum(tile_start[:, None] >= ends[None, :], axis=1), N_EXPERTS - 1)
    n_used = (ends[-1] // tg).reshape(1)
    bef = before.reshape(-1, SUBLANES, LANES)[:, 0, :N_EXPERTS].astype(jnp.int32)
    nwin = (jnp.concatenate([bef[1:], cnt[None, :]], axis=0) - bef) // win
    base = jnp.cumsum(nwin, axis=1) - nwin
    slot = jnp.arange(ROW_NWIN, dtype=jnp.int32)[None, :, None]
    mine = (slot >= base[:, None, :]) & (slot < (base + nwin)[:, None, :])
    rows = jnp.sum(jnp.where(mine, (starts[None, :] + bef)[:, None, :] + win * (slot - base[:, None, :]), 0), axis=2)
    used = jnp.any(mine, axis=2)
    parity = (jnp.arange(bef.shape[0], dtype=jnp.int32) % 2)[:, None]
    scatter_rows = jnp.where(used, rows, n_rows + win * (parity * ROW_NWIN + slot[:, :, 0]))
    gather_rows = jnp.where(used, rows, 0)
    slot_rows = jnp.where(used[..., None], rows[..., None] + jnp.arange(win, dtype=jnp.int32), -1)
    slot_rows = slot_rows.reshape(-1, 1, ROW_NWIN * win).astype(F32)
    starts_row = jnp.zeros((1, LANES), F32).at[0, :N_EXPERTS].set(starts.astype(F32))
    local_off = jnp.zeros(bef.shape[:1] + (SUBLANES,), jnp.int32).at[:, :N_EXPERTS].set(win * base - bef)
    local_off = jnp.broadcast_to(local_off.astype(F32)[:, :, None], local_off.shape + (LANES,)).reshape(-1, LANES)
    return (tile_expert.astype(jnp.int32), n_used.astype(jnp.int32), scatter_rows.reshape(-1).astype(jnp.int32),
            gather_rows.reshape(-1).astype(jnp.int32), slot_rows, starts_row, local_off)


def _interleave_gate_up(wg, wu, tf):
    e, d, ff = wg.shape
    g = wg.astype(BF16).reshape(e, d, ff // tf, tf)
    u = wu.astype(BF16).reshape(e, d, ff // tf, tf)
    return jnp.concatenate([g, u], axis=3).reshape(e, d, 2 * ff)


def _dense_block_diag(w):
    nb, bd, _ = w.shape
    eye = jnp.eye(nb, dtype=w.dtype)
    return (w[:, :, None, :] * eye[:, None, :, None]).reshape(nb * bd, nb * bd)


def _pack_lru_gates(wr, wi, windows):
    dr = _dense_block_diag(wr).astype(BF16)
    di = _dense_block_diag(wi).astype(BF16)
    cols = []
    for (c0, nc, _, _) in windows:
        cols += [dr[:, c0:c0 + nc], di[:, c0:c0 + nc]]
    return jnp.concatenate(cols, axis=1)


def _ffn_chunk(ff):
    for tf in (896, 512, 384, 256, 128):
        if ff % tf == 0:
            return tf
    raise ValueError(f"unsupported d_ff {ff}")


def kernel(x, meta_tokens, mix_norm_even, w_in_even, conv_w, conv_b, conv_ln_g, conv_ln_b, q_norm_g, k_norm_g, w_out_even, ffn_norm_even, ffn_w_gate, ffn_w_up, ffn_w_down, mix_norm_odd, w_in_odd, lru_conv_w, lru_conv_b, gate_r_w, gate_r_b, gate_i_w, gate_i_b, lru_lambda, w_out_odd, ffn_norm_odd, router_w, moe_w_gate, moe_w_up, moe_w_down):
    bsz, seq, d = x.shape
    t_real = N_META + seq
    tp = -(-t_real // TIME_TILE) * TIME_TILE
    n = bsz * tp
    assert n % ROW_TILE == 0
    depth = mix_norm_even.shape[0] + mix_norm_odd.shape[0]

    meta = jnp.broadcast_to(meta_tokens[None].astype(x.dtype), (bsz, N_META, d))
    h = jnp.concatenate([meta, x, jnp.zeros((bsz, tp - t_real, d), x.dtype)], axis=1).reshape(n, d)

    head_mean = jnp.kron(jnp.eye(MXU_DIM // SB_HEAD_DIM, dtype=F32),
                         jnp.full((SB_HEAD_DIM, SB_HEAD_DIM), 1.0 / SB_HEAD_DIM, F32)).astype(BF16)
    kk = jnp.arange(ATT_TK)
    neg_tri = -(kk[:, None] >= kk[None, :]).astype(BF16)
    row2 = lambda a: a.reshape(1, -1)

    time_major = False
    for layer in range(depth):
        p = layer // 2
        if layer % 2 == 0:
            if time_major:
                h = h.reshape(tp, bsz, d).transpose(1, 0, 2).reshape(n, d)
                time_major = False
            u, q, k, v = _even_in_proj(h, row2(mix_norm_even[p]), w_in_even[p].astype(BF16),
                                       row2(jnp.tile(q_norm_g[p], SB_HEADS)),
                                       row2(jnp.tile(k_norm_g[p], SB_HEADS)), head_mean)
            u = _conv_module(u.reshape(bsz, tp, CONV_CH), conv_w[p], row2(conv_b[p]),
                             row2(conv_ln_g[p]), row2(conv_ln_b[p]))
            o = _attention(q.reshape(bsz, tp, SB_WIDTH), k.reshape(bsz, tp, SB_WIDTH),
                           v.reshape(bsz, tp, SB_WIDTH), neg_tri, t_real)
            tf = _ffn_chunk(ffn_w_gate.shape[-1])
            h = _even_ffn(h, u.reshape(n, CONV_CH), o.reshape(n, SB_WIDTH), w_out_even[p].astype(BF16),
                          row2(ffn_norm_even[p]),
                          _interleave_gate_up(ffn_w_gate[p][None], ffn_w_up[p][None], tf)[0],
                          ffn_w_down[p].astype(BF16), tf)
        else:
            if time_major:
                h = h.reshape(tp, bsz, d).transpose(1, 0, 2).reshape(n, d)
            time_major = True
            lw = lru_lambda.shape[-1]
            windows = _gate_windows(lw, lw // LRU_BLOCKS)
            h = _lru_mixer(h.reshape(bsz, tp, d), row2(mix_norm_odd[p]), w_in_odd[p].astype(BF16),
                           lru_conv_w[p], row2(lru_conv_b[p]),
                           _pack_lru_gates(gate_r_w[p], gate_i_w[p], windows),
                           row2(gate_r_b[p]), row2(gate_i_b[p]), row2(lru_lambda[p]),
                           w_out_odd[p].astype(BF16), windows)
            rw = jnp.pad(router_w[p], ((0, 0), (0, LANES - N_EXPERTS)))
            kk = jnp.arange(ROW_TILE)
            ltri = (kk[:, None] > kk[None, :]).astype(BF16)
            xn, info, tok, w1b, w2b, before, counts = _router(h, row2(ffn_norm_odd[p]), rw, ltri, t_real * bsz)
            tg = MOE_TILE
            assert ROW_TILE == COMBINE_TILE
            n_tiles = n // COMBINE_TILE
            n_rows = 2 * bsz * t_real + n_tiles * N_EXPERTS * (ROW_WIN - 1) + N_EXPERTS * (tg - 1)
            n_rows = -(-n_rows // tg) * tg
            tile_expert, n_used, scatter_rows, gather_rows, slot_rows, starts_row, local_off = _routing_tables(
                before, counts, n_rows)
            xs = _row_scatter(scatter_rows, xn, info, local_off, n_rows + 2 * ROW_NWIN * ROW_WIN)
            ys = _grouped_ffn(tile_expert, n_used, xs, n_rows, moe_w_gate[p].astype(BF16),
                              moe_w_up[p].astype(BF16), moe_w_down[p].astype(BF16))
            if layer + 1 == depth:
                return _combine(gather_rows, h, tok, w1b, w2b, starts_row, slot_rows, ys, (bsz, N_META, t_real))
            h = _combine(gather_rows, h, tok, w1b, w2b, starts_row, slot_rows, ys)
    if time_major:
        return h.reshape(tp, bsz, d)[N_META:t_real].transpose(1, 0, 2)
    return h.reshape(bsz, tp, d)[:, N_META:t_real]
```

```python
import functools
import math

import jax
import jax.numpy as jnp
from jax import lax
from jax.experimental import pallas as pl
from jax.experimental.pallas import tpu as pltpu

F32 = jnp.float32
BF16 = jnp.bfloat16

EPS = 1e-6
N_META = 16
CONV_CH = 512
CONV_WIDTH = 31
SB_HEADS = 8
SB_HEAD_DIM = 64
SB_WIDTH = SB_HEADS * SB_HEAD_DIM
LRU_BLOCKS = 16
LRU_CONV_WIDTH = 4
LRU_C = 8.0
N_EXPERTS = 8

LANES = 128
MXU_DIM = 256
TIME_TILE = 256
VMEM_LIMIT = 50 * 1024 * 1024

ROW_TILE = 512
CONV_TT = 256
CONV_HALO = 32
CONV_CHUNK = 32
ATT_TQ = 256
ATT_TK = 256
LOG2E = 1.4426950408889634
MASKED_EXPONENT = -1e30
SOFTPLUS_CUTOFF = 126.0
LRU_ROWS = 512
LRU_SCAN_UNROLL = 4
LRU_GATE_TILE = 256
MOE_TILE = 512
MOE_FF_TILE = 1792
COMBINE_TILE = 512
ROW_WIN = 16
ROW_NWIN = 2 * COMBINE_TILE // ROW_WIN + N_EXPERTS
SUBLANES = 8
BF16_ROWS = 16


def _dot(a, b):
    return jnp.dot(a, b, preferred_element_type=F32)


def _params(*sem):
    return pltpu.CompilerParams(dimension_semantics=sem, vmem_limit_bytes=VMEM_LIMIT)


def _sigmoid(x):
    return 0.5 * jnp.tanh(0.5 * x) + 0.5


def _rms(x, g):
    ms = jnp.mean(x * x, axis=-1, keepdims=True)
    return x * lax.rsqrt(ms + EPS) * g


def _even_in_kernel(h_ref, g_ref, w_ref, qg_ref, kg_ref, hm_ref, u_ref, q_ref, k_ref, v_ref):
    xn = _rms(h_ref[...], g_ref[...]).astype(BF16)
    a = _dot(xn, w_ref[:, 0:CONV_CH])
    gate = _dot(xn, w_ref[:, CONV_CH:2 * CONV_CH])
    u_ref[...] = (a * _sigmoid(gate)).astype(BF16)

    def head_norm(y, gain):
        yy = (y * y).astype(BF16)
        parts = [_dot(yy[:, c:c + MXU_DIM], hm_ref[...]) for c in range(0, SB_WIDTH, MXU_DIM)]
        ms = jnp.concatenate(parts, axis=1)
        return y * lax.rsqrt(ms + EPS) * gain

    c0 = 2 * CONV_CH
    q = head_norm(_dot(xn, w_ref[:, c0:c0 + SB_WIDTH]), qg_ref[...])
    q_ref[...] = (q * (LOG2E / math.sqrt(SB_HEAD_DIM))).astype(BF16)
    k = head_norm(_dot(xn, w_ref[:, c0 + SB_WIDTH:c0 + 2 * SB_WIDTH]), kg_ref[...])
    k_ref[...] = k.astype(BF16)
    v_ref[...] = _dot(xn, w_ref[:, c0 + 2 * SB_WIDTH:c0 + 3 * SB_WIDTH]).astype(BF16)


def _even_in_proj(h, g, w, qg, kg, hm):
    n, d = h.shape
    tm = ROW_TILE
    row = lambda i: (i, 0)
    const = lambda i: (0, 0)
    out = jax.ShapeDtypeStruct((n, SB_WIDTH), BF16)
    return pl.pallas_call(
        _even_in_kernel,
        grid=(n // tm,),
        in_specs=[pl.BlockSpec((tm, d), row), pl.BlockSpec((1, d), const),
                  pl.BlockSpec(w.shape, const), pl.BlockSpec((1, SB_WIDTH), const),
                  pl.BlockSpec((1, SB_WIDTH), const), pl.BlockSpec(hm.shape, const)],
        out_specs=[pl.BlockSpec((tm, SB_WIDTH), row)] * 4,
        out_shape=[out] * 4,
        compiler_params=_params("parallel"),
        name="even_in_proj",
    )(h, g, w, qg, kg, hm)


def _conv_kernel(cur_ref, halo_ref, w_ref, b_ref, lg_ref, lb_ref, o_ref, buf_ref, sh_ref):
    i = pl.program_id(1)
    tt = cur_ref.shape[1]
    rows = CONV_HALO + tt
    halo = halo_ref[0].astype(F32)
    buf_ref[0:CONV_HALO, :] = jnp.where(i > 0, halo, 0.0)
    buf_ref[CONV_HALO:rows, :] = cur_ref[0].astype(F32)
    for s in range(1, SUBLANES):
        sh_ref[s - 1, 0:rows - SUBLANES, :] = buf_ref[s:s + rows - SUBLANES, :]
    off = CONV_HALO - (CONV_WIDTH - 1)
    for c in range(tt // CONV_CHUNK):
        r0 = c * CONV_CHUNK
        acc = jnp.broadcast_to(b_ref[...], (CONV_CHUNK, CONV_CH))
        for k in range(CONV_WIDTH):
            shift = (off + k) % SUBLANES
            base = r0 + off + k - shift
            if shift == 0:
                tap = buf_ref[base:base + CONV_CHUNK, :]
            else:
                tap = sh_ref[shift - 1, base:base + CONV_CHUNK, :]
            acc = acc + w_ref[k:k + 1, :] * tap
        mu = jnp.mean(acc, axis=-1, keepdims=True)
        xc = acc - mu
        var = jnp.mean(xc * xc, axis=-1, keepdims=True)
        y = xc * lax.rsqrt(var + EPS) * lg_ref[...] + lb_ref[...]
        o_ref[0, r0:r0 + CONV_CHUNK, :] = (y * _sigmoid(y)).astype(BF16)


def _conv_module(u, w, b, lg, lb):
    bsz, tp, c = u.shape
    tt = CONV_TT
    per = tt // CONV_HALO
    const = lambda bi, i: (0, 0)
    return pl.pallas_call(
        _conv_kernel,
        grid=(bsz, tp // tt),
        in_specs=[pl.BlockSpec((1, tt, c), lambda bi, i: (bi, i, 0)),
                  pl.BlockSpec((1, CONV_HALO, c), lambda bi, i: (bi, jnp.maximum(i * per - 1, 0), 0)),
                  pl.BlockSpec(w.shape, const), pl.BlockSpec((1, c), const),
                  pl.BlockSpec((1, c), const), pl.BlockSpec((1, c), const)],
        out_specs=pl.BlockSpec((1, tt, c), lambda bi, i: (bi, i, 0)),
        out_shape=jax.ShapeDtypeStruct(u.shape, BF16),
        scratch_shapes=[pltpu.VMEM((CONV_HALO + tt, c), F32),
                        pltpu.VMEM((SUBLANES - 1, CONV_HALO + tt - SUBLANES, c), F32)],
        compiler_params=_params("parallel", "parallel"),
        name="conv_module",
    )(u, u, w, b, lg, lb)


def _attn_kernel(*refs, tq, tile0, aliased, n_compute):
    if aliased:
        refs = refs[1:]
    o_ref = refs[4]
    step = pl.program_id(1)

    @pl.when(step < n_compute)
    def _():
        _attn_tile(*refs, tq=tq, row0=(step + tile0) * ATT_TQ)

    @pl.when(step >= n_compute)
    def _():
        o_ref[...] = jnp.zeros(o_ref.shape, BF16)


def _attn_tile(q_ref, k_ref, v_ref, tri_ref, o_ref, q2_ref, e_ref, tot_ref, acc_ref, car_ref, *, tq, row0):
    tk = ATT_TK
    npairs = q_ref.shape[2] // LANES
    lane_q = lax.broadcasted_iota(jnp.int32, (tq, LANES), 1)

    for hp in range(npairs):
        q = q_ref[0, 0:tq, hp * LANES:(hp + 1) * LANES]
        zero = jnp.zeros_like(q)
        q2_ref[hp, 0:tq, :] = jnp.where(lane_q < SB_HEAD_DIM, q, zero)
        q2_ref[hp, tq:2 * tq, :] = jnp.where(lane_q >= SB_HEAD_DIM, q, zero)
        car_ref[hp] = jnp.zeros((2 * tq, LANES), F32)
        acc_ref[hp] = jnp.zeros((2 * tq, LANES), F32)

    def stage_a(j, mask, slot):
        k0 = pl.multiple_of(j * tk, tk)
        for hp in range(npairs):
            kt = k_ref[0, pl.ds(k0, tk), hp * LANES:(hp + 1) * LANES]
            z = lax.dot_general(q2_ref[hp], kt, (((1,), (1,)), ((), ())), preferred_element_type=F32)
            sp = jnp.where(z > SOFTPLUS_CUTOFF, z, jnp.log(1.0 + jnp.exp2(z)) * LOG2E)
            if mask is not None:
                sp = jnp.where(mask, sp, 0.0)
            e = z + _dot(sp.astype(BF16), tri_ref[...])
            if mask is not None:
                e = jnp.where(mask, e, MASKED_EXPONENT)
            e_ref[slot, hp] = e
            tot_ref[slot, hp] = jnp.broadcast_to(jnp.sum(sp, axis=1, keepdims=True), (2 * tq, LANES))

    def stage_b(j, slot):
        k0 = pl.multiple_of(j * tk, tk)
        for hp in range(npairs):
            car = car_ref[hp]
            w = jnp.exp2(e_ref[slot, hp] + jnp.concatenate([car] * (tk // LANES), axis=1)).astype(BF16)
            vt = v_ref[0, pl.ds(k0, tk), hp * LANES:(hp + 1) * LANES]
            acc_ref[hp] += _dot(w, vt)
            car_ref[hp] = car - tot_ref[slot, hp]

    j_last = row0 // tk
    row = lax.broadcasted_iota(jnp.int32, (2 * tq, tk), 0)
    col = lax.broadcasted_iota(jnp.int32, (2 * tq, tk), 1)
    row = jnp.where(row >= tq, row - tq, row) + row0
    odd = j_last % 2
    stage_a(j_last, (col + j_last * tk) < row, odd)

    @pl.when(odd == 1)
    def _():
        stage_b(j_last, 1)
        stage_a(j_last - 1, None, 0)

    def body(jj, carry):
        j = j_last - odd - 1 - 2 * jj
        stage_b(j + 1, 0)
        stage_a(j, None, 1)
        stage_b(j, 1)
        stage_a(j - 1, None, 0)
        return carry

    lax.fori_loop(0, (j_last - odd) // 2, body, 0)
    stage_b(0, 0)
    if tq < o_ref.shape[1]:
        o_ref[0, tq:, :] = jnp.zeros((o_ref.shape[1] - tq, o_ref.shape[2]), BF16)
    for hp in range(npairs):
        o_ref[0, 0:tq, hp * LANES:(hp + 1) * LANES] = jnp.where(
            lane_q < SB_HEAD_DIM, acc_ref[hp, 0:tq, :], acc_ref[hp, tq:2 * tq, :]).astype(BF16)


def _attention(q, k, v, tri, t_real):
    bsz, tp, width = q.shape
    bq, tk = ATT_TQ, ATT_TK
    assert bq <= tk and tk % bq == 0
    npairs = width // LANES
    n_full = t_real // bq
    rem = -(-(t_real - n_full * bq) // BF16_ROWS) * BF16_ROWS

    def call(tq, tile0, n_compute, n_tiles, prev):
        blk = lambda b, i: (b, i + tile0, 0)
        full = lambda b, i: (b, 0, 0)
        in_specs = [pl.BlockSpec((1, bq, width), blk), pl.BlockSpec((1, tp, width), full),
                    pl.BlockSpec((1, tp, width), full), pl.BlockSpec(tri.shape, lambda b, i: (0, 0))]
        args = [q, k, v, tri]
        if prev is not None:
            in_specs.insert(0, pl.BlockSpec(memory_space=pl.ANY))
            args.insert(0, prev)
        return pl.pallas_call(
            functools.partial(_attn_kernel, tq=tq, tile0=tile0, aliased=prev is not None, n_compute=n_compute),
            grid=(bsz, n_tiles),
            in_specs=in_specs,
            out_specs=pl.BlockSpec((1, bq, width), blk),
            out_shape=jax.ShapeDtypeStruct(q.shape, BF16),
            input_output_aliases={} if prev is None else {0: 0},
            scratch_shapes=[pltpu.VMEM((npairs, 2 * tq, LANES), BF16),
                            pltpu.VMEM((2, npairs, 2 * tq, tk), F32),
                            pltpu.VMEM((2, npairs, 2 * tq, LANES), F32),
                            pltpu.VMEM((npairs, 2 * tq, LANES), F32),
                            pltpu.VMEM((npairs, 2 * tq, LANES), F32)],
            compiler_params=_params("parallel", "arbitrary"),
            name="stick_breaking" if prev is None else "stick_breaking_tail",
        )(*args)

    out = call(bq, 0, n_full, tp // bq, None)
    if rem:
        out = call(rem, n_full, 1, 1, out)
    return out


def _even_ffn_kernel(h_ref, u_ref, o_ref, wo_ref, g_ref, wgu_ref, wd_ref, out_ref, xn_ref, *, tf):
    @pl.when(pl.program_id(1) == 0)
    def _():
        h1 = h_ref[...] + _dot(u_ref[...], wo_ref[0:CONV_CH, :]) + _dot(o_ref[...], wo_ref[CONV_CH:, :])
        out_ref[...] = h1
        xn_ref[...] = _rms(h1, g_ref[...]).astype(BF16)

    gu = _dot(xn_ref[...], wgu_ref[...])
    g = gu[:, 0:tf]
    mid = g * _sigmoid(g) * gu[:, tf:2 * tf]
    out_ref[...] += _dot(mid.astype(BF16), wd_ref[...])


def _even_ffn(h, u, o, wo, g, wgu, wd, tf):
    n, d = h.shape
    ff = wd.shape[0]
    tm = ROW_TILE
    row = lambda i, f: (i, 0)
    const = lambda i, f: (0, 0)
    return pl.pallas_call(
        functools.partial(_even_ffn_kernel, tf=tf),
        grid=(n // tm, ff // tf),
        in_specs=[pl.BlockSpec((tm, d), row), pl.BlockSpec((tm, CONV_CH), row),
                  pl.BlockSpec((tm, SB_WIDTH), row), pl.BlockSpec(wo.shape, const),
                  pl.BlockSpec((1, d), const),
                  pl.BlockSpec((d, 2 * tf), lambda i, f: (0, f)),
                  pl.BlockSpec((tf, d), lambda i, f: (f, 0))],
        out_specs=pl.BlockSpec((tm, d), row),
        out_shape=jax.ShapeDtypeStruct((n, d), F32),
        scratch_shapes=[pltpu.VMEM((tm, d), BF16)],
        compiler_params=_params("parallel", "arbitrary"),
        name="even_out_swiglu",
    )(h, u, o, wo, g, wgu, wd)


def _row_scatter_kernel(win_ref, x_ref, info_ref, off_ref, init_hbm, o_hbm, cbuf, sems):
    del init_hbm
    i = pl.program_id(0)
    last = pl.num_programs(0) - 1
    n_slots = cbuf.shape[1]
    buf = i % 2

    def drained(b):
        return pltpu.make_async_copy(cbuf.at[b], o_hbm.at[pl.ds(0, n_slots)], sems.at[b])

    info = info_ref[...]
    sub = lax.broadcasted_iota(jnp.int32, info.shape, 0)
    valid = info[ROUTE_VALID:ROUTE_VALID + 1, :] > 0.5

    def local_slot(e_row, r_row):
        expert = info[e_row:e_row + 1, :].astype(jnp.int32)
        off = jnp.sum(jnp.where(sub == expert, off_ref[:, 0:1], 0.0), axis=0, keepdims=True)
        return jnp.where(valid, off + info[r_row:r_row + 1, :], -1.0)

    slot = lax.broadcasted_iota(jnp.int32, (n_slots, info.shape[1]), 0).astype(F32)
    pick = (jnp.where(slot == local_slot(ROUTE_E1, ROUTE_R1), 1.0, 0.0)
            + jnp.where(slot == local_slot(ROUTE_E2, ROUTE_R2), 1.0, 0.0)).astype(BF16)
    rows = _dot(pick, x_ref[...]).astype(cbuf.dtype)

    @pl.when(i >= 2)
    def _():
        drained(buf).wait()

    cbuf[buf] = rows
    for s in range(ROW_NWIN):
        dst = pl.multiple_of(win_ref[i * ROW_NWIN + s], ROW_WIN)
        pltpu.make_async_copy(cbuf.at[buf, pl.ds(s * ROW_WIN, ROW_WIN)], o_hbm.at[pl.ds(dst, ROW_WIN)],
                              sems.at[buf]).start(priority=s % 2)

    @pl.when(i == last)
    def _():
        drained(buf).wait()

        @pl.when(i >= 1)
        def _():
            drained(1 - buf).wait()


def _row_scatter(win_rows, x, info, local_off, n_rows_alloc):
    n, d = x.shape
    tm = COMBINE_TILE
    init = jnp.zeros((n_rows_alloc, d), x.dtype)
    return pl.pallas_call(
        _row_scatter_kernel,
        grid_spec=pltpu.PrefetchScalarGridSpec(
            num_scalar_prefetch=1, grid=(n // tm,),
            in_specs=[pl.BlockSpec((tm, d), lambda i, w: (i, 0)), pl.BlockSpec((SUBLANES, tm), lambda i, w: (i, 0)),
                      pl.BlockSpec((SUBLANES, LANES), lambda i, w: (i, 0)), pl.BlockSpec(memory_space=pl.ANY)],
            out_specs=pl.BlockSpec(memory_space=pl.ANY),
            scratch_shapes=[pltpu.VMEM((2, ROW_NWIN * ROW_WIN, d), x.dtype), pltpu.SemaphoreType.DMA((2,))]),
        out_shape=jax.ShapeDtypeStruct(init.shape, init.dtype),
        input_output_aliases={4: 0},
        compiler_params=_params("arbitrary"),
        name="moe_row_scatter",
    )(win_rows, x, info, local_off, init)


def _gmm_kernel(te_ref, nu_ref, xs_ref, wg_ref, wu_ref, wd_ref, o_ref, acc_ref):
    p = pl.program_id(0)
    f = pl.program_id(1)
    last = pl.num_programs(1) - 1
    used = p < nu_ref[0]

    @pl.when(used)
    def _():
        x = xs_ref[...]
        g = _dot(x, wg_ref[0])
        mid = g * _sigmoid(g) * _dot(x, wu_ref[0])
        part = _dot(mid.astype(BF16), wd_ref[0])

        @pl.when(f == 0)
        def _():
            acc_ref[...] = part

        @pl.when(f > 0)
        def _():
            acc_ref[...] += part

        @pl.when(f == last)
        def _():
            o_ref[...] = acc_ref[...].astype(o_ref.dtype)

    @pl.when(jnp.logical_not(used) & (f == last))
    def _():
        o_ref[...] = jnp.zeros(o_ref.shape, o_ref.dtype)


def _grouped_ffn(tile_expert, n_used, xs, n_rows, wg, wu, wd):
    d = xs.shape[1]
    ff = wd.shape[1]
    tf = MOE_FF_TILE
    nf = ff // tf
    tg = MOE_TILE
    row = lambda p, f, te, nu: (p, 0)
    fidx = lambda p, f, nu: jnp.where(p < nu[0], f, nf - 1)
    up = pl.BlockSpec((1, d, tf), lambda p, f, te, nu: (te[p], 0, fidx(p, f, nu)))
    return pl.pallas_call(
        _gmm_kernel,
        grid_spec=pltpu.PrefetchScalarGridSpec(
            num_scalar_prefetch=2, grid=(n_rows // tg, nf),
            in_specs=[pl.BlockSpec((tg, d), row), up, up,
                      pl.BlockSpec((1, tf, d), lambda p, f, te, nu: (te[p], fidx(p, f, nu), 0))],
            out_specs=pl.BlockSpec((tg, d), row),
            scratch_shapes=[pltpu.VMEM((tg, d), F32)]),
        out_shape=jax.ShapeDtypeStruct((n_rows, d), BF16),
        compiler_params=_params("arbitrary", "arbitrary"),
        name="moe_grouped_ffn",
    )(tile_expert, n_used, xs, wg, wu, wd)


def _combine_kernel(win_ref, h_ref, tok_ref, w1_ref, w2_ref, starts_ref, rows_ref, ys_hbm, o_ref, gbuf, sems,
                    *scratch, final_rows=None):
    i = pl.program_id(0)
    tm = h_ref.shape[0]
    win = ROW_WIN

    def fetch(tile, slot):
        for s in range(ROW_NWIN):
            start = pl.multiple_of(win_ref[tile * ROW_NWIN + s], win)
            pltpu.make_async_copy(ys_hbm.at[pl.ds(start, win)], gbuf.at[slot, pl.ds(s * win, win)],
                                  sems.at[slot]).start(priority=s % 2)

    @pl.when(i == 0)
    def _():
        fetch(0, 0)

    @pl.when(i + 1 < pl.num_programs(0))
    def _():
        fetch(i + 1, (i + 1) % 2)

    slot = i % 2
    pltpu.make_async_copy(ys_hbm.at[pl.ds(0, ROW_NWIN * win)], gbuf.at[slot], sems.at[slot]).wait()

    tok = tok_ref[...]
    lane = lax.broadcasted_iota(jnp.int32, tok.shape, 1)
    field = lambda c: jnp.sum(jnp.where(lane == c, tok, 0.0), axis=1, keepdims=True)
    valid = field(ROUTE_VALID) > 0.5
    rows = rows_ref[0]
    reps = h_ref.shape[1] // LANES
    out = h_ref[...]
    for e_col, r_col, w_ref in ((ROUTE_E1, ROUTE_R1, w1_ref), (ROUTE_E2, ROUTE_R2, w2_ref)):
        start = jnp.sum(jnp.where(lane == field(e_col).astype(jnp.int32), starts_ref[...], 0.0),
                        axis=1, keepdims=True)
        dest = jnp.where(valid, start + field(r_col), -2.0)
        pick = jnp.where(dest == rows, 1.0, 0.0).astype(BF16)
        out = out + jnp.concatenate([w_ref[...]] * reps, axis=1) * _dot(pick, gbuf[slot])
    if final_rows is None:
        o_ref[...] = out
        return

    obuf, osems = scratch
    n_meta, t_real = final_rows
    bsz, ts = obuf.shape[1], obuf.shape[2]
    steps = h_ref.shape[0] // (bsz * ts)
    assert steps == 1 and n_meta % SUBLANES == 0 and n_meta < ts
    i_last = t_real // ts
    rem = t_real - i_last * ts
    n_steps = pl.num_programs(0)
    oslot = i % 2

    def full_copy(step, s):
        dst = pl.multiple_of(step * ts - n_meta, SUBLANES)
        return pltpu.make_async_copy(obuf.at[s], o_ref.at[:, pl.ds(dst, ts), :], osems.at[s])

    is_full = lambda step: (step >= 1) & (step < i_last)

    @pl.when(is_full(i - 2))
    def _():
        full_copy(i - 2, oslot).wait()

    for t in range(ts):
        obuf[oslot, :, t, :] = out[t * bsz:(t + 1) * bsz, :]

    @pl.when(i == 0)
    def _():
        c = pltpu.make_async_copy(obuf.at[oslot, :, pl.ds(n_meta, ts - n_meta), :],
                                  o_ref.at[:, pl.ds(0, ts - n_meta), :], osems.at[oslot])
        c.start()
        c.wait()

    @pl.when(is_full(i))
    def _():
        full_copy(i, oslot).start()

    if rem:
        @pl.when(i == i_last)
        def _():
            c = pltpu.make_async_copy(obuf.at[oslot, :, pl.ds(0, rem), :],
                                      o_ref.at[:, pl.ds(i_last * ts - n_meta, rem), :], osems.at[oslot])
            c.start()
            c.wait()

    @pl.when(i == n_steps - 1)
    def _():
        @pl.when(is_full(i - 1))
        def _():
            full_copy(i - 1, 1 - oslot).wait()

        @pl.when(is_full(i))
        def _():
            full_copy(i, oslot).wait()


def _combine(win_start, h, tok, w1b, w2b, starts_row, slot_rows, ys, final=None):
    n, d = h.shape
    tm = COMBINE_TILE
    n_slots = ROW_NWIN * ROW_WIN
    row = lambda i, w: (i, 0)
    scratch = [pltpu.VMEM((2, n_slots, d), ys.dtype), pltpu.SemaphoreType.DMA((2,))]
    if final is None:
        kern = _combine_kernel
        out_spec = pl.BlockSpec((tm, d), row)
        out_shape = jax.ShapeDtypeStruct(h.shape, F32)
    else:
        bsz, n_meta, t_real = final
        assert tm % bsz == 0 and bsz % SUBLANES == 0
        kern = functools.partial(_combine_kernel, final_rows=(n_meta, t_real))
        out_spec = pl.BlockSpec(memory_space=pl.ANY)
        out_shape = jax.ShapeDtypeStruct((bsz, t_real - n_meta, d), F32)
        scratch += [pltpu.VMEM((2, bsz, tm // bsz, d), F32), pltpu.SemaphoreType.DMA((2,))]
    return pl.pallas_call(
        kern,
        grid_spec=pltpu.PrefetchScalarGridSpec(
            num_scalar_prefetch=1, grid=(n // tm,),
            in_specs=[pl.BlockSpec((tm, d), row), pl.BlockSpec((tm, LANES), row), pl.BlockSpec((tm, LANES), row),
                      pl.BlockSpec((tm, LANES), row), pl.BlockSpec((1, LANES), lambda i, w: (0, 0)),
                      pl.BlockSpec((1, 1, n_slots), lambda i, w: (i, 0, 0)),
                      pl.BlockSpec(memory_space=pl.ANY)],
            out_specs=out_spec,
            scratch_shapes=scratch),
        out_shape=out_shape,
        compiler_params=_params("arbitrary"),
        name="moe_combine",
    )(win_start, h, tok, w1b, w2b, starts_row, slot_rows, ys)


def _gate_windows(lw, bd):
    out = []
    for c0 in range(0, lw, LRU_GATE_TILE):
        nc = min(LRU_GATE_TILE, lw - c0)
        r0 = (c0 // bd) * bd
        r1 = ((c0 + nc - 1) // bd + 1) * bd
        r0 = (r0 // LANES) * LANES
        r1 = min(-(-r1 // LANES) * LANES, lw)
        out.append((c0, nc, r0, r1 - r0))
    return out


def _lru_kernel(h_ref, ng_ref, win_ref, cw_ref, cb_ref, wg_ref, br_ref, bi_ref, lam_ref, wo_ref,
                o_ref, xbuf_ref, hs_ref, a_ref, b_ref, y_ref, gate_ref, res_ref, *, windows, bsz):
    ti = pl.program_id(0)
    rows, lw = a_ref.shape
    halo = (LRU_CONV_WIDTH - 1) * bsz

    @pl.when(ti == 0)
    def _():
        xbuf_ref[0:halo, :] = jnp.zeros((halo, lw), F32)
        hs_ref[...] = jnp.zeros(hs_ref.shape, F32)

    @pl.when(ti > 0)
    def _():
        xbuf_ref[0:halo, :] = xbuf_ref[rows:rows + halo, :]

    res_ref[...] = jnp.concatenate([h_ref[:, t, :] for t in range(rows // bsz)], axis=0)
    proj = _dot(_rms(res_ref[...], ng_ref[...]).astype(BF16), win_ref[...])
    gate_ref[...] = jax.nn.gelu(proj[:, 0:lw], approximate=True)
    xbuf_ref[halo:halo + rows, :] = proj[:, lw:2 * lw]
    xc = jnp.broadcast_to(cb_ref[...], (rows, lw))
    for k in range(LRU_CONV_WIDTH):
        xc = xc + cw_ref[k:k + 1, :] * xbuf_ref[k * bsz:k * bsz + rows, :]
    xcb = xc.astype(BF16)
    sp_lam = jnp.log(1.0 + jnp.exp(-lam_ref[...]))
    log2_a_per_r = (-LRU_C * LOG2E) * sp_lam
    wcol = 0
    for (c0, nc, r0, nr) in windows:
        rg = _dot(xcb[:, r0:r0 + nr], wg_ref[r0:r0 + nr, wcol:wcol + 2 * nc])
        wcol += 2 * nc
        r = _sigmoid(rg[:, 0:nc] + br_ref[:, c0:c0 + nc])
        ig = _sigmoid(rg[:, nc:2 * nc] + bi_ref[:, c0:c0 + nc])
        a = jnp.exp2(r * log2_a_per_r[:, c0:c0 + nc])
        a_ref[:, c0:c0 + nc] = a
        one_m_a2 = 1.0 - a * a
        root = jnp.where(one_m_a2 > 0.0, one_m_a2 * lax.rsqrt(one_m_a2), 0.0)
        b_ref[:, c0:c0 + nc] = root * (ig * xc[:, c0:c0 + nc])

    def step(t, hprev):
        r0 = pl.multiple_of(t * bsz, SUBLANES)
        hnew = a_ref[pl.ds(r0, bsz), :] * hprev + b_ref[pl.ds(r0, bsz), :]
        y_ref[pl.ds(r0, bsz), :] = hnew
        return hnew

    hs_ref[...] = lax.fori_loop(0, rows // bsz, step, hs_ref[...], unroll=LRU_SCAN_UNROLL)
    gy = (gate_ref[...] * y_ref[...]).astype(BF16)
    o_ref[...] = res_ref[...] + _dot(gy, wo_ref[...])


def _lru_mixer(h, ng, win, cw, cb, wg, br, bi, lam, wo, windows):
    bsz, tp, d = h.shape
    n = bsz * tp
    lw = lam.shape[-1]
    assert bsz % SUBLANES == 0 and LRU_ROWS % bsz == 0 and n % LRU_ROWS == 0
    rows = LRU_ROWS
    halo = (LRU_CONV_WIDTH - 1) * bsz
    const = lambda t: (0, 0)
    blk = lambda t: (t, 0)
    resident = lambda w: pl.BlockSpec(w.shape, const, pipeline_mode=pl.Buffered(1))
    return pl.pallas_call(
        functools.partial(_lru_kernel, windows=windows, bsz=bsz),
        grid=(n // rows,),
        in_specs=[pl.BlockSpec((bsz, rows // bsz, d), lambda t: (0, t, 0)), pl.BlockSpec((1, d), const), resident(win),
                  pl.BlockSpec(cw.shape, const), pl.BlockSpec((1, lw), const), resident(wg),
                  pl.BlockSpec((1, lw), const), pl.BlockSpec((1, lw), const),
                  pl.BlockSpec((1, lw), const), resident(wo)],
        out_specs=pl.BlockSpec((rows, d), blk),
        out_shape=jax.ShapeDtypeStruct((n, d), F32),
        scratch_shapes=[pltpu.VMEM((halo + rows, lw), F32), pltpu.VMEM((bsz, lw), F32),
                        pltpu.VMEM((rows, lw), F32), pltpu.VMEM((rows, lw), F32), pltpu.VMEM((rows, lw), F32),
                        pltpu.VMEM((rows, lw), F32), pltpu.VMEM((rows, d), F32)],
        compiler_params=_params("arbitrary"),
        name="rglru_mixer",
    )(h, ng, win, cw, cb, wg, br, bi, lam, wo)


ROUTE_E1, ROUTE_E2, ROUTE_R1, ROUTE_R2, ROUTE_VALID = range(5)


def _router_kernel(h_ref, g_ref, rw_ref, ltri_ref, xn_ref, info_ref, tok_ref, w1_ref, w2_ref, before_ref, cnt_ref,
                   run_ref, *, n_real_rows):
    i = pl.program_id(0)
    tm = h_ref.shape[0]

    @pl.when(i == 0)
    def _():
        run_ref[...] = jnp.zeros(run_ref.shape, F32)

    xn = _rms(h_ref[...], g_ref[...])
    xn_ref[...] = xn.astype(BF16)
    split = lambda a: (a.astype(BF16), (a - a.astype(BF16).astype(F32)).astype(BF16))
    (x_hi, x_lo), (r_hi, r_lo) = split(xn), split(rw_ref[...])
    logits = _dot(x_hi, r_hi) + (_dot(x_hi, r_lo) + _dot(x_lo, r_hi))
    lane = lax.broadcasted_iota(jnp.int32, logits.shape, 1)
    neg = jnp.float32(-jnp.inf)
    logits = jnp.where(lane < N_EXPERTS, logits, neg)
    top1 = jnp.max(logits, axis=1, keepdims=True)
    idx1 = jnp.min(jnp.where(logits == top1, lane, LANES), axis=1, keepdims=True)
    rest = jnp.where(lane == idx1, neg, logits)
    top2 = jnp.max(rest, axis=1, keepdims=True)
    idx2 = jnp.min(jnp.where(rest == top2, lane, LANES), axis=1, keepdims=True)
    w1 = 1.0 / (1.0 + jnp.exp(top2 - top1))

    rowg = lax.broadcasted_iota(jnp.int32, (tm, 1), 0) + i * tm
    valid = jnp.where(rowg < n_real_rows, 1.0, 0.0)
    onehot = (jnp.where(lane == idx1, 1.0, 0.0) + jnp.where(lane == idx2, 1.0, 0.0)) * valid
    before_ref[...] = jnp.broadcast_to(run_ref[...], before_ref.shape)
    before = _dot(ltri_ref[...], onehot.astype(BF16)) + run_ref[...]
    rank1 = jnp.sum(jnp.where(lane == idx1, before, 0.0), axis=1, keepdims=True)
    rank2 = jnp.sum(jnp.where(lane == idx2, before, 0.0), axis=1, keepdims=True)
    run_ref[...] += jnp.ceil(jnp.sum(onehot, axis=0, keepdims=True) / ROW_WIN) * ROW_WIN
    cnt_ref[...] = run_ref[...]
    w1_ref[...] = jnp.broadcast_to(w1, (tm, LANES))
    w2_ref[...] = jnp.broadcast_to(1.0 - w1, (tm, LANES))
    fields = {ROUTE_E1: idx1.astype(F32), ROUTE_E2: idx2.astype(F32),
              ROUTE_R1: rank1, ROUTE_R2: rank2, ROUTE_VALID: valid}
    info = jnp.zeros(logits.shape, F32)
    for col, val in fields.items():
        info = jnp.where(lane == col, val, info)
    tok_ref[...] = info
    info_ref[...] = info.T[0:SUBLANES, :]


def _router(h, g, rw, ltri, n_real_rows):
    n, d = h.shape
    tm = ROW_TILE
    row = lambda i: (i, 0)
    const = lambda i: (0, 0)
    return pl.pallas_call(
        functools.partial(_router_kernel, n_real_rows=n_real_rows),
        grid=(n // tm,),
        in_specs=[pl.BlockSpec((tm, d), row), pl.BlockSpec((1, d), const), pl.BlockSpec(rw.shape, const),
                  pl.BlockSpec(ltri.shape, const)],
        out_specs=[pl.BlockSpec((tm, d), row), pl.BlockSpec((SUBLANES, tm), row),
                   pl.BlockSpec((tm, LANES), row), pl.BlockSpec((tm, LANES), row),
                   pl.BlockSpec((tm, LANES), row), pl.BlockSpec((SUBLANES, LANES), row),
                   pl.BlockSpec((1, LANES), const)],
        out_shape=[jax.ShapeDtypeStruct((n, d), BF16), jax.ShapeDtypeStruct((n // tm * SUBLANES, tm), F32),
                   jax.ShapeDtypeStruct((n, LANES), F32), jax.ShapeDtypeStruct((n, LANES), F32),
                   jax.ShapeDtypeStruct((n, LANES), F32), jax.ShapeDtypeStruct((n // tm * SUBLANES, LANES), F32),
                   jax.ShapeDtypeStruct((1, LANES), F32)],
        scratch_shapes=[pltpu.VMEM((1, LANES), F32)],
        compiler_params=_params("arbitrary"),
        name="router",
    )(h, g, rw, ltri)


def _routing_tables(before, counts, n_rows):
    tg = MOE_TILE
    win = ROW_WIN
    cnt = counts[0, :N_EXPERTS].astype(jnp.int32)
    padded = (cnt + tg - 1) // tg * tg
    ends = jnp.cumsum(padded)
    starts = ends - padded
    tile_start = jnp.arange(n_rows // tg, dtype=jnp.int32) * tg
    tile_expert = jnp.minimum(jnp.sum(tile_start[:, None] >= ends[None, :], axis=1), N_EXPERTS - 1)
    n_used = (ends[-1] // tg).reshape(1)
    bef = before.reshape(-1, SUBLANES, LANES)[:, 0, :N_EXPERTS].astype(jnp.int32)
    nwin = (jnp.concatenate([bef[1:], cnt[None, :]], axis=0) - bef) // win
    base = jnp.cumsum(nwin, axis=1) - nwin
    slot = jnp.arange(ROW_NWIN, dtype=jnp.int32)[None, :, None]
    mine = (slot >= base[:, None, :]) & (slot < (base + nwin)[:, None, :])
    rows = jnp.sum(jnp.where(mine, (starts[None, :] + bef)[:, None, :] + win * (slot - base[:, None, :]), 0), axis=2)
    used = jnp.any(mine, axis=2)
    parity = (jnp.arange(bef.shape[0], dtype=jnp.int32) % 2)[:, None]
    scatter_rows = jnp.where(used, rows, n_rows + win * (parity * ROW_NWIN + slot[:, :, 0]))
    gather_rows = jnp.where(used, rows, 0)
    slot_rows = jnp.where(used[..., None], rows[..., None] + jnp.arange(win, dtype=jnp.int32), -1)
    slot_rows = slot_rows.reshape(-1, 1, ROW_NWIN * win).astype(F32)
    starts_row = jnp.zeros((1, LANES), F32).at[0, :N_EXPERTS].set(starts.astype(F32))
    local_off = jnp.zeros(bef.shape[:1] + (SUBLANES,), jnp.int32).at[:, :N_EXPERTS].set(win * base - bef)
    local_off = jnp.broadcast_to(local_off.astype(F32)[:, :, None], local_off.shape + (LANES,)).reshape(-1, LANES)
    return (tile_expert.astype(jnp.int32), n_used.astype(jnp.int32), scatter_rows.reshape(-1).astype(jnp.int32),
            gather_rows.reshape(-1).astype(jnp.int32), slot_rows, starts_row, local_off)


def _interleave_gate_up(wg, wu, tf):
    e, d, ff = wg.shape
    g = wg.astype(BF16).reshape(e, d, ff // tf, tf)
    u = wu.astype(BF16).reshape(e, d, ff // tf, tf)
    return jnp.concatenate([g, u], axis=3).reshape(e, d, 2 * ff)


def _dense_block_diag(w):
    nb, bd, _ = w.shape
    eye = jnp.eye(nb, dtype=w.dtype)
    return (w[:, :, None, :] * eye[:, None, :, None]).reshape(nb * bd, nb * bd)


def _pack_lru_gates(wr, wi, windows):
    dr = _dense_block_diag(wr).astype(BF16)
    di = _dense_block_diag(wi).astype(BF16)
    cols = []
    for (c0, nc, _, _) in windows:
        cols += [dr[:, c0:c0 + nc], di[:, c0:c0 + nc]]
    return jnp.concatenate(cols, axis=1)


def _ffn_chunk(ff):
    for tf in (896, 512, 384, 256, 128):
        if ff % tf == 0:
            return tf
    raise ValueError(f"unsupported d_ff {ff}")


def kernel(x, meta_tokens, mix_norm_even, w_in_even, conv_w, conv_b, conv_ln_g, conv_ln_b, q_norm_g, k_norm_g, w_out_even, ffn_norm_even, ffn_w_gate, ffn_w_up, ffn_w_down, mix_norm_odd, w_in_odd, lru_conv_w, lru_conv_b, gate_r_w, gate_r_b, gate_i_w, gate_i_b, lru_lambda, w_out_odd, ffn_norm_odd, router_w, moe_w_gate, moe_w_up, moe_w_down):
    bsz, seq, d = x.shape
    t_real = N_META + seq
    tp = -(-t_real // TIME_TILE) * TIME_TILE
    n = bsz * tp
    assert n % ROW_TILE == 0
    depth = mix_norm_even.shape[0] + mix_norm_odd.shape[0]

    meta = jnp.broadcast_to(meta_tokens[None].astype(x.dtype), (bsz, N_META, d))
    h = jnp.concatenate([meta, x, jnp.zeros((bsz, tp - t_real, d), x.dtype)], axis=1).reshape(n, d)

    head_mean = jnp.kron(jnp.eye(MXU_DIM // SB_HEAD_DIM, dtype=F32),
                         jnp.full((SB_HEAD_DIM, SB_HEAD_DIM), 1.0 / SB_HEAD_DIM, F32)).astype(BF16)
    kk = jnp.arange(ATT_TK)
    neg_tri = -(kk[:, None] >= kk[None, :]).astype(BF16)
    row2 = lambda a: a.reshape(1, -1)

    time_major = False
    for layer in range(depth):
        p = layer // 2
        if layer % 2 == 0:
            if time_major:
                h = h.reshape(tp, bsz, d).transpose(1, 0, 2).reshape(n, d)
                time_major = False
            u, q, k, v = _even_in_proj(h, row2(mix_norm_even[p]), w_in_even[p].astype(BF16),
                                       row2(jnp.tile(q_norm_g[p], SB_HEADS)),
                                       row2(jnp.tile(k_norm_g[p], SB_HEADS)), head_mean)
            u = _conv_module(u.reshape(bsz, tp, CONV_CH), conv_w[p], row2(conv_b[p]),
                             row2(conv_ln_g[p]), row2(conv_ln_b[p]))
            o = _attention(q.reshape(bsz, tp, SB_WIDTH), k.reshape(bsz, tp, SB_WIDTH),
                           v.reshape(bsz, tp, SB_WIDTH), neg_tri, t_real)
            tf = _ffn_chunk(ffn_w_gate.shape[-1])
            h = _even_ffn(h, u.reshape(n, CONV_CH), o.reshape(n, SB_WIDTH), w_out_even[p].astype(BF16),
                          row2(ffn_norm_even[p]),
                          _interleave_gate_up(ffn_w_gate[p][None], ffn_w_up[p][None], tf)[0],
                          ffn_w_down[p].astype(BF16), tf)
        else:
            if time_major:
                h = h.reshape(tp, bsz, d).transpose(1, 0, 2).reshape(n, d)
            time_major = True
            lw = lru_lambda.shape[-1]
            windows = _gate_windows(lw, lw // LRU_BLOCKS)
            h = _lru_mixer(h.reshape(bsz, tp, d), row2(mix_norm_odd[p]), w_in_odd[p].astype(BF16),
                           lru_conv_w[p], row2(lru_conv_b[p]),
                           _pack_lru_gates(gate_r_w[p], gate_i_w[p], windows),
                           row2(gate_r_b[p]), row2(gate_i_b[p]), row2(lru_lambda[p]),
                           w_out_odd[p].astype(BF16), windows)
            rw = jnp.pad(router_w[p], ((0, 0), (0, LANES - N_EXPERTS)))
            kk = jnp.arange(ROW_TILE)
            ltri = (kk[:, None] > kk[None, :]).astype(BF16)
            xn, info, tok, w1b, w2b, before, counts = _router(h, row2(ffn_norm_odd[p]), rw, ltri, t_real * bsz)
            tg = MOE_TILE
            assert ROW_TILE == COMBINE_TILE
            n_tiles = n // COMBINE_TILE
            n_rows = 2 * bsz * t_real + n_tiles * N_EXPERTS * (ROW_WIN - 1) + N_EXPERTS * (tg - 1)
            n_rows = -(-n_rows // tg) * tg
            tile_expert, n_used, scatter_rows, gather_rows, slot_rows, starts_row, local_off = _routing_tables(
                before, counts, n_rows)
            xs = _row_scatter(scatter_rows, xn, info, local_off, n_rows + 2 * ROW_NWIN * ROW_WIN)
            ys = _grouped_ffn(tile_expert, n_used, xs, n_rows, moe_w_gate[p].astype(BF16),
                              moe_w_up[p].astype(BF16), moe_w_down[p].astype(BF16))
            if layer + 1 == depth:
                return _combine(gather_rows, h, tok, w1b, w2b, starts_row, slot_rows, ys, (bsz, N_META, t_real))
            h = _combine(gather_rows, h, tok, w1b, w2b, starts_row, slot_rows, ys)
    if time_major:
        return h.reshape(tp, bsz, d)[N_META:t_real].transpose(1, 0, 2)
    return h.reshape(bsz, tp, d)[:, N_META:t_real]
```

```python
import functools
import math

import jax
import jax.numpy as jnp
from jax import lax
from jax.experimental import pallas as pl
from jax.experimental.pallas import tpu as pltpu

F32 = jnp.float32
BF16 = jnp.bfloat16

EPS = 1e-6
N_META = 16
CONV_CH = 512
CONV_WIDTH = 31
SB_HEADS = 8
SB_HEAD_DIM = 64
SB_WIDTH = SB_HEADS * SB_HEAD_DIM
LRU_BLOCKS = 16
LRU_CONV_WIDTH = 4
LRU_C = 8.0
N_EXPERTS = 8

LANES = 128
MXU_DIM = 256
TIME_TILE = 256
VMEM_LIMIT = 50 * 1024 * 1024

ROW_TILE = 512
CONV_TT = 256
CONV_HALO = 32
CONV_CHUNK = 128
ATT_TQ = 256
ATT_TK = 256
LOG2E = 1.4426950408889634
MASKED_EXPONENT = -1e30
SOFTPLUS_CUTOFF = 126.0
LRU_ROWS = 512
LRU_SCAN_UNROLL = 4
LRU_GATE_TILE = 256
MOE_TILE = 512
MOE_FF_TILE = 1792
COMBINE_TILE = 512
ROW_WIN = 16
ROW_NWIN = 2 * COMBINE_TILE // ROW_WIN + N_EXPERTS
SUBLANES = 8
BF16_ROWS = 16


def _dot(a, b):
    return jnp.dot(a, b, preferred_element_type=F32)


def _params(*sem):
    return pltpu.CompilerParams(dimension_semantics=sem, vmem_limit_bytes=VMEM_LIMIT)


def _sigmoid(x):
    return 0.5 * jnp.tanh(0.5 * x) + 0.5


def _rms(x, g):
    ms = jnp.mean(x * x, axis=-1, keepdims=True)
    return x * lax.rsqrt(ms + EPS) * g


def _even_in_kernel(h_ref, g_ref, w_ref, qg_ref, kg_ref, hm_ref, u_ref, q_ref, k_ref, v_ref):
    xn = _rms(h_ref[...], g_ref[...]).astype(BF16)
    a = _dot(xn, w_ref[:, 0:CONV_CH])
    gate = _dot(xn, w_ref[:, CONV_CH:2 * CONV_CH])
    u_ref[...] = (a * _sigmoid(gate)).astype(BF16)

    def head_norm(y, gain):
        yy = (y * y).astype(BF16)
        parts = [_dot(yy[:, c:c + MXU_DIM], hm_ref[...]) for c in range(0, SB_WIDTH, MXU_DIM)]
        ms = jnp.concatenate(parts, axis=1)
        return y * lax.rsqrt(ms + EPS) * gain

    c0 = 2 * CONV_CH
    q = head_norm(_dot(xn, w_ref[:, c0:c0 + SB_WIDTH]), qg_ref[...])
    q_ref[...] = (q * (LOG2E / math.sqrt(SB_HEAD_DIM))).astype(BF16)
    k = head_norm(_dot(xn, w_ref[:, c0 + SB_WIDTH:c0 + 2 * SB_WIDTH]), kg_ref[...])
    k_ref[...] = k.astype(BF16)
    v_ref[...] = _dot(xn, w_ref[:, c0 + 2 * SB_WIDTH:c0 + 3 * SB_WIDTH]).astype(BF16)


def _even_in_proj(h, g, w, qg, kg, hm):
    n, d = h.shape
    tm = ROW_TILE
    row = lambda i: (i, 0)
    const = lambda i: (0, 0)
    out = jax.ShapeDtypeStruct((n, SB_WIDTH), BF16)
    return pl.pallas_call(
        _even_in_kernel,
        grid=(n // tm,),
        in_specs=[pl.BlockSpec((tm, d), row), pl.BlockSpec((1, d), const),
                  pl.BlockSpec(w.shape, const), pl.BlockSpec((1, SB_WIDTH), const),
                  pl.BlockSpec((1, SB_WIDTH), const), pl.BlockSpec(hm.shape, const)],
        out_specs=[pl.BlockSpec((tm, SB_WIDTH), row)] * 4,
        out_shape=[out] * 4,
        compiler_params=_params("parallel"),
        name="even_in_proj",
    )(h, g, w, qg, kg, hm)


def _conv_kernel(cur_ref, halo_ref, w_ref, b_ref, lg_ref, lb_ref, o_ref, buf_ref, sh_ref):
    i = pl.program_id(1)
    tt = cur_ref.shape[1]
    rows = CONV_HALO + tt
    halo = halo_ref[0].astype(F32)
    buf_ref[0:CONV_HALO, :] = jnp.where(i > 0, halo, 0.0)
    buf_ref[CONV_HALO:rows, :] = cur_ref[0].astype(F32)
    for s in range(1, SUBLANES):
        sh_ref[s - 1, 0:rows - SUBLANES, :] = buf_ref[s:s + rows - SUBLANES, :]
    off = CONV_HALO - (CONV_WIDTH - 1)
    for c in range(tt // CONV_CHUNK):
        r0 = c * CONV_CHUNK
        acc = jnp.broadcast_to(b_ref[...], (CONV_CHUNK, CONV_CH))
        for k in range(CONV_WIDTH):
            shift = (off + k) % SUBLANES
            base = r0 + off + k - shift
            if shift == 0:
                tap = buf_ref[base:base + CONV_CHUNK, :]
            else:
                tap = sh_ref[shift - 1, base:base + CONV_CHUNK, :]
            acc = acc + w_ref[k:k + 1, :] * tap
        mu = jnp.mean(acc, axis=-1, keepdims=True)
        xc = acc - mu
        var = jnp.mean(xc * xc, axis=-1, keepdims=True)
        y = xc * lax.rsqrt(var + EPS) * lg_ref[...] + lb_ref[...]
        o_ref[0, r0:r0 + CONV_CHUNK, :] = (y * _sigmoid(y)).astype(BF16)


def _conv_module(u, w, b, lg, lb):
    bsz, tp, c = u.shape
    tt = CONV_TT
    per = tt // CONV_HALO
    const = lambda bi, i: (0, 0)
    return pl.pallas_call(
        _conv_kernel,
        grid=(bsz, tp // tt),
        in_specs=[pl.BlockSpec((1, tt, c), lambda bi, i: (bi, i, 0)),
                  pl.BlockSpec((1, CONV_HALO, c), lambda bi, i: (bi, jnp.maximum(i * per - 1, 0), 0)),
                  pl.BlockSpec(w.shape, const), pl.BlockSpec((1, c), const),
                  pl.BlockSpec((1, c), const), pl.BlockSpec((1, c), const)],
        out_specs=pl.BlockSpec((1, tt, c), lambda bi, i: (bi, i, 0)),
        out_shape=jax.ShapeDtypeStruct(u.shape, BF16),
        scratch_shapes=[pltpu.VMEM((CONV_HALO + tt, c), F32),
                        pltpu.VMEM((SUBLANES - 1, CONV_HALO + tt - SUBLANES, c), F32)],
        compiler_params=_params("parallel", "parallel"),
        name="conv_module",
    )(u, u, w, b, lg, lb)


def _attn_kernel(*refs, tq, tile0, aliased, n_compute):
    if aliased:
        refs = refs[1:]
    o_ref = refs[4]
    step = pl.program_id(1)

    @pl.when(step < n_compute)
    def _():
        _attn_tile(*refs, tq=tq, row0=(step + tile0) * ATT_TQ)

    @pl.when(step >= n_compute)
    def _():
        o_ref[...] = jnp.zeros(o_ref.shape, BF16)


def _attn_tile(q_ref, k_ref, v_ref, tri_ref, o_ref, q2_ref, e_ref, tot_ref, acc_ref, car_ref, *, tq, row0):
    tk = ATT_TK
    npairs = q_ref.shape[2] // LANES
    lane_q = lax.broadcasted_iota(jnp.int32, (tq, LANES), 1)

    for hp in range(npairs):
        q = q_ref[0, 0:tq, hp * LANES:(hp + 1) * LANES]
        zero = jnp.zeros_like(q)
        q2_ref[hp, 0:tq, :] = jnp.where(lane_q < SB_HEAD_DIM, q, zero)
        q2_ref[hp, tq:2 * tq, :] = jnp.where(lane_q >= SB_HEAD_DIM, q, zero)
        car_ref[hp] = jnp.zeros((2 * tq, LANES), F32)
        acc_ref[hp] = jnp.zeros((2 * tq, LANES), F32)

    def stage_a(j, mask, slot):
        k0 = pl.multiple_of(j * tk, tk)
        for hp in range(npairs):
            kt = k_ref[0, pl.ds(k0, tk), hp * LANES:(hp + 1) * LANES]
            z = lax.dot_general(q2_ref[hp], kt, (((1,), (1,)), ((), ())), preferred_element_type=F32)
            sp = jnp.where(z > SOFTPLUS_CUTOFF, z, jnp.log(1.0 + jnp.exp2(z)) * LOG2E)
            if mask is not None:
                sp = jnp.where(mask, sp, 0.0)
            e = z + _dot(sp.astype(BF16), tri_ref[...])
            if mask is not None:
                e = jnp.where(mask, e, MASKED_EXPONENT)
            e_ref[slot, hp] = e
            tot_ref[slot, hp] = jnp.broadcast_to(jnp.sum(sp, axis=1, keepdims=True), (2 * tq, LANES))

    def stage_b(j, slot):
        k0 = pl.multiple_of(j * tk, tk)
        for hp in range(npairs):
            car = car_ref[hp]
            w = jnp.exp2(e_ref[slot, hp] + jnp.concatenate([car] * (tk // LANES), axis=1)).astype(BF16)
            vt = v_ref[0, pl.ds(k0, tk), hp * LANES:(hp + 1) * LANES]
            acc_ref[hp] += _dot(w, vt)
            car_ref[hp] = car - tot_ref[slot, hp]

    j_last = row0 // tk
    row = lax.broadcasted_iota(jnp.int32, (2 * tq, tk), 0)
    col = lax.broadcasted_iota(jnp.int32, (2 * tq, tk), 1)
    row = jnp.where(row >= tq, row - tq, row) + row0
    odd = j_last % 2
    stage_a(j_last, (col + j_last * tk) < row, odd)

    @pl.when(odd == 1)
    def _():
        stage_b(j_last, 1)
        stage_a(j_last - 1, None, 0)

    def body(jj, carry):
        j = j_last - odd - 1 - 2 * jj
        stage_b(j + 1, 0)
        stage_a(j, None, 1)
        stage_b(j, 1)
        stage_a(j - 1, None, 0)
        return carry

    lax.fori_loop(0, (j_last - odd) // 2, body, 0)
    stage_b(0, 0)
    if tq < o_ref.shape[1]:
        o_ref[0, tq:, :] = jnp.zeros((o_ref.shape[1] - tq, o_ref.shape[2]), BF16)
    for hp in range(npairs):
        o_ref[0, 0:tq, hp * LANES:(hp + 1) * LANES] = jnp.where(
            lane_q < SB_HEAD_DIM, acc_ref[hp, 0:tq, :], acc_ref[hp, tq:2 * tq, :]).astype(BF16)


def _attention(q, k, v, tri, t_real):
    bsz, tp, width = q.shape
    bq, tk = ATT_TQ, ATT_TK
    assert bq <= tk and tk % bq == 0
    npairs = width // LANES
    n_full = t_real // bq
    rem = -(-(t_real - n_full * bq) // BF16_ROWS) * BF16_ROWS

    def call(tq, tile0, n_compute, n_tiles, prev):
        blk = lambda b, i: (b, i + tile0, 0)
        full = lambda b, i: (b, 0, 0)
        in_specs = [pl.BlockSpec((1, bq, width), blk), pl.BlockSpec((1, tp, width), full),
                    pl.BlockSpec((1, tp, width), full), pl.BlockSpec(tri.shape, lambda b, i: (0, 0))]
        args = [q, k, v, tri]
        if prev is not None:
            in_specs.insert(0, pl.BlockSpec(memory_space=pl.ANY))
            args.insert(0, prev)
        return pl.pallas_call(
            functools.partial(_attn_kernel, tq=tq, tile0=tile0, aliased=prev is not None, n_compute=n_compute),
            grid=(bsz, n_tiles),
            in_specs=in_specs,
            out_specs=pl.BlockSpec((1, bq, width), blk),
            out_shape=jax.ShapeDtypeStruct(q.shape, BF16),
            input_output_aliases={} if prev is None else {0: 0},
            scratch_shapes=[pltpu.VMEM((npairs, 2 * tq, LANES), BF16),
                            pltpu.VMEM((2, npairs, 2 * tq, tk), F32),
                            pltpu.VMEM((2, npairs, 2 * tq, LANES), F32),
                            pltpu.VMEM((npairs, 2 * tq, LANES), F32),
                            pltpu.VMEM((npairs, 2 * tq, LANES), F32)],
            compiler_params=_params("parallel", "arbitrary"),
            name="stick_breaking" if prev is None else "stick_breaking_tail",
        )(*args)

    out = call(bq, 0, n_full, tp // bq, None)
    if rem:
        out = call(rem, n_full, 1, 1, out)
    return out


def _even_ffn_kernel(h_ref, u_ref, o_ref, wo_ref, g_ref, wgu_ref, wd_ref, out_ref, xn_ref, *, tf):
    @pl.when(pl.program_id(1) == 0)
    def _():
        h1 = h_ref[...] + _dot(u_ref[...], wo_ref[0:CONV_CH, :]) + _dot(o_ref[...], wo_ref[CONV_CH:, :])
        out_ref[...] = h1
        xn_ref[...] = _rms(h1, g_ref[...]).astype(BF16)

    gu = _dot(xn_ref[...], wgu_ref[...])
    g = gu[:, 0:tf]
    mid = g * _sigmoid(g) * gu[:, tf:2 * tf]
    out_ref[...] += _dot(mid.astype(BF16), wd_ref[...])


def _even_ffn(h, u, o, wo, g, wgu, wd, tf):
    n, d = h.shape
    ff = wd.shape[0]
    tm = ROW_TILE
    row = lambda i, f: (i, 0)
    const = lambda i, f: (0, 0)
    return pl.pallas_call(
        functools.partial(_even_ffn_kernel, tf=tf),
        grid=(n // tm, ff // tf),
        in_specs=[pl.BlockSpec((tm, d), row), pl.BlockSpec((tm, CONV_CH), row),
                  pl.BlockSpec((tm, SB_WIDTH), row), pl.BlockSpec(wo.shape, const),
                  pl.BlockSpec((1, d), const),
                  pl.BlockSpec((d, 2 * tf), lambda i, f: (0, f)),
                  pl.BlockSpec((tf, d), lambda i, f: (f, 0))],
        out_specs=pl.BlockSpec((tm, d), row),
        out_shape=jax.ShapeDtypeStruct((n, d), F32),
        scratch_shapes=[pltpu.VMEM((tm, d), BF16)],
        compiler_params=_params("parallel", "arbitrary"),
        name="even_out_swiglu",
    )(h, u, o, wo, g, wgu, wd)


def _row_scatter_kernel(win_ref, x_ref, info_ref, off_ref, init_hbm, o_hbm, cbuf, sems):
    del init_hbm
    i = pl.program_id(0)
    last = pl.num_programs(0) - 1
    n_slots = cbuf.shape[1]
    buf = i % 2

    def drained(b):
        return pltpu.make_async_copy(cbuf.at[b], o_hbm.at[pl.ds(0, n_slots)], sems.at[b])

    info = info_ref[...]
    sub = lax.broadcasted_iota(jnp.int32, info.shape, 0)
    valid = info[ROUTE_VALID:ROUTE_VALID + 1, :] > 0.5

    def local_slot(e_row, r_row):
        expert = info[e_row:e_row + 1, :].astype(jnp.int32)
        off = jnp.sum(jnp.where(sub == expert, off_ref[:, 0:1], 0.0), axis=0, keepdims=True)
        return jnp.where(valid, off + info[r_row:r_row + 1, :], -1.0)

    slot = lax.broadcasted_iota(jnp.int32, (n_slots, info.shape[1]), 0).astype(F32)
    pick = (jnp.where(slot == local_slot(ROUTE_E1, ROUTE_R1), 1.0, 0.0)
            + jnp.where(slot == local_slot(ROUTE_E2, ROUTE_R2), 1.0, 0.0)).astype(BF16)
    rows = _dot(pick, x_ref[...]).astype(cbuf.dtype)

    @pl.when(i >= 2)
    def _():
        drained(buf).wait()

    cbuf[buf] = rows
    for s in range(ROW_NWIN):
        dst = pl.multiple_of(win_ref[i * ROW_NWIN + s], ROW_WIN)
        pltpu.make_async_copy(cbuf.at[buf, pl.ds(s * ROW_WIN, ROW_WIN)], o_hbm.at[pl.ds(dst, ROW_WIN)],
                              sems.at[buf]).start(priority=s % 2)

    @pl.when(i == last)
    def _():
        drained(buf).wait()

        @pl.when(i >= 1)
        def _():
            drained(1 - buf).wait()


def _row_scatter(win_rows, x, info, local_off, n_rows_alloc):
    n, d = x.shape
    tm = COMBINE_TILE
    init = jnp.zeros((n_rows_alloc, d), x.dtype)
    return pl.pallas_call(
        _row_scatter_kernel,
        grid_spec=pltpu.PrefetchScalarGridSpec(
            num_scalar_prefetch=1, grid=(n // tm,),
            in_specs=[pl.BlockSpec((tm, d), lambda i, w: (i, 0)), pl.BlockSpec((SUBLANES, tm), lambda i, w: (i, 0)),
                      pl.BlockSpec((SUBLANES, LANES), lambda i, w: (i, 0)), pl.BlockSpec(memory_space=pl.ANY)],
            out_specs=pl.BlockSpec(memory_space=pl.ANY),
            scratch_shapes=[pltpu.VMEM((2, ROW_NWIN * ROW_WIN, d), x.dtype), pltpu.SemaphoreType.DMA((2,))]),
        out_shape=jax.ShapeDtypeStruct(init.shape, init.dtype),
        input_output_aliases={4: 0},
        compiler_params=_params("arbitrary"),
        name="moe_row_scatter",
    )(win_rows, x, info, local_off, init)


def _gmm_kernel(te_ref, nu_ref, xs_ref, wg_ref, wu_ref, wd_ref, o_ref, acc_ref):
    p = pl.program_id(0)
    f = pl.program_id(1)
    last = pl.num_programs(1) - 1
    used = p < nu_ref[0]

    @pl.when(used)
    def _():
        x = xs_ref[...]
        g = _dot(x, wg_ref[0])
        mid = g * _sigmoid(g) * _dot(x, wu_ref[0])
        part = _dot(mid.astype(BF16), wd_ref[0])

        @pl.when(f == 0)
        def _():
            acc_ref[...] = part

        @pl.when(f > 0)
        def _():
            acc_ref[...] += part

        @pl.when(f == last)
        def _():
            o_ref[...] = acc_ref[...].astype(o_ref.dtype)

    @pl.when(jnp.logical_not(used) & (f == last))
    def _():
        o_ref[...] = jnp.zeros(o_ref.shape, o_ref.dtype)


def _grouped_ffn(tile_expert, n_used, xs, n_rows, wg, wu, wd):
    d = xs.shape[1]
    ff = wd.shape[1]
    tf = MOE_FF_TILE
    nf = ff // tf
    tg = MOE_TILE
    row = lambda p, f, te, nu: (p, 0)
    fidx = lambda p, f, nu: jnp.where(p < nu[0], f, nf - 1)
    up = pl.BlockSpec((1, d, tf), lambda p, f, te, nu: (te[p], 0, fidx(p, f, nu)))
    return pl.pallas_call(
        _gmm_kernel,
        grid_spec=pltpu.PrefetchScalarGridSpec(
            num_scalar_prefetch=2, grid=(n_rows // tg, nf),
            in_specs=[pl.BlockSpec((tg, d), row), up, up,
                      pl.BlockSpec((1, tf, d), lambda p, f, te, nu: (te[p], fidx(p, f, nu), 0))],
            out_specs=pl.BlockSpec((tg, d), row),
            scratch_shapes=[pltpu.VMEM((tg, d), F32)]),
        out_shape=jax.ShapeDtypeStruct((n_rows, d), BF16),
        compiler_params=_params("arbitrary", "arbitrary"),
        name="moe_grouped_ffn",
    )(tile_expert, n_used, xs, wg, wu, wd)


def _combine_kernel(win_ref, h_ref, tok_ref, w1_ref, w2_ref, starts_ref, rows_ref, ys_hbm, o_ref, gbuf, sems,
                    *scratch, final_rows=None):
    i = pl.program_id(0)
    tm = h_ref.shape[0]
    win = ROW_WIN

    def fetch(tile, slot):
        for s in range(ROW_NWIN):
            start = pl.multiple_of(win_ref[tile * ROW_NWIN + s], win)
            pltpu.make_async_copy(ys_hbm.at[pl.ds(start, win)], gbuf.at[slot, pl.ds(s * win, win)],
                                  sems.at[slot]).start(priority=s % 2)

    @pl.when(i == 0)
    def _():
        fetch(0, 0)

    @pl.when(i + 1 < pl.num_programs(0))
    def _():
        fetch(i + 1, (i + 1) % 2)

    slot = i % 2
    pltpu.make_async_copy(ys_hbm.at[pl.ds(0, ROW_NWIN * win)], gbuf.at[slot], sems.at[slot]).wait()

    tok = tok_ref[...]
    lane = lax.broadcasted_iota(jnp.int32, tok.shape, 1)
    field = lambda c: jnp.sum(jnp.where(lane == c, tok, 0.0), axis=1, keepdims=True)
    valid = field(ROUTE_VALID) > 0.5
    rows = rows_ref[0]
    reps = h_ref.shape[1] // LANES
    out = h_ref[...]
    for e_col, r_col, w_ref in ((ROUTE_E1, ROUTE_R1, w1_ref), (ROUTE_E2, ROUTE_R2, w2_ref)):
        start = jnp.sum(jnp.where(lane == field(e_col).astype(jnp.int32), starts_ref[...], 0.0),
                        axis=1, keepdims=True)
        dest = jnp.where(valid, start + field(r_col), -2.0)
        pick = jnp.where(dest == rows, 1.0, 0.0).astype(BF16)
        out = out + jnp.concatenate([w_ref[...]] * reps, axis=1) * _dot(pick, gbuf[slot])
    if final_rows is None:
        o_ref[...] = out
        return

    obuf, osems = scratch
    n_meta, t_real = final_rows
    bsz, ts = obuf.shape[1], obuf.shape[2]
    steps = h_ref.shape[0] // (bsz * ts)
    assert steps == 1 and n_meta % SUBLANES == 0 and n_meta < ts
    i_last = t_real // ts
    rem = t_real - i_last * ts
    n_steps = pl.num_programs(0)
    oslot = i % 2

    def full_copy(step, s):
        dst = pl.multiple_of(step * ts - n_meta, SUBLANES)
        return pltpu.make_async_copy(obuf.at[s], o_ref.at[:, pl.ds(dst, ts), :], osems.at[s])

    is_full = lambda step: (step >= 1) & (step < i_last)

    @pl.when(is_full(i - 2))
    def _():
        full_copy(i - 2, oslot).wait()

    for t in range(ts):
        obuf[oslot, :, t, :] = out[t * bsz:(t + 1) * bsz, :]

    @pl.when(i == 0)
    def _():
        c = pltpu.make_async_copy(obuf.at[oslot, :, pl.ds(n_meta, ts - n_meta), :],
                                  o_ref.at[:, pl.ds(0, ts - n_meta), :], osems.at[oslot])
        c.start()
        c.wait()

    @pl.when(is_full(i))
    def _():
        full_copy(i, oslot).start()

    if rem:
        @pl.when(i == i_last)
        def _():
            c = pltpu.make_async_copy(obuf.at[oslot, :, pl.ds(0, rem), :],
                                      o_ref.at[:, pl.ds(i_last * ts - n_meta, rem), :], osems.at[oslot])
            c.start()
            c.wait()

    @pl.when(i == n_steps - 1)
    def _():
        @pl.when(is_full(i - 1))
        def _():
            full_copy(i - 1, 1 - oslot).wait()

        @pl.when(is_full(i))
        def _():
            full_copy(i, oslot).wait()


def _combine(win_start, h, tok, w1b, w2b, starts_row, slot_rows, ys, final=None):
    n, d = h.shape
    tm = COMBINE_TILE
    n_slots = ROW_NWIN * ROW_WIN
    row = lambda i, w: (i, 0)
    scratch = [pltpu.VMEM((2, n_slots, d), ys.dtype), pltpu.SemaphoreType.DMA((2,))]
    if final is None:
        kern = _combine_kernel
        out_spec = pl.BlockSpec((tm, d), row)
        out_shape = jax.ShapeDtypeStruct(h.shape, F32)
    else:
        bsz, n_meta, t_real = final
        assert tm % bsz == 0 and bsz % SUBLANES == 0
        kern = functools.partial(_combine_kernel, final_rows=(n_meta, t_real))
        out_spec = pl.BlockSpec(memory_space=pl.ANY)
        out_shape = jax.ShapeDtypeStruct((bsz, t_real - n_meta, d), F32)
        scratch += [pltpu.VMEM((2, bsz, tm // bsz, d), F32), pltpu.SemaphoreType.DMA((2,))]
    return pl.pallas_call(
        kern,
        grid_spec=pltpu.PrefetchScalarGridSpec(
            num_scalar_prefetch=1, grid=(n // tm,),
            in_specs=[pl.BlockSpec((tm, d), row), pl.BlockSpec((tm, LANES), row), pl.BlockSpec((tm, LANES), row),
                      pl.BlockSpec((tm, LANES), row), pl.BlockSpec((1, LANES), lambda i, w: (0, 0)),
                      pl.BlockSpec((1, 1, n_slots), lambda i, w: (i, 0, 0)),
                      pl.BlockSpec(memory_space=pl.ANY)],
            out_specs=out_spec,
            scratch_shapes=scratch),
        out_shape=out_shape,
        compiler_params=_params("arbitrary"),
        name="moe_combine",
    )(win_start, h, tok, w1b, w2b, starts_row, slot_rows, ys)


def _gate_windows(lw, bd):
    out = []
    for c0 in range(0, lw, LRU_GATE_TILE):
        nc = min(LRU_GATE_TILE, lw - c0)
        r0 = (c0 // bd) * bd
        r1 = ((c0 + nc - 1) // bd + 1) * bd
        r0 = (r0 // LANES) * LANES
        r1 = min(-(-r1 // LANES) * LANES, lw)
        out.append((c0, nc, r0, r1 - r0))
    return out


def _lru_kernel(h_ref, ng_ref, win_ref, cw_ref, cb_ref, wg_ref, br_ref, bi_ref, lam_ref, wo_ref,
                o_ref, xbuf_ref, hs_ref, a_ref, b_ref, y_ref, gate_ref, res_ref, *, windows, bsz):
    ti = pl.program_id(0)
    rows, lw = a_ref.shape
    halo = (LRU_CONV_WIDTH - 1) * bsz

    @pl.when(ti == 0)
    def _():
        xbuf_ref[0:halo, :] = jnp.zeros((halo, lw), F32)
        hs_ref[...] = jnp.zeros(hs_ref.shape, F32)

    @pl.when(ti > 0)
    def _():
        xbuf_ref[0:halo, :] = xbuf_ref[rows:rows + halo, :]

    res_ref[...] = jnp.concatenate([h_ref[:, t, :] for t in range(rows // bsz)], axis=0)
    proj = _dot(_rms(res_ref[...], ng_ref[...]).astype(BF16), win_ref[...])
    gate_ref[...] = jax.nn.gelu(proj[:, 0:lw], approximate=True)
    xbuf_ref[halo:halo + rows, :] = proj[:, lw:2 * lw]
    xc = jnp.broadcast_to(cb_ref[...], (rows, lw))
    for k in range(LRU_CONV_WIDTH):
        xc = xc + cw_ref[k:k + 1, :] * xbuf_ref[k * bsz:k * bsz + rows, :]
    xcb = xc.astype(BF16)
    sp_lam = jnp.log(1.0 + jnp.exp(-lam_ref[...]))
    log2_a_per_r = (-LRU_C * LOG2E) * sp_lam
    wcol = 0
    for (c0, nc, r0, nr) in windows:
        rg = _dot(xcb[:, r0:r0 + nr], wg_ref[r0:r0 + nr, wcol:wcol + 2 * nc])
        wcol += 2 * nc
        r = _sigmoid(rg[:, 0:nc] + br_ref[:, c0:c0 + nc])
        ig = _sigmoid(rg[:, nc:2 * nc] + bi_ref[:, c0:c0 + nc])
        a = jnp.exp2(r * log2_a_per_r[:, c0:c0 + nc])
        a_ref[:, c0:c0 + nc] = a
        one_m_a2 = 1.0 - a * a
        root = jnp.where(one_m_a2 > 0.0, one_m_a2 * lax.rsqrt(one_m_a2), 0.0)
        b_ref[:, c0:c0 + nc] = root * (ig * xc[:, c0:c0 + nc])

    def step(t, hprev):
        r0 = pl.multiple_of(t * bsz, SUBLANES)
        hnew = a_ref[pl.ds(r0, bsz), :] * hprev + b_ref[pl.ds(r0, bsz), :]
        y_ref[pl.ds(r0, bsz), :] = hnew
        return hnew

    hs_ref[...] = lax.fori_loop(0, rows // bsz, step, hs_ref[...], unroll=LRU_SCAN_UNROLL)
    gy = (gate_ref[...] * y_ref[...]).astype(BF16)
    o_ref[...] = res_ref[...] + _dot(gy, wo_ref[...])


def _lru_mixer(h, ng, win, cw, cb, wg, br, bi, lam, wo, windows):
    bsz, tp, d = h.shape
    n = bsz * tp
    lw = lam.shape[-1]
    assert bsz % SUBLANES == 0 and LRU_ROWS % bsz == 0 and n % LRU_ROWS == 0
    rows = LRU_ROWS
    halo = (LRU_CONV_WIDTH - 1) * bsz
    const = lambda t: (0, 0)
    blk = lambda t: (t, 0)
    resident = lambda w: pl.BlockSpec(w.shape, const, pipeline_mode=pl.Buffered(1))
    return pl.pallas_call(
        functools.partial(_lru_kernel, windows=windows, bsz=bsz),
        grid=(n // rows,),
        in_specs=[pl.BlockSpec((bsz, rows // bsz, d), lambda t: (0, t, 0)), pl.BlockSpec((1, d), const), resident(win),
                  pl.BlockSpec(cw.shape, const), pl.BlockSpec((1, lw), const), resident(wg),
                  pl.BlockSpec((1, lw), const), pl.BlockSpec((1, lw), const),
                  pl.BlockSpec((1, lw), const), resident(wo)],
        out_specs=pl.BlockSpec((rows, d), blk),
        out_shape=jax.ShapeDtypeStruct((n, d), F32),
        scratch_shapes=[pltpu.VMEM((halo + rows, lw), F32), pltpu.VMEM((bsz, lw), F32),
                        pltpu.VMEM((rows, lw), F32), pltpu.VMEM((rows, lw), F32), pltpu.VMEM((rows, lw), F32),
                        pltpu.VMEM((rows, lw), F32), pltpu.VMEM((rows, d), F32)],
        compiler_params=_params("arbitrary"),
        name="rglru_mixer",
    )(h, ng, win, cw, cb, wg, br, bi, lam, wo)


ROUTE_E1, ROUTE_E2, ROUTE_R1, ROUTE_R2, ROUTE_VALID = range(5)


def _router_kernel(h_ref, g_ref, rw_ref, ltri_ref, xn_ref, info_ref, tok_ref, w1_ref, w2_ref, before_ref, cnt_ref,
                   run_ref, *, n_real_rows):
    i = pl.program_id(0)
    tm = h_ref.shape[0]

    @pl.when(i == 0)
    def _():
        run_ref[...] = jnp.zeros(run_ref.shape, F32)

    xn = _rms(h_ref[...], g_ref[...])
    xn_ref[...] = xn.astype(BF16)
    split = lambda a: (a.astype(BF16), (a - a.astype(BF16).astype(F32)).astype(BF16))
    (x_hi, x_lo), (r_hi, r_lo) = split(xn), split(rw_ref[...])
    logits = _dot(x_hi, r_hi) + (_dot(x_hi, r_lo) + _dot(x_lo, r_hi))
    lane = lax.broadcasted_iota(jnp.int32, logits.shape, 1)
    neg = jnp.float32(-jnp.inf)
    logits = jnp.where(lane < N_EXPERTS, logits, neg)
    top1 = jnp.max(logits, axis=1, keepdims=True)
    idx1 = jnp.min(jnp.where(logits == top1, lane, LANES), axis=1, keepdims=True)
    rest = jnp.where(lane == idx1, neg, logits)
    top2 = jnp.max(rest, axis=1, keepdims=True)
    idx2 = jnp.min(jnp.where(rest == top2, lane, LANES), axis=1, keepdims=True)
    w1 = 1.0 / (1.0 + jnp.exp(top2 - top1))

    rowg = lax.broadcasted_iota(jnp.int32, (tm, 1), 0) + i * tm
    valid = jnp.where(rowg < n_real_rows, 1.0, 0.0)
    onehot = (jnp.where(lane == idx1, 1.0, 0.0) + jnp.where(lane == idx2, 1.0, 0.0)) * valid
    before_ref[...] = jnp.broadcast_to(run_ref[...], before_ref.shape)
    before = _dot(ltri_ref[...], onehot.astype(BF16)) + run_ref[...]
    rank1 = jnp.sum(jnp.where(lane == idx1, before, 0.0), axis=1, keepdims=True)
    rank2 = jnp.sum(jnp.where(lane == idx2, before, 0.0), axis=1, keepdims=True)
    run_ref[...] += jnp.ceil(jnp.sum(onehot, axis=0, keepdims=True) / ROW_WIN) * ROW_WIN
    cnt_ref[...] = run_ref[...]
    w1_ref[...] = jnp.broadcast_to(w1, (tm, LANES))
    w2_ref[...] = jnp.broadcast_to(1.0 - w1, (tm, LANES))
    fields = {ROUTE_E1: idx1.astype(F32), ROUTE_E2: idx2.astype(F32),
              ROUTE_R1: rank1, ROUTE_R2: rank2, ROUTE_VALID: valid}
    info = jnp.zeros(logits.shape, F32)
    for col, val in fields.items():
        info = jnp.where(lane == col, val, info)
    tok_ref[...] = info
    info_ref[...] = info.T[0:SUBLANES, :]


def _router(h, g, rw, ltri, n_real_rows):
    n, d = h.shape
    tm = ROW_TILE
    row = lambda i: (i, 0)
    const = lambda i: (0, 0)
    return pl.pallas_call(
        functools.partial(_router_kernel, n_real_rows=n_real_rows),
        grid=(n // tm,),
        in_specs=[pl.BlockSpec((tm, d), row), pl.BlockSpec((1, d), const), pl.BlockSpec(rw.shape, const),
                  pl.BlockSpec(ltri.shape, const)],
        out_specs=[pl.BlockSpec((tm, d), row), pl.BlockSpec((SUBLANES, tm), row),
                   pl.BlockSpec((tm, LANES), row), pl.BlockSpec((tm, LANES), row),
                   pl.BlockSpec((tm, LANES), row), pl.BlockSpec((SUBLANES, LANES), row),
                   pl.BlockSpec((1, LANES), const)],
        out_shape=[jax.ShapeDtypeStruct((n, d), BF16), jax.ShapeDtypeStruct((n // tm * SUBLANES, tm), F32),
                   jax.ShapeDtypeStruct((n, LANES), F32), jax.ShapeDtypeStruct((n, LANES), F32),
                   jax.ShapeDtypeStruct((n, LANES), F32), jax.ShapeDtypeStruct((n // tm * SUBLANES, LANES), F32),
                   jax.ShapeDtypeStruct((1, LANES), F32)],
        scratch_shapes=[pltpu.VMEM((1, LANES), F32)],
        compiler_params=_params("arbitrary"),
        name="router",
    )(h, g, rw, ltri)


def _routing_tables(before, counts, n_rows):
    tg = MOE_TILE
    win = ROW_WIN
    cnt = counts[0, :N_EXPERTS].astype(jnp.int32)
    padded = (cnt + tg - 1) // tg * tg
    ends = jnp.cumsum(padded)
    starts = ends - padded
    tile_start = jnp.arange(n_rows // tg, dtype=jnp.int32) * tg
    tile_expert = jnp.minimum(jnp.sum(tile_start[:, None] >= ends[None, :], axis=1), N_EXPERTS - 1)
    n_used = (ends[-1] // tg).reshape(1)
    bef = before.reshape(-1, SUBLANES, LANES)[:, 0, :N_EXPERTS].astype(jnp.int32)
    nwin = (jnp.concatenate([bef[1:], cnt[None, :]], axis=0) - bef) // win
    base = jnp.cumsum(nwin, axis=1) - nwin
    slot = jnp.arange(ROW_NWIN, dtype=jnp.int32)[None, :, None]
    mine = (slot >= base[:, None, :]) & (slot < (base + nwin)[:, None, :])
    rows = jnp.sum(jnp.where(mine, (starts[None, :] + bef)[:, None, :] + win * (slot - base[:, None, :]), 0), axis=2)
    used = jnp.any(mine, axis=2)
    parity = (jnp.arange(bef.shape[0], dtype=jnp.int32) % 2)[:, None]
    scatter_rows = jnp.where(used, rows, n_rows + win * (parity * ROW_NWIN + slot[:, :, 0]))
    gather_rows = jnp.where(used, rows, 0)
    slot_rows = jnp.where(used[..., None], rows[..., None] + jnp.arange(win, dtype=jnp.int32), -1)
    slot_rows = slot_rows.reshape(-1, 1, ROW_NWIN * win).astype(F32)
    starts_row = jnp.zeros((1, LANES), F32).at[0, :N_EXPERTS].set(starts.astype(F32))
    local_off = jnp.zeros(bef.shape[:1] + (SUBLANES,), jnp.int32).at[:, :N_EXPERTS].set(win * base - bef)
    local_off = jnp.broadcast_to(local_off.astype(F32)[:, :, None], local_off.shape + (LANES,)).reshape(-1, LANES)
    return (tile_expert.astype(jnp.int32), n_used.astype(jnp.int32), scatter_rows.reshape(-1).astype(jnp.int32),
            gather_rows.reshape(-1).astype(jnp.int32), slot_rows, starts_row, local_off)


def _interleave_gate_up(wg, wu, tf):
    e, d, ff = wg.shape
    g = wg.astype(BF16).reshape(e, d, ff // tf, tf)
    u = wu.astype(BF16).reshape(e, d, ff // tf, tf)
    return jnp.concatenate([g, u], axis=3).reshape(e, d, 2 * ff)


def _dense_block_diag(w):
    nb, bd, _ = w.shape
    eye = jnp.eye(nb, dtype=w.dtype)
    return (w[:, :, None, :] * eye[:, None, :, None]).reshape(nb * bd, nb * bd)


def _pack_lru_gates(wr, wi, windows):
    dr = _dense_block_diag(wr).astype(BF16)
    di = _dense_block_diag(wi).astype(BF16)
    cols = []
    for (c0, nc, _, _) in windows:
        cols += [dr[:, c0:c0 + nc], di[:, c0:c0 + nc]]
    return jnp.concatenate(cols, axis=1)


def _ffn_chunk(ff):
    for tf in (896, 512, 384, 256, 128):
        if ff % tf == 0:
            return tf
    raise ValueError(f"unsupported d_ff {ff}")


def kernel(x, meta_tokens, mix_norm_even, w_in_even, conv_w, conv_b, conv_ln_g, conv_ln_b, q_norm_g, k_norm_g, w_out_even, ffn_norm_even, ffn_w_gate, ffn_w_up, ffn_w_down, mix_norm_odd, w_in_odd, lru_conv_w, lru_conv_b, gate_r_w, gate_r_b, gate_i_w, gate_i_b, lru_lambda, w_out_odd, ffn_norm_odd, router_w, moe_w_gate, moe_w_up, moe_w_down):
    bsz, seq, d = x.shape
    t_real = N_META + seq
    tp = -(-t_real // TIME_TILE) * TIME_TILE
    n = bsz * tp
    assert n % ROW_TILE == 0
    depth = mix_norm_even.shape[0] + mix_norm_odd.shape[0]

    meta = jnp.broadcast_to(meta_tokens[None].astype(x.dtype), (bsz, N_META, d))
    h = jnp.concatenate([meta, x, jnp.zeros((bsz, tp - t_real, d), x.dtype)], axis=1).reshape(n, d)

    head_mean = jnp.kron(jnp.eye(MXU_DIM // SB_HEAD_DIM, dtype=F32),
                         jnp.full((SB_HEAD_DIM, SB_HEAD_DIM), 1.0 / SB_HEAD_DIM, F32)).astype(BF16)
    kk = jnp.arange(ATT_TK)
    neg_tri = -(kk[:, None] >= kk[None, :]).astype(BF16)
    row2 = lambda a: a.reshape(1, -1)

    time_major = False
    for layer in range(depth):
        p = layer // 2
        if layer % 2 == 0:
            if time_major:
                h = h.reshape(tp, bsz, d).transpose(1, 0, 2).reshape(n, d)
                time_major = False
            u, q, k, v = _even_in_proj(h, row2(mix_norm_even[p]), w_in_even[p].astype(BF16),
                                       row2(jnp.tile(q_norm_g[p], SB_HEADS)),
                                       row2(jnp.tile(k_norm_g[p], SB_HEADS)), head_mean)
            u = _conv_module(u.reshape(bsz, tp, CONV_CH), conv_w[p], row2(conv_b[p]),
                             row2(conv_ln_g[p]), row2(conv_ln_b[p]))
            o = _attention(q.reshape(bsz, tp, SB_WIDTH), k.reshape(bsz, tp, SB_WIDTH),
                           v.reshape(bsz, tp, SB_WIDTH), neg_tri, t_real)
            tf = _ffn_chunk(ffn_w_gate.shape[-1])
            h = _even_ffn(h, u.reshape(n, CONV_CH), o.reshape(n, SB_WIDTH), w_out_even[p].astype(BF16),
                          row2(ffn_norm_even[p]),
                          _interleave_gate_up(ffn_w_gate[p][None], ffn_w_up[p][None], tf)[0],
                          ffn_w_down[p].astype(BF16), tf)
        else:
            if time_major:
                h = h.reshape(tp, bsz, d).transpose(1, 0, 2).reshape(n, d)
            time_major = True
            lw = lru_lambda.shape[-1]
            windows = _gate_windows(lw, lw // LRU_BLOCKS)
            h = _lru_mixer(h.reshape(bsz, tp, d), row2(mix_norm_odd[p]), w_in_odd[p].astype(BF16),
                           lru_conv_w[p], row2(lru_conv_b[p]),
                           _pack_lru_gates(gate_r_w[p], gate_i_w[p], windows),
                           row2(gate_r_b[p]), row2(gate_i_b[p]), row2(lru_lambda[p]),
                           w_out_odd[p].astype(BF16), windows)
            rw = jnp.pad(router_w[p], ((0, 0), (0, LANES - N_EXPERTS)))
            kk = jnp.arange(ROW_TILE)
            ltri = (kk[:, None] > kk[None, :]).astype(BF16)
            xn, info, tok, w1b, w2b, before, counts = _router(h, row2(ffn_norm_odd[p]), rw, ltri, t_real * bsz)
            tg = MOE_TILE
            assert ROW_TILE == COMBINE_TILE
            n_tiles = n // COMBINE_TILE
            n_rows = 2 * bsz * t_real + n_tiles * N_EXPERTS * (ROW_WIN - 1) + N_EXPERTS * (tg - 1)
            n_rows = -(-n_rows // tg) * tg
            tile_expert, n_used, scatter_rows, gather_rows, slot_rows, starts_row, local_off = _routing_tables(
                before, counts, n_rows)
            xs = _row_scatter(scatter_rows, xn, info, local_off, n_rows + 2 * ROW_NWIN * ROW_WIN)
            ys = _grouped_ffn(tile_expert, n_used, xs, n_rows, moe_w_gate[p].astype(BF16),
                              moe_w_up[p].astype(BF16), moe_w_down[p].astype(BF16))
            if layer + 1 == depth:
                return _combine(gather_rows, h, tok, w1b, w2b, starts_row, slot_rows, ys, (bsz, N_META, t_real))
            h = _combine(gather_rows, h, tok, w1b, w2b, starts_row, slot_rows, ys)
    if time_major:
        return h.reshape(tp, bsz, d)[N_META:t_real].transpose(1, 0, 2)
    return h.reshape(bsz, tp, d)[:, N_META:t_real]
```

```python
import functools
import math

import jax
import jax.numpy as jnp
from jax import lax
from jax.experimental import pallas as pl
from jax.experimental.pallas import tpu as pltpu

F32 = jnp.float32
BF16 = jnp.bfloat16

EPS = 1e-6
N_META = 16
CONV_CH = 512
CONV_WIDTH = 31
SB_HEADS = 8
SB_HEAD_DIM = 64
SB_WIDTH = SB_HEADS * SB_HEAD_DIM
LRU_BLOCKS = 16
LRU_CONV_WIDTH = 4
LRU_C = 8.0
N_EXPERTS = 8

LANES = 128
MXU_DIM = 256
TIME_TILE = 256
VMEM_LIMIT = 50 * 1024 * 1024

ROW_TILE = 512
CONV_TT = 256
CONV_HALO = 32
CONV_CHUNK = 128
ATT_TQ = 256
ATT_TK = 256
LOG2E = 1.4426950408889634
MASKED_EXPONENT = -1e30
SOFTPLUS_CUTOFF = 126.0
LRU_ROWS = 512
LRU_SCAN_UNROLL = 4
LRU_GATE_TILE = 256
MOE_TILE = 512
MOE_FF_TILE = 1792
COMBINE_TILE = 512
ROW_WIN = 16
ROW_NWIN = 2 * COMBINE_TILE // ROW_WIN + N_EXPERTS
SUBLANES = 8
BF16_ROWS = 16


def _dot(a, b):
    return jnp.dot(a, b, preferred_element_type=F32)


def _params(*sem):
    return pltpu.CompilerParams(dimension_semantics=sem, vmem_limit_bytes=VMEM_LIMIT)


def _sigmoid(x):
    return 0.5 * jnp.tanh(0.5 * x) + 0.5


def _rms(x, g):
    ms = jnp.mean(x * x, axis=-1, keepdims=True)
    return x * lax.rsqrt(ms + EPS) * g


def _even_in_kernel(h_ref, g_ref, w_ref, qg_ref, kg_ref, hm_ref, u_ref, q_ref, k_ref, v_ref):
    xn = _rms(h_ref[...], g_ref[...]).astype(BF16)
    a = _dot(xn, w_ref[:, 0:CONV_CH])
    gate = _dot(xn, w_ref[:, CONV_CH:2 * CONV_CH])
    u_ref[...] = (a * _sigmoid(gate)).astype(BF16)

    def head_norm(y, gain):
        yy = (y * y).astype(BF16)
        parts = [_dot(yy[:, c:c + MXU_DIM], hm_ref[...]) for c in range(0, SB_WIDTH, MXU_DIM)]
        ms = jnp.concatenate(parts, axis=1)
        return y * lax.rsqrt(ms + EPS) * gain

    c0 = 2 * CONV_CH
    q = head_norm(_dot(xn, w_ref[:, c0:c0 + SB_WIDTH]), qg_ref[...])
    q_ref[...] = (q * (LOG2E / math.sqrt(SB_HEAD_DIM))).astype(BF16)
    k = head_norm(_dot(xn, w_ref[:, c0 + SB_WIDTH:c0 + 2 * SB_WIDTH]), kg_ref[...])
    k_ref[...] = k.astype(BF16)
    v_ref[...] = _dot(xn, w_ref[:, c0 + 2 * SB_WIDTH:c0 + 3 * SB_WIDTH]).astype(BF16)


def _even_in_proj(h, g, w, qg, kg, hm):
    n, d = h.shape
    tm = ROW_TILE
    row = lambda i: (i, 0)
    const = lambda i: (0, 0)
    out = jax.ShapeDtypeStruct((n, SB_WIDTH), BF16)
    return pl.pallas_call(
        _even_in_kernel,
        grid=(n // tm,),
        in_specs=[pl.BlockSpec((tm, d), row), pl.BlockSpec((1, d), const),
                  pl.BlockSpec(w.shape, const), pl.BlockSpec((1, SB_WIDTH), const),
                  pl.BlockSpec((1, SB_WIDTH), const), pl.BlockSpec(hm.shape, const)],
        out_specs=[pl.BlockSpec((tm, SB_WIDTH), row)] * 4,
        out_shape=[out] * 4,
        compiler_params=_params("parallel"),
        name="even_in_proj",
    )(h, g, w, qg, kg, hm)


def _conv_kernel(cur_ref, halo_ref, w_ref, b_ref, lg_ref, lb_ref, o_ref, buf_ref, sh_ref):
    i = pl.program_id(1)
    tt = cur_ref.shape[1]
    rows = CONV_HALO + tt
    halo = halo_ref[0].astype(F32)
    buf_ref[0:CONV_HALO, :] = jnp.where(i > 0, halo, 0.0)
    buf_ref[CONV_HALO:rows, :] = cur_ref[0].astype(F32)
    for s in range(1, SUBLANES):
        sh_ref[s - 1, 0:rows - SUBLANES, :] = buf_ref[s:s + rows - SUBLANES, :]
    off = CONV_HALO - (CONV_WIDTH - 1)
    for c in range(tt // CONV_CHUNK):
        r0 = c * CONV_CHUNK
        acc = jnp.broadcast_to(b_ref[...], (CONV_CHUNK, CONV_CH))
        for k in range(CONV_WIDTH):
            shift = (off + k) % SUBLANES
            base = r0 + off + k - shift
            if shift == 0:
                tap = buf_ref[base:base + CONV_CHUNK, :]
            else:
                tap = sh_ref[shift - 1, base:base + CONV_CHUNK, :]
            acc = acc + w_ref[k:k + 1, :] * tap
        mu = jnp.mean(acc, axis=-1, keepdims=True)
        xc = acc - mu
        var = jnp.mean(xc * xc, axis=-1, keepdims=True)
        y = xc * lax.rsqrt(var + EPS) * lg_ref[...] + lb_ref[...]
        o_ref[0, r0:r0 + CONV_CHUNK, :] = (y * _sigmoid(y)).astype(BF16)


def _conv_module(u, w, b, lg, lb):
    bsz, tp, c = u.shape
    tt = CONV_TT
    per = tt // CONV_HALO
    const = lambda bi, i: (0, 0)
    return pl.pallas_call(
        _conv_kernel,
        grid=(bsz, tp // tt),
        in_specs=[pl.BlockSpec((1, tt, c), lambda bi, i: (bi, i, 0)),
                  pl.BlockSpec((1, CONV_HALO, c), lambda bi, i: (bi, jnp.maximum(i * per - 1, 0), 0)),
                  pl.BlockSpec(w.shape, const), pl.BlockSpec((1, c), const),
                  pl.BlockSpec((1, c), const), pl.BlockSpec((1, c), const)],
        out_specs=pl.BlockSpec((1, tt, c), lambda bi, i: (bi, i, 0)),
        out_shape=jax.ShapeDtypeStruct(u.shape, BF16),
        scratch_shapes=[pltpu.VMEM((CONV_HALO + tt, c), F32),
                        pltpu.VMEM((SUBLANES - 1, CONV_HALO + tt - SUBLANES, c), F32)],
        compiler_params=_params("parallel", "parallel"),
        name="conv_module",
    )(u, u, w, b, lg, lb)


def _attn_kernel(*refs, tq, tile0, aliased, n_compute, tail_rows=0):
    if aliased:
        refs = refs[1:]
    o_ref = refs[4]
    step = pl.program_id(1)

    @pl.when(step < n_compute)
    def _():
        _attn_tile(*refs, tq=tq, row0=(step + tile0) * ATT_TQ)

    if tail_rows:
        io, (q2_ref, e_ref, tot_ref, acc_ref, car_ref) = refs[:5], refs[5:]
        r2 = 2 * tail_rows
        views = (q2_ref.at[:, 0:r2, :], e_ref.at[:, :, 0:r2, :], tot_ref.at[:, :, 0:r2, :],
                 acc_ref.at[:, 0:r2, :], car_ref.at[:, 0:r2, :])

        @pl.when(step == n_compute)
        def _():
            _attn_tile(*io, *views, tq=tail_rows, row0=(step + tile0) * ATT_TQ)

    @pl.when(step >= n_compute + (1 if tail_rows else 0))
    def _():
        o_ref[...] = jnp.zeros(o_ref.shape, BF16)


def _attn_tile(q_ref, k_ref, v_ref, tri_ref, o_ref, q2_ref, e_ref, tot_ref, acc_ref, car_ref, *, tq, row0):
    tk = ATT_TK
    npairs = q_ref.shape[2] // LANES
    lane_q = lax.broadcasted_iota(jnp.int32, (tq, LANES), 1)

    for hp in range(npairs):
        q = q_ref[0, 0:tq, hp * LANES:(hp + 1) * LANES]
        zero = jnp.zeros_like(q)
        q2_ref[hp, 0:tq, :] = jnp.where(lane_q < SB_HEAD_DIM, q, zero)
        q2_ref[hp, tq:2 * tq, :] = jnp.where(lane_q >= SB_HEAD_DIM, q, zero)
        car_ref[hp] = jnp.zeros((2 * tq, LANES), F32)
        acc_ref[hp] = jnp.zeros((2 * tq, LANES), F32)

    def stage_a(j, mask, slot):
        k0 = pl.multiple_of(j * tk, tk)
        for hp in range(npairs):
            kt = k_ref[0, pl.ds(k0, tk), hp * LANES:(hp + 1) * LANES]
            z = lax.dot_general(q2_ref[hp], kt, (((1,), (1,)), ((), ())), preferred_element_type=F32)
            sp = jnp.where(z > SOFTPLUS_CUTOFF, z, jnp.log(1.0 + jnp.exp2(z)) * LOG2E)
            if mask is not None:
                sp = jnp.where(mask, sp, 0.0)
            e = z + _dot(sp.astype(BF16), tri_ref[...])
            if mask is not None:
                e = jnp.where(mask, e, MASKED_EXPONENT)
            e_ref[slot, hp] = e
            tot_ref[slot, hp] = jnp.broadcast_to(jnp.sum(sp, axis=1, keepdims=True), (2 * tq, LANES))

    def stage_b(j, slot):
        k0 = pl.multiple_of(j * tk, tk)
        for hp in range(npairs):
            car = car_ref[hp]
            w = jnp.exp2(e_ref[slot, hp] + jnp.concatenate([car] * (tk // LANES), axis=1)).astype(BF16)
            vt = v_ref[0, pl.ds(k0, tk), hp * LANES:(hp + 1) * LANES]
            acc_ref[hp] += _dot(w, vt)
            car_ref[hp] = car - tot_ref[slot, hp]

    j_last = row0 // tk
    row = lax.broadcasted_iota(jnp.int32, (2 * tq, tk), 0)
    col = lax.broadcasted_iota(jnp.int32, (2 * tq, tk), 1)
    row = jnp.where(row >= tq, row - tq, row) + row0
    odd = j_last % 2
    stage_a(j_last, (col + j_last * tk) < row, odd)

    @pl.when(odd == 1)
    def _():
        stage_b(j_last, 1)
        stage_a(j_last - 1, None, 0)

    def body(jj, carry):
        j = j_last - odd - 1 - 2 * jj
        stage_b(j + 1, 0)
        stage_a(j, None, 1)
        stage_b(j, 1)
        stage_a(j - 1, None, 0)
        return carry

    lax.fori_loop(0, (j_last - odd) // 2, body, 0)
    stage_b(0, 0)
    if tq < o_ref.shape[1]:
        o_ref[0, tq:, :] = jnp.zeros((o_ref.shape[1] - tq, o_ref.shape[2]), BF16)
    for hp in range(npairs):
        o_ref[0, 0:tq, hp * LANES:(hp + 1) * LANES] = jnp.where(
            lane_q < SB_HEAD_DIM, acc_ref[hp, 0:tq, :], acc_ref[hp, tq:2 * tq, :]).astype(BF16)


def _attention(q, k, v, tri, t_real):
    bsz, tp, width = q.shape
    bq, tk = ATT_TQ, ATT_TK
    assert bq <= tk and tk % bq == 0
    npairs = width // LANES
    n_full = t_real // bq
    rem = -(-(t_real - n_full * bq) // BF16_ROWS) * BF16_ROWS

    def call(tq, tile0, n_compute, n_tiles, prev, tail_rows=0):
        blk = lambda b, i: (b, i + tile0, 0)
        full = lambda b, i: (b, 0, 0)
        in_specs = [pl.BlockSpec((1, bq, width), blk), pl.BlockSpec((1, tp, width), full),
                    pl.BlockSpec((1, tp, width), full), pl.BlockSpec(tri.shape, lambda b, i: (0, 0))]
        args = [q, k, v, tri]
        if prev is not None:
            in_specs.insert(0, pl.BlockSpec(memory_space=pl.ANY))
            args.insert(0, prev)
        return pl.pallas_call(
            functools.partial(_attn_kernel, tq=tq, tile0=tile0, aliased=prev is not None, n_compute=n_compute,
                              tail_rows=tail_rows),
            grid=(bsz, n_tiles),
            in_specs=in_specs,
            out_specs=pl.BlockSpec((1, bq, width), blk),
            out_shape=jax.ShapeDtypeStruct(q.shape, BF16),
            input_output_aliases={} if prev is None else {0: 0},
            scratch_shapes=[pltpu.VMEM((npairs, 2 * tq, LANES), BF16),
                            pltpu.VMEM((2, npairs, 2 * tq, tk), F32),
                            pltpu.VMEM((2, npairs, 2 * tq, LANES), F32),
                            pltpu.VMEM((npairs, 2 * tq, LANES), F32),
                            pltpu.VMEM((npairs, 2 * tq, LANES), F32)],
            compiler_params=_params("parallel", "arbitrary"),
            name="stick_breaking" if prev is None else "stick_breaking_tail",
        )(*args)

    if n_full == 0:
        return call(rem, 0, 1, tp // bq, None)
    return call(bq, 0, n_full, tp // bq, None, tail_rows=rem)


def _even_ffn_kernel(h_ref, u_ref, o_ref, wo_ref, g_ref, wgu_ref, wd_ref, out_ref, xn_ref, *, tf):
    @pl.when(pl.program_id(1) == 0)
    def _():
        h1 = h_ref[...] + _dot(u_ref[...], wo_ref[0:CONV_CH, :]) + _dot(o_ref[...], wo_ref[CONV_CH:, :])
        out_ref[...] = h1
        xn_ref[...] = _rms(h1, g_ref[...]).astype(BF16)

    gu = _dot(xn_ref[...], wgu_ref[...])
    g = gu[:, 0:tf]
    mid = g * _sigmoid(g) * gu[:, tf:2 * tf]
    out_ref[...] += _dot(mid.astype(BF16), wd_ref[...])


def _even_ffn(h, u, o, wo, g, wgu, wd, tf):
    n, d = h.shape
    ff = wd.shape[0]
    tm = ROW_TILE
    row = lambda i, f: (i, 0)
    const = lambda i, f: (0, 0)
    return pl.pallas_call(
        functools.partial(_even_ffn_kernel, tf=tf),
        grid=(n // tm, ff // tf),
        in_specs=[pl.BlockSpec((tm, d), row), pl.BlockSpec((tm, CONV_CH), row),
                  pl.BlockSpec((tm, SB_WIDTH), row), pl.BlockSpec(wo.shape, const),
                  pl.BlockSpec((1, d), const),
                  pl.BlockSpec((d, 2 * tf), lambda i, f: (0, f)),
                  pl.BlockSpec((tf, d), lambda i, f: (f, 0))],
        out_specs=pl.BlockSpec((tm, d), row),
        out_shape=jax.ShapeDtypeStruct((n, d), F32),
        scratch_shapes=[pltpu.VMEM((tm, d), BF16)],
        compiler_params=_params("parallel", "arbitrary"),
        name="even_out_swiglu",
    )(h, u, o, wo, g, wgu, wd)


def _row_scatter_kernel(win_ref, x_ref, info_ref, off_ref, init_hbm, o_hbm, cbuf, sems):
    del init_hbm
    i = pl.program_id(0)
    last = pl.num_programs(0) - 1
    n_slots = cbuf.shape[1]
    buf = i % 2

    def drained(b):
        return pltpu.make_async_copy(cbuf.at[b], o_hbm.at[pl.ds(0, n_slots)], sems.at[b])

    info = info_ref[...]
    sub = lax.broadcasted_iota(jnp.int32, info.shape, 0)
    valid = info[ROUTE_VALID:ROUTE_VALID + 1, :] > 0.5

    def local_slot(e_row, r_row):
        expert = info[e_row:e_row + 1, :].astype(jnp.int32)
        off = jnp.sum(jnp.where(sub == expert, off_ref[:, 0:1], 0.0), axis=0, keepdims=True)
        return jnp.where(valid, off + info[r_row:r_row + 1, :], -1.0)

    slot = lax.broadcasted_iota(jnp.int32, (n_slots, info.shape[1]), 0).astype(F32)
    pick = (jnp.where(slot == local_slot(ROUTE_E1, ROUTE_R1), 1.0, 0.0)
            + jnp.where(slot == local_slot(ROUTE_E2, ROUTE_R2), 1.0, 0.0)).astype(BF16)
    rows = _dot(pick, x_ref[...]).astype(cbuf.dtype)

    @pl.when(i >= 2)
    def _():
        drained(buf).wait()

    cbuf[buf] = rows
    for s in range(ROW_NWIN):
        dst = pl.multiple_of(win_ref[i * ROW_NWIN + s], ROW_WIN)
        pltpu.make_async_copy(cbuf.at[buf, pl.ds(s * ROW_WIN, ROW_WIN)], o_hbm.at[pl.ds(dst, ROW_WIN)],
                              sems.at[buf]).start(priority=s % 2)

    @pl.when(i == last)
    def _():
        drained(buf).wait()

        @pl.when(i >= 1)
        def _():
            drained(1 - buf).wait()


def _row_scatter(win_rows, x, info, local_off, n_rows_alloc):
    n, d = x.shape
    tm = COMBINE_TILE
    init = jnp.zeros((n_rows_alloc, d), x.dtype)
    return pl.pallas_call(
        _row_scatter_kernel,
        grid_spec=pltpu.PrefetchScalarGridSpec(
            num_scalar_prefetch=1, grid=(n // tm,),
            in_specs=[pl.BlockSpec((tm, d), lambda i, w: (i, 0)), pl.BlockSpec((SUBLANES, tm), lambda i, w: (i, 0)),
                      pl.BlockSpec((SUBLANES, LANES), lambda i, w: (i, 0)), pl.BlockSpec(memory_space=pl.ANY)],
            out_specs=pl.BlockSpec(memory_space=pl.ANY),
            scratch_shapes=[pltpu.VMEM((2, ROW_NWIN * ROW_WIN, d), x.dtype), pltpu.SemaphoreType.DMA((2,))]),
        out_shape=jax.ShapeDtypeStruct(init.shape, init.dtype),
        input_output_aliases={4: 0},
        compiler_params=_params("arbitrary"),
        name="moe_row_scatter",
    )(win_rows, x, info, local_off, init)


def _gmm_kernel(te_ref, nu_ref, xs_ref, wg_ref, wu_ref, wd_ref, o_ref, acc_ref):
    p = pl.program_id(0)
    f = pl.program_id(1)
    last = pl.num_programs(1) - 1
    used = p < nu_ref[0]

    @pl.when(used)
    def _():
        x = xs_ref[...]
        g = _dot(x, wg_ref[0])
        mid = g * _sigmoid(g) * _dot(x, wu_ref[0])
        part = _dot(mid.astype(BF16), wd_ref[0])

        @pl.when(f == 0)
        def _():
            acc_ref[...] = part

        @pl.when(f > 0)
        def _():
            acc_ref[...] += part

        @pl.when(f == last)
        def _():
            o_ref[...] = acc_ref[...].astype(o_ref.dtype)

    @pl.when(jnp.logical_not(used) & (f == last))
    def _():
        o_ref[...] = jnp.zeros(o_ref.shape, o_ref.dtype)


def _grouped_ffn(tile_expert, n_used, xs, n_rows, wg, wu, wd):
    d = xs.shape[1]
    ff = wd.shape[1]
    tf = MOE_FF_TILE
    nf = ff // tf
    tg = MOE_TILE
    row = lambda p, f, te, nu: (p, 0)
    fidx = lambda p, f, nu: jnp.where(p < nu[0], f, nf - 1)
    up = pl.BlockSpec((1, d, tf), lambda p, f, te, nu: (te[p], 0, fidx(p, f, nu)))
    return pl.pallas_call(
        _gmm_kernel,
        grid_spec=pltpu.PrefetchScalarGridSpec(
            num_scalar_prefetch=2, grid=(n_rows // tg, nf),
            in_specs=[pl.BlockSpec((tg, d), row), up, up,
                      pl.BlockSpec((1, tf, d), lambda p, f, te, nu: (te[p], fidx(p, f, nu), 0))],
            out_specs=pl.BlockSpec((tg, d), row),
            scratch_shapes=[pltpu.VMEM((tg, d), F32)]),
        out_shape=jax.ShapeDtypeStruct((n_rows, d), BF16),
        compiler_params=_params("arbitrary", "arbitrary"),
        name="moe_grouped_ffn",
    )(tile_expert, n_used, xs, wg, wu, wd)


def _combine_kernel(win_ref, h_ref, tok_ref, w1_ref, w2_ref, starts_ref, rows_ref, ys_hbm, o_ref, gbuf, sems,
                    *scratch, final_rows=None):
    i = pl.program_id(0)
    tm = h_ref.shape[0]
    win = ROW_WIN

    def fetch(tile, slot):
        for s in range(ROW_NWIN):
            start = pl.multiple_of(win_ref[tile * ROW_NWIN + s], win)
            pltpu.make_async_copy(ys_hbm.at[pl.ds(start, win)], gbuf.at[slot, pl.ds(s * win, win)],
                                  sems.at[slot]).start(priority=s % 2)

    @pl.when(i == 0)
    def _():
        fetch(0, 0)

    @pl.when(i + 1 < pl.num_programs(0))
    def _():
        fetch(i + 1, (i + 1) % 2)

    slot = i % 2
    pltpu.make_async_copy(ys_hbm.at[pl.ds(0, ROW_NWIN * win)], gbuf.at[slot], sems.at[slot]).wait()

    tok = tok_ref[...]
    lane = lax.broadcasted_iota(jnp.int32, tok.shape, 1)
    field = lambda c: jnp.sum(jnp.where(lane == c, tok, 0.0), axis=1, keepdims=True)
    valid = field(ROUTE_VALID) > 0.5
    rows = rows_ref[0]
    reps = h_ref.shape[1] // LANES
    out = h_ref[...]
    for e_col, r_col, w_ref in ((ROUTE_E1, ROUTE_R1, w1_ref), (ROUTE_E2, ROUTE_R2, w2_ref)):
        start = jnp.sum(jnp.where(lane == field(e_col).astype(jnp.int32), starts_ref[...], 0.0),
                        axis=1, keepdims=True)
        dest = jnp.where(valid, start + field(r_col), -2.0)
        pick = jnp.where(dest == rows, 1.0, 0.0).astype(BF16)
        out = out + jnp.concatenate([w_ref[...]] * reps, axis=1) * _dot(pick, gbuf[slot])
    if final_rows is None:
        o_ref[...] = out
        return

    obuf, osems = scratch
    n_meta, t_real = final_rows
    bsz, ts = obuf.shape[1], obuf.shape[2]
    steps = h_ref.shape[0] // (bsz * ts)
    assert steps == 1 and n_meta % SUBLANES == 0 and n_meta < ts
    i_last = t_real // ts
    rem = t_real - i_last * ts
    n_steps = pl.num_programs(0)
    oslot = i % 2

    def full_copy(step, s):
        dst = pl.multiple_of(step * ts - n_meta, SUBLANES)
        return pltpu.make_async_copy(obuf.at[s], o_ref.at[:, pl.ds(dst, ts), :], osems.at[s])

    is_full = lambda step: (step >= 1) & (step < i_last)

    @pl.when(is_full(i - 2))
    def _():
        full_copy(i - 2, oslot).wait()

    for t in range(ts):
        obuf[oslot, :, t, :] = out[t * bsz:(t + 1) * bsz, :]

    @pl.when(i == 0)
    def _():
        c = pltpu.make_async_copy(obuf.at[oslot, :, pl.ds(n_meta, ts - n_meta), :],
                                  o_ref.at[:, pl.ds(0, ts - n_meta), :], osems.at[oslot])
        c.start()
        c.wait()

    @pl.when(is_full(i))
    def _():
        full_copy(i, oslot).start()

    if rem:
        @pl.when(i == i_last)
        def _():
            c = pltpu.make_async_copy(obuf.at[oslot, :, pl.ds(0, rem), :],
                                      o_ref.at[:, pl.ds(i_last * ts - n_meta, rem), :], osems.at[oslot])
            c.start()
            c.wait()

    @pl.when(i == n_steps - 1)
    def _():
        @pl.when(is_full(i - 1))
        def _():
            full_copy(i - 1, 1 - oslot).wait()

        @pl.when(is_full(i))
        def _():
            full_copy(i, oslot).wait()


def _combine(win_start, h, tok, w1b, w2b, starts_row, slot_rows, ys, final=None):
    n, d = h.shape
    tm = COMBINE_TILE
    n_slots = ROW_NWIN * ROW_WIN
    row = lambda i, w: (i, 0)
    scratch = [pltpu.VMEM((2, n_slots, d), ys.dtype), pltpu.SemaphoreType.DMA((2,))]
    if final is None:
        kern = _combine_kernel
        out_spec = pl.BlockSpec((tm, d), row)
        out_shape = jax.ShapeDtypeStruct(h.shape, F32)
    else:
        bsz, n_meta, t_real = final
        assert tm % bsz == 0 and bsz % SUBLANES == 0
        kern = functools.partial(_combine_kernel, final_rows=(n_meta, t_real))
        out_spec = pl.BlockSpec(memory_space=pl.ANY)
        out_shape = jax.ShapeDtypeStruct((bsz, t_real - n_meta, d), F32)
        scratch += [pltpu.VMEM((2, bsz, tm // bsz, d), F32), pltpu.SemaphoreType.DMA((2,))]
    return pl.pallas_call(
        kern,
        grid_spec=pltpu.PrefetchScalarGridSpec(
            num_scalar_prefetch=1, grid=(n // tm,),
            in_specs=[pl.BlockSpec((tm, d), row), pl.BlockSpec((tm, LANES), row), pl.BlockSpec((tm, LANES), row),
                      pl.BlockSpec((tm, LANES), row), pl.BlockSpec((1, LANES), lambda i, w: (0, 0)),
                      pl.BlockSpec((1, 1, n_slots), lambda i, w: (i, 0, 0)),
                      pl.BlockSpec(memory_space=pl.ANY)],
            out_specs=out_spec,
            scratch_shapes=scratch),
        out_shape=out_shape,
        compiler_params=_params("arbitrary"),
        name="moe_combine",
    )(win_start, h, tok, w1b, w2b, starts_row, slot_rows, ys)


def _gate_windows(lw, bd):
    out = []
    for c0 in range(0, lw, LRU_GATE_TILE):
        nc = min(LRU_GATE_TILE, lw - c0)
        r0 = (c0 // bd) * bd
        r1 = ((c0 + nc - 1) // bd + 1) * bd
        r0 = (r0 // LANES) * LANES
        r1 = min(-(-r1 // LANES) * LANES, lw)
        out.append((c0, nc, r0, r1 - r0))
    return out


def _lru_kernel(h_ref, ng_ref, win_ref, cw_ref, cb_ref, wg_ref, br_ref, bi_ref, lam_ref, wo_ref,
                o_ref, xbuf_ref, hs_ref, a_ref, b_ref, y_ref, gate_ref, res_ref, *, windows, bsz):
    ti = pl.program_id(0)
    rows, lw = a_ref.shape
    halo = (LRU_CONV_WIDTH - 1) * bsz

    @pl.when(ti == 0)
    def _():
        xbuf_ref[0:halo, :] = jnp.zeros((halo, lw), F32)
        hs_ref[...] = jnp.zeros(hs_ref.shape, F32)

    @pl.when(ti > 0)
    def _():
        xbuf_ref[0:halo, :] = xbuf_ref[rows:rows + halo, :]

    res_ref[...] = jnp.concatenate([h_ref[:, t, :] for t in range(rows // bsz)], axis=0)
    proj = _dot(_rms(res_ref[...], ng_ref[...]).astype(BF16), win_ref[...])
    gate_ref[...] = jax.nn.gelu(proj[:, 0:lw], approximate=True)
    xbuf_ref[halo:halo + rows, :] = proj[:, lw:2 * lw]
    xc = jnp.broadcast_to(cb_ref[...], (rows, lw))
    for k in range(LRU_CONV_WIDTH):
        xc = xc + cw_ref[k:k + 1, :] * xbuf_ref[k * bsz:k * bsz + rows, :]
    xcb = xc.astype(BF16)
    sp_lam = jnp.log(1.0 + jnp.exp(-lam_ref[...]))
    log2_a_per_r = (-LRU_C * LOG2E) * sp_lam
    wcol = 0
    for (c0, nc, r0, nr) in windows:
        rg = _dot(xcb[:, r0:r0 + nr], wg_ref[r0:r0 + nr, wcol:wcol + 2 * nc])
        wcol += 2 * nc
        r = _sigmoid(rg[:, 0:nc] + br_ref[:, c0:c0 + nc])
        ig = _sigmoid(rg[:, nc:2 * nc] + bi_ref[:, c0:c0 + nc])
        a = jnp.exp2(r * log2_a_per_r[:, c0:c0 + nc])
        a_ref[:, c0:c0 + nc] = a
        one_m_a2 = 1.0 - a * a
        root = jnp.where(one_m_a2 > 0.0, one_m_a2 * lax.rsqrt(one_m_a2), 0.0)
        b_ref[:, c0:c0 + nc] = root * (ig * xc[:, c0:c0 + nc])

    def step(t, hprev):
        r0 = pl.multiple_of(t * bsz, SUBLANES)
        hnew = a_ref[pl.ds(r0, bsz), :] * hprev + b_ref[pl.ds(r0, bsz), :]
        y_ref[pl.ds(r0, bsz), :] = hnew
        return hnew

    hs_ref[...] = lax.fori_loop(0, rows // bsz, step, hs_ref[...], unroll=LRU_SCAN_UNROLL)
    gy = (gate_ref[...] * y_ref[...]).astype(BF16)
    o_ref[...] = res_ref[...] + _dot(gy, wo_ref[...])


def _lru_mixer(h, ng, win, cw, cb, wg, br, bi, lam, wo, windows):
    bsz, tp, d = h.shape
    n = bsz * tp
    lw = lam.shape[-1]
    assert bsz % SUBLANES == 0 and LRU_ROWS % bsz == 0 and n % LRU_ROWS == 0
    rows = LRU_ROWS
    halo = (LRU_CONV_WIDTH - 1) * bsz
    const = lambda t: (0, 0)
    blk = lambda t: (t, 0)
    resident = lambda w: pl.BlockSpec(w.shape, const, pipeline_mode=pl.Buffered(1))
    return pl.pallas_call(
        functools.partial(_lru_kernel, windows=windows, bsz=bsz),
        grid=(n // rows,),
        in_specs=[pl.BlockSpec((bsz, rows // bsz, d), lambda t: (0, t, 0)), pl.BlockSpec((1, d), const), resident(win),
                  pl.BlockSpec(cw.shape, const), pl.BlockSpec((1, lw), const), resident(wg),
                  pl.BlockSpec((1, lw), const), pl.BlockSpec((1, lw), const),
                  pl.BlockSpec((1, lw), const), resident(wo)],
        out_specs=pl.BlockSpec((rows, d), blk),
        out_shape=jax.ShapeDtypeStruct((n, d), F32),
        scratch_shapes=[pltpu.VMEM((halo + rows, lw), F32), pltpu.VMEM((bsz, lw), F32),
                        pltpu.VMEM((rows, lw), F32), pltpu.VMEM((rows, lw), F32), pltpu.VMEM((rows, lw), F32),
                        pltpu.VMEM((rows, lw), F32), pltpu.VMEM((rows, d), F32)],
        compiler_params=_params("arbitrary"),
        name="rglru_mixer",
    )(h, ng, win, cw, cb, wg, br, bi, lam, wo)


ROUTE_E1, ROUTE_E2, ROUTE_R1, ROUTE_R2, ROUTE_VALID = range(5)


def _router_kernel(h_ref, g_ref, rw_ref, ltri_ref, xn_ref, info_ref, tok_ref, w1_ref, w2_ref, before_ref, cnt_ref,
                   run_ref, *, n_real_rows):
    i = pl.program_id(0)
    tm = h_ref.shape[0]

    @pl.when(i == 0)
    def _():
        run_ref[...] = jnp.zeros(run_ref.shape, F32)

    xn = _rms(h_ref[...], g_ref[...])
    xn_ref[...] = xn.astype(BF16)
    split = lambda a: (a.astype(BF16), (a - a.astype(BF16).astype(F32)).astype(BF16))
    (x_hi, x_lo), (r_hi, r_lo) = split(xn), split(rw_ref[...])
    logits = _dot(x_hi, r_hi) + (_dot(x_hi, r_lo) + _dot(x_lo, r_hi))
    lane = lax.broadcasted_iota(jnp.int32, logits.shape, 1)
    neg = jnp.float32(-jnp.inf)
    logits = jnp.where(lane < N_EXPERTS, logits, neg)
    top1 = jnp.max(logits, axis=1, keepdims=True)
    idx1 = jnp.min(jnp.where(logits == top1, lane, LANES), axis=1, keepdims=True)
    rest = jnp.where(lane == idx1, neg, logits)
    top2 = jnp.max(rest, axis=1, keepdims=True)
    idx2 = jnp.min(jnp.where(rest == top2, lane, LANES), axis=1, keepdims=True)
    w1 = 1.0 / (1.0 + jnp.exp(top2 - top1))

    rowg = lax.broadcasted_iota(jnp.int32, (tm, 1), 0) + i * tm
    valid = jnp.where(rowg < n_real_rows, 1.0, 0.0)
    onehot = (jnp.where(lane == idx1, 1.0, 0.0) + jnp.where(lane == idx2, 1.0, 0.0)) * valid
    before_ref[...] = jnp.broadcast_to(run_ref[...], before_ref.shape)
    before = _dot(ltri_ref[...], onehot.astype(BF16)) + run_ref[...]
    rank1 = jnp.sum(jnp.where(lane == idx1, before, 0.0), axis=1, keepdims=True)
    rank2 = jnp.sum(jnp.where(lane == idx2, before, 0.0), axis=1, keepdims=True)
    run_ref[...] += jnp.ceil(jnp.sum(onehot, axis=0, keepdims=True) / ROW_WIN) * ROW_WIN
    cnt_ref[...] = run_ref[...]
    w1_ref[...] = jnp.broadcast_to(w1, (tm, LANES))
    w2_ref[...] = jnp.broadcast_to(1.0 - w1, (tm, LANES))
    fields = {ROUTE_E1: idx1.astype(F32), ROUTE_E2: idx2.astype(F32),
              ROUTE_R1: rank1, ROUTE_R2: rank2, ROUTE_VALID: valid}
    info = jnp.zeros(logits.shape, F32)
    for col, val in fields.items():
        info = jnp.where(lane == col, val, info)
    tok_ref[...] = info
    info_ref[...] = info.T[0:SUBLANES, :]


def _router(h, g, rw, ltri, n_real_rows):
    n, d = h.shape
    tm = ROW_TILE
    row = lambda i: (i, 0)
    const = lambda i: (0, 0)
    return pl.pallas_call(
        functools.partial(_router_kernel, n_real_rows=n_real_rows),
        grid=(n // tm,),
        in_specs=[pl.BlockSpec((tm, d), row), pl.BlockSpec((1, d), const), pl.BlockSpec(rw.shape, const),
                  pl.BlockSpec(ltri.shape, const)],
        out_specs=[pl.BlockSpec((tm, d), row), pl.BlockSpec((SUBLANES, tm), row),
                   pl.BlockSpec((tm, LANES), row), pl.BlockSpec((tm, LANES), row),
                   pl.BlockSpec((tm, LANES), row), pl.BlockSpec((SUBLANES, LANES), row),
                   pl.BlockSpec((1, LANES), const)],
        out_shape=[jax.ShapeDtypeStruct((n, d), BF16), jax.ShapeDtypeStruct((n // tm * SUBLANES, tm), F32),
                   jax.ShapeDtypeStruct((n, LANES), F32), jax.ShapeDtypeStruct((n, LANES), F32),
                   jax.ShapeDtypeStruct((n, LANES), F32), jax.ShapeDtypeStruct((n // tm * SUBLANES, LANES), F32),
                   jax.ShapeDtypeStruct((1, LANES), F32)],
        scratch_shapes=[pltpu.VMEM((1, LANES), F32)],
        compiler_params=_params("arbitrary"),
        name="router",
    )(h, g, rw, ltri)


def _routing_tables(before, counts, n_rows):
    tg = MOE_TILE
    win = ROW_WIN
    cnt = counts[0, :N_EXPERTS].astype(jnp.int32)
    padded = (cnt + tg - 1) // tg * tg
    ends = jnp.cumsum(padded)
    starts = ends - padded
    tile_start = jnp.arange(n_rows // tg, dtype=jnp.int32) * tg
    tile_expert = jnp.minimum(jnp.sum(tile_start[:, None] >= ends[None, :], axis=1), N_EXPERTS - 1)
    n_used = (ends[-1] // tg).reshape(1)
    bef = before.reshape(-1, SUBLANES, LANES)[:, 0, :N_EXPERTS].astype(jnp.int32)
    nwin = (jnp.concatenate([bef[1:], cnt[None, :]], axis=0) - bef) // win
    base = jnp.cumsum(nwin, axis=1) - nwin
    slot = jnp.arange(ROW_NWIN, dtype=jnp.int32)[None, :, None]
    mine = (slot >= base[:, None, :]) & (slot < (base + nwin)[:, None, :])
    rows = jnp.sum(jnp.where(mine, (starts[None, :] + bef)[:, None, :] + win * (slot - base[:, None, :]), 0), axis=2)
    used = jnp.any(mine, axis=2)
    parity = (jnp.arange(bef.shape[0], dtype=jnp.int32) % 2)[:, None]
    scatter_rows = jnp.where(used, rows, n_rows + win * (parity * ROW_NWIN + slot[:, :, 0]))
    gather_rows = jnp.where(used, rows, 0)
    slot_rows = jnp.where(used[..., None], rows[..., None] + jnp.arange(win, dtype=jnp.int32), -1)
    slot_rows = slot_rows.reshape(-1, 1, ROW_NWIN * win).astype(F32)
    starts_row = jnp.zeros((1, LANES), F32).at[0, :N_EXPERTS].set(starts.astype(F32))
    local_off = jnp.zeros(bef.shape[:1] + (SUBLANES,), jnp.int32).at[:, :N_EXPERTS].set(win * base - bef)
    local_off = jnp.broadcast_to(local_off.astype(F32)[:, :, None], local_off.shape + (LANES,)).reshape(-1, LANES)
    return (tile_expert.astype(jnp.int32), n_used.astype(jnp.int32), scatter_rows.reshape(-1).astype(jnp.int32),
            gather_rows.reshape(-1).astype(jnp.int32), slot_rows, starts_row, local_off)


def _interleave_gate_up(wg, wu, tf):
    e, d, ff = wg.shape
    g = wg.astype(BF16).reshape(e, d, ff // tf, tf)
    u = wu.astype(BF16).reshape(e, d, ff // tf, tf)
    return jnp.concatenate([g, u], axis=3).reshape(e, d, 2 * ff)


def _dense_block_diag(w):
    nb, bd, _ = w.shape
    eye = jnp.eye(nb, dtype=w.dtype)
    return (w[:, :, None, :] * eye[:, None, :, None]).reshape(nb * bd, nb * bd)


def _pack_lru_gates(wr, wi, windows):
    dr = _dense_block_diag(wr).astype(BF16)
    di = _dense_block_diag(wi).astype(BF16)
    cols = []
    for (c0, nc, _, _) in windows:
        cols += [dr[:, c0:c0 + nc], di[:, c0:c0 + nc]]
    return jnp.concatenate(cols, axis=1)


def _ffn_chunk(ff):
    for tf in (896, 512, 384, 256, 128):
        if ff % tf == 0:
            return tf
    raise ValueError(f"unsupported d_ff {ff}")


def kernel(x, meta_tokens, mix_norm_even, w_in_even, conv_w, conv_b, conv_ln_g, conv_ln_b, q_norm_g, k_norm_g, w_out_even, ffn_norm_even, ffn_w_gate, ffn_w_up, ffn_w_down, mix_norm_odd, w_in_odd, lru_conv_w, lru_conv_b, gate_r_w, gate_r_b, gate_i_w, gate_i_b, lru_lambda, w_out_odd, ffn_norm_odd, router_w, moe_w_gate, moe_w_up, moe_w_down):
    bsz, seq, d = x.shape
    t_real = N_META + seq
    tp = -(-t_real // TIME_TILE) * TIME_TILE
    n = bsz * tp
    assert n % ROW_TILE == 0
    depth = mix_norm_even.shape[0] + mix_norm_odd.shape[0]

    meta = jnp.broadcast_to(meta_tokens[None].astype(x.dtype), (bsz, N_META, d))
    h = jnp.concatenate([meta, x, jnp.zeros((bsz, tp - t_real, d), x.dtype)], axis=1).reshape(n, d)

    head_mean = jnp.kron(jnp.eye(MXU_DIM // SB_HEAD_DIM, dtype=F32),
                         jnp.full((SB_HEAD_DIM, SB_HEAD_DIM), 1.0 / SB_HEAD_DIM, F32)).astype(BF16)
    kk = jnp.arange(ATT_TK)
    neg_tri = -(kk[:, None] >= kk[None, :]).astype(BF16)
    row2 = lambda a: a.reshape(1, -1)

    time_major = False
    for layer in range(depth):
        p = layer // 2
        if layer % 2 == 0:
            if time_major:
                h = h.reshape(tp, bsz, d).transpose(1, 0, 2).reshape(n, d)
                time_major = False
            u, q, k, v = _even_in_proj(h, row2(mix_norm_even[p]), w_in_even[p].astype(BF16),
                                       row2(jnp.tile(q_norm_g[p], SB_HEADS)),
                                       row2(jnp.tile(k_norm_g[p], SB_HEADS)), head_mean)
            u = _conv_module(u.reshape(bsz, tp, CONV_CH), conv_w[p], row2(conv_b[p]),
                             row2(conv_ln_g[p]), row2(conv_ln_b[p]))
            o = _attention(q.reshape(bsz, tp, SB_WIDTH), k.reshape(bsz, tp, SB_WIDTH),
                           v.reshape(bsz, tp, SB_WIDTH), neg_tri, t_real)
            tf = _ffn_chunk(ffn_w_gate.shape[-1])
            h = _even_ffn(h, u.reshape(n, CONV_CH), o.reshape(n, SB_WIDTH), w_out_even[p].astype(BF16),
                          row2(ffn_norm_even[p]),
                          _interleave_gate_up(ffn_w_gate[p][None], ffn_w_up[p][None], tf)[0],
                          ffn_w_down[p].astype(BF16), tf)
        else:
            if time_major:
                h = h.reshape(tp, bsz, d).transpose(1, 0, 2).reshape(n, d)
            time_major = True
            lw = lru_lambda.shape[-1]
            windows = _gate_windows(lw, lw // LRU_BLOCKS)
            h = _lru_mixer(h.reshape(bsz, tp, d), row2(mix_norm_odd[p]), w_in_odd[p].astype(BF16),
                           lru_conv_w[p], row2(lru_conv_b[p]),
                           _pack_lru_gates(gate_r_w[p], gate_i_w[p], windows),
                           row2(gate_r_b[p]), row2(gate_i_b[p]), row2(lru_lambda[p]),
                           w_out_odd[p].astype(BF16), windows)
            rw = jnp.pad(router_w[p], ((0, 0), (0, LANES - N_EXPERTS)))
            kk = jnp.arange(ROW_TILE)
            ltri = (kk[:, None] > kk[None, :]).astype(BF16)
            xn, info, tok, w1b, w2b, before, counts = _router(h, row2(ffn_norm_odd[p]), rw, ltri, t_real * bsz)
            tg = MOE_TILE
            assert ROW_TILE == COMBINE_TILE
            n_tiles = n // COMBINE_TILE
            n_rows = 2 * bsz * t_real + n_tiles * N_EXPERTS * (ROW_WIN - 1) + N_EXPERTS * (tg - 1)
            n_rows = -(-n_rows // tg) * tg
            tile_expert, n_used, scatter_rows, gather_rows, slot_rows, starts_row, local_off = _routing_tables(
                before, counts, n_rows)
            xs = _row_scatter(scatter_rows, xn, info, local_off, n_rows + 2 * ROW_NWIN * ROW_WIN)
            ys = _grouped_ffn(tile_expert, n_used, xs, n_rows, moe_w_gate[p].astype(BF16),
                              moe_w_up[p].astype(BF16), moe_w_down[p].astype(BF16))
            if layer + 1 == depth:
                return _combine(gather_rows, h, tok, w1b, w2b, starts_row, slot_rows, ys, (bsz, N_META, t_real))
            h = _combine(gather_rows, h, tok, w1b, w2b, starts_row, slot_rows, ys)
    if time_major:
        return h.reshape(tp, bsz, d)[N_META:t_real].transpose(1, 0, 2)
    return h.reshape(bsz, tp, d)[:, N_META:t_real]
```
